```python
import jax, jax.numpy as jnp
from jax import lax
import numpy as np

D_MODEL = 2048
BATCH = 2
SEQ = 4096
DEPTH = 4

HEAD_DIM = 128
Q_BLOCK = 128
EPS = 1e-6
ATTN_SCALE = HEAD_DIM ** -0.5
FOX_HEADS = 8
FOX_GATE_BIAS = 4.0
HGRN_HEADS = 8
HGRN_KDIM = 128
HGRN_VDIM = 128
HGRN_CHUNK = 64
NSA_HEADS = 16
NSA_KV_HEADS = 4
NSA_GROUP = NSA_HEADS // NSA_KV_HEADS
CMP_BLOCK = 32
CMP_STRIDE = 16
CMP_HIDDEN = 256
SLC_BLOCK = 64
SLC_TOPK = 16
SLC_Q_CHUNK = 16
WINDOW = 512
MEM_LEN = 256
MEM_HEADS = 4
MEM_W = MEM_HEADS * HEAD_DIM
FFN_HIDDEN = -(-8 * D_MODEL // (3 * 256)) * 256

FOX_W = FOX_HEADS * HEAD_DIM
HGRN_KW = HGRN_HEADS * HGRN_KDIM
HGRN_VW = HGRN_HEADS * HGRN_VDIM
EVEN_SIZES = [FOX_W, FOX_W, FOX_W, FOX_HEADS, HGRN_KW, HGRN_KW, HGRN_VW, HGRN_VW]
EVEN_IN = sum(EVEN_SIZES)
EVEN_SPLITS = [int(v) for v in np.cumsum(EVEN_SIZES)[:-1]]
EVEN_OUT = FOX_W + HGRN_VW
NSA_QW = NSA_HEADS * HEAD_DIM
NSA_KVW = NSA_KV_HEADS * HEAD_DIM
ODD_SIZES = [NSA_QW] + [NSA_KVW] * 6 + [3 * NSA_HEADS]
ODD_IN = sum(ODD_SIZES)
ODD_SPLITS = [int(v) for v in np.cumsum(ODD_SIZES)[:-1]]

kernel_name = 'fox_hgrn2_nsa_hybrid'


def rms_norm(x, gain):
    xf = x.astype(jnp.float32)
    y = xf * lax.rsqrt(jnp.mean(xf * xf, axis=-1, keepdims=True) + EPS)
    return (y * gain.astype(jnp.float32)).astype(x.dtype)


def split_heads(t, n):
    B, T, _ = t.shape
    return t.reshape(B, T, n, -1).transpose(0, 2, 1, 3)


def merge_heads(t):
    B, n, T, d = t.shape
    return t.transpose(0, 2, 1, 3).reshape(B, T, n * d)


def alibi_slopes(n):
    return 2.0 ** (-8.0 * jnp.arange(1, n + 1, dtype=jnp.float32) / n)


def masked_softmax(s, mask):
    s = jnp.where(mask, s.astype(jnp.float32), -jnp.inf)
    m = jnp.max(s, axis=-1, keepdims=True)
    m = jnp.where(jnp.isfinite(m), m, 0.0)
    p = jnp.exp(s - m)
    z = jnp.sum(p, axis=-1, keepdims=True)
    return p / jnp.where(z > 0, z, 1.0)


def forgetting_attention(q, k, v, log_f):
    B, H, T, d = q.shape
    c = jnp.cumsum(log_f, axis=-1)
    nb = T // Q_BLOCK
    qb = jnp.moveaxis(q.reshape(B, H, nb, Q_BLOCK, d), 2, 0)
    cb = jnp.moveaxis(c.reshape(B, H, nb, Q_BLOCK), 2, 0)
    kpos = jnp.arange(T)

    def block(args):
        qi, ci, i = args
        tq = i * Q_BLOCK + jnp.arange(Q_BLOCK)
        s = jnp.einsum('bhqd,bhkd->bhqk', qi, k, preferred_element_type=jnp.float32) * ATTN_SCALE
        s = s + ci[..., :, None] - c[..., None, :]
        s = jnp.where(tq[:, None] >= kpos[None, :], s, -jnp.inf)
        p = jax.nn.softmax(s, axis=-1)
        return jnp.einsum('bhqk,bhkd->bhqd', p, v.astype(jnp.float32))

    o = lax.map(block, (qb, cb, jnp.arange(nb)))
    return jnp.moveaxis(o, 0, 2).reshape(B, H, T, d)


def hgrn2_recurrence(q, k, v, log_f):
    B, H, T, dk = q.shape
    dv = v.shape[-1]
    C = HGRN_CHUNK
    nc = T // C

    def to_chunks(t):
        return jnp.moveaxis(t.reshape(B, H, nc, C, t.shape[-1]), 2, 0)

    causal = jnp.tril(jnp.ones((C, C), dtype=bool))[:, :, None]

    def step(S, inp):
        qi, ki, vi, gi = inp
        b = jnp.cumsum(gi, axis=2)
        diff = b[:, :, :, None, :] - b[:, :, None, :, :]
        decay = jnp.exp(jnp.where(causal, diff, -jnp.inf))
        A = jnp.einsum('bhtk,bhtsk,bhsk->bhts', qi, decay, ki)
        o = jnp.einsum('bhts,bhsv->bhtv', A, vi) + jnp.einsum('bhtk,bhkv->bhtv', qi * jnp.exp(b), S)
        b_last = b[:, :, -1:, :]
        S = jnp.exp(b_last[:, :, 0, :])[..., None] * S + jnp.einsum('bhsk,bhsv->bhkv', ki * jnp.exp(b_last - b), vi)
        return S, o

    S0 = jnp.zeros((B, H, dk, dv), jnp.float32)
    _, o = lax.scan(step, S0, tuple(to_chunks(t) for t in (q, k, v, log_f)))
    return jnp.moveaxis(o, 0, 2).reshape(B, H, T, dv)


def even_mixer(h, w_in, w_out, f_bias, q_gain, k_gain, lb, o_gain):
    f32 = jnp.float32
    fq, fk, fv, ff, hq, hf, hi, hg = jnp.split(h @ w_in, EVEN_SPLITS, axis=-1)
    q = rms_norm(split_heads(fq, FOX_HEADS), q_gain)
    k = rms_norm(split_heads(fk, FOX_HEADS), k_gain)
    v = split_heads(fv, FOX_HEADS)
    log_f_fox = jax.nn.log_sigmoid(ff.astype(f32) + f_bias.astype(f32)).transpose(0, 2, 1)
    o_fox = forgetting_attention(q, k, v, log_f_fox)
    z = split_heads(hf, HGRN_HEADS).astype(f32)
    lb_h = lb.astype(f32).reshape(HGRN_HEADS, 1, HGRN_KDIM)
    log_f = jnp.logaddexp(jnp.log(lb_h), jnp.log1p(-lb_h) + jax.nn.log_sigmoid(z))
    k_h = (1.0 - lb_h) * jax.nn.sigmoid(-z)
    q_h = jax.nn.silu(split_heads(hq, HGRN_HEADS).astype(f32))
    o_h = hgrn2_recurrence(q_h, k_h, split_heads(hi, HGRN_HEADS).astype(f32), log_f)
    o_h = rms_norm(o_h, o_gain) * jax.nn.silu(split_heads(hg, HGRN_HEADS).astype(f32))
    o = jnp.concatenate([merge_heads(o_fox), merge_heads(o_h)], axis=-1).astype(h.dtype)
    return o @ w_out


def compress_blocks(t, pe, w1, w2):
    T = t.shape[2]
    n_cmp = (T - CMP_BLOCK) // CMP_STRIDE + 1
    idx = (jnp.arange(n_cmp) * CMP_STRIDE)[:, None] + jnp.arange(CMP_BLOCK)[None, :]
    blocks = t[:, :, idx, :] + pe
    hid = jax.nn.gelu(jnp.einsum('bgnld,ldm->bgnm', blocks, w1))
    return jnp.einsum('bgnm,md->bgnd', hid, w2)


def nsa_mixer(h, w_in, w_out, q_gain, k_gain, pe_k, w1_k, w2_k, pe_v, w1_v, w2_v):
    B, T, _ = h.shape
    G, J, d = NSA_KV_HEADS, NSA_GROUP, HEAD_DIM
    f32 = jnp.float32
    q, kc, vc, ks, vs, kw, vw, gt = jnp.split(h @ w_in, ODD_SPLITS, axis=-1)
    q = rms_norm(q.reshape(B, T, G, J, d).transpose(0, 2, 3, 1, 4), q_gain)

    def kv_heads(t):
        return t.reshape(B, T, G, d).transpose(0, 2, 1, 3)

    k_cmp = rms_norm(compress_blocks(kv_heads(kc), pe_k, w1_k, w2_k), k_gain[0])
    v_cmp = compress_blocks(kv_heads(vc), pe_v, w1_v, w2_v).astype(f32)
    k_slc = rms_norm(kv_heads(ks), k_gain[1])
    v_slc = kv_heads(vs)
    k_win = rms_norm(kv_heads(kw), k_gain[2])
    v_win = kv_heads(vw)
    gates = jax.nn.sigmoid(gt.astype(f32)).reshape(B, T, G, J, 3).transpose(0, 2, 3, 1, 4)
    sl5 = alibi_slopes(NSA_HEADS).reshape(G, J)[None, :, :, None, None]

    n_cmp = (T - CMP_BLOCK) // CMP_STRIDE + 1
    cmp_start = jnp.arange(n_cmp) * CMP_STRIDE
    cmp_end = cmp_start + CMP_BLOCK - 1
    cmp_mid = cmp_start.astype(f32) + 0.5 * (CMP_BLOCK - 1)
    n_slc = T // SLC_BLOCK
    slc_start = jnp.arange(n_slc) * SLC_BLOCK
    overlap = ((cmp_start[:, None] <= slc_start[None, :] + SLC_BLOCK - 1)
               & (cmp_end[:, None] >= slc_start[None, :])).astype(f32)
    n_sel = min(SLC_TOPK, n_slc)
    k_win_p = jnp.pad(k_win, ((0, 0), (0, 0), (WINDOW, 0), (0, 0)))
    v_win_p = jnp.pad(v_win, ((0, 0), (0, 0), (WINDOW, 0), (0, 0))).astype(f32)
    band = jnp.arange(WINDOW + Q_BLOCK)
    blk = jnp.arange(n_slc)

    def query_block(i):
        t = i * Q_BLOCK + jnp.arange(Q_BLOCK)
        qi = lax.dynamic_slice_in_dim(q, i * Q_BLOCK, Q_BLOCK, axis=3)
        s = jnp.einsum('bgjqd,bgnd->bgjqn', qi, k_cmp, preferred_element_type=f32) * ATTN_SCALE
        s = s - sl5 * (t[:, None].astype(f32) - cmp_mid[None, :])
        p_cmp = masked_softmax(s, cmp_end[None, :] <= t[:, None])
        o_cmp = jnp.einsum('bgjqn,bgnd->bgjqd', p_cmp, v_cmp)
        imp = jnp.einsum('bgjqn,nm->bgqm', p_cmp, overlap)
        imp = jnp.where(slc_start[None, :] > t[:, None], -jnp.inf, imp)
        imp = jnp.where((blk[None, :] == (t // SLC_BLOCK)[:, None]) | (blk[None, :] == 0), jnp.inf, imp)
        _, idx = lax.top_k(imp, n_sel)
        kwi = lax.dynamic_slice_in_dim(k_win_p, i * Q_BLOCK, WINDOW + Q_BLOCK, axis=2)
        vwi = lax.dynamic_slice_in_dim(v_win_p, i * Q_BLOCK, WINDOW + Q_BLOCK, axis=2)
        kpos = i * Q_BLOCK - WINDOW + band
        dist = t[:, None] - kpos[None, :]
        s = jnp.einsum('bgjqd,bgkd->bgjqk', qi, kwi, preferred_element_type=f32) * ATTN_SCALE
        s = s - sl5 * dist.astype(f32)
        p_win = masked_softmax(s, (dist >= 0) & (dist < WINDOW) & (kpos[None, :] >= 0))
        o_win = jnp.einsum('bgjqk,bgkd->bgjqd', p_win, vwi)
        return o_cmp, o_win, idx.astype(jnp.int32)

    nb = T // Q_BLOCK
    o_cmp, o_win, idx = lax.map(query_block, jnp.arange(nb))
    o_cmp = jnp.moveaxis(o_cmp, 0, 3).reshape(B, G, J, T, d)
    o_win = jnp.moveaxis(o_win, 0, 3).reshape(B, G, J, T, d)
    idx = jnp.moveaxis(idx, 0, 2).reshape(B, G, T, n_sel)

    c = SLC_Q_CHUNK
    n_sc = T // c
    ks_blocks = k_slc.reshape(B, G, n_slc, SLC_BLOCK, d)
    vs_blocks = v_slc.reshape(B, G, n_slc, SLC_BLOCK, d)
    q_ch = jnp.moveaxis(q.reshape(B, G, J, n_sc, c, d), 3, 0)
    idx_ch = jnp.moveaxis(idx.reshape(B, G, n_sc, c, n_sel), 2, 0)
    bi = jnp.arange(B)[:, None, None, None]
    gi = jnp.arange(G)[None, :, None, None]
    inner = jnp.arange(SLC_BLOCK)
    M = n_sel * SLC_BLOCK

    def select_chunk(args):
        qj, ij, j = args
        t = j * c + jnp.arange(c)
        kg = ks_blocks[bi, gi, ij].reshape(B, G, c, M, d)
        vg = vs_blocks[bi, gi, ij].reshape(B, G, c, M, d).astype(f32)
        kpos = (ij[..., None] * SLC_BLOCK + inner).reshape(B, G, c, M)
        dist = t[None, None, :, None] - kpos
        s = jnp.einsum('bgjcd,bgcmd->bgjcm', qj, kg, preferred_element_type=f32) * ATTN_SCALE
        s = s - sl5 * dist[:, :, None].astype(f32)
        p = masked_softmax(s, (dist >= 0)[:, :, None])
        return jnp.einsum('bgjcm,bgcmd->bgjcd', p, vg)

    o_slc = lax.map(select_chunk, (q_ch, idx_ch, jnp.arange(n_sc)))
    o_slc = jnp.moveaxis(o_slc, 0, 3).reshape(B, G, J, T, d)

    o = gates[..., 0:1] * o_cmp + gates[..., 1:2] * o_slc + gates[..., 2:3] * o_win
    o = o.transpose(0, 3, 1, 2, 4).reshape(B, T, NSA_QW).astype(h.dtype)
    return o @ w_out


def memory_attention(h, mem_n, wq, wkv, wo, q_gain, k_gain):
    q = rms_norm(split_heads(h @ wq, MEM_HEADS), q_gain)
    k, v = jnp.split(mem_n @ wkv, 2, axis=-1)
    k = rms_norm(split_heads(k, MEM_HEADS), k_gain)
    v = split_heads(v, MEM_HEADS).astype(jnp.float32)
    s = jnp.einsum('bhqd,bhkd->bhqk', q, k, preferred_element_type=jnp.float32) * ATTN_SCALE
    p = jax.nn.softmax(s, axis=-1)
    o = jnp.einsum('bhqk,bhkd->bhqd', p, v)
    return merge_heads(o).astype(h.dtype) @ wo


def swiglu(h, w1, w3, w2):
    return (jax.nn.silu(h @ w1) * (h @ w3)) @ w2


def setup_inputs(seed: int = 0) -> dict:
    key = jax.random.key(seed)
    keys = iter(jax.random.split(key, 64))
    f32 = jnp.float32
    n_even = (DEPTH + 1) // 2
    n_odd = DEPTH // 2

    def dense(shape, fan_in):
        return jax.random.normal(next(keys), shape, f32) * (fan_in ** -0.5)

    def gain(shape):
        return 1.0 + 0.02 * jax.random.normal(next(keys), shape, f32)

    def small(shape, scale):
        return scale * jax.random.normal(next(keys), shape, f32)

    return {
        'x': jax.random.normal(next(keys), (BATCH, SEQ, D_MODEL), f32),
        'mem': jax.random.normal(next(keys), (BATCH, MEM_LEN, D_MODEL), f32),
        'norm_mix': gain((DEPTH, D_MODEL)),
        'norm_mem': gain((DEPTH, D_MODEL)),
        'norm_ffn': gain((DEPTH, D_MODEL)),
        'mem_in_gain': gain((DEPTH, D_MODEL)),
        'even_w_in': dense((n_even, D_MODEL, EVEN_IN), D_MODEL),
        'even_w_out': dense((n_even, EVEN_OUT, D_MODEL), EVEN_OUT),
        'fox_f_bias': FOX_GATE_BIAS + small((n_even, FOX_HEADS), 1.0),
        'fox_q_gain': gain((n_even, HEAD_DIM)),
        'fox_k_gain': gain((n_even, HEAD_DIM)),
        'hgrn_lb_logits': small((n_even, HGRN_KW), 1.0),
        'hgrn_o_gain': gain((n_even, HGRN_VDIM)),
        'odd_w_in': dense((n_odd, D_MODEL, ODD_IN), D_MODEL),
        'odd_w_out': dense((n_odd, NSA_QW, D_MODEL), NSA_QW),
        'nsa_q_gain': gain((n_odd, HEAD_DIM)),
        'nsa_k_gain': gain((n_odd, 3, HEAD_DIM)),
        'cmp_pe_k': small((n_odd, CMP_BLOCK, HEAD_DIM), 0.02),
        'cmp_w1_k': dense((n_odd, CMP_BLOCK, HEAD_DIM, CMP_HIDDEN), CMP_BLOCK * HEAD_DIM),
        'cmp_w2_k': dense((n_odd, CMP_HIDDEN, HEAD_DIM), CMP_HIDDEN),
        'cmp_pe_v': small((n_odd, CMP_BLOCK, HEAD_DIM), 0.02),
        'cmp_w1_v': dense((n_odd, CMP_BLOCK, HEAD_DIM, CMP_HIDDEN), CMP_BLOCK * HEAD_DIM),
        'cmp_w2_v': dense((n_odd, CMP_HIDDEN, HEAD_DIM), CMP_HIDDEN),
        'mem_wq': dense((DEPTH, D_MODEL, MEM_W), D_MODEL),
        'mem_wkv': dense((DEPTH, D_MODEL, 2 * MEM_W), D_MODEL),
        'mem_wo': dense((DEPTH, MEM_W, D_MODEL), MEM_W),
        'mem_q_gain': gain((DEPTH, HEAD_DIM)),
        'mem_k_gain': gain((DEPTH, HEAD_DIM)),
        'ffn_w1': dense((DEPTH, D_MODEL, FFN_HIDDEN), D_MODEL),
        'ffn_w3': dense((DEPTH, D_MODEL, FFN_HIDDEN), D_MODEL),
        'ffn_w2': dense((DEPTH, FFN_HIDDEN, D_MODEL), FFN_HIDDEN),
    }


def reference(x, mem, norm_mix, norm_mem, norm_ffn, mem_in_gain, even_w_in, even_w_out, fox_f_bias,
              fox_q_gain, fox_k_gain, hgrn_lb_logits, hgrn_o_gain, odd_w_in, odd_w_out, nsa_q_gain,
              nsa_k_gain, cmp_pe_k, cmp_w1_k, cmp_w2_k, cmp_pe_v, cmp_w1_v, cmp_w2_v, mem_wq, mem_wkv,
              mem_wo, mem_q_gain, mem_k_gain, ffn_w1, ffn_w3, ffn_w2):
    lb_cum = jnp.cumsum(jax.nn.softmax(hgrn_lb_logits.astype(jnp.float32), axis=0), axis=0)
    hgrn_lb = lb_cum - lb_cum[0:1]
    for layer in range(DEPTH):
        h = rms_norm(x, norm_mix[layer])
        if layer % 2 == 0:
            e = layer // 2
            mix = even_mixer(h, even_w_in[e], even_w_out[e], fox_f_bias[e], fox_q_gain[e], fox_k_gain[e],
                             hgrn_lb[e], hgrn_o_gain[e])
        else:
            o = layer // 2
            mix = nsa_mixer(h, odd_w_in[o], odd_w_out[o], nsa_q_gain[o], nsa_k_gain[o], cmp_pe_k[o],
                            cmp_w1_k[o], cmp_w2_k[o], cmp_pe_v[o], cmp_w1_v[o], cmp_w2_v[o])
        x = x + mix.astype(x.dtype)
        h = rms_norm(x, norm_mem[layer])
        mem_n = rms_norm(mem, mem_in_gain[layer])
        x = x + memory_attention(h, mem_n, mem_wq[layer], mem_wkv[layer], mem_wo[layer],
                                 mem_q_gain[layer], mem_k_gain[layer]).astype(x.dtype)
        h = rms_norm(x, norm_ffn[layer])
        x = x + swiglu(h, ffn_w1[layer], ffn_w3[layer], ffn_w2[layer]).astype(x.dtype)
    return x
```

```python
import functools

import jax
import jax.numpy as jnp
from jax import lax
from jax.experimental import pallas as pl
from jax.experimental.pallas import tpu as pltpu

F32 = jnp.float32
BF16 = jnp.bfloat16
NEG_INF = float("-inf")

EPS = 1e-6
HEAD_DIM = 128
ATTN_SCALE = HEAD_DIM ** -0.5
LANES = 128
FOX_HEADS = 8
HGRN_HEADS = 8
HGRN_BLOCK = 16
NSA_HEADS = 16
NSA_KV_HEADS = 4
NSA_GROUP = NSA_HEADS // NSA_KV_HEADS
CMP_BLOCK = 32
CMP_STRIDE = 16
SLC_BLOCK = 64
SLC_TOPK = 16
WINDOW = 512
MEM_HEADS = 4
V7X_VMEM_LIMIT = 56 * 1024 * 1024


def _params(*sem):
    return pltpu.CompilerParams(dimension_semantics=sem, vmem_limit_bytes=V7X_VMEM_LIMIT)


def _rms(x, gain):
    return x * lax.rsqrt(jnp.mean(x * x, axis=-1, keepdims=True) + EPS) * gain


def _dot(a, b):
    return jnp.dot(a, b, preferred_element_type=F32)


def _dot_nt(a, b):
    return lax.dot_general(a, b, (((1,), (1,)), ((), ())), preferred_element_type=F32)


def _dot_tn(a, b):
    return lax.dot_general(a, b, (((0,), (0,)), ((), ())), preferred_element_type=F32)


def _exact_dot(sel, x):
    x1 = x.astype(BF16)
    r1 = x - x1.astype(F32)
    x2 = r1.astype(BF16)
    x3 = (r1 - x2.astype(F32)).astype(BF16)
    return _dot(sel, x1) + _dot(sel, x2) + _dot(sel, x3)


def _log_sigmoid(z):
    return jnp.minimum(z, 0.0) - jnp.log1p(jnp.exp(-jnp.abs(z)))


def _softmax_step(s, m, l):
    m_new = jnp.maximum(m, jnp.max(s, axis=-1, keepdims=True))
    m_safe = jnp.where(m_new == NEG_INF, 0.0, m_new)
    p = jnp.exp(s - m_safe)
    alpha = jnp.exp(m - m_safe)
    l_new = alpha * l + jnp.sum(p, axis=-1, keepdims=True)
    return p, alpha, m_new, l_new


def _rms_matmul_kernel(x_ref, g_ref, w_ref, o_ref, hn_ref):
    @pl.when(pl.program_id(1) == 0)
    def _():
        hn_ref[...] = _rms(x_ref[...], g_ref[...]).astype(BF16)

    o_ref[...] = _dot(hn_ref[...], w_ref[...])


def rms_matmul(x, gain, w, tm, tn):
    M, K = x.shape
    N = w.shape[1]
    tm, tn = min(tm, M), min(tn, N)
    assert M % tm == 0 and N % tn == 0
    return pl.pallas_call(
        _rms_matmul_kernel,
        name="rms_matmul",
        grid=(M // tm, N // tn),
        in_specs=[pl.BlockSpec((tm, K), lambda i, j: (i, 0)),
                  pl.BlockSpec((1, K), lambda i, j: (0, 0)),
                  pl.BlockSpec((K, tn), lambda i, j: (0, j))],
        out_specs=pl.BlockSpec((tm, tn), lambda i, j: (i, j)),
        out_shape=jax.ShapeDtypeStruct((M, N), F32),
        scratch_shapes=[pltpu.VMEM((tm, K), BF16)],
        compiler_params=_params("parallel", "arbitrary"),
    )(x, gain.reshape(1, K), w)


def _out_proj_kernel(*refs, n_in):
    a_refs, w_refs = refs[:n_in], refs[n_in:2 * n_in]
    res_ref, o_ref = refs[2 * n_in], refs[2 * n_in + 1]
    acc = res_ref[...]
    for a_ref, w_ref in zip(a_refs, w_refs):
        acc = acc + _dot(a_ref[...], w_ref[...])
    o_ref[...] = acc


def out_proj(acts, weights, res, tm):
    M, N = res.shape
    tm = min(tm, M)
    assert M % tm == 0
    n_in = len(acts)
    in_specs = ([pl.BlockSpec((tm, a.shape[1]), lambda i: (i, 0)) for a in acts]
                + [pl.BlockSpec(w.shape, lambda i: (0, 0)) for w in weights]
                + [pl.BlockSpec((tm, N), lambda i: (i, 0))])
    return pl.pallas_call(
        functools.partial(_out_proj_kernel, n_in=n_in),
        name="out_proj",
        grid=(M // tm,),
        in_specs=in_specs,
        out_specs=pl.BlockSpec((tm, N), lambda i: (i, 0)),
        out_shape=jax.ShapeDtypeStruct((M, N), F32),
        compiler_params=_params("parallel"),
    )(*acts, *weights, res)


def _ffn_kernel(x_ref, g_ref, w1_ref, w3_ref, w2_ref, o_ref, hn_ref):
    f = pl.program_id(1)

    @pl.when(f == 0)
    def _():
        x = x_ref[...]
        hn_ref[...] = _rms(x, g_ref[...]).astype(BF16)
        o_ref[...] = x

    h = hn_ref[...]
    a = _dot(h, w1_ref[...])
    b = _dot(h, w3_ref[...])
    u = (a * jax.nn.sigmoid(a) * b).astype(BF16)
    o_ref[...] += _dot(u, w2_ref[...])


def ffn(x, gain, w1, w3, w2, tm, tf):
    M, D = x.shape
    Fh = w1.shape[1]
    tm, tf = min(tm, M), min(tf, Fh)
    assert M % tm == 0 and Fh % tf == 0
    return pl.pallas_call(
        _ffn_kernel,
        name="ffn",
        grid=(M // tm, Fh // tf),
        in_specs=[pl.BlockSpec((tm, D), lambda i, f: (i, 0)),
                  pl.BlockSpec((1, D), lambda i, f: (0, 0)),
                  pl.BlockSpec((D, tf), lambda i, f: (0, f)),
                  pl.BlockSpec((D, tf), lambda i, f: (0, f)),
                  pl.BlockSpec((tf, D), lambda i, f: (f, 0))],
        out_specs=pl.BlockSpec((tm, D), lambda i, f: (i, 0)),
        out_shape=jax.ShapeDtypeStruct((M, D), F32),
        scratch_shapes=[pltpu.VMEM((tm, D), BF16)],
        compiler_params=_params("parallel", "arbitrary"),
    )(x, gain.reshape(1, D), w1, w3, w2)


def _mem_attn_kernel(x_ref, g_ref, wq_ref, kv_ref, qg_ref, kg_ref, wo_ref, o_ref, *, heads):
    x = x_ref[0]
    h = _rms(x, g_ref[...]).astype(BF16)
    q = _dot(h, wq_ref[...])
    kv = kv_ref[0]
    w = heads * HEAD_DIM
    outs = []
    for hh in range(heads):
        sl = slice(hh * HEAD_DIM, (hh + 1) * HEAD_DIM)
        qh = (_rms(q[:, sl], qg_ref[...]) * ATTN_SCALE).astype(BF16)
        kh = _rms(kv[:, sl], kg_ref[...]).astype(BF16)
        vh = kv[:, w + hh * HEAD_DIM: w + (hh + 1) * HEAD_DIM].astype(BF16)
        s = _dot_nt(qh, kh)
        p = jnp.exp(s - jnp.max(s, axis=-1, keepdims=True))
        z = jnp.sum(p, axis=-1, keepdims=True)
        outs.append((_dot(p.astype(BF16), vh) / z).astype(BF16))
    o = jnp.concatenate(outs, axis=-1)
    o_ref[0] = x + _dot(o, wo_ref[...])


def mem_attn(x, gain, wq, kv, q_gain, k_gain, wo, tm):
    B, T, D = x.shape
    L, W2 = kv.shape[1], kv.shape[2]
    W = W2 // 2
    tm = min(tm, T)
    assert T % tm == 0
    return pl.pallas_call(
        functools.partial(_mem_attn_kernel, heads=W // HEAD_DIM),
        name="mem_attn",
        grid=(B, T // tm),
        in_specs=[pl.BlockSpec((1, tm, D), lambda b, i: (b, i, 0)),
                  pl.BlockSpec((1, D), lambda b, i: (0, 0)),
                  pl.BlockSpec((D, W), lambda b, i: (0, 0)),
                  pl.BlockSpec((1, L, W2), lambda b, i: (b, 0, 0)),
                  pl.BlockSpec((1, HEAD_DIM), lambda b, i: (0, 0)),
                  pl.BlockSpec((1, HEAD_DIM), lambda b, i: (0, 0)),
                  pl.BlockSpec((W, D), lambda b, i: (0, 0))],
        out_specs=pl.BlockSpec((1, tm, D), lambda b, i: (b, i, 0)),
        out_shape=jax.ShapeDtypeStruct((B, T, D), F32),
        compiler_params=_params("parallel", "parallel"),
    )(x, gain.reshape(1, D), wq, kv, q_gain.reshape(1, HEAD_DIM), k_gain.reshape(1, HEAD_DIM), wo)


def _fox_gate_kernel(g_ref, b_ref, c_ref):
    T = g_ref.shape[1]
    r = lax.broadcasted_iota(jnp.int32, (LANES, LANES), 0)
    c = lax.broadcasted_iota(jnp.int32, (LANES, LANES), 1)
    tri = (r >= c).astype(BF16)

    def body(i, carry):
        rows = pl.ds(pl.multiple_of(i * LANES, LANES), LANES)
        cs = _exact_dot(tri, _log_sigmoid(g_ref[0, rows, :] + b_ref[...])) + carry
        c_ref[0, rows, :] = cs
        return cs[LANES - 1:LANES, :]

    lax.fori_loop(0, T // LANES, body, jnp.zeros((1, LANES), F32))


def fox_gate(g, bias):
    B, T, _ = g.shape
    return pl.pallas_call(
        _fox_gate_kernel,
        name="fox_gate",
        grid=(B,),
        in_specs=[pl.BlockSpec((1, T, LANES), lambda b: (b, 0, 0)),
                  pl.BlockSpec((1, LANES), lambda b: (0, 0))],
        out_specs=pl.BlockSpec((1, T, LANES), lambda b: (b, 0, 0)),
        out_shape=jax.ShapeDtypeStruct((B, T, LANES), F32),
        compiler_params=_params("parallel"),
    )(g, bias)


def _row_to_col(row):
    n = row.shape[1]
    r = lax.broadcasted_iota(jnp.int32, (n, n), 0)
    c = lax.broadcasted_iota(jnp.int32, (n, n), 1)
    return jnp.sum(jnp.where(r == c, jnp.broadcast_to(row, (n, n)), 0.0), axis=-1, keepdims=True)


def _fox_attn_kernel(q_ref, k_ref, v_ref, c_ref, qg_ref, kg_ref, o_ref, kn_ref, vb_ref, *, tq):
    qi = pl.program_id(2)

    @pl.when(qi == 0)
    def _():
        kn_ref[...] = _rms(k_ref[0], kg_ref[...]).astype(BF16)
        vb_ref[...] = v_ref[0].astype(BF16)

    q0 = pl.multiple_of(qi * tq, tq)
    q = (_rms(q_ref[0], qg_ref[...]) * ATTN_SCALE).astype(BF16)
    c_col = _row_to_col(c_ref[0, :, pl.ds(q0, tq)])
    tpos = q0 + lax.broadcasted_iota(jnp.int32, (tq, 1), 0)
    koff = lax.broadcasted_iota(jnp.int32, (1, tq), 1)

    def body(j, carry):
        m, l, acc = carry
        k0 = pl.multiple_of(j * tq, tq)
        s = _dot_nt(q, kn_ref[pl.ds(k0, tq), :])
        s = s + (c_col - c_ref[0, :, pl.ds(k0, tq)])
        s = jnp.where(tpos >= k0 + koff, s, NEG_INF)
        p, alpha, m, l = _softmax_step(s, m, l)
        acc = alpha * acc + _dot(p.astype(BF16), vb_ref[pl.ds(k0, tq), :])
        return m, l, acc

    init = (jnp.full((tq, 1), NEG_INF, F32), jnp.zeros((tq, 1), F32), jnp.zeros((tq, HEAD_DIM), F32))
    _, l, acc = lax.fori_loop(0, qi + 1, body, init)
    o_ref[0] = (acc / l).astype(o_ref.dtype)


def fox_attn(proj, c_rows, q_gain, k_gain, tq):
    B, T, _ = proj.shape
    H = FOX_HEADS
    tq = min(tq, T)
    assert T % tq == 0
    return pl.pallas_call(
        functools.partial(_fox_attn_kernel, tq=tq),
        name="fox_attn",
        grid=(B, H, T // tq),
        in_specs=[pl.BlockSpec((1, tq, HEAD_DIM), lambda b, h, i: (b, i, h)),
                  pl.BlockSpec((1, T, HEAD_DIM), lambda b, h, i: (b, 0, H + h)),
                  pl.BlockSpec((1, T, HEAD_DIM), lambda b, h, i: (b, 0, 2 * H + h)),
                  pl.BlockSpec((1, 1, T), lambda b, h, i: (b * H + h, 0, 0)),
                  pl.BlockSpec((1, HEAD_DIM), lambda b, h, i: (0, 0)),
                  pl.BlockSpec((1, HEAD_DIM), lambda b, h, i: (0, 0))],
        out_specs=pl.BlockSpec((1, tq, HEAD_DIM), lambda b, h, i: (b, i, h)),
        out_shape=jax.ShapeDtypeStruct((B, T, H * HEAD_DIM), BF16),
        scratch_shapes=[pltpu.VMEM((T, HEAD_DIM), BF16), pltpu.VMEM((T, HEAD_DIM), BF16)],
        compiler_params=_params("parallel", "parallel", "arbitrary"),
    )(proj, proj, proj, c_rows, q_gain.reshape(1, HEAD_DIM), k_gain.reshape(1, HEAD_DIM))


def _hgrn_kernel(q_ref, f_ref, i_ref, g_ref, la_ref, l1_ref, oml_ref, og_ref, o_ref, *, bs):
    T = q_ref.shape[1]
    R = LANES
    ri = lax.broadcasted_iota(jnp.int32, (R, R), 0)
    ci = lax.broadcasted_iota(jnp.int32, (R, R), 1)
    same = (ri // bs) == (ci // bs)
    cum_sel = (same & (ri >= ci)).astype(BF16)
    tot_sel = same.astype(BF16)
    s_idx = lax.broadcasted_iota(jnp.int32, (bs, HEAD_DIM), 0)

    def step(sb, st):
        rows = pl.ds(pl.multiple_of(sb * R, R), R)
        z = f_ref[0, rows, :]
        a = la_ref[0]
        y = l1_ref[0] + _log_sigmoid(z)
        log_f = jnp.maximum(a, y) + jnp.log1p(jnp.exp(-jnp.abs(a - y)))
        b = _exact_dot(cum_sel, log_f)
        bt = _exact_dot(tot_sel, log_f)
        kk = oml_ref[0] * jax.nn.sigmoid(-z)
        hq = q_ref[0, rows, :]
        qq = hq * jax.nn.sigmoid(hq)
        vv = i_ref[0, rows, :]
        qe = (qq * jnp.exp(b)).astype(BF16)
        ke = (kk * jnp.exp(bt - b)).astype(BF16)
        dec = jnp.exp(bt)
        vb = vv.astype(BF16)
        outs = []
        for u in range(R // bs):
            sl = slice(u * bs, (u + 1) * bs)
            o_inter = _dot_nt(qe[sl], st.astype(BF16))
            bu, vu = b[sl], vv[sl]
            qk = qq[sl]
            ku = kk[sl]
            intra = []
            for t in range(bs):
                w = jnp.exp(jnp.where(s_idx <= t, bu[t:t + 1, :] - bu, NEG_INF))
                a_t = jnp.sum(w * (qk[t:t + 1, :] * ku), axis=-1, keepdims=True)
                intra.append(jnp.sum(a_t * vu, axis=0, keepdims=True))
            outs.append(o_inter + jnp.concatenate(intra, axis=0))
            st = dec[u * bs:u * bs + 1, :] * st + _dot_tn(vb[sl], ke[sl])
        o = jnp.concatenate(outs, axis=0)
        hg = g_ref[0, rows, :]
        o_ref[0, rows, :] = (_rms(o, og_ref[...]) * (hg * jax.nn.sigmoid(hg))).astype(o_ref.dtype)
        return st

    lax.fori_loop(0, T // R, step, jnp.zeros((HEAD_DIM, HEAD_DIM), F32))


def hgrn(proj, col0, log_lb, log1m_lb, one_m_lb, o_gain):
    B, T, _ = proj.shape
    H = HGRN_HEADS
    row = lambda off: pl.BlockSpec((1, T, HEAD_DIM), lambda b, h: (b, 0, col0 + off * H + h))
    par = pl.BlockSpec((1, 1, HEAD_DIM), lambda b, h: (h, 0, 0))
    log_lb, log1m_lb, one_m_lb = (p.reshape(H, 1, HEAD_DIM) for p in (log_lb, log1m_lb, one_m_lb))
    return pl.pallas_call(
        functools.partial(_hgrn_kernel, bs=HGRN_BLOCK),
        name="hgrn",
        grid=(B, H),
        in_specs=[row(0), row(1), row(2), row(3), par, par, par,
                  pl.BlockSpec((1, HEAD_DIM), lambda b, h: (0, 0))],
        out_specs=pl.BlockSpec((1, T, HEAD_DIM), lambda b, h: (b, 0, h)),
        out_shape=jax.ShapeDtypeStruct((B, T, H * HEAD_DIM), BF16),
        compiler_params=_params("parallel", "parallel"),
    )(proj, proj, proj, proj, log_lb, log1m_lb, one_m_lb, o_gain.reshape(1, HEAD_DIM))


def _compress_kernel(t_ref, pe_ref, w1_ref, w2_ref, g_ref, o_ref, *, normalize, n_cmp):
    half = t_ref.shape[3]
    t = t_ref[0, 0]
    N = t.shape[0]
    top = _dot((t + pe_ref[:, :half]).astype(BF16), w1_ref[:half, :])
    bot = _dot((t + pe_ref[:, half:]).astype(BF16), w1_ref[half:, :])
    hid = top + pltpu.roll(bot, N - 1, 0)
    out = _dot(jax.nn.gelu(hid).astype(BF16), w2_ref[...])
    if normalize:
        out = _rms(out, g_ref[...])
    valid = lax.broadcasted_iota(jnp.int32, out.shape, 0) < n_cmp
    o_ref[0, 0] = jnp.where(valid, out, 0.0)


def compress(t, pe, w1, w2, gain, normalize):
    B, G, N, half = t.shape
    Hc = w1.shape[1]
    return pl.pallas_call(
        functools.partial(_compress_kernel, normalize=normalize, n_cmp=N - 1),
        name="nsa_compress",
        grid=(B, G),
        in_specs=[pl.BlockSpec((1, 1, N, half), lambda b, g: (b, g, 0, 0)),
                  pl.BlockSpec((1, 2 * half), lambda b, g: (0, 0)),
                  pl.BlockSpec((2 * half, Hc), lambda b, g: (0, 0)),
                  pl.BlockSpec((Hc, HEAD_DIM), lambda b, g: (0, 0)),
                  pl.BlockSpec((1, HEAD_DIM), lambda b, g: (0, 0))],
        out_specs=pl.BlockSpec((1, 1, N, HEAD_DIM), lambda b, g: (b, g, 0, 0)),
        out_shape=jax.ShapeDtypeStruct((B, G, N, HEAD_DIM), F32),
        compiler_params=_params("parallel", "parallel"),
    )(t, pe, w1, w2, gain.reshape(1, HEAD_DIM))


def _nsa_kernel(q_ref, ks_ref, vs_ref, kw_ref, vw_ref, kc_ref, vc_ref, gt_ref, qg_ref, kg_ref,
                o_ref, ksn_ref, vsb_ref, kwn_ref, vwb_ref, bias_ref, *, tq, tk, n_sel):
    g = pl.program_id(1)
    qi = pl.program_id(2)
    T = ks_ref.shape[1]
    J = NSA_GROUP
    n_slc = T // SLC_BLOCK
    nc = kc_ref.shape[2]
    n_cmp = nc - 1

    @pl.when(qi == 0)
    def _():
        ksn_ref[...] = _rms(ks_ref[0], kg_ref[1:2, :]).astype(BF16)
        kwn_ref[...] = _rms(kw_ref[0], kg_ref[2:3, :]).astype(BF16)
        vsb_ref[...] = vs_ref[0].astype(BF16)
        vwb_ref[...] = vw_ref[0].astype(BF16)

    q0 = pl.multiple_of(qi * tq, tq)
    qraw = q_ref[0]
    q = jnp.concatenate(
        [(_rms(qraw[:, j * HEAD_DIM:(j + 1) * HEAD_DIM], qg_ref[...]) * ATTN_SCALE).astype(BF16)
         for j in range(J)], axis=0)
    rows = J * tq
    rid = lax.broadcasted_iota(jnp.int32, (rows, 1), 0)
    head = g * J + rid // tq
    slope = jnp.exp2(-0.5 * (head + 1).astype(F32))
    tpos = q0 + rid % tq
    tq_pos = q0 + lax.broadcasted_iota(jnp.int32, (tq, 1), 0)

    n_id = lax.broadcasted_iota(jnp.int32, (1, nc), 1)
    cmp_end = n_id * CMP_STRIDE + (CMP_BLOCK - 1)
    cmp_mid = (n_id * CMP_STRIDE).astype(F32) + 0.5 * (CMP_BLOCK - 1)
    s = _dot_nt(q, kc_ref[0, 0].astype(BF16))
    s = s - slope * (tpos.astype(F32) - cmp_mid)
    s = jnp.where((cmp_end <= tpos) & (n_id < n_cmp), s, NEG_INF)
    m = jnp.max(s, axis=-1, keepdims=True)
    m = jnp.where(m == NEG_INF, 0.0, m)
    p = jnp.exp(s - m)
    z = jnp.sum(p, axis=-1, keepdims=True)
    p = p / jnp.where(z > 0, z, 1.0)
    o_cmp = _dot(p.astype(BF16), vc_ref[0, 0].astype(BF16))

    p_grp = p[0:tq]
    for j in range(1, J):
        p_grp = p_grp + p[j * tq:(j + 1) * tq]
    n_row = lax.broadcasted_iota(jnp.int32, (nc, LANES), 0)
    m_col = lax.broadcasted_iota(jnp.int32, (nc, LANES), 1)
    overlap = ((n_row * CMP_STRIDE <= m_col * SLC_BLOCK + (SLC_BLOCK - 1))
               & (n_row * CMP_STRIDE + (CMP_BLOCK - 1) >= m_col * SLC_BLOCK)
               & (n_row < n_cmp) & (m_col < n_slc)).astype(BF16)
    imp = _exact_dot_rhs(p_grp, overlap)
    blk = lax.broadcasted_iota(jnp.int32, (1, LANES), 1)
    imp = jnp.where(blk * SLC_BLOCK > tq_pos, NEG_INF, imp)
    imp = jnp.where((blk == tq_pos // SLC_BLOCK) | (blk == 0), jnp.inf, imp)
    blk_f = blk.astype(F32)
    taken = jnp.broadcast_to(blk >= n_slc, (tq, LANES))
    for _ in range(n_sel):
        vals = jnp.where(taken, NEG_INF, imp)
        best = jnp.max(vals, axis=-1, keepdims=True)
        cand = (vals == best) & jnp.logical_not(taken)
        first = jnp.min(jnp.where(cand, blk_f, float(LANES)), axis=-1, keepdims=True)
        taken = taken | (blk_f == first)
    sel = (taken & (blk < n_slc)).astype(BF16)

    n_kt = (q0 + tq + tk - 1) // tk

    def fill(jt, _):
        k0 = pl.multiple_of(jt * tk, tk)
        kpos = k0 + lax.broadcasted_iota(jnp.int32, (1, tk), 1)
        e_row = lax.broadcasted_iota(jnp.int32, (LANES, tk), 0)
        e_col = k0 + lax.broadcasted_iota(jnp.int32, (LANES, tk), 1)
        expand = (e_row == e_col // SLC_BLOCK).astype(BF16)
        hit = _dot(sel, expand)
        bias_ref[:, pl.ds(k0, tk)] = jnp.where((hit > 0.5) & (kpos <= tq_pos), 0.0, NEG_INF)
        return 0

    lax.fori_loop(0, n_kt, fill, 0)

    def attend(lo, hi, width, k_ref, v_ref, bias_fn):
        def body(jt, carry):
            m, l, acc = carry
            k0 = pl.multiple_of(jt * width, width)
            kpos = k0 + lax.broadcasted_iota(jnp.int32, (1, width), 1)
            s = _dot_nt(q, k_ref[pl.ds(k0, width), :])
            s = s - slope * (tpos - kpos).astype(F32)
            s = bias_fn(s, k0, kpos)
            p, alpha, m, l = _softmax_step(s, m, l)
            acc = alpha * acc + _dot(p.astype(BF16), v_ref[pl.ds(k0, width), :])
            return m, l, acc

        init = (jnp.full((rows, 1), NEG_INF, F32), jnp.zeros((rows, 1), F32),
                jnp.zeros((rows, HEAD_DIM), F32))
        _, l, acc = lax.fori_loop(lo, hi, body, init)
        return acc / jnp.where(l > 0, l, 1.0)

    def slc_bias(s, k0, kpos):
        b = bias_ref[:, pl.ds(k0, tk)]
        return s + jnp.concatenate([b] * J, axis=0)

    o_slc = attend(0, n_kt, tk, ksn_ref, vsb_ref, slc_bias)

    def win_bias(s, k0, kpos):
        dist = tpos - kpos
        return jnp.where((dist >= 0) & (dist < WINDOW), s, NEG_INF)

    o_win = attend(jnp.maximum(qi - WINDOW // tq, 0), qi + 1, tq, kwn_ref, vwb_ref, win_bias)

    gates = jax.nn.sigmoid(gt_ref[0])
    for j in range(J):
        rs = slice(j * tq, (j + 1) * tq)
        o = (gates[:, 3 * j:3 * j + 1] * o_cmp[rs] + gates[:, 3 * j + 1:3 * j + 2] * o_slc[rs]
             + gates[:, 3 * j + 2:3 * j + 3] * o_win[rs])
        o_ref[0, :, j * HEAD_DIM:(j + 1) * HEAD_DIM] = o.astype(o_ref.dtype)


def _exact_dot_rhs(x, sel):
    return _exact_dot_generic(x, sel)


def _exact_dot_generic(x, sel):
    x1 = x.astype(BF16)
    r1 = x - x1.astype(F32)
    x2 = r1.astype(BF16)
    x3 = (r1 - x2.astype(F32)).astype(BF16)
    return _dot(x1, sel) + _dot(x2, sel) + _dot(x3, sel)


def nsa_attn(proj, gates, k_cmp, v_cmp, q_gain, k_gain, tk):
    B, T, _ = proj.shape
    G, J = NSA_KV_HEADS, NSA_GROUP
    tq = LANES
    tk = min(tk, T)
    nc = k_cmp.shape[2]
    qcols = NSA_HEADS
    kv = lambda slab: pl.BlockSpec((1, T, HEAD_DIM), lambda b, g, i: (b, 0, qcols + slab * G + g))
    cmp_spec = pl.BlockSpec((1, 1, nc, HEAD_DIM), lambda b, g, i: (b, g, 0, 0))
    return pl.pallas_call(
        functools.partial(_nsa_kernel, tq=tq, tk=tk, n_sel=min(SLC_TOPK, T // SLC_BLOCK)),
        name="nsa_attn",
        grid=(B, G, T // tq),
        in_specs=[pl.BlockSpec((1, tq, J * HEAD_DIM), lambda b, g, i: (b, i, g)),
                  kv(2), kv(3), kv(4), kv(5), cmp_spec, cmp_spec,
                  pl.BlockSpec((1, tq, LANES), lambda b, g, i: (b, i, g)),
                  pl.BlockSpec((1, HEAD_DIM), lambda b, g, i: (0, 0)),
                  pl.BlockSpec((3, HEAD_DIM), lambda b, g, i: (0, 0))],
        out_specs=pl.BlockSpec((1, tq, J * HEAD_DIM), lambda b, g, i: (b, i, g)),
        out_shape=jax.ShapeDtypeStruct((B, T, NSA_HEADS * HEAD_DIM), BF16),
        scratch_shapes=[pltpu.VMEM((T, HEAD_DIM), BF16)] * 4 + [pltpu.VMEM((tq, T), F32)],
        compiler_params=_params("parallel", "parallel", "arbitrary"),
    )(proj, proj, proj, proj, proj, k_cmp, v_cmp, gates, q_gain.reshape(1, HEAD_DIM), k_gain)


def _pad_cols(w, n):
    return jnp.pad(w, ((0, 0), (0, n - w.shape[1])))


def even_mixer(x, gain, w_in, w_out, f_bias, q_gain, k_gain, lb, o_gain):
    B, T, D = x.shape
    xf = x.reshape(B * T, D)
    fw = FOX_HEADS * HEAD_DIM
    w_main = jnp.concatenate([w_in[:, :3 * fw], w_in[:, 3 * fw + FOX_HEADS:]], axis=1).astype(BF16)
    w_gate = _pad_cols(w_in[:, 3 * fw:3 * fw + FOX_HEADS], LANES).astype(BF16)
    proj = rms_matmul(xf, gain, w_main, 512, 1024).reshape(B, T, -1)
    gate = rms_matmul(xf, gain, w_gate, 512, LANES).reshape(B, T, LANES)
    c = fox_gate(gate, _pad_cols(f_bias.reshape(1, FOX_HEADS).astype(F32), LANES))
    c_rows = c[:, :, :FOX_HEADS].transpose(0, 2, 1).reshape(B * FOX_HEADS, 1, T)
    o_fox = fox_attn(proj, c_rows, q_gain, k_gain, 256)
    lb = lb.astype(F32).reshape(HGRN_HEADS, HEAD_DIM)
    o_h = hgrn(proj, 3 * FOX_HEADS, jnp.log(lb), jnp.log1p(-lb), 1.0 - lb, o_gain)
    w_out = w_out.astype(BF16)
    out = out_proj([o_fox.reshape(B * T, -1), o_h.reshape(B * T, -1)],
                   [w_out[:fw], w_out[fw:]], xf, 256)
    return out.reshape(B, T, D)


def nsa_mixer(x, gain, w_in, w_out, q_gain, k_gain, pe_k, w1_k, w2_k, pe_v, w1_v, w2_v):
    B, T, D = x.shape
    G, J = NSA_KV_HEADS, NSA_GROUP
    xf = x.reshape(B * T, D)
    qw = NSA_HEADS * HEAD_DIM
    kvw = G * HEAD_DIM
    n_main = qw + 6 * kvw
    w_main = w_in[:, :n_main].astype(BF16)
    w_gate = w_in[:, n_main:].reshape(D, G, 3 * J)
    w_gate = jnp.pad(w_gate, ((0, 0), (0, 0), (0, LANES - 3 * J))).reshape(D, G * LANES).astype(BF16)
    proj = rms_matmul(xf, gain, w_main, 512, 1024).reshape(B, T, n_main)
    gates = rms_matmul(xf, gain, w_gate, 512, G * LANES).reshape(B, T, G * LANES)

    N = T // CMP_STRIDE

    def blocks(slab):
        t = proj[:, :, qw + slab * kvw: qw + (slab + 1) * kvw]
        return t.reshape(B, N, CMP_STRIDE, G, HEAD_DIM).transpose(0, 3, 1, 2, 4).reshape(
            B, G, N, CMP_STRIDE * HEAD_DIM)

    flat = lambda w1: w1.reshape(CMP_BLOCK * HEAD_DIM, -1).astype(BF16)
    k_cmp = compress(blocks(0), pe_k.reshape(1, -1), flat(w1_k), w2_k.astype(BF16), k_gain[0], True)
    v_cmp = compress(blocks(1), pe_v.reshape(1, -1), flat(w1_v), w2_v.astype(BF16), k_gain[0], False)
    o = nsa_attn(proj, gates, k_cmp, v_cmp, q_gain, k_gain, 512)
    out = out_proj([o.reshape(B * T, qw)], [w_out.astype(BF16)], xf, 256)
    return out.reshape(B, T, D)


def kernel(x, mem, norm_mix, norm_mem, norm_ffn, mem_in_gain, even_w_in, even_w_out, fox_f_bias,
           fox_q_gain, fox_k_gain, hgrn_lb_logits, hgrn_o_gain, odd_w_in, odd_w_out, nsa_q_gain,
           nsa_k_gain, cmp_pe_k, cmp_w1_k, cmp_w2_k, cmp_pe_v, cmp_w1_v, cmp_w2_v, mem_wq, mem_wkv,
           mem_wo, mem_q_gain, mem_k_gain, ffn_w1, ffn_w3, ffn_w2):
    B, T, D = x.shape
    depth = norm_mix.shape[0]
    L = mem.shape[1]
    lb_cum = jnp.cumsum(jax.nn.softmax(hgrn_lb_logits.astype(F32), axis=0), axis=0)
    hgrn_lb = lb_cum - lb_cum[0:1]
    memf = mem.reshape(B * L, D)
    for layer in range(depth):
        if layer % 2 == 0:
            e = layer // 2
            x = even_mixer(x, norm_mix[layer], even_w_in[e], even_w_out[e], fox_f_bias[e],
                           fox_q_gain[e], fox_k_gain[e], hgrn_lb[e], hgrn_o_gain[e])
        else:
            o = layer // 2
            x = nsa_mixer(x, norm_mix[layer], odd_w_in[o], odd_w_out[o], nsa_q_gain[o], nsa_k_gain[o],
                          cmp_pe_k[o], cmp_w1_k[o], cmp_w2_k[o], cmp_pe_v[o], cmp_w1_v[o], cmp_w2_v[o])
        kv = rms_matmul(memf, mem_in_gain[layer], mem_wkv[layer].astype(BF16), 256, 1024)
        x = mem_attn(x, norm_mem[layer], mem_wq[layer].astype(BF16), kv.reshape(B, L, -1),
                     mem_q_gain[layer], mem_k_gain[layer], mem_wo[layer].astype(BF16), 256)
        x = ffn(x.reshape(B * T, D), norm_ffn[layer], ffn_w1[layer].astype(BF16),
                ffn_w3[layer].astype(BF16), ffn_w2[layer].astype(BF16), 512, 512).reshape(B, T, D)
    return x
```

```python
import functools

import jax
import jax.numpy as jnp
from jax import lax
from jax.experimental import pallas as pl
from jax.experimental.pallas import tpu as pltpu

F32 = jnp.float32
BF16 = jnp.bfloat16
NEG_INF = float("-inf")

EPS = 1e-6
HEAD_DIM = 128
ATTN_SCALE = HEAD_DIM ** -0.5
LANES = 128
FOX_HEADS = 8
HGRN_HEADS = 8
HGRN_BLOCK = 16
NSA_HEADS = 16
NSA_KV_HEADS = 4
NSA_GROUP = NSA_HEADS // NSA_KV_HEADS
CMP_BLOCK = 32
CMP_STRIDE = 16
SLC_BLOCK = 64
SLC_TOPK = 16
WINDOW = 512
MEM_HEADS = 4
V7X_VMEM_LIMIT = 56 * 1024 * 1024
LOG2E = 1.4426950408889634
MASK_BIG = 1e30
AUG = 2 * LANES
VT_ROWS = HEAD_DIM + 16


def _params(*sem):
    return pltpu.CompilerParams(dimension_semantics=sem, vmem_limit_bytes=V7X_VMEM_LIMIT)


def _rms(x, gain):
    return x * lax.rsqrt(jnp.mean(x * x, axis=-1, keepdims=True) + EPS) * gain


def _dot(a, b):
    return jnp.dot(a, b, preferred_element_type=F32)


def _dot_nt(a, b):
    return lax.dot_general(a, b, (((1,), (1,)), ((), ())), preferred_element_type=F32)


def _dot_tn(a, b):
    return lax.dot_general(a, b, (((0,), (0,)), ((), ())), preferred_element_type=F32)


def _split3(x):
    x1 = x.astype(BF16)
    r1 = x - x1.astype(F32)
    x2 = r1.astype(BF16)
    return x1, x2, (r1 - x2.astype(F32)).astype(BF16)


def _exact_dot(sel, x):
    x1, x2, x3 = _split3(x)
    return _dot(sel, x1) + _dot(sel, x2) + _dot(sel, x3)


def _log_sigmoid(z):
    return jnp.minimum(z, 0.0) - jnp.log1p(jnp.exp(-jnp.abs(z)))


def _rms_matmul_kernel(x_ref, g_ref, w_ref, o_ref, hn_ref):
    @pl.when(pl.program_id(1) == 0)
    def _():
        hn_ref[...] = _rms(x_ref[...], g_ref[...]).astype(BF16)

    o_ref[...] = _dot(hn_ref[...], w_ref[...])


def rms_matmul(x, gain, w, tm, tn):
    M, K = x.shape
    N = w.shape[1]
    tm, tn = min(tm, M), min(tn, N)
    assert M % tm == 0 and N % tn == 0
    return pl.pallas_call(
        _rms_matmul_kernel,
        name="rms_matmul",
        grid=(M // tm, N // tn),
        in_specs=[pl.BlockSpec((tm, K), lambda i, j: (i, 0)),
                  pl.BlockSpec((1, K), lambda i, j: (0, 0)),
                  pl.BlockSpec((K, tn), lambda i, j: (0, j))],
        out_specs=pl.BlockSpec((tm, tn), lambda i, j: (i, j)),
        out_shape=jax.ShapeDtypeStruct((M, N), F32),
        scratch_shapes=[pltpu.VMEM((tm, K), BF16)],
        compiler_params=_params("parallel", "arbitrary"),
    )(x, gain.reshape(1, K), w)


def _out_proj_kernel(*refs, n_in):
    a_refs, w_refs = refs[:n_in], refs[n_in:2 * n_in]
    res_ref, o_ref = refs[2 * n_in], refs[2 * n_in + 1]
    acc = res_ref[...]
    for a_ref, w_ref in zip(a_refs, w_refs):
        acc = acc + _dot(a_ref[...], w_ref[...])
    o_ref[...] = acc


def out_proj(acts, weights, res, tm):
    M, N = res.shape
    tm = min(tm, M)
    assert M % tm == 0
    n_in = len(acts)
    in_specs = ([pl.BlockSpec((tm, a.shape[1]), lambda i: (i, 0)) for a in acts]
                + [pl.BlockSpec(w.shape, lambda i: (0, 0)) for w in weights]
                + [pl.BlockSpec((tm, N), lambda i: (i, 0))])
    return pl.pallas_call(
        functools.partial(_out_proj_kernel, n_in=n_in),
        name="out_proj",
        grid=(M // tm,),
        in_specs=in_specs,
        out_specs=pl.BlockSpec((tm, N), lambda i: (i, 0)),
        out_shape=jax.ShapeDtypeStruct((M, N), F32),
        compiler_params=_params("parallel"),
    )(*acts, *weights, res)


def _ffn_kernel(x_ref, g_ref, w1_ref, w3_ref, w2_ref, o_ref, hn_ref):
    f = pl.program_id(1)

    @pl.when(f == 0)
    def _():
        x = x_ref[...]
        hn_ref[...] = _rms(x, g_ref[...]).astype(BF16)
        o_ref[...] = x

    h = hn_ref[...]
    a = _dot(h, w1_ref[...])
    b = _dot(h, w3_ref[...])
    u = (a * jax.nn.sigmoid(a) * b).astype(BF16)
    o_ref[...] += _dot(u, w2_ref[...])


def ffn(x, gain, w1, w3, w2, tm, tf):
    M, D = x.shape
    Fh = w1.shape[1]
    tm, tf = min(tm, M), min(tf, Fh)
    assert M % tm == 0 and Fh % tf == 0
    return pl.pallas_call(
        _ffn_kernel,
        name="ffn",
        grid=(M // tm, Fh // tf),
        in_specs=[pl.BlockSpec((tm, D), lambda i, f: (i, 0)),
                  pl.BlockSpec((1, D), lambda i, f: (0, 0)),
                  pl.BlockSpec((D, tf), lambda i, f: (0, f)),
                  pl.BlockSpec((D, tf), lambda i, f: (0, f)),
                  pl.BlockSpec((tf, D), lambda i, f: (f, 0))],
        out_specs=pl.BlockSpec((tm, D), lambda i, f: (i, 0)),
        out_shape=jax.ShapeDtypeStruct((M, D), F32),
        scratch_shapes=[pltpu.VMEM((tm, D), BF16)],
        compiler_params=_params("parallel", "arbitrary"),
    )(x, gain.reshape(1, D), w1, w3, w2)


def _mem_attn_kernel(x_ref, g_ref, wq_ref, kv_ref, qg_ref, kg_ref, wo_ref, o_ref, *, heads):
    x = x_ref[0]
    h = _rms(x, g_ref[...]).astype(BF16)
    q = _dot(h, wq_ref[...])
    kv = kv_ref[0]
    w = heads * HEAD_DIM
    outs = []
    for hh in range(heads):
        sl = slice(hh * HEAD_DIM, (hh + 1) * HEAD_DIM)
        qh = (_rms(q[:, sl], qg_ref[...]) * ATTN_SCALE).astype(BF16)
        kh = _rms(kv[:, sl], kg_ref[...]).astype(BF16)
        vh = kv[:, w + hh * HEAD_DIM: w + (hh + 1) * HEAD_DIM].astype(BF16)
        s = _dot_nt(qh, kh)
        p = jnp.exp(s - jnp.max(s, axis=-1, keepdims=True))
        z = jnp.sum(p, axis=-1, keepdims=True)
        outs.append((_dot(p.astype(BF16), vh) / z).astype(BF16))
    o = jnp.concatenate(outs, axis=-1)
    o_ref[0] = x + _dot(o, wo_ref[...])


def mem_attn(x, gain, wq, kv, q_gain, k_gain, wo, tm):
    B, T, D = x.shape
    L, W2 = kv.shape[1], kv.shape[2]
    W = W2 // 2
    tm = min(tm, T)
    assert T % tm == 0
    return pl.pallas_call(
        functools.partial(_mem_attn_kernel, heads=W // HEAD_DIM),
        name="mem_attn",
        grid=(B, T // tm),
        in_specs=[pl.BlockSpec((1, tm, D), lambda b, i: (b, i, 0)),
                  pl.BlockSpec((1, D), lambda b, i: (0, 0)),
                  pl.BlockSpec((D, W), lambda b, i: (0, 0)),
                  pl.BlockSpec((1, L, W2), lambda b, i: (b, 0, 0)),
                  pl.BlockSpec((1, HEAD_DIM), lambda b, i: (0, 0)),
                  pl.BlockSpec((1, HEAD_DIM), lambda b, i: (0, 0)),
                  pl.BlockSpec((W, D), lambda b, i: (0, 0))],
        out_specs=pl.BlockSpec((1, tm, D), lambda b, i: (b, i, 0)),
        out_shape=jax.ShapeDtypeStruct((B, T, D), F32),
        compiler_params=_params("parallel", "parallel"),
    )(x, gain.reshape(1, D), wq, kv, q_gain.reshape(1, HEAD_DIM), k_gain.reshape(1, HEAD_DIM), wo)


def _eye(n):
    return (lax.broadcasted_iota(jnp.int32, (n, n), 0)
            == lax.broadcasted_iota(jnp.int32, (n, n), 1)).astype(BF16)


def _ones_row(rows, n):
    return (lax.broadcasted_iota(jnp.int32, (rows, n), 0) == 0).astype(BF16)


def _split3_f32(x):
    return tuple(t.astype(F32) for t in _split3(x))


def _attn_tile(s, m, acc, vt):
    m_new = jnp.maximum(m, jnp.max(s, axis=0, keepdims=True))
    p = jnp.exp2(s - m_new).astype(BF16)
    return m_new, jnp.exp2(m - m_new) * acc + _dot(vt, p)


def _fox_gate_kernel(g_ref, b_ref, c_ref, kf_ref, *, heads):
    T = g_ref.shape[1]
    r = lax.broadcasted_iota(jnp.int32, (LANES, LANES), 0)
    lane = lax.broadcasted_iota(jnp.int32, (LANES, LANES), 1)
    tri = (r >= lane).astype(BF16)

    def body(i, carry):
        rows = pl.ds(pl.multiple_of(i * LANES, LANES), LANES)
        cs = _exact_dot(tri, _log_sigmoid(g_ref[0, rows, :] + b_ref[...])) + carry
        c_ref[0, rows, :] = cs
        c2 = cs * LOG2E
        for h in range(heads):
            d1, d2, d3 = _split3_f32(c2[:, h:h + 1])
            feat = jnp.where(lane < 3, 1.0,
                             jnp.where(lane == 3, -d1, jnp.where(lane == 4, -d2,
                                                                 jnp.where(lane == 5, -d3, 0.0))))
            kf_ref[0, h, rows, :] = feat.astype(BF16)
        return cs[LANES - 1:LANES, :]

    lax.fori_loop(0, T // LANES, body, jnp.zeros((1, LANES), F32))


def fox_gate(g, bias, heads):
    B, T, _ = g.shape
    return pl.pallas_call(
        functools.partial(_fox_gate_kernel, heads=heads),
        name="fox_gate",
        grid=(B,),
        in_specs=[pl.BlockSpec((1, T, LANES), lambda b: (b, 0, 0)),
                  pl.BlockSpec((1, LANES), lambda b: (0, 0))],
        out_specs=[pl.BlockSpec((1, T, LANES), lambda b: (b, 0, 0)),
                   pl.BlockSpec((1, heads, T, LANES), lambda b: (b, 0, 0, 0))],
        out_shape=[jax.ShapeDtypeStruct((B, T, LANES), F32),
                   jax.ShapeDtypeStruct((B, heads, T, LANES), BF16)],
        compiler_params=_params("parallel"),
    )(g, bias)


def _fox_attn_kernel(q_ref, k_ref, v_ref, kf_ref, c_ref, qg_ref, kg_ref, o_ref, ka_ref, vt_ref, *, tq):
    qi = pl.program_id(2)
    T = k_ref.shape[1]
    eye = _eye(HEAD_DIM)

    @pl.when(qi == 0)
    def _():
        ka_ref[:, :HEAD_DIM] = _rms(k_ref[0], kg_ref[...]).astype(BF16)
        ka_ref[:, HEAD_DIM:] = kf_ref[0, 0]
        vt_ref[:HEAD_DIM, :] = _dot_nt(eye, v_ref[0].astype(BF16)).astype(BF16)
        vt_ref[HEAD_DIM:, :] = _ones_row(VT_ROWS - HEAD_DIM, T)

    q0 = pl.multiple_of(qi * tq, tq)
    qn = (_rms(q_ref[0], qg_ref[...]) * (ATTN_SCALE * LOG2E)).astype(BF16)
    qt = _dot_nt(eye, qn).astype(BF16)
    c1, c2, c3 = _split3_f32(c_ref[0, :, pl.ds(q0, tq)] * LOG2E)
    r = lax.broadcasted_iota(jnp.int32, (16, tq), 0)
    feat = jnp.where(r == 0, c1, jnp.where(r == 1, c2, jnp.where(r == 2, c3,
                                                                 jnp.where(r < 6, 1.0, 0.0))))
    qa = jnp.concatenate([qt, feat.astype(BF16), jnp.zeros((AUG - HEAD_DIM - 16, tq), BF16)], axis=0)

    def tile(j, carry, diag):
        m, acc = carry
        k0 = pl.multiple_of(j * tq, tq)
        s = _dot(ka_ref[pl.ds(k0, tq), :], qa)
        if diag:
            kpos = lax.broadcasted_iota(jnp.int32, (tq, 1), 0)
            tpos = lax.broadcasted_iota(jnp.int32, (1, tq), 1)
            s = jnp.where(kpos <= tpos, s, NEG_INF)
        return _attn_tile(s, m, acc, vt_ref[:, pl.ds(k0, tq)])

    init = (jnp.full((1, tq), NEG_INF, F32), jnp.zeros((VT_ROWS, tq), F32))
    carry = lax.fori_loop(0, qi, lambda j, c: tile(j, c, False), init)
    _, acc = tile(qi, carry, True)
    ot = (acc[:HEAD_DIM] / acc[HEAD_DIM:HEAD_DIM + 1]).astype(BF16)
    for c in range(tq // LANES):
        cs = slice(c * LANES, (c + 1) * LANES)
        o_ref[0, cs, :] = _dot_nt(eye, ot[:, cs]).astype(o_ref.dtype)


def fox_attn(proj, kfeat, c_rows, q_gain, k_gain, tq):
    B, T, _ = proj.shape
    H = FOX_HEADS
    tq = min(tq, T)
    assert T % tq == 0 and tq % LANES == 0
    return pl.pallas_call(
        functools.partial(_fox_attn_kernel, tq=tq),
        name="fox_attn",
        grid=(B, H, T // tq),
        in_specs=[pl.BlockSpec((1, tq, HEAD_DIM), lambda b, h, i: (b, i, h)),
                  pl.BlockSpec((1, T, HEAD_DIM), lambda b, h, i: (b, 0, H + h)),
                  pl.BlockSpec((1, T, HEAD_DIM), lambda b, h, i: (b, 0, 2 * H + h)),
                  pl.BlockSpec((1, 1, T, LANES), lambda b, h, i: (b, h, 0, 0)),
                  pl.BlockSpec((1, 1, T), lambda b, h, i: (b * H + h, 0, 0)),
                  pl.BlockSpec((1, HEAD_DIM), lambda b, h, i: (0, 0)),
                  pl.BlockSpec((1, HEAD_DIM), lambda b, h, i: (0, 0))],
        out_specs=pl.BlockSpec((1, tq, HEAD_DIM), lambda b, h, i: (b, i, h)),
        out_shape=jax.ShapeDtypeStruct((B, T, H * HEAD_DIM), BF16),
        scratch_shapes=[pltpu.VMEM((T, AUG), BF16), pltpu.VMEM((VT_ROWS, T), BF16)],
        compiler_params=_params("parallel", "parallel", "arbitrary"),
    )(proj, proj, proj, kfeat, c_rows, q_gain.reshape(1, HEAD_DIM), k_gain.reshape(1, HEAD_DIM))


def _hgrn_kernel(q_ref, f_ref, i_ref, g_ref, la_ref, l1_ref, oml_ref, og_ref, o_ref, *, bs):
    T = q_ref.shape[1]
    R = LANES
    ri = lax.broadcasted_iota(jnp.int32, (R, R), 0)
    ci = lax.broadcasted_iota(jnp.int32, (R, R), 1)
    same = (ri // bs) == (ci // bs)
    cum_sel = (same & (ri >= ci)).astype(BF16)
    tot_sel = same.astype(BF16)
    s_idx = lax.broadcasted_iota(jnp.int32, (bs, HEAD_DIM), 0)

    def step(sb, st):
        rows = pl.ds(pl.multiple_of(sb * R, R), R)
        z = f_ref[0, rows, :]
        a = la_ref[0]
        y = l1_ref[0] + _log_sigmoid(z)
        log_f = jnp.maximum(a, y) + jnp.log1p(jnp.exp(-jnp.abs(a - y)))
        b = _exact_dot(cum_sel, log_f)
        bt = _exact_dot(tot_sel, log_f)
        kk = oml_ref[0] * jax.nn.sigmoid(-z)
        hq = q_ref[0, rows, :]
        qq = hq * jax.nn.sigmoid(hq)
        vv = i_ref[0, rows, :]
        qe = (qq * jnp.exp(b)).astype(BF16)
        ke = (kk * jnp.exp(bt - b)).astype(BF16)
        dec = jnp.exp(bt)
        vb = vv.astype(BF16)
        outs = []
        for u in range(R // bs):
            sl = slice(u * bs, (u + 1) * bs)
            o_inter = _dot_nt(qe[sl], st.astype(BF16))
            bu, vu = b[sl], vv[sl]
            qk = qq[sl]
            ku = kk[sl]
            intra = []
            for t in range(bs):
                w = jnp.exp(jnp.where(s_idx <= t, bu[t:t + 1, :] - bu, NEG_INF))
                a_t = jnp.sum(w * (qk[t:t + 1, :] * ku), axis=-1, keepdims=True)
                intra.append(jnp.sum(a_t * vu, axis=0, keepdims=True))
            outs.append(o_inter + jnp.concatenate(intra, axis=0))
            st = dec[u * bs:u * bs + 1, :] * st + _dot_tn(vb[sl], ke[sl])
        o = jnp.concatenate(outs, axis=0)
        hg = g_ref[0, rows, :]
        o_ref[0, rows, :] = (_rms(o, og_ref[...]) * (hg * jax.nn.sigmoid(hg))).astype(o_ref.dtype)
        return st

    lax.fori_loop(0, T // R, step, jnp.zeros((HEAD_DIM, HEAD_DIM), F32))


def hgrn(proj, col0, log_lb, log1m_lb, one_m_lb, o_gain):
    B, T, _ = proj.shape
    H = HGRN_HEADS
    row = lambda off: pl.BlockSpec((1, T, HEAD_DIM), lambda b, h: (b, 0, col0 + off * H + h))
    par = pl.BlockSpec((1, 1, HEAD_DIM), lambda b, h: (h, 0, 0))
    log_lb, log1m_lb, one_m_lb = (p.reshape(H, 1, HEAD_DIM) for p in (log_lb, log1m_lb, one_m_lb))
    return pl.pallas_call(
        functools.partial(_hgrn_kernel, bs=HGRN_BLOCK),
        name="hgrn",
        grid=(B, H),
        in_specs=[row(0), row(1), row(2), row(3), par, par, par,
                  pl.BlockSpec((1, HEAD_DIM), lambda b, h: (0, 0))],
        out_specs=pl.BlockSpec((1, T, HEAD_DIM), lambda b, h: (b, 0, h)),
        out_shape=jax.ShapeDtypeStruct((B, T, H * HEAD_DIM), BF16),
        compiler_params=_params("parallel", "parallel"),
    )(proj, proj, proj, proj, log_lb, log1m_lb, one_m_lb, o_gain.reshape(1, HEAD_DIM))


def _compress_kernel(t_ref, pe_ref, w1_ref, w2_ref, g_ref, o_ref, *, normalize, n_cmp):
    half = t_ref.shape[3]
    t = t_ref[0, 0]
    N = t.shape[0]
    top = _dot((t + pe_ref[:, :half]).astype(BF16), w1_ref[:half, :])
    bot = _dot((t + pe_ref[:, half:]).astype(BF16), w1_ref[half:, :])
    hid = top + pltpu.roll(bot, N - 1, 0)
    out = _dot(jax.nn.gelu(hid).astype(BF16), w2_ref[...])
    if normalize:
        out = _rms(out, g_ref[...])
    valid = lax.broadcasted_iota(jnp.int32, out.shape, 0) < n_cmp
    o_ref[0, 0] = jnp.where(valid, out, 0.0)


def compress(t, pe, w1, w2, gain, normalize):
    B, G, N, half = t.shape
    Hc = w1.shape[1]
    return pl.pallas_call(
        functools.partial(_compress_kernel, normalize=normalize, n_cmp=N - 1),
        name="nsa_compress",
        grid=(B, G),
        in_specs=[pl.BlockSpec((1, 1, N, half), lambda b, g: (b, g, 0, 0)),
                  pl.BlockSpec((1, 2 * half), lambda b, g: (0, 0)),
                  pl.BlockSpec((2 * half, Hc), lambda b, g: (0, 0)),
                  pl.BlockSpec((Hc, HEAD_DIM), lambda b, g: (0, 0)),
                  pl.BlockSpec((1, HEAD_DIM), lambda b, g: (0, 0))],
        out_specs=pl.BlockSpec((1, 1, N, HEAD_DIM), lambda b, g: (b, g, 0, 0)),
        out_shape=jax.ShapeDtypeStruct((B, G, N, HEAD_DIM), F32),
        compiler_params=_params("parallel", "parallel"),
    )(t, pe, w1, w2, gain.reshape(1, HEAD_DIM))


_SEL_LANE0 = 64
_CMP_OV_ROW0 = VT_ROWS


def _nsa_kernel(q_ref, ks_ref, vs_ref, kw_ref, vw_ref, kc_ref, vc_ref, gt_ref, qg_ref, kg_ref,
                o_ref, ksa_ref, kwa_ref, vst_ref, vwt_ref, kca_ref, vcov_ref, *, tq, tk, n_sel):
    g = pl.program_id(1)
    qi = pl.program_id(2)
    T = ks_ref.shape[1]
    J = NSA_GROUP
    n_slc = T // SLC_BLOCK
    nc = kc_ref.shape[2]
    n_cmp = nc - 1
    cols = J * tq
    eye = _eye(HEAD_DIM)

    @pl.when(qi == 0)
    def _():
        row = lax.broadcasted_iota(jnp.int32, (T, LANES), 0)
        lane = lax.broadcasted_iota(jnp.int32, (T, LANES), 1)
        blk = row // SLC_BLOCK
        feats = jnp.where(lane < 3, blk,
                          jnp.where(lane < 6, row % SLC_BLOCK,
                                    jnp.where(lane < 9, 1,
                                              jnp.where(lane - _SEL_LANE0 == blk, 1, 0))))
        feats = feats.astype(F32).astype(BF16)
        ksa_ref[:, :HEAD_DIM] = _rms(ks_ref[0], kg_ref[1:2, :]).astype(BF16)
        ksa_ref[:, HEAD_DIM:] = feats
        kwa_ref[:, :HEAD_DIM] = _rms(kw_ref[0], kg_ref[2:3, :]).astype(BF16)
        kwa_ref[:, HEAD_DIM:] = feats
        vst_ref[:HEAD_DIM, :] = _dot_nt(eye, vs_ref[0].astype(BF16)).astype(BF16)
        vst_ref[HEAD_DIM:, :] = _ones_row(VT_ROWS - HEAD_DIM, T)
        vwt_ref[:HEAD_DIM, :] = _dot_nt(eye, vw_ref[0].astype(BF16)).astype(BF16)
        vwt_ref[HEAD_DIM:, :] = _ones_row(VT_ROWS - HEAD_DIM, T)
        n_r = lax.broadcasted_iota(jnp.int32, (nc, LANES), 0)
        n_l = lax.broadcasted_iota(jnp.int32, (nc, LANES), 1)
        per = SLC_BLOCK // CMP_STRIDE
        cfe = jnp.where(n_l < 3, (n_r // per).astype(F32),
                        jnp.where(n_l < 6, (CMP_STRIDE * (n_r % per)).astype(F32) + 0.5 * (CMP_BLOCK - 1),
                                  jnp.where(n_l < 9, 1.0, 0.0)))
        kca_ref[:, :HEAD_DIM] = kc_ref[0, 0].astype(BF16)
        kca_ref[:, HEAD_DIM:] = cfe.astype(BF16)
        vcov_ref[:HEAD_DIM, :] = _dot_nt(eye, vc_ref[0, 0].astype(BF16)).astype(BF16)
        vcov_ref[HEAD_DIM:_CMP_OV_ROW0, :] = _ones_row(VT_ROWS - HEAD_DIM, nc)
        m_r = lax.broadcasted_iota(jnp.int32, (SLC_BLOCK, nc), 0)
        n_c = lax.broadcasted_iota(jnp.int32, (SLC_BLOCK, nc), 1)
        vcov_ref[_CMP_OV_ROW0:, :] = ((n_c * CMP_STRIDE <= m_r * SLC_BLOCK + (SLC_BLOCK - 1))
                                      & (n_c * CMP_STRIDE + (CMP_BLOCK - 1) >= m_r * SLC_BLOCK)
                                      & (n_c < n_cmp) & (m_r < n_slc)).astype(BF16)

    q0 = pl.multiple_of(qi * tq, tq)
    qraw = q_ref[0]
    qt = jnp.concatenate(
        [_dot_nt(eye, (_rms(qraw[:, j * HEAD_DIM:(j + 1) * HEAD_DIM], qg_ref[...])
                       * (ATTN_SCALE * LOG2E)).astype(BF16)).astype(BF16)
         for j in range(J)], axis=1)

    col = lax.broadcasted_iota(jnp.int32, (1, cols), 1)
    tt = q0 + col % tq
    slope2 = jnp.exp2(-0.5 * (g * J + col // tq + 1).astype(F32)) * LOG2E
    s1, s2, s3 = _split3_f32(slope2)
    r1, r2, r3 = _split3_f32(-slope2 * tt.astype(F32))
    r16 = lax.broadcasted_iota(jnp.int32, (16, cols), 0)
    feat = jnp.zeros((16, cols), F32)
    for i, v in enumerate((s1 * SLC_BLOCK, s2 * SLC_BLOCK, s3 * SLC_BLOCK, s1, s2, s3, r1, r2, r3)):
        feat = jnp.where(r16 == i, v, feat)
    feat = feat.astype(BF16)
    qa_base = jnp.concatenate([qt, feat, jnp.zeros((AUG - HEAD_DIM - 16, cols), BF16)], axis=0)

    n_col = lax.broadcasted_iota(jnp.int32, (nc, 1), 0)
    s = _dot(kca_ref[...], qa_base)
    s = jnp.where((n_col * CMP_STRIDE + (CMP_BLOCK - 1) <= tt) & (n_col < n_cmp), s, NEG_INF)
    m = jnp.max(s, axis=0, keepdims=True)
    p = jnp.exp2(s - jnp.where(m == NEG_INF, 0.0, m))
    p_hi = p.astype(BF16)
    p_lo = (p - p_hi.astype(F32)).astype(BF16)
    r_hi = _dot(vcov_ref[...], p_hi)
    r_lo = _dot(vcov_ref[HEAD_DIM:, :], p_lo)
    z = r_hi[HEAD_DIM:HEAD_DIM + 1] + r_lo[0:1]
    inv_z = 1.0 / jnp.where(z > 0, z, 1.0)

    imp4 = (r_hi[_CMP_OV_ROW0:] + r_lo[_CMP_OV_ROW0 - HEAD_DIM:]) * inv_z
    imp = imp4[:, 0:tq]
    for j in range(1, J):
        imp = imp + imp4[:, j * tq:(j + 1) * tq]
    blk = lax.broadcasted_iota(jnp.int32, (SLC_BLOCK, tq), 0)
    t_q = q0 + lax.broadcasted_iota(jnp.int32, (1, tq), 1)
    imp = jnp.where(blk * SLC_BLOCK > t_q, NEG_INF, imp)
    imp = jnp.where((blk == t_q // SLC_BLOCK) | (blk == 0), jnp.inf, imp)
    blk_f = blk.astype(F32)
    taken = blk >= n_slc
    for _ in range(n_sel):
        vals = jnp.where(taken, NEG_INF, imp)
        best = jnp.max(vals, axis=0, keepdims=True)
        cand = (vals == best) & jnp.logical_not(taken)
        first = jnp.min(jnp.where(cand, blk_f, float(SLC_BLOCK)), axis=0, keepdims=True)
        taken = taken | (blk_f == first)
    not_sel = jnp.where(taken & (blk < n_slc), 0.0, -MASK_BIG).astype(BF16)
    qa_slc = jnp.concatenate([qt, feat, jnp.zeros((_SEL_LANE0 - 16, cols), BF16)]
                             + [jnp.concatenate([not_sel] * J, axis=1)], axis=0)

    def slc_tile(jt, carry, diag):
        m, acc = carry
        k0 = pl.multiple_of(jt * tk, tk)
        s = _dot(ksa_ref[pl.ds(k0, tk), :], qa_slc)
        if diag:
            kpos = k0 + lax.broadcasted_iota(jnp.int32, (tk, 1), 0)
            s = jnp.where(kpos <= tt, s, NEG_INF)
        return _attn_tile(s, m, acc, vst_ref[:, pl.ds(k0, tk)])

    jd = q0 // tk
    init = (jnp.full((1, cols), NEG_INF, F32), jnp.zeros((VT_ROWS, cols), F32))
    carry = lax.fori_loop(0, jd, lambda jt, c: slc_tile(jt, c, False), init)
    _, acc_s = slc_tile(jd, carry, True)

    span = WINDOW + tq
    k0w = pl.multiple_of(jnp.maximum(q0 - WINDOW, 0), tq)
    s = _dot(kwa_ref[pl.ds(k0w, span), :], qa_base)
    d = (lax.broadcasted_iota(jnp.int32, (span, 1), 0)
         - lax.broadcasted_iota(jnp.int32, (1, tq), 1))
    off = q0 - k0w
    bias = jnp.where((d <= off) & (d > off - WINDOW), 0.0, NEG_INF)
    s = s + jnp.concatenate([bias] * J, axis=1)
    p = jnp.exp2(s - jnp.max(s, axis=0, keepdims=True)).astype(BF16)
    acc_w = _dot(vwt_ref[:, pl.ds(k0w, span)], p)

    gates_t = jnp.transpose(jax.nn.sigmoid(gt_ref[0]))
    gate = lambda c: jnp.concatenate([gates_t[3 * j + c:3 * j + c + 1, :] for j in range(J)], axis=1)
    o_t = ((gate(0) * inv_z) * r_hi[:HEAD_DIM]
           + (gate(1) / acc_s[HEAD_DIM:HEAD_DIM + 1]) * acc_s[:HEAD_DIM]
           + (gate(2) / acc_w[HEAD_DIM:HEAD_DIM + 1]) * acc_w[:HEAD_DIM]).astype(BF16)
    for j in range(J):
        o_ref[0, :, j * HEAD_DIM:(j + 1) * HEAD_DIM] = _dot_nt(
            eye, o_t[:, j * tq:(j + 1) * tq]).astype(o_ref.dtype)


def nsa_attn(proj, gates, k_cmp, v_cmp, q_gain, k_gain, tk):
    B, T, _ = proj.shape
    G, J = NSA_KV_HEADS, NSA_GROUP
    tq = LANES
    tk = min(tk, T)
    nc = k_cmp.shape[2]
    assert T % tk == 0 and tk % tq == 0 and T >= WINDOW + tq
    assert T // SLC_BLOCK <= LANES - _SEL_LANE0 and SLC_BLOCK <= LANES - _SEL_LANE0
    qcols = NSA_HEADS
    kv = lambda slab: pl.BlockSpec((1, T, HEAD_DIM), lambda b, g, i: (b, 0, qcols + slab * G + g))
    cmp_spec = pl.BlockSpec((1, 1, nc, HEAD_DIM), lambda b, g, i: (b, g, 0, 0))
    return pl.pallas_call(
        functools.partial(_nsa_kernel, tq=tq, tk=tk, n_sel=min(SLC_TOPK, T // SLC_BLOCK)),
        name="nsa_attn",
        grid=(B, G, T // tq),
        in_specs=[pl.BlockSpec((1, tq, J * HEAD_DIM), lambda b, g, i: (b, i, g)),
                  kv(2), kv(3), kv(4), kv(5), cmp_spec, cmp_spec,
                  pl.BlockSpec((1, tq, LANES), lambda b, g, i: (b, i, g)),
                  pl.BlockSpec((1, HEAD_DIM), lambda b, g, i: (0, 0)),
                  pl.BlockSpec((3, HEAD_DIM), lambda b, g, i: (0, 0))],
        out_specs=pl.BlockSpec((1, tq, J * HEAD_DIM), lambda b, g, i: (b, i, g)),
        out_shape=jax.ShapeDtypeStruct((B, T, NSA_HEADS * HEAD_DIM), BF16),
        scratch_shapes=[pltpu.VMEM((T, AUG), BF16), pltpu.VMEM((T, AUG), BF16),
                        pltpu.VMEM((VT_ROWS, T), BF16), pltpu.VMEM((VT_ROWS, T), BF16),
                        pltpu.VMEM((nc, AUG), BF16),
                        pltpu.VMEM((VT_ROWS + SLC_BLOCK, nc), BF16)],
        compiler_params=_params("parallel", "parallel", "arbitrary"),
    )(proj, proj, proj, proj, proj, k_cmp, v_cmp, gates, q_gain.reshape(1, HEAD_DIM), k_gain)


def _pad_cols(w, n):
    return jnp.pad(w, ((0, 0), (0, n - w.shape[1])))


def even_mixer(x, gain, w_in, w_out, f_bias, q_gain, k_gain, lb, o_gain):
    B, T, D = x.shape
    xf = x.reshape(B * T, D)
    fw = FOX_HEADS * HEAD_DIM
    w_main = jnp.concatenate([w_in[:, :3 * fw], w_in[:, 3 * fw + FOX_HEADS:]], axis=1).astype(BF16)
    w_gate = _pad_cols(w_in[:, 3 * fw:3 * fw + FOX_HEADS], LANES).astype(BF16)
    proj = rms_matmul(xf, gain, w_main, 512, 1024).reshape(B, T, -1)
    gate = rms_matmul(xf, gain, w_gate, 512, LANES).reshape(B, T, LANES)
    c, kfeat = fox_gate(gate, _pad_cols(f_bias.reshape(1, FOX_HEADS).astype(F32), LANES), FOX_HEADS)
    c_rows = c[:, :, :FOX_HEADS].transpose(0, 2, 1).reshape(B * FOX_HEADS, 1, T)
    o_fox = fox_attn(proj, kfeat, c_rows, q_gain, k_gain, 512)
    lb = lb.astype(F32).reshape(HGRN_HEADS, HEAD_DIM)
    o_h = hgrn(proj, 3 * FOX_HEADS, jnp.log(lb), jnp.log1p(-lb), 1.0 - lb, o_gain)
    w_out = w_out.astype(BF16)
    out = out_proj([o_fox.reshape(B * T, -1), o_h.reshape(B * T, -1)],
                   [w_out[:fw], w_out[fw:]], xf, 256)
    return out.reshape(B, T, D)


def nsa_mixer(x, gain, w_in, w_out, q_gain, k_gain, pe_k, w1_k, w2_k, pe_v, w1_v, w2_v):
    B, T, D = x.shape
    G, J = NSA_KV_HEADS, NSA_GROUP
    xf = x.reshape(B * T, D)
    qw = NSA_HEADS * HEAD_DIM
    kvw = G * HEAD_DIM
    n_main = qw + 6 * kvw
    w_main = w_in[:, :n_main].astype(BF16)
    w_gate = w_in[:, n_main:].reshape(D, G, 3 * J)
    w_gate = jnp.pad(w_gate, ((0, 0), (0, 0), (0, LANES - 3 * J))).reshape(D, G * LANES).astype(BF16)
    proj = rms_matmul(xf, gain, w_main, 512, 1024).reshape(B, T, n_main)
    gates = rms_matmul(xf, gain, w_gate, 512, G * LANES).reshape(B, T, G * LANES)

    N = T // CMP_STRIDE

    def blocks(slab):
        t = proj[:, :, qw + slab * kvw: qw + (slab + 1) * kvw]
        return t.reshape(B, N, CMP_STRIDE, G, HEAD_DIM).transpose(0, 3, 1, 2, 4).reshape(
            B, G, N, CMP_STRIDE * HEAD_DIM)

    flat = lambda w1: w1.reshape(CMP_BLOCK * HEAD_DIM, -1).astype(BF16)
    k_cmp = compress(blocks(0), pe_k.reshape(1, -1), flat(w1_k), w2_k.astype(BF16), k_gain[0], True)
    v_cmp = compress(blocks(1), pe_v.reshape(1, -1), flat(w1_v), w2_v.astype(BF16), k_gain[0], False)
    o = nsa_attn(proj, gates, k_cmp, v_cmp, q_gain, k_gain, 512)
    out = out_proj([o.reshape(B * T, qw)], [w_out.astype(BF16)], xf, 256)
    return out.reshape(B, T, D)


def kernel(x, mem, norm_mix, norm_mem, norm_ffn, mem_in_gain, even_w_in, even_w_out, fox_f_bias,
           fox_q_gain, fox_k_gain, hgrn_lb_logits, hgrn_o_gain, odd_w_in, odd_w_out, nsa_q_gain,
           nsa_k_gain, cmp_pe_k, cmp_w1_k, cmp_w2_k, cmp_pe_v, cmp_w1_v, cmp_w2_v, mem_wq, mem_wkv,
           mem_wo, mem_q_gain, mem_k_gain, ffn_w1, ffn_w3, ffn_w2):
    B, T, D = x.shape
    depth = norm_mix.shape[0]
    L = mem.shape[1]
    lb_cum = jnp.cumsum(jax.nn.softmax(hgrn_lb_logits.astype(F32), axis=0), axis=0)
    hgrn_lb = lb_cum - lb_cum[0:1]
    memf = mem.reshape(B * L, D)
    for layer in range(depth):
        if layer % 2 == 0:
            e = layer // 2
            x = even_mixer(x, norm_mix[layer], even_w_in[e], even_w_out[e], fox_f_bias[e],
                           fox_q_gain[e], fox_k_gain[e], hgrn_lb[e], hgrn_o_gain[e])
        else:
            o = layer // 2
            x = nsa_mixer(x, norm_mix[layer], odd_w_in[o], odd_w_out[o], nsa_q_gain[o], nsa_k_gain[o],
                          cmp_pe_k[o], cmp_w1_k[o], cmp_w2_k[o], cmp_pe_v[o], cmp_w1_v[o], cmp_w2_v[o])
        kv = rms_matmul(memf, mem_in_gain[layer], mem_wkv[layer].astype(BF16), 256, 1024)
        x = mem_attn(x, norm_mem[layer], mem_wq[layer].astype(BF16), kv.reshape(B, L, -1),
                     mem_q_gain[layer], mem_k_gain[layer], mem_wo[layer].astype(BF16), 256)
        x = ffn(x.reshape(B * T, D), norm_ffn[layer], ffn_w1[layer].astype(BF16),
                ffn_w3[layer].astype(BF16), ffn_w2[layer].astype(BF16), 512, 512).reshape(B, T, D)
    return x
```

```python
import functools

import jax
import jax.numpy as jnp
from jax import lax
from jax.experimental import pallas as pl
from jax.experimental.pallas import tpu as pltpu

F32 = jnp.float32
BF16 = jnp.bfloat16
NEG_INF = float("-inf")

EPS = 1e-6
HEAD_DIM = 128
ATTN_SCALE = HEAD_DIM ** -0.5
LANES = 128
FOX_HEADS = 8
HGRN_HEADS = 8
HGRN_BLOCK = 16
NSA_HEADS = 16
NSA_KV_HEADS = 4
NSA_GROUP = NSA_HEADS // NSA_KV_HEADS
CMP_BLOCK = 32
CMP_STRIDE = 16
SLC_BLOCK = 64
SLC_TOPK = 16
WINDOW = 512
MEM_HEADS = 4
V7X_VMEM_LIMIT = 56 * 1024 * 1024
LOG2E = 1.4426950408889634
MASK_BIG = 1e30
AUG = 2 * LANES
VT_ROWS = HEAD_DIM + 16


def _params(*sem):
    return pltpu.CompilerParams(dimension_semantics=sem, vmem_limit_bytes=V7X_VMEM_LIMIT)


def _rms(x, gain):
    return x * lax.rsqrt(jnp.mean(x * x, axis=-1, keepdims=True) + EPS) * gain


def _dot(a, b):
    return jnp.dot(a, b, preferred_element_type=F32)


def _dot_nt(a, b):
    return lax.dot_general(a, b, (((1,), (1,)), ((), ())), preferred_element_type=F32)


def _dot_tn(a, b):
    return lax.dot_general(a, b, (((0,), (0,)), ((), ())), preferred_element_type=F32)


def _split3(x):
    x1 = x.astype(BF16)
    r1 = x - x1.astype(F32)
    x2 = r1.astype(BF16)
    return x1, x2, (r1 - x2.astype(F32)).astype(BF16)


def _exact_dot(sel, x):
    x1, x2, x3 = _split3(x)
    return _dot(sel, x1) + _dot(sel, x2) + _dot(sel, x3)


def _log_sigmoid(z):
    return jnp.minimum(z, 0.0) - jnp.log1p(jnp.exp(-jnp.abs(z)))


def _rms_matmul_kernel(x_ref, g_ref, w_ref, o_ref, hn_ref):
    @pl.when(pl.program_id(1) == 0)
    def _():
        hn_ref[...] = _rms(x_ref[...], g_ref[...]).astype(BF16)

    o_ref[...] = _dot(hn_ref[...], w_ref[...])


def rms_matmul(x, gain, w, tm, tn):
    M, K = x.shape
    N = w.shape[1]
    tm, tn = min(tm, M), min(tn, N)
    assert M % tm == 0 and N % tn == 0
    return pl.pallas_call(
        _rms_matmul_kernel,
        name="rms_matmul",
        grid=(M // tm, N // tn),
        in_specs=[pl.BlockSpec((tm, K), lambda i, j: (i, 0)),
                  pl.BlockSpec((1, K), lambda i, j: (0, 0)),
                  pl.BlockSpec((K, tn), lambda i, j: (0, j))],
        out_specs=pl.BlockSpec((tm, tn), lambda i, j: (i, j)),
        out_shape=jax.ShapeDtypeStruct((M, N), F32),
        scratch_shapes=[pltpu.VMEM((tm, K), BF16)],
        compiler_params=_params("parallel", "arbitrary"),
    )(x, gain.reshape(1, K), w)


def _out_proj_kernel(*refs, n_in):
    a_refs, w_refs = refs[:n_in], refs[n_in:2 * n_in]
    res_ref, o_ref = refs[2 * n_in], refs[2 * n_in + 1]
    acc = res_ref[...]
    for a_ref, w_ref in zip(a_refs, w_refs):
        acc = acc + _dot(a_ref[...], w_ref[...])
    o_ref[...] = acc


def out_proj(acts, weights, res, tm):
    M, N = res.shape
    tm = min(tm, M)
    assert M % tm == 0
    n_in = len(acts)
    in_specs = ([pl.BlockSpec((tm, a.shape[1]), lambda i: (i, 0)) for a in acts]
                + [pl.BlockSpec(w.shape, lambda i: (0, 0)) for w in weights]
                + [pl.BlockSpec((tm, N), lambda i: (i, 0))])
    return pl.pallas_call(
        functools.partial(_out_proj_kernel, n_in=n_in),
        name="out_proj",
        grid=(M // tm,),
        in_specs=in_specs,
        out_specs=pl.BlockSpec((tm, N), lambda i: (i, 0)),
        out_shape=jax.ShapeDtypeStruct((M, N), F32),
        compiler_params=_params("parallel"),
    )(*acts, *weights, res)


def _ffn_kernel(x_ref, g_ref, w1_ref, w3_ref, w2_ref, o_ref, hn_ref):
    f = pl.program_id(1)

    @pl.when(f == 0)
    def _():
        x = x_ref[...]
        hn_ref[...] = _rms(x, g_ref[...]).astype(BF16)
        o_ref[...] = x

    h = hn_ref[...]
    a = _dot(h, w1_ref[...])
    b = _dot(h, w3_ref[...])
    u = (a * jax.nn.sigmoid(a) * b).astype(BF16)
    o_ref[...] += _dot(u, w2_ref[...])


def ffn(x, gain, w1, w3, w2, tm, tf):
    M, D = x.shape
    Fh = w1.shape[1]
    tm, tf = min(tm, M), min(tf, Fh)
    assert M % tm == 0 and Fh % tf == 0
    return pl.pallas_call(
        _ffn_kernel,
        name="ffn",
        grid=(M // tm, Fh // tf),
        in_specs=[pl.BlockSpec((tm, D), lambda i, f: (i, 0)),
                  pl.BlockSpec((1, D), lambda i, f: (0, 0)),
                  pl.BlockSpec((D, tf), lambda i, f: (0, f)),
                  pl.BlockSpec((D, tf), lambda i, f: (0, f)),
                  pl.BlockSpec((tf, D), lambda i, f: (f, 0))],
        out_specs=pl.BlockSpec((tm, D), lambda i, f: (i, 0)),
        out_shape=jax.ShapeDtypeStruct((M, D), F32),
        scratch_shapes=[pltpu.VMEM((tm, D), BF16)],
        compiler_params=_params("parallel", "arbitrary"),
    )(x, gain.reshape(1, D), w1, w3, w2)


def _mem_attn_kernel(x_ref, g_ref, wq_ref, kv_ref, qg_ref, kg_ref, wo_ref, o_ref, *, heads):
    x = x_ref[0]
    h = _rms(x, g_ref[...]).astype(BF16)
    q = _dot(h, wq_ref[...])
    kv = kv_ref[0]
    w = heads * HEAD_DIM
    outs = []
    for hh in range(heads):
        sl = slice(hh * HEAD_DIM, (hh + 1) * HEAD_DIM)
        qh = (_rms(q[:, sl], qg_ref[...]) * ATTN_SCALE).astype(BF16)
        kh = _rms(kv[:, sl], kg_ref[...]).astype(BF16)
        vh = kv[:, w + hh * HEAD_DIM: w + (hh + 1) * HEAD_DIM].astype(BF16)
        s = _dot_nt(qh, kh)
        p = jnp.exp(s - jnp.max(s, axis=-1, keepdims=True))
        z = jnp.sum(p, axis=-1, keepdims=True)
        outs.append((_dot(p.astype(BF16), vh) / z).astype(BF16))
    o = jnp.concatenate(outs, axis=-1)
    o_ref[0] = x + _dot(o, wo_ref[...])


def mem_attn(x, gain, wq, kv, q_gain, k_gain, wo, tm):
    B, T, D = x.shape
    L, W2 = kv.shape[1], kv.shape[2]
    W = W2 // 2
    tm = min(tm, T)
    assert T % tm == 0
    return pl.pallas_call(
        functools.partial(_mem_attn_kernel, heads=W // HEAD_DIM),
        name="mem_attn",
        grid=(B, T // tm),
        in_specs=[pl.BlockSpec((1, tm, D), lambda b, i: (b, i, 0)),
                  pl.BlockSpec((1, D), lambda b, i: (0, 0)),
                  pl.BlockSpec((D, W), lambda b, i: (0, 0)),
                  pl.BlockSpec((1, L, W2), lambda b, i: (b, 0, 0)),
                  pl.BlockSpec((1, HEAD_DIM), lambda b, i: (0, 0)),
                  pl.BlockSpec((1, HEAD_DIM), lambda b, i: (0, 0)),
                  pl.BlockSpec((W, D), lambda b, i: (0, 0))],
        out_specs=pl.BlockSpec((1, tm, D), lambda b, i: (b, i, 0)),
        out_shape=jax.ShapeDtypeStruct((B, T, D), F32),
        compiler_params=_params("parallel", "parallel"),
    )(x, gain.reshape(1, D), wq, kv, q_gain.reshape(1, HEAD_DIM), k_gain.reshape(1, HEAD_DIM), wo)


def _eye(n):
    return (lax.broadcasted_iota(jnp.int32, (n, n), 0)
            == lax.broadcasted_iota(jnp.int32, (n, n), 1)).astype(BF16)


def _ones_row(rows, n):
    return (lax.broadcasted_iota(jnp.int32, (rows, n), 0) == 0).astype(BF16)


def _split3_f32(x):
    return tuple(t.astype(F32) for t in _split3(x))


COL_GROUPS = 1


def _col_groups(n):
    w = n // COL_GROUPS
    return [slice(i * w, (i + 1) * w) for i in range(COL_GROUPS)]


def _attn_init(n):
    return tuple((jnp.full((1, c.stop - c.start), NEG_INF, F32), jnp.zeros((VT_ROWS, c.stop - c.start), F32))
                 for c in _col_groups(n))


def _attn_tile(k_tile, vt_tile, qa, carry, score_fn=None):
    out = []
    for (m, acc), c in zip(carry, _col_groups(qa.shape[1])):
        s = _dot(k_tile, qa[:, c])
        if score_fn is not None:
            s = score_fn(s, c)
        m_new = jnp.maximum(m, jnp.max(s, axis=0, keepdims=True))
        p = jnp.exp2(s - m_new).astype(BF16)
        out.append((m_new, jnp.exp2(m - m_new) * acc + _dot(vt_tile, p)))
    return tuple(out)


def _attn_finish(carry):
    return jnp.concatenate([acc for _, acc in carry], axis=1)


def _fox_gate_kernel(g_ref, b_ref, c_ref, kf_ref, *, heads):
    T = g_ref.shape[1]
    r = lax.broadcasted_iota(jnp.int32, (LANES, LANES), 0)
    lane = lax.broadcasted_iota(jnp.int32, (LANES, LANES), 1)
    tri = (r >= lane).astype(BF16)

    def body(i, carry):
        rows = pl.ds(pl.multiple_of(i * LANES, LANES), LANES)
        cs = _exact_dot(tri, _log_sigmoid(g_ref[0, rows, :] + b_ref[...])) + carry
        c_ref[0, rows, :] = cs
        c2 = cs * LOG2E
        for h in range(heads):
            d1, d2, d3 = _split3_f32(c2[:, h:h + 1])
            feat = jnp.where(lane < 3, 1.0,
                             jnp.where(lane == 3, -d1, jnp.where(lane == 4, -d2,
                                                                 jnp.where(lane == 5, -d3, 0.0))))
            kf_ref[0, h, rows, :] = feat.astype(BF16)
        return cs[LANES - 1:LANES, :]

    lax.fori_loop(0, T // LANES, body, jnp.zeros((1, LANES), F32))


def fox_gate(g, bias, heads):
    B, T, _ = g.shape
    return pl.pallas_call(
        functools.partial(_fox_gate_kernel, heads=heads),
        name="fox_gate",
        grid=(B,),
        in_specs=[pl.BlockSpec((1, T, LANES), lambda b: (b, 0, 0)),
                  pl.BlockSpec((1, LANES), lambda b: (0, 0))],
        out_specs=[pl.BlockSpec((1, T, LANES), lambda b: (b, 0, 0)),
                   pl.BlockSpec((1, heads, T, LANES), lambda b: (b, 0, 0, 0))],
        out_shape=[jax.ShapeDtypeStruct((B, T, LANES), F32),
                   jax.ShapeDtypeStruct((B, heads, T, LANES), BF16)],
        compiler_params=_params("parallel"),
    )(g, bias)


def _fox_attn_kernel(q_ref, k_ref, v_ref, kf_ref, c_ref, qg_ref, kg_ref, o_ref, ka_ref, vt_ref, *, tq):
    qi = pl.program_id(2)
    T = k_ref.shape[1]
    eye = _eye(HEAD_DIM)

    @pl.when(qi == 0)
    def _():
        ka_ref[:, :HEAD_DIM] = _rms(k_ref[0], kg_ref[...]).astype(BF16)
        ka_ref[:, HEAD_DIM:] = kf_ref[0, 0]
        vt_ref[:HEAD_DIM, :] = _dot_nt(eye, v_ref[0].astype(BF16)).astype(BF16)
        vt_ref[HEAD_DIM:, :] = _ones_row(VT_ROWS - HEAD_DIM, T)

    q0 = pl.multiple_of(qi * tq, tq)
    qn = (_rms(q_ref[0], qg_ref[...]) * (ATTN_SCALE * LOG2E)).astype(BF16)
    qt = _dot_nt(eye, qn).astype(BF16)
    c1, c2, c3 = _split3_f32(c_ref[0, :, pl.ds(q0, tq)] * LOG2E)
    r = lax.broadcasted_iota(jnp.int32, (16, tq), 0)
    feat = jnp.where(r == 0, c1, jnp.where(r == 1, c2, jnp.where(r == 2, c3,
                                                                 jnp.where(r < 6, 1.0, 0.0))))
    qa = jnp.concatenate([qt, feat.astype(BF16), jnp.zeros((AUG - HEAD_DIM - 16, tq), BF16)], axis=0)

    kpos = lax.broadcasted_iota(jnp.int32, (tq, 1), 0)
    tpos = lax.broadcasted_iota(jnp.int32, (1, tq), 1)

    def tile(j, carry, score_fn=None):
        k0 = pl.multiple_of(j * tq, tq)
        return _attn_tile(ka_ref[pl.ds(k0, tq), :], vt_ref[:, pl.ds(k0, tq)], qa, carry, score_fn)

    carry = tile(qi, _attn_init(tq), lambda s, c: jnp.where(kpos <= tpos[:, c], s, NEG_INF))
    acc = _attn_finish(lax.fori_loop(0, qi, tile, carry))
    ot = (acc[:HEAD_DIM] / acc[HEAD_DIM:HEAD_DIM + 1]).astype(BF16)
    for c in range(tq // LANES):
        cs = slice(c * LANES, (c + 1) * LANES)
        o_ref[0, cs, :] = _dot_nt(eye, ot[:, cs]).astype(o_ref.dtype)


def fox_attn(proj, kfeat, c_rows, q_gain, k_gain, tq):
    B, T, _ = proj.shape
    H = FOX_HEADS
    tq = min(tq, T)
    assert T % tq == 0 and tq % LANES == 0
    return pl.pallas_call(
        functools.partial(_fox_attn_kernel, tq=tq),
        name="fox_attn",
        grid=(B, H, T // tq),
        in_specs=[pl.BlockSpec((1, tq, HEAD_DIM), lambda b, h, i: (b, i, h)),
                  pl.BlockSpec((1, T, HEAD_DIM), lambda b, h, i: (b, 0, H + h)),
                  pl.BlockSpec((1, T, HEAD_DIM), lambda b, h, i: (b, 0, 2 * H + h)),
                  pl.BlockSpec((1, 1, T, LANES), lambda b, h, i: (b, h, 0, 0)),
                  pl.BlockSpec((1, 1, T), lambda b, h, i: (b * H + h, 0, 0)),
                  pl.BlockSpec((1, HEAD_DIM), lambda b, h, i: (0, 0)),
                  pl.BlockSpec((1, HEAD_DIM), lambda b, h, i: (0, 0))],
        out_specs=pl.BlockSpec((1, tq, HEAD_DIM), lambda b, h, i: (b, i, h)),
        out_shape=jax.ShapeDtypeStruct((B, T, H * HEAD_DIM), BF16),
        scratch_shapes=[pltpu.VMEM((T, AUG), BF16), pltpu.VMEM((VT_ROWS, T), BF16)],
        compiler_params=_params("parallel", "parallel", "arbitrary"),
    )(proj, proj, proj, kfeat, c_rows, q_gain.reshape(1, HEAD_DIM), k_gain.reshape(1, HEAD_DIM))


def _hgrn_kernel(q_ref, f_ref, i_ref, g_ref, la_ref, l1_ref, oml_ref, og_ref, o_ref, *, bs):
    T = q_ref.shape[1]
    R = LANES
    ri = lax.broadcasted_iota(jnp.int32, (R, R), 0)
    ci = lax.broadcasted_iota(jnp.int32, (R, R), 1)
    same = (ri // bs) == (ci // bs)
    cum_sel = (same & (ri >= ci)).astype(BF16)
    tot_sel = same.astype(BF16)
    s_idx = lax.broadcasted_iota(jnp.int32, (bs, HEAD_DIM), 0)

    def step(sb, st):
        rows = pl.ds(pl.multiple_of(sb * R, R), R)
        z = f_ref[0, rows, :]
        a = la_ref[0]
        y = l1_ref[0] + _log_sigmoid(z)
        log_f = jnp.maximum(a, y) + jnp.log1p(jnp.exp(-jnp.abs(a - y)))
        b = _exact_dot(cum_sel, log_f)
        bt = _exact_dot(tot_sel, log_f)
        kk = oml_ref[0] * jax.nn.sigmoid(-z)
        hq = q_ref[0, rows, :]
        qq = hq * jax.nn.sigmoid(hq)
        vv = i_ref[0, rows, :]
        qe = (qq * jnp.exp(b)).astype(BF16)
        ke = (kk * jnp.exp(bt - b)).astype(BF16)
        dec = jnp.exp(bt)
        vb = vv.astype(BF16)
        outs = []
        for u in range(R // bs):
            sl = slice(u * bs, (u + 1) * bs)
            o_inter = _dot_nt(qe[sl], st.astype(BF16))
            bu, vu = b[sl], vv[sl]
            qk = qq[sl]
            ku = kk[sl]
            intra = []
            for t in range(bs):
                w = jnp.exp(jnp.where(s_idx <= t, bu[t:t + 1, :] - bu, NEG_INF))
                a_t = jnp.sum(w * (qk[t:t + 1, :] * ku), axis=-1, keepdims=True)
                intra.append(jnp.sum(a_t * vu, axis=0, keepdims=True))
            outs.append(o_inter + jnp.concatenate(intra, axis=0))
            st = dec[u * bs:u * bs + 1, :] * st + _dot_tn(vb[sl], ke[sl])
        o = jnp.concatenate(outs, axis=0)
        hg = g_ref[0, rows, :]
        o_ref[0, rows, :] = (_rms(o, og_ref[...]) * (hg * jax.nn.sigmoid(hg))).astype(o_ref.dtype)
        return st

    lax.fori_loop(0, T // R, step, jnp.zeros((HEAD_DIM, HEAD_DIM), F32))


def hgrn(proj, col0, log_lb, log1m_lb, one_m_lb, o_gain):
    B, T, _ = proj.shape
    H = HGRN_HEADS
    row = lambda off: pl.BlockSpec((1, T, HEAD_DIM), lambda b, h: (b, 0, col0 + off * H + h))
    par = pl.BlockSpec((1, 1, HEAD_DIM), lambda b, h: (h, 0, 0))
    log_lb, log1m_lb, one_m_lb = (p.reshape(H, 1, HEAD_DIM) for p in (log_lb, log1m_lb, one_m_lb))
    return pl.pallas_call(
        functools.partial(_hgrn_kernel, bs=HGRN_BLOCK),
        name="hgrn",
        grid=(B, H),
        in_specs=[row(0), row(1), row(2), row(3), par, par, par,
                  pl.BlockSpec((1, HEAD_DIM), lambda b, h: (0, 0))],
        out_specs=pl.BlockSpec((1, T, HEAD_DIM), lambda b, h: (b, 0, h)),
        out_shape=jax.ShapeDtypeStruct((B, T, H * HEAD_DIM), BF16),
        compiler_params=_params("parallel", "parallel"),
    )(proj, proj, proj, proj, log_lb, log1m_lb, one_m_lb, o_gain.reshape(1, HEAD_DIM))


def _compress_kernel(t_ref, pe_ref, w1_ref, w2_ref, g_ref, o_ref, *, normalize, n_cmp):
    half = t_ref.shape[3]
    t = t_ref[0, 0]
    N = t.shape[0]
    top = _dot((t + pe_ref[:, :half]).astype(BF16), w1_ref[:half, :])
    bot = _dot((t + pe_ref[:, half:]).astype(BF16), w1_ref[half:, :])
    hid = top + pltpu.roll(bot, N - 1, 0)
    out = _dot(jax.nn.gelu(hid).astype(BF16), w2_ref[...])
    if normalize:
        out = _rms(out, g_ref[...])
    valid = lax.broadcasted_iota(jnp.int32, out.shape, 0) < n_cmp
    o_ref[0, 0] = jnp.where(valid, out, 0.0)


def compress(t, pe, w1, w2, gain, normalize):
    B, G, N, half = t.shape
    Hc = w1.shape[1]
    return pl.pallas_call(
        functools.partial(_compress_kernel, normalize=normalize, n_cmp=N - 1),
        name="nsa_compress",
        grid=(B, G),
        in_specs=[pl.BlockSpec((1, 1, N, half), lambda b, g: (b, g, 0, 0)),
                  pl.BlockSpec((1, 2 * half), lambda b, g: (0, 0)),
                  pl.BlockSpec((2 * half, Hc), lambda b, g: (0, 0)),
                  pl.BlockSpec((Hc, HEAD_DIM), lambda b, g: (0, 0)),
                  pl.BlockSpec((1, HEAD_DIM), lambda b, g: (0, 0))],
        out_specs=pl.BlockSpec((1, 1, N, HEAD_DIM), lambda b, g: (b, g, 0, 0)),
        out_shape=jax.ShapeDtypeStruct((B, G, N, HEAD_DIM), F32),
        compiler_params=_params("parallel", "parallel"),
    )(t, pe, w1, w2, gain.reshape(1, HEAD_DIM))


_SEL_LANE0 = 64
_CMP_OV_ROW0 = VT_ROWS


def _nsa_kernel(q_ref, ks_ref, vs_ref, kw_ref, vw_ref, kc_ref, vc_ref, gt_ref, qg_ref, kg_ref,
                o_ref, ksa_ref, kwa_ref, vst_ref, vwt_ref, kca_ref, vcov_ref, *, tq, tk, n_sel):
    g = pl.program_id(1)
    qi = pl.program_id(2)
    T = ks_ref.shape[1]
    J = NSA_GROUP
    n_slc = T // SLC_BLOCK
    nc = kc_ref.shape[2]
    n_cmp = nc - 1
    cols = J * tq
    eye = _eye(HEAD_DIM)

    @pl.when(qi == 0)
    def _():
        row = lax.broadcasted_iota(jnp.int32, (T, LANES), 0)
        lane = lax.broadcasted_iota(jnp.int32, (T, LANES), 1)
        blk = row // SLC_BLOCK
        feats = jnp.where(lane < 3, blk,
                          jnp.where(lane < 6, row % SLC_BLOCK,
                                    jnp.where(lane < 9, 1,
                                              jnp.where(lane - _SEL_LANE0 == blk, 1, 0))))
        feats = feats.astype(F32).astype(BF16)
        ksa_ref[:, :HEAD_DIM] = _rms(ks_ref[0], kg_ref[1:2, :]).astype(BF16)
        ksa_ref[:, HEAD_DIM:] = feats
        kwa_ref[:, :HEAD_DIM] = _rms(kw_ref[0], kg_ref[2:3, :]).astype(BF16)
        kwa_ref[:, HEAD_DIM:] = feats
        vst_ref[:HEAD_DIM, :] = _dot_nt(eye, vs_ref[0].astype(BF16)).astype(BF16)
        vst_ref[HEAD_DIM:, :] = _ones_row(VT_ROWS - HEAD_DIM, T)
        vwt_ref[:HEAD_DIM, :] = _dot_nt(eye, vw_ref[0].astype(BF16)).astype(BF16)
        vwt_ref[HEAD_DIM:, :] = _ones_row(VT_ROWS - HEAD_DIM, T)
        n_r = lax.broadcasted_iota(jnp.int32, (nc, LANES), 0)
        n_l = lax.broadcasted_iota(jnp.int32, (nc, LANES), 1)
        per = SLC_BLOCK // CMP_STRIDE
        cfe = jnp.where(n_l < 3, (n_r // per).astype(F32),
                        jnp.where(n_l < 6, (CMP_STRIDE * (n_r % per)).astype(F32) + 0.5 * (CMP_BLOCK - 1),
                                  jnp.where(n_l < 9, 1.0, 0.0)))
        kca_ref[:, :HEAD_DIM] = kc_ref[0, 0].astype(BF16)
        kca_ref[:, HEAD_DIM:] = cfe.astype(BF16)
        vcov_ref[:HEAD_DIM, :] = _dot_nt(eye, vc_ref[0, 0].astype(BF16)).astype(BF16)
        vcov_ref[HEAD_DIM:_CMP_OV_ROW0, :] = _ones_row(VT_ROWS - HEAD_DIM, nc)
        m_r = lax.broadcasted_iota(jnp.int32, (SLC_BLOCK, nc), 0)
        n_c = lax.broadcasted_iota(jnp.int32, (SLC_BLOCK, nc), 1)
        vcov_ref[_CMP_OV_ROW0:, :] = ((n_c * CMP_STRIDE <= m_r * SLC_BLOCK + (SLC_BLOCK - 1))
                                      & (n_c * CMP_STRIDE + (CMP_BLOCK - 1) >= m_r * SLC_BLOCK)
                                      & (n_c < n_cmp) & (m_r < n_slc)).astype(BF16)

    q0 = pl.multiple_of(qi * tq, tq)
    qraw = q_ref[0]
    qt = jnp.concatenate(
        [_dot_nt(eye, (_rms(qraw[:, j * HEAD_DIM:(j + 1) * HEAD_DIM], qg_ref[...])
                       * (ATTN_SCALE * LOG2E)).astype(BF16)).astype(BF16)
         for j in range(J)], axis=1)

    col = lax.broadcasted_iota(jnp.int32, (1, cols), 1)
    tt = q0 + col % tq
    slope2 = jnp.exp2(-0.5 * (g * J + col // tq + 1).astype(F32)) * LOG2E
    s1, s2, s3 = _split3_f32(slope2)
    r1, r2, r3 = _split3_f32(-slope2 * tt.astype(F32))
    r16 = lax.broadcasted_iota(jnp.int32, (16, cols), 0)
    feat = jnp.zeros((16, cols), F32)
    for i, v in enumerate((s1 * SLC_BLOCK, s2 * SLC_BLOCK, s3 * SLC_BLOCK, s1, s2, s3, r1, r2, r3)):
        feat = jnp.where(r16 == i, v, feat)
    feat = feat.astype(BF16)
    qa_base = jnp.concatenate([qt, feat, jnp.zeros((AUG - HEAD_DIM - 16, cols), BF16)], axis=0)

    n_col = lax.broadcasted_iota(jnp.int32, (nc, 1), 0)
    cmp_end = n_col * CMP_STRIDE + (CMP_BLOCK - 1)
    kca = kca_ref[...]
    r_hi, r_lo = [], []
    for c in _col_groups(cols):
        s = _dot(kca, qa_base[:, c])
        s = jnp.where((cmp_end <= tt[:, c]) & (n_col < n_cmp), s, NEG_INF)
        m = jnp.max(s, axis=0, keepdims=True)
        p = jnp.exp2(s - jnp.where(m == NEG_INF, 0.0, m))
        p_hi = p.astype(BF16)
        r_hi.append(_dot(vcov_ref[...], p_hi))
        r_lo.append(_dot(vcov_ref[HEAD_DIM:, :], (p - p_hi.astype(F32)).astype(BF16)))
    r_hi = jnp.concatenate(r_hi, axis=1)
    r_lo = jnp.concatenate(r_lo, axis=1)
    z = r_hi[HEAD_DIM:HEAD_DIM + 1] + r_lo[0:1]
    inv_z = 1.0 / jnp.where(z > 0, z, 1.0)

    span = WINDOW + tq
    k0w = pl.multiple_of(jnp.maximum(q0 - WINDOW, 0), tq)
    d = (lax.broadcasted_iota(jnp.int32, (span, 1), 0)
         - lax.broadcasted_iota(jnp.int32, (1, tq), 1))
    off = q0 - k0w
    bias = jnp.where((d <= off) & (d > off - WINDOW), 0.0, NEG_INF)
    acc_w = _attn_finish(_attn_tile(
        kwa_ref[pl.ds(k0w, span), :], vwt_ref[:, pl.ds(k0w, span)], qa_base, _attn_init(cols),
        lambda s, c: s + jnp.concatenate([bias] * ((c.stop - c.start) // tq), axis=1)))

    imp4 = (r_hi[_CMP_OV_ROW0:] + r_lo[_CMP_OV_ROW0 - HEAD_DIM:]) * inv_z
    imp = imp4[:, 0:tq]
    for j in range(1, J):
        imp = imp + imp4[:, j * tq:(j + 1) * tq]
    blk = lax.broadcasted_iota(jnp.int32, (SLC_BLOCK, tq), 0)
    t_q = q0 + lax.broadcasted_iota(jnp.int32, (1, tq), 1)
    imp = jnp.where(blk * SLC_BLOCK > t_q, NEG_INF, imp)
    imp = jnp.where((blk == t_q // SLC_BLOCK) | (blk == 0), jnp.inf, imp)
    blk_f = blk.astype(F32)
    taken = blk >= n_slc
    for _ in range(n_sel):
        vals = jnp.where(taken, NEG_INF, imp)
        best = jnp.max(vals, axis=0, keepdims=True)
        cand = (vals == best) & jnp.logical_not(taken)
        first = jnp.min(jnp.where(cand, blk_f, float(SLC_BLOCK)), axis=0, keepdims=True)
        taken = taken | (blk_f == first)
    not_sel = jnp.where(taken & (blk < n_slc), 0.0, -MASK_BIG).astype(BF16)
    qa_slc = jnp.concatenate([qt, feat, jnp.zeros((_SEL_LANE0 - 16, cols), BF16)]
                             + [jnp.concatenate([not_sel] * J, axis=1)], axis=0)

    def slc_tile(jt, carry, score_fn=None):
        k0 = pl.multiple_of(jt * tk, tk)
        return _attn_tile(ksa_ref[pl.ds(k0, tk), :], vst_ref[:, pl.ds(k0, tk)], qa_slc, carry, score_fn)

    jd = q0 // tk
    kpos = jd * tk + lax.broadcasted_iota(jnp.int32, (tk, 1), 0)
    carry = slc_tile(jd, _attn_init(cols), lambda s, c: jnp.where(kpos <= tt[:, c], s, NEG_INF))
    acc_s = _attn_finish(lax.fori_loop(0, jd, slc_tile, carry))

    gates_t = jnp.transpose(jax.nn.sigmoid(gt_ref[0]))
    gate = lambda c: jnp.concatenate([gates_t[3 * j + c:3 * j + c + 1, :] for j in range(J)], axis=1)
    o_t = ((gate(0) * inv_z) * r_hi[:HEAD_DIM]
           + (gate(1) / acc_s[HEAD_DIM:HEAD_DIM + 1]) * acc_s[:HEAD_DIM]
           + (gate(2) / acc_w[HEAD_DIM:HEAD_DIM + 1]) * acc_w[:HEAD_DIM]).astype(BF16)
    for j in range(J):
        for c in range(tq // LANES):
            o_ref[0, c * LANES:(c + 1) * LANES, j * HEAD_DIM:(j + 1) * HEAD_DIM] = _dot_nt(
                eye, o_t[:, j * tq + c * LANES:j * tq + (c + 1) * LANES]).astype(o_ref.dtype)


def nsa_attn(proj, gates, k_cmp, v_cmp, q_gain, k_gain, tq, tk):
    B, T, _ = proj.shape
    G, J = NSA_KV_HEADS, NSA_GROUP
    tk = min(tk, T)
    nc = k_cmp.shape[2]
    assert T % tk == 0 and tk % tq == 0 and T >= WINDOW + tq
    assert T // SLC_BLOCK <= LANES - _SEL_LANE0 and SLC_BLOCK <= LANES - _SEL_LANE0
    qcols = NSA_HEADS
    kv = lambda slab: pl.BlockSpec((1, T, HEAD_DIM), lambda b, g, i: (b, 0, qcols + slab * G + g))
    cmp_spec = pl.BlockSpec((1, 1, nc, HEAD_DIM), lambda b, g, i: (b, g, 0, 0))
    return pl.pallas_call(
        functools.partial(_nsa_kernel, tq=tq, tk=tk, n_sel=min(SLC_TOPK, T // SLC_BLOCK)),
        name="nsa_attn",
        grid=(B, G, T // tq),
        in_specs=[pl.BlockSpec((1, tq, J * HEAD_DIM), lambda b, g, i: (b, i, g)),
                  kv(2), kv(3), kv(4), kv(5), cmp_spec, cmp_spec,
                  pl.BlockSpec((1, tq, LANES), lambda b, g, i: (b, i, g)),
                  pl.BlockSpec((1, HEAD_DIM), lambda b, g, i: (0, 0)),
                  pl.BlockSpec((3, HEAD_DIM), lambda b, g, i: (0, 0))],
        out_specs=pl.BlockSpec((1, tq, J * HEAD_DIM), lambda b, g, i: (b, i, g)),
        out_shape=jax.ShapeDtypeStruct((B, T, NSA_HEADS * HEAD_DIM), BF16),
        scratch_shapes=[pltpu.VMEM((T, AUG), BF16), pltpu.VMEM((T, AUG), BF16),
                        pltpu.VMEM((VT_ROWS, T), BF16), pltpu.VMEM((VT_ROWS, T), BF16),
                        pltpu.VMEM((nc, AUG), BF16),
                        pltpu.VMEM((VT_ROWS + SLC_BLOCK, nc), BF16)],
        compiler_params=_params("parallel", "parallel", "arbitrary"),
    )(proj, proj, proj, proj, proj, k_cmp, v_cmp, gates, q_gain.reshape(1, HEAD_DIM), k_gain)


def _pad_cols(w, n):
    return jnp.pad(w, ((0, 0), (0, n - w.shape[1])))


def even_mixer(x, gain, w_in, w_out, f_bias, q_gain, k_gain, lb, o_gain):
    B, T, D = x.shape
    xf = x.reshape(B * T, D)
    fw = FOX_HEADS * HEAD_DIM
    w_main = jnp.concatenate([w_in[:, :3 * fw], w_in[:, 3 * fw + FOX_HEADS:]], axis=1).astype(BF16)
    w_gate = _pad_cols(w_in[:, 3 * fw:3 * fw + FOX_HEADS], LANES).astype(BF16)
    proj = rms_matmul(xf, gain, w_main, 1024, 1024).reshape(B, T, -1)
    gate = rms_matmul(xf, gain, w_gate, 512, LANES).reshape(B, T, LANES)
    c, kfeat = fox_gate(gate, _pad_cols(f_bias.reshape(1, FOX_HEADS).astype(F32), LANES), FOX_HEADS)
    c_rows = c[:, :, :FOX_HEADS].transpose(0, 2, 1).reshape(B * FOX_HEADS, 1, T)
    o_fox = fox_attn(proj, kfeat, c_rows, q_gain, k_gain, 1024)
    lb = lb.astype(F32).reshape(HGRN_HEADS, HEAD_DIM)
    o_h = hgrn(proj, 3 * FOX_HEADS, jnp.log(lb), jnp.log1p(-lb), 1.0 - lb, o_gain)
    w_out = w_out.astype(BF16)
    out = out_proj([o_fox.reshape(B * T, -1), o_h.reshape(B * T, -1)],
                   [w_out[:fw], w_out[fw:]], xf, 256)
    return out.reshape(B, T, D)


def nsa_mixer(x, gain, w_in, w_out, q_gain, k_gain, pe_k, w1_k, w2_k, pe_v, w1_v, w2_v):
    B, T, D = x.shape
    G, J = NSA_KV_HEADS, NSA_GROUP
    xf = x.reshape(B * T, D)
    qw = NSA_HEADS * HEAD_DIM
    kvw = G * HEAD_DIM
    n_main = qw + 6 * kvw
    w_main = w_in[:, :n_main].astype(BF16)
    w_gate = w_in[:, n_main:].reshape(D, G, 3 * J)
    w_gate = jnp.pad(w_gate, ((0, 0), (0, 0), (0, LANES - 3 * J))).reshape(D, G * LANES).astype(BF16)
    proj = rms_matmul(xf, gain, w_main, 1024, 1024).reshape(B, T, n_main)
    gates = rms_matmul(xf, gain, w_gate, 512, G * LANES).reshape(B, T, G * LANES)

    N = T // CMP_STRIDE

    def blocks(slab):
        t = proj[:, :, qw + slab * kvw: qw + (slab + 1) * kvw]
        return t.reshape(B, N, CMP_STRIDE, G, HEAD_DIM).transpose(0, 3, 1, 2, 4).reshape(
            B, G, N, CMP_STRIDE * HEAD_DIM)

    flat = lambda w1: w1.reshape(CMP_BLOCK * HEAD_DIM, -1).astype(BF16)
    k_cmp = compress(blocks(0), pe_k.reshape(1, -1), flat(w1_k), w2_k.astype(BF16), k_gain[0], True)
    v_cmp = compress(blocks(1), pe_v.reshape(1, -1), flat(w1_v), w2_v.astype(BF16), k_gain[0], False)
    o = nsa_attn(proj, gates, k_cmp, v_cmp, q_gain, k_gain, 512, 512)
    out = out_proj([o.reshape(B * T, qw)], [w_out.astype(BF16)], xf, 256)
    return out.reshape(B, T, D)


def kernel(x, mem, norm_mix, norm_mem, norm_ffn, mem_in_gain, even_w_in, even_w_out, fox_f_bias,
           fox_q_gain, fox_k_gain, hgrn_lb_logits, hgrn_o_gain, odd_w_in, odd_w_out, nsa_q_gain,
           nsa_k_gain, cmp_pe_k, cmp_w1_k, cmp_w2_k, cmp_pe_v, cmp_w1_v, cmp_w2_v, mem_wq, mem_wkv,
           mem_wo, mem_q_gain, mem_k_gain, ffn_w1, ffn_w3, ffn_w2):
    B, T, D = x.shape
    depth = norm_mix.shape[0]
    L = mem.shape[1]
    lb_cum = jnp.cumsum(jax.nn.softmax(hgrn_lb_logits.astype(F32), axis=0), axis=0)
    hgrn_lb = lb_cum - lb_cum[0:1]
    memf = mem.reshape(B * L, D)
    for layer in range(depth):
        if layer % 2 == 0:
            e = layer // 2
            x = even_mixer(x, norm_mix[layer], even_w_in[e], even_w_out[e], fox_f_bias[e],
                           fox_q_gain[e], fox_k_gain[e], hgrn_lb[e], hgrn_o_gain[e])
        else:
            o = layer // 2
            x = nsa_mixer(x, norm_mix[layer], odd_w_in[o], odd_w_out[o], nsa_q_gain[o], nsa_k_gain[o],
                          cmp_pe_k[o], cmp_w1_k[o], cmp_w2_k[o], cmp_pe_v[o], cmp_w1_v[o], cmp_w2_v[o])
        kv = rms_matmul(memf, mem_in_gain[layer], mem_wkv[layer].astype(BF16), 256, 1024)
        x = mem_attn(x, norm_mem[layer], mem_wq[layer].astype(BF16), kv.reshape(B, L, -1),
                     mem_q_gain[layer], mem_k_gain[layer], mem_wo[layer].astype(BF16), 256)
        x = ffn(x.reshape(B * T, D), norm_ffn[layer], ffn_w1[layer].astype(BF16),
                ffn_w3[layer].astype(BF16), ffn_w2[layer].astype(BF16), 512, 512).reshape(B, T, D)
    return x
```

```python
import functools

import jax
import jax.numpy as jnp
from jax import lax
from jax.experimental import pallas as pl
from jax.experimental.pallas import tpu as pltpu

F32 = jnp.float32
BF16 = jnp.bfloat16
NEG_INF = float("-inf")

EPS = 1e-6
HEAD_DIM = 128
ATTN_SCALE = HEAD_DIM ** -0.5
LANES = 128
FOX_HEADS = 8
HGRN_HEADS = 8
HGRN_BLOCK = 16
HGRN_CHUNK = 64
HGRN_SPAN = 512
HGRN_SAFE_LOG_DECAY = -80.0
NSA_HEADS = 16
NSA_KV_HEADS = 4
NSA_GROUP = NSA_HEADS // NSA_KV_HEADS
CMP_BLOCK = 32
CMP_STRIDE = 16
SLC_BLOCK = 64
SLC_TOPK = 16
WINDOW = 512
MEM_HEADS = 4
V7X_VMEM_LIMIT = 56 * 1024 * 1024
LOG2E = 1.4426950408889634
MASK_BIG = 1e30
AUG = 2 * LANES
VT_ROWS = HEAD_DIM + 16


def _params(*sem):
    return pltpu.CompilerParams(dimension_semantics=sem, vmem_limit_bytes=V7X_VMEM_LIMIT)


def _rms(x, gain):
    return x * lax.rsqrt(jnp.mean(x * x, axis=-1, keepdims=True) + EPS) * gain


def _dot(a, b):
    return jnp.dot(a, b, preferred_element_type=F32)


def _dot_nt(a, b):
    return lax.dot_general(a, b, (((1,), (1,)), ((), ())), preferred_element_type=F32)


def _dot_tn(a, b):
    return lax.dot_general(a, b, (((0,), (0,)), ((), ())), preferred_element_type=F32)


def _split3(x):
    x1 = x.astype(BF16)
    r1 = x - x1.astype(F32)
    x2 = r1.astype(BF16)
    return x1, x2, (r1 - x2.astype(F32)).astype(BF16)


def _exact_dot(sel, x):
    x1, x2, x3 = _split3(x)
    return _dot(sel, x1) + _dot(sel, x2) + _dot(sel, x3)


def _log_sigmoid(z):
    return jnp.minimum(z, 0.0) - jnp.log1p(jnp.exp(-jnp.abs(z)))


def _rms_matmul_kernel(x_ref, g_ref, w_ref, o_ref, hn_ref):
    @pl.when(pl.program_id(1) == 0)
    def _():
        hn_ref[...] = _rms(x_ref[...], g_ref[...]).astype(BF16)

    o_ref[...] = _dot(hn_ref[...], w_ref[...])


def rms_matmul(x, gain, w, tm, tn):
    M, K = x.shape
    N = w.shape[1]
    tm, tn = min(tm, M), min(tn, N)
    assert M % tm == 0 and N % tn == 0
    return pl.pallas_call(
        _rms_matmul_kernel,
        name="rms_matmul",
        grid=(M // tm, N // tn),
        in_specs=[pl.BlockSpec((tm, K), lambda i, j: (i, 0)),
                  pl.BlockSpec((1, K), lambda i, j: (0, 0)),
                  pl.BlockSpec((K, tn), lambda i, j: (0, j))],
        out_specs=pl.BlockSpec((tm, tn), lambda i, j: (i, j)),
        out_shape=jax.ShapeDtypeStruct((M, N), F32),
        scratch_shapes=[pltpu.VMEM((tm, K), BF16)],
        compiler_params=_params("parallel", "arbitrary"),
    )(x, gain.reshape(1, K), w)


def _out_proj_kernel(*refs, n_in):
    a_refs, w_refs = refs[:n_in], refs[n_in:2 * n_in]
    res_ref, o_ref = refs[2 * n_in], refs[2 * n_in + 1]
    acc = res_ref[...]
    for a_ref, w_ref in zip(a_refs, w_refs):
        acc = acc + _dot(a_ref[...], w_ref[...])
    o_ref[...] = acc


def out_proj(acts, weights, res, tm):
    M, N = res.shape
    tm = min(tm, M)
    assert M % tm == 0
    n_in = len(acts)
    in_specs = ([pl.BlockSpec((tm, a.shape[1]), lambda i: (i, 0)) for a in acts]
                + [pl.BlockSpec(w.shape, lambda i: (0, 0)) for w in weights]
                + [pl.BlockSpec((tm, N), lambda i: (i, 0))])
    return pl.pallas_call(
        functools.partial(_out_proj_kernel, n_in=n_in),
        name="out_proj",
        grid=(M // tm,),
        in_specs=in_specs,
        out_specs=pl.BlockSpec((tm, N), lambda i: (i, 0)),
        out_shape=jax.ShapeDtypeStruct((M, N), F32),
        compiler_params=_params("parallel"),
    )(*acts, *weights, res)


def _ffn_kernel(x_ref, g_ref, w1_ref, w3_ref, w2_ref, o_ref, hn_ref):
    f = pl.program_id(1)

    @pl.when(f == 0)
    def _():
        x = x_ref[...]
        hn_ref[...] = _rms(x, g_ref[...]).astype(BF16)
        o_ref[...] = x

    h = hn_ref[...]
    a = _dot(h, w1_ref[...])
    b = _dot(h, w3_ref[...])
    u = (a * jax.nn.sigmoid(a) * b).astype(BF16)
    o_ref[...] += _dot(u, w2_ref[...])


def ffn(x, gain, w1, w3, w2, tm, tf):
    M, D = x.shape
    Fh = w1.shape[1]
    tm, tf = min(tm, M), min(tf, Fh)
    assert M % tm == 0 and Fh % tf == 0
    return pl.pallas_call(
        _ffn_kernel,
        name="ffn",
        grid=(M // tm, Fh // tf),
        in_specs=[pl.BlockSpec((tm, D), lambda i, f: (i, 0)),
                  pl.BlockSpec((1, D), lambda i, f: (0, 0)),
                  pl.BlockSpec((D, tf), lambda i, f: (0, f)),
                  pl.BlockSpec((D, tf), lambda i, f: (0, f)),
                  pl.BlockSpec((tf, D), lambda i, f: (f, 0))],
        out_specs=pl.BlockSpec((tm, D), lambda i, f: (i, 0)),
        out_shape=jax.ShapeDtypeStruct((M, D), F32),
        scratch_shapes=[pltpu.VMEM((tm, D), BF16)],
        compiler_params=_params("parallel", "arbitrary"),
    )(x, gain.reshape(1, D), w1, w3, w2)


def _mem_attn_kernel(x_ref, g_ref, wq_ref, kv_ref, qg_ref, kg_ref, wo_ref, o_ref, *, heads):
    x = x_ref[0]
    h = _rms(x, g_ref[...]).astype(BF16)
    q = _dot(h, wq_ref[...])
    kv = kv_ref[0]
    w = heads * HEAD_DIM
    outs = []
    for hh in range(heads):
        sl = slice(hh * HEAD_DIM, (hh + 1) * HEAD_DIM)
        qh = (_rms(q[:, sl], qg_ref[...]) * ATTN_SCALE).astype(BF16)
        kh = _rms(kv[:, sl], kg_ref[...]).astype(BF16)
        vh = kv[:, w + hh * HEAD_DIM: w + (hh + 1) * HEAD_DIM].astype(BF16)
        s = _dot_nt(qh, kh)
        p = jnp.exp(s - jnp.max(s, axis=-1, keepdims=True))
        z = jnp.sum(p, axis=-1, keepdims=True)
        outs.append((_dot(p.astype(BF16), vh) / z).astype(BF16))
    o = jnp.concatenate(outs, axis=-1)
    o_ref[0] = x + _dot(o, wo_ref[...])


def mem_attn(x, gain, wq, kv, q_gain, k_gain, wo, tm):
    B, T, D = x.shape
    L, W2 = kv.shape[1], kv.shape[2]
    W = W2 // 2
    tm = min(tm, T)
    assert T % tm == 0
    return pl.pallas_call(
        functools.partial(_mem_attn_kernel, heads=W // HEAD_DIM),
        name="mem_attn",
        grid=(B, T // tm),
        in_specs=[pl.BlockSpec((1, tm, D), lambda b, i: (b, i, 0)),
                  pl.BlockSpec((1, D), lambda b, i: (0, 0)),
                  pl.BlockSpec((D, W), lambda b, i: (0, 0)),
                  pl.BlockSpec((1, L, W2), lambda b, i: (b, 0, 0)),
                  pl.BlockSpec((1, HEAD_DIM), lambda b, i: (0, 0)),
                  pl.BlockSpec((1, HEAD_DIM), lambda b, i: (0, 0)),
                  pl.BlockSpec((W, D), lambda b, i: (0, 0))],
        out_specs=pl.BlockSpec((1, tm, D), lambda b, i: (b, i, 0)),
        out_shape=jax.ShapeDtypeStruct((B, T, D), F32),
        compiler_params=_params("parallel", "parallel"),
    )(x, gain.reshape(1, D), wq, kv, q_gain.reshape(1, HEAD_DIM), k_gain.reshape(1, HEAD_DIM), wo)


def _eye(n):
    return (lax.broadcasted_iota(jnp.int32, (n, n), 0)
            == lax.broadcasted_iota(jnp.int32, (n, n), 1)).astype(BF16)


def _ones_row(rows, n):
    return (lax.broadcasted_iota(jnp.int32, (rows, n), 0) == 0).astype(BF16)


def _split3_f32(x):
    return tuple(t.astype(F32) for t in _split3(x))


COL_GROUPS = 1


def _col_groups(n):
    w = n // COL_GROUPS
    return [slice(i * w, (i + 1) * w) for i in range(COL_GROUPS)]


def _attn_init(n):
    return tuple((jnp.full((1, c.stop - c.start), NEG_INF, F32), jnp.zeros((VT_ROWS, c.stop - c.start), F32))
                 for c in _col_groups(n))


def _attn_tile(k_tile, vt_tile, qa, carry, score_fn=None):
    out = []
    for (m, acc), c in zip(carry, _col_groups(qa.shape[1])):
        s = _dot(k_tile, qa[:, c])
        if score_fn is not None:
            s = score_fn(s, c)
        m_new = jnp.maximum(m, jnp.max(s, axis=0, keepdims=True))
        p = jnp.exp2(s - m_new).astype(BF16)
        out.append((m_new, jnp.exp2(m - m_new) * acc + _dot(vt_tile, p)))
    return tuple(out)


def _attn_finish(carry):
    return jnp.concatenate([acc for _, acc in carry], axis=1)


def _fox_gate_kernel(g_ref, b_ref, c_ref, kf_ref, *, heads):
    T = g_ref.shape[1]
    r = lax.broadcasted_iota(jnp.int32, (LANES, LANES), 0)
    lane = lax.broadcasted_iota(jnp.int32, (LANES, LANES), 1)
    tri = (r >= lane).astype(BF16)

    def body(i, carry):
        rows = pl.ds(pl.multiple_of(i * LANES, LANES), LANES)
        cs = _exact_dot(tri, _log_sigmoid(g_ref[0, rows, :] + b_ref[...])) + carry
        c_ref[0, rows, :] = cs
        c2 = cs * LOG2E
        for h in range(heads):
            d1, d2, d3 = _split3_f32(c2[:, h:h + 1])
            feat = jnp.where(lane < 3, 1.0,
                             jnp.where(lane == 3, -d1, jnp.where(lane == 4, -d2,
                                                                 jnp.where(lane == 5, -d3, 0.0))))
            kf_ref[0, h, rows, :] = feat.astype(BF16)
        return cs[LANES - 1:LANES, :]

    lax.fori_loop(0, T // LANES, body, jnp.zeros((1, LANES), F32))


def fox_gate(g, bias, heads):
    B, T, _ = g.shape
    return pl.pallas_call(
        functools.partial(_fox_gate_kernel, heads=heads),
        name="fox_gate",
        grid=(B,),
        in_specs=[pl.BlockSpec((1, T, LANES), lambda b: (b, 0, 0)),
                  pl.BlockSpec((1, LANES), lambda b: (0, 0))],
        out_specs=[pl.BlockSpec((1, T, LANES), lambda b: (b, 0, 0)),
                   pl.BlockSpec((1, heads, T, LANES), lambda b: (b, 0, 0, 0))],
        out_shape=[jax.ShapeDtypeStruct((B, T, LANES), F32),
                   jax.ShapeDtypeStruct((B, heads, T, LANES), BF16)],
        compiler_params=_params("parallel"),
    )(g, bias)


def _fox_attn_kernel(q_ref, k_ref, v_ref, kf_ref, c_ref, qg_ref, kg_ref, o_ref, ka_ref, vt_ref, *, tq):
    qi = pl.program_id(2)
    T = k_ref.shape[1]
    eye = _eye(HEAD_DIM)

    @pl.when(qi == 0)
    def _():
        ka_ref[:, :HEAD_DIM] = _rms(k_ref[0], kg_ref[...]).astype(BF16)
        ka_ref[:, HEAD_DIM:] = kf_ref[0, 0]
        vt_ref[:HEAD_DIM, :] = _dot_nt(eye, v_ref[0].astype(BF16)).astype(BF16)
        vt_ref[HEAD_DIM:, :] = _ones_row(VT_ROWS - HEAD_DIM, T)

    q0 = pl.multiple_of(qi * tq, tq)
    qn = (_rms(q_ref[0], qg_ref[...]) * (ATTN_SCALE * LOG2E)).astype(BF16)
    qt = _dot_nt(eye, qn).astype(BF16)
    c1, c2, c3 = _split3_f32(c_ref[0, :, pl.ds(q0, tq)] * LOG2E)
    r = lax.broadcasted_iota(jnp.int32, (16, tq), 0)
    feat = jnp.where(r == 0, c1, jnp.where(r == 1, c2, jnp.where(r == 2, c3,
                                                                 jnp.where(r < 6, 1.0, 0.0))))
    qa = jnp.concatenate([qt, feat.astype(BF16), jnp.zeros((AUG - HEAD_DIM - 16, tq), BF16)], axis=0)

    kpos = lax.broadcasted_iota(jnp.int32, (tq, 1), 0)
    tpos = lax.broadcasted_iota(jnp.int32, (1, tq), 1)

    def tile(j, carry, score_fn=None):
        k0 = pl.multiple_of(j * tq, tq)
        return _attn_tile(ka_ref[pl.ds(k0, tq), :], vt_ref[:, pl.ds(k0, tq)], qa, carry, score_fn)

    carry = tile(qi, _attn_init(tq), lambda s, c: jnp.where(kpos <= tpos[:, c], s, NEG_INF))
    acc = _attn_finish(lax.fori_loop(0, qi, tile, carry))
    ot = (acc[:HEAD_DIM] / acc[HEAD_DIM:HEAD_DIM + 1]).astype(BF16)
    for c in range(tq // LANES):
        cs = slice(c * LANES, (c + 1) * LANES)
        o_ref[0, cs, :] = _dot_nt(eye, ot[:, cs]).astype(o_ref.dtype)


def fox_attn(proj, kfeat, c_rows, q_gain, k_gain, tq):
    B, T, _ = proj.shape
    H = FOX_HEADS
    tq = min(tq, T)
    assert T % tq == 0 and tq % LANES == 0
    return pl.pallas_call(
        functools.partial(_fox_attn_kernel, tq=tq),
        name="fox_attn",
        grid=(B, H, T // tq),
        in_specs=[pl.BlockSpec((1, tq, HEAD_DIM), lambda b, h, i: (b, i, h)),
                  pl.BlockSpec((1, T, HEAD_DIM), lambda b, h, i: (b, 0, H + h)),
                  pl.BlockSpec((1, T, HEAD_DIM), lambda b, h, i: (b, 0, 2 * H + h)),
                  pl.BlockSpec((1, 1, T, LANES), lambda b, h, i: (b, h, 0, 0)),
                  pl.BlockSpec((1, 1, T), lambda b, h, i: (b * H + h, 0, 0)),
                  pl.BlockSpec((1, HEAD_DIM), lambda b, h, i: (0, 0)),
                  pl.BlockSpec((1, HEAD_DIM), lambda b, h, i: (0, 0))],
        out_specs=pl.BlockSpec((1, tq, HEAD_DIM), lambda b, h, i: (b, i, h)),
        out_shape=jax.ShapeDtypeStruct((B, T, H * HEAD_DIM), BF16),
        scratch_shapes=[pltpu.VMEM((T, AUG), BF16), pltpu.VMEM((VT_ROWS, T), BF16)],
        compiler_params=_params("parallel", "parallel", "arbitrary"),
    )(proj, proj, proj, kfeat, c_rows, q_gain.reshape(1, HEAD_DIM), k_gain.reshape(1, HEAD_DIM))


def _hgrn_kernel(q_ref, f_ref, i_ref, g_ref, la_ref, l1_ref, oml_ref, og_ref, o_ref, *, bs, chunk, span):
    T = q_ref.shape[1]
    R = LANES
    ri = lax.broadcasted_iota(jnp.int32, (R, R), 0)
    ci = lax.broadcasted_iota(jnp.int32, (R, R), 1)

    def prefix_sel(n):
        return (((ri // n) == (ci // n)) & (ri >= ci)).astype(BF16)

    s_idx = lax.broadcasted_iota(jnp.int32, (bs, HEAD_DIM), 0)
    tril = (lax.broadcasted_iota(jnp.int32, (chunk, chunk), 0)
            >= lax.broadcasted_iota(jnp.int32, (chunk, chunk), 1))

    def gates(row0, n):
        rows = pl.ds(row0, n)
        z = f_ref[0, rows, :]
        a = la_ref[0]
        y = l1_ref[0] + _log_sigmoid(z)
        log_f = jnp.maximum(a, y) + jnp.log1p(jnp.exp(-jnp.abs(a - y)))
        hq = q_ref[0, rows, :]
        return log_f, oml_ref[0] * jax.nn.sigmoid(-z), hq * jax.nn.sigmoid(hq), i_ref[0, rows, :]

    def prefixes(log_f, n):
        sel = prefix_sel(n)
        pieces = _split3(log_f)
        b = jnp.concatenate([sum(_dot(sel, p[g * R:(g + 1) * R]) for p in pieces)
                             for g in range(log_f.shape[0] // R)], axis=0)
        bt = jnp.concatenate([jnp.broadcast_to(b[(u + 1) * n - 1:(u + 1) * n, :], (n, HEAD_DIM))
                              for u in range(log_f.shape[0] // n)], axis=0)
        return b, bt

    def finish(row0, n, o):
        rows = pl.ds(row0, n)
        hg = g_ref[0, rows, :]
        o_ref[0, rows, :] = (_rms(o, og_ref[...]) * (hg * jax.nn.sigmoid(hg))).astype(o_ref.dtype)

    def matmul_path(row0, st, kk, qq, vv, b, bt):
        qe = (qq * jnp.exp(b)).astype(BF16)
        k_inv = (kk * jnp.exp(-b)).astype(BF16)
        ke = (kk * jnp.exp(bt - b)).astype(BF16)
        dec = jnp.exp(bt)
        vb = vv.astype(BF16)
        sls = [slice(u * chunk, (u + 1) * chunk) for u in range(span // chunk)]
        scores = [jnp.where(tril, _dot_nt(qe[sl], k_inv[sl]), 0.0).astype(BF16) for sl in sls]
        intra = [_dot(a, vb[sl]) for a, sl in zip(scores, sls)]
        update = [_dot_tn(vb[sl], ke[sl]) for sl in sls]
        outs = []
        for sl, o_in, upd in zip(sls, intra, update):
            outs.append(_dot_nt(qe[sl], st.astype(BF16)) + o_in)
            st = dec[sl.start:sl.start + 1, :] * st + upd
        finish(row0, span, jnp.concatenate(outs, axis=0))
        return st

    def pairwise_step(i, st, base):
        row0 = pl.multiple_of(base + i * R, R)
        log_f, kk, qq, vv = gates(row0, R)
        b, bt = prefixes(log_f, bs)
        qe = (qq * jnp.exp(b)).astype(BF16)
        ke = (kk * jnp.exp(bt - b)).astype(BF16)
        dec = jnp.exp(bt)
        vb = vv.astype(BF16)
        outs = []
        for u in range(R // bs):
            sl = slice(u * bs, (u + 1) * bs)
            bu, vu, qk, ku = b[sl], vv[sl], qq[sl], kk[sl]
            intra = []
            for t in range(bs):
                w = jnp.exp(jnp.where(s_idx <= t, bu[t:t + 1, :] - bu, NEG_INF))
                a_t = jnp.sum(w * (qk[t:t + 1, :] * ku), axis=-1, keepdims=True)
                intra.append(jnp.sum(a_t * vu, axis=0, keepdims=True))
            outs.append(_dot_nt(qe[sl], st.astype(BF16)) + jnp.concatenate(intra, axis=0))
            st = dec[u * bs:u * bs + 1, :] * st + _dot_tn(vb[sl], ke[sl])
        finish(row0, R, jnp.concatenate(outs, axis=0))
        return st

    def step(i, st):
        row0 = pl.multiple_of(i * span, span)
        log_f, kk, qq, vv = gates(row0, span)
        b, bt = prefixes(log_f, chunk)
        return lax.cond(
            jnp.min(bt) >= HGRN_SAFE_LOG_DECAY,
            lambda st: matmul_path(row0, st, kk, qq, vv, b, bt),
            lambda st: lax.fori_loop(0, span // R, functools.partial(pairwise_step, base=row0), st),
            st)

    lax.fori_loop(0, T // span, step, jnp.zeros((HEAD_DIM, HEAD_DIM), F32))


def hgrn(proj, col0, log_lb, log1m_lb, one_m_lb, o_gain):
    B, T, _ = proj.shape
    H = HGRN_HEADS
    row = lambda off: pl.BlockSpec((1, T, HEAD_DIM), lambda b, h: (b, 0, col0 + off * H + h))
    par = pl.BlockSpec((1, 1, HEAD_DIM), lambda b, h: (h, 0, 0))
    log_lb, log1m_lb, one_m_lb = (p.reshape(H, 1, HEAD_DIM) for p in (log_lb, log1m_lb, one_m_lb))
    return pl.pallas_call(
        functools.partial(_hgrn_kernel, bs=HGRN_BLOCK, chunk=HGRN_CHUNK, span=min(HGRN_SPAN, T)),
        name="hgrn",
        grid=(B, H),
        in_specs=[row(0), row(1), row(2), row(3), par, par, par,
                  pl.BlockSpec((1, HEAD_DIM), lambda b, h: (0, 0))],
        out_specs=pl.BlockSpec((1, T, HEAD_DIM), lambda b, h: (b, 0, h)),
        out_shape=jax.ShapeDtypeStruct((B, T, H * HEAD_DIM), BF16),
        compiler_params=_params("parallel", "parallel"),
    )(proj, proj, proj, proj, log_lb, log1m_lb, one_m_lb, o_gain.reshape(1, HEAD_DIM))


def _compress_kernel(t_ref, pe_ref, w1_ref, w2_ref, g_ref, o_ref, *, normalize, n_cmp):
    half = t_ref.shape[3]
    t = t_ref[0, 0]
    N = t.shape[0]
    top = _dot((t + pe_ref[:, :half]).astype(BF16), w1_ref[:half, :])
    bot = _dot((t + pe_ref[:, half:]).astype(BF16), w1_ref[half:, :])
    hid = top + pltpu.roll(bot, N - 1, 0)
    out = _dot(jax.nn.gelu(hid).astype(BF16), w2_ref[...])
    if normalize:
        out = _rms(out, g_ref[...])
    valid = lax.broadcasted_iota(jnp.int32, out.shape, 0) < n_cmp
    o_ref[0, 0] = jnp.where(valid, out, 0.0)


def compress(t, pe, w1, w2, gain, normalize):
    B, G, N, half = t.shape
    Hc = w1.shape[1]
    return pl.pallas_call(
        functools.partial(_compress_kernel, normalize=normalize, n_cmp=N - 1),
        name="nsa_compress",
        grid=(B, G),
        in_specs=[pl.BlockSpec((1, 1, N, half), lambda b, g: (b, g, 0, 0)),
                  pl.BlockSpec((1, 2 * half), lambda b, g: (0, 0)),
                  pl.BlockSpec((2 * half, Hc), lambda b, g: (0, 0)),
                  pl.BlockSpec((Hc, HEAD_DIM), lambda b, g: (0, 0)),
                  pl.BlockSpec((1, HEAD_DIM), lambda b, g: (0, 0))],
        out_specs=pl.BlockSpec((1, 1, N, HEAD_DIM), lambda b, g: (b, g, 0, 0)),
        out_shape=jax.ShapeDtypeStruct((B, G, N, HEAD_DIM), F32),
        compiler_params=_params("parallel", "parallel"),
    )(t, pe, w1, w2, gain.reshape(1, HEAD_DIM))


_SEL_LANE0 = 64
_CMP_OV_ROW0 = VT_ROWS


def _nsa_kernel(q_ref, ks_ref, vs_ref, kw_ref, vw_ref, kc_ref, vc_ref, gt_ref, qg_ref, kg_ref,
                o_ref, ksa_ref, kwa_ref, vst_ref, vwt_ref, kca_ref, vcov_ref, *, tq, tk, n_sel):
    g = pl.program_id(1)
    qi = pl.program_id(2)
    T = ks_ref.shape[1]
    J = NSA_GROUP
    n_slc = T // SLC_BLOCK
    nc = kc_ref.shape[2]
    n_cmp = nc - 1
    cols = J * tq
    eye = _eye(HEAD_DIM)

    @pl.when(qi == 0)
    def _():
        row = lax.broadcasted_iota(jnp.int32, (T, LANES), 0)
        lane = lax.broadcasted_iota(jnp.int32, (T, LANES), 1)
        blk = row // SLC_BLOCK
        feats = jnp.where(lane < 3, blk,
                          jnp.where(lane < 6, row % SLC_BLOCK,
                                    jnp.where(lane < 9, 1,
                                              jnp.where(lane - _SEL_LANE0 == blk, 1, 0))))
        feats = feats.astype(F32).astype(BF16)
        ksa_ref[:, :HEAD_DIM] = _rms(ks_ref[0], kg_ref[1:2, :]).astype(BF16)
        ksa_ref[:, HEAD_DIM:] = feats
        kwa_ref[:, :HEAD_DIM] = _rms(kw_ref[0], kg_ref[2:3, :]).astype(BF16)
        kwa_ref[:, HEAD_DIM:] = feats
        vst_ref[:HEAD_DIM, :] = _dot_nt(eye, vs_ref[0].astype(BF16)).astype(BF16)
        vst_ref[HEAD_DIM:, :] = _ones_row(VT_ROWS - HEAD_DIM, T)
        vwt_ref[:HEAD_DIM, :] = _dot_nt(eye, vw_ref[0].astype(BF16)).astype(BF16)
        vwt_ref[HEAD_DIM:, :] = _ones_row(VT_ROWS - HEAD_DIM, T)
        n_r = lax.broadcasted_iota(jnp.int32, (nc, LANES), 0)
        n_l = lax.broadcasted_iota(jnp.int32, (nc, LANES), 1)
        per = SLC_BLOCK // CMP_STRIDE
        cfe = jnp.where(n_l < 3, (n_r // per).astype(F32),
                        jnp.where(n_l < 6, (CMP_STRIDE * (n_r % per)).astype(F32) + 0.5 * (CMP_BLOCK - 1),
                                  jnp.where(n_l < 9, 1.0, 0.0)))
        kca_ref[:, :HEAD_DIM] = kc_ref[0, 0].astype(BF16)
        kca_ref[:, HEAD_DIM:] = cfe.astype(BF16)
        vcov_ref[:HEAD_DIM, :] = _dot_nt(eye, vc_ref[0, 0].astype(BF16)).astype(BF16)
        vcov_ref[HEAD_DIM:_CMP_OV_ROW0, :] = _ones_row(VT_ROWS - HEAD_DIM, nc)
        m_r = lax.broadcasted_iota(jnp.int32, (SLC_BLOCK, nc), 0)
        n_c = lax.broadcasted_iota(jnp.int32, (SLC_BLOCK, nc), 1)
        vcov_ref[_CMP_OV_ROW0:, :] = ((n_c * CMP_STRIDE <= m_r * SLC_BLOCK + (SLC_BLOCK - 1))
                                      & (n_c * CMP_STRIDE + (CMP_BLOCK - 1) >= m_r * SLC_BLOCK)
                                      & (n_c < n_cmp) & (m_r < n_slc)).astype(BF16)

    q0 = pl.multiple_of(qi * tq, tq)
    qraw = q_ref[0]
    qt = jnp.concatenate(
        [_dot_nt(eye, (_rms(qraw[:, j * HEAD_DIM:(j + 1) * HEAD_DIM], qg_ref[...])
                       * (ATTN_SCALE * LOG2E)).astype(BF16)).astype(BF16)
         for j in range(J)], axis=1)

    col = lax.broadcasted_iota(jnp.int32, (1, cols), 1)
    tt = q0 + col % tq
    slope2 = jnp.exp2(-0.5 * (g * J + col // tq + 1).astype(F32)) * LOG2E
    s1, s2, s3 = _split3_f32(slope2)
    r1, r2, r3 = _split3_f32(-slope2 * tt.astype(F32))
    r16 = lax.broadcasted_iota(jnp.int32, (16, cols), 0)
    feat = jnp.zeros((16, cols), F32)
    for i, v in enumerate((s1 * SLC_BLOCK, s2 * SLC_BLOCK, s3 * SLC_BLOCK, s1, s2, s3, r1, r2, r3)):
        feat = jnp.where(r16 == i, v, feat)
    feat = feat.astype(BF16)
    qa_base = jnp.concatenate([qt, feat, jnp.zeros((AUG - HEAD_DIM - 16, cols), BF16)], axis=0)

    n_col = lax.broadcasted_iota(jnp.int32, (nc, 1), 0)
    cmp_end = n_col * CMP_STRIDE + (CMP_BLOCK - 1)
    kca = kca_ref[...]
    r_hi, r_lo = [], []
    for c in _col_groups(cols):
        s = _dot(kca, qa_base[:, c])
        s = jnp.where((cmp_end <= tt[:, c]) & (n_col < n_cmp), s, NEG_INF)
        m = jnp.max(s, axis=0, keepdims=True)
        p = jnp.exp2(s - jnp.where(m == NEG_INF, 0.0, m))
        p_hi = p.astype(BF16)
        r_hi.append(_dot(vcov_ref[...], p_hi))
        r_lo.append(_dot(vcov_ref[HEAD_DIM:, :], (p - p_hi.astype(F32)).astype(BF16)))
    r_hi = jnp.concatenate(r_hi, axis=1)
    r_lo = jnp.concatenate(r_lo, axis=1)
    z = r_hi[HEAD_DIM:HEAD_DIM + 1] + r_lo[0:1]
    inv_z = 1.0 / jnp.where(z > 0, z, 1.0)

    span = WINDOW + tq
    k0w = pl.multiple_of(jnp.maximum(q0 - WINDOW, 0), tq)
    d = (lax.broadcasted_iota(jnp.int32, (span, 1), 0)
         - lax.broadcasted_iota(jnp.int32, (1, tq), 1))
    off = q0 - k0w
    bias = jnp.where((d <= off) & (d > off - WINDOW), 0.0, NEG_INF)
    acc_w = _attn_finish(_attn_tile(
        kwa_ref[pl.ds(k0w, span), :], vwt_ref[:, pl.ds(k0w, span)], qa_base, _attn_init(cols),
        lambda s, c: s + jnp.concatenate([bias] * ((c.stop - c.start) // tq), axis=1)))

    imp4 = (r_hi[_CMP_OV_ROW0:] + r_lo[_CMP_OV_ROW0 - HEAD_DIM:]) * inv_z
    imp = imp4[:, 0:tq]
    for j in range(1, J):
        imp = imp + imp4[:, j * tq:(j + 1) * tq]
    blk = lax.broadcasted_iota(jnp.int32, (SLC_BLOCK, tq), 0)
    t_q = q0 + lax.broadcasted_iota(jnp.int32, (1, tq), 1)
    imp = jnp.where(blk * SLC_BLOCK > t_q, NEG_INF, imp)
    imp = jnp.where((blk == t_q // SLC_BLOCK) | (blk == 0), jnp.inf, imp)
    blk_f = blk.astype(F32)
    taken = blk >= n_slc
    for _ in range(n_sel):
        vals = jnp.where(taken, NEG_INF, imp)
        best = jnp.max(vals, axis=0, keepdims=True)
        cand = (vals == best) & jnp.logical_not(taken)
        first = jnp.min(jnp.where(cand, blk_f, float(SLC_BLOCK)), axis=0, keepdims=True)
        taken = taken | (blk_f == first)
    not_sel = jnp.where(taken & (blk < n_slc), 0.0, -MASK_BIG).astype(BF16)
    qa_slc = jnp.concatenate([qt, feat, jnp.zeros((_SEL_LANE0 - 16, cols), BF16)]
                             + [jnp.concatenate([not_sel] * J, axis=1)], axis=0)

    def slc_tile(jt, carry, score_fn=None):
        k0 = pl.multiple_of(jt * tk, tk)
        return _attn_tile(ksa_ref[pl.ds(k0, tk), :], vst_ref[:, pl.ds(k0, tk)], qa_slc, carry, score_fn)

    jd = q0 // tk
    kpos = jd * tk + lax.broadcasted_iota(jnp.int32, (tk, 1), 0)
    carry = slc_tile(jd, _attn_init(cols), lambda s, c: jnp.where(kpos <= tt[:, c], s, NEG_INF))
    acc_s = _attn_finish(lax.fori_loop(0, jd, slc_tile, carry))

    gates_t = jnp.transpose(jax.nn.sigmoid(gt_ref[0]))
    gate = lambda c: jnp.concatenate([gates_t[3 * j + c:3 * j + c + 1, :] for j in range(J)], axis=1)
    o_t = ((gate(0) * inv_z) * r_hi[:HEAD_DIM]
           + (gate(1) / acc_s[HEAD_DIM:HEAD_DIM + 1]) * acc_s[:HEAD_DIM]
           + (gate(2) / acc_w[HEAD_DIM:HEAD_DIM + 1]) * acc_w[:HEAD_DIM]).astype(BF16)
    for j in range(J):
        for c in range(tq // LANES):
            o_ref[0, c * LANES:(c + 1) * LANES, j * HEAD_DIM:(j + 1) * HEAD_DIM] = _dot_nt(
                eye, o_t[:, j * tq + c * LANES:j * tq + (c + 1) * LANES]).astype(o_ref.dtype)


def nsa_attn(proj, gates, k_cmp, v_cmp, q_gain, k_gain, tq, tk):
    B, T, _ = proj.shape
    G, J = NSA_KV_HEADS, NSA_GROUP
    tk = min(tk, T)
    nc = k_cmp.shape[2]
    assert T % tk == 0 and tk % tq == 0 and T >= WINDOW + tq
    assert T // SLC_BLOCK <= LANES - _SEL_LANE0 and SLC_BLOCK <= LANES - _SEL_LANE0
    qcols = NSA_HEADS
    kv = lambda slab: pl.BlockSpec((1, T, HEAD_DIM), lambda b, g, i: (b, 0, qcols + slab * G + g))
    cmp_spec = pl.BlockSpec((1, 1, nc, HEAD_DIM), lambda b, g, i: (b, g, 0, 0))
    return pl.pallas_call(
        functools.partial(_nsa_kernel, tq=tq, tk=tk, n_sel=min(SLC_TOPK, T // SLC_BLOCK)),
        name="nsa_attn",
        grid=(B, G, T // tq),
        in_specs=[pl.BlockSpec((1, tq, J * HEAD_DIM), lambda b, g, i: (b, i, g)),
                  kv(2), kv(3), kv(4), kv(5), cmp_spec, cmp_spec,
                  pl.BlockSpec((1, tq, LANES), lambda b, g, i: (b, i, g)),
                  pl.BlockSpec((1, HEAD_DIM), lambda b, g, i: (0, 0)),
                  pl.BlockSpec((3, HEAD_DIM), lambda b, g, i: (0, 0))],
        out_specs=pl.BlockSpec((1, tq, J * HEAD_DIM), lambda b, g, i: (b, i, g)),
        out_shape=jax.ShapeDtypeStruct((B, T, NSA_HEADS * HEAD_DIM), BF16),
        scratch_shapes=[pltpu.VMEM((T, AUG), BF16), pltpu.VMEM((T, AUG), BF16),
                        pltpu.VMEM((VT_ROWS, T), BF16), pltpu.VMEM((VT_ROWS, T), BF16),
                        pltpu.VMEM((nc, AUG), BF16),
                        pltpu.VMEM((VT_ROWS + SLC_BLOCK, nc), BF16)],
        compiler_params=_params("parallel", "parallel", "arbitrary"),
    )(proj, proj, proj, proj, proj, k_cmp, v_cmp, gates, q_gain.reshape(1, HEAD_DIM), k_gain)


def _pad_cols(w, n):
    return jnp.pad(w, ((0, 0), (0, n - w.shape[1])))


def even_mixer(x, gain, w_in, w_out, f_bias, q_gain, k_gain, lb, o_gain):
    B, T, D = x.shape
    xf = x.reshape(B * T, D)
    fw = FOX_HEADS * HEAD_DIM
    w_main = jnp.concatenate([w_in[:, :3 * fw], w_in[:, 3 * fw + FOX_HEADS:]], axis=1).astype(BF16)
    w_gate = _pad_cols(w_in[:, 3 * fw:3 * fw + FOX_HEADS], LANES).astype(BF16)
    proj = rms_matmul(xf, gain, w_main, 1024, 1024).reshape(B, T, -1)
    gate = rms_matmul(xf, gain, w_gate, 512, LANES).reshape(B, T, LANES)
    c, kfeat = fox_gate(gate, _pad_cols(f_bias.reshape(1, FOX_HEADS).astype(F32), LANES), FOX_HEADS)
    c_rows = c[:, :, :FOX_HEADS].transpose(0, 2, 1).reshape(B * FOX_HEADS, 1, T)
    o_fox = fox_attn(proj, kfeat, c_rows, q_gain, k_gain, 1024)
    lb = lb.astype(F32).reshape(HGRN_HEADS, HEAD_DIM)
    o_h = hgrn(proj, 3 * FOX_HEADS, jnp.log(lb), jnp.log1p(-lb), 1.0 - lb, o_gain)
    w_out = w_out.astype(BF16)
    out = out_proj([o_fox.reshape(B * T, -1), o_h.reshape(B * T, -1)],
                   [w_out[:fw], w_out[fw:]], xf, 256)
    return out.reshape(B, T, D)


def nsa_mixer(x, gain, w_in, w_out, q_gain, k_gain, pe_k, w1_k, w2_k, pe_v, w1_v, w2_v):
    B, T, D = x.shape
    G, J = NSA_KV_HEADS, NSA_GROUP
    xf = x.reshape(B * T, D)
    qw = NSA_HEADS * HEAD_DIM
    kvw = G * HEAD_DIM
    n_main = qw + 6 * kvw
    w_main = w_in[:, :n_main].astype(BF16)
    w_gate = w_in[:, n_main:].reshape(D, G, 3 * J)
    w_gate = jnp.pad(w_gate, ((0, 0), (0, 0), (0, LANES - 3 * J))).reshape(D, G * LANES).astype(BF16)
    proj = rms_matmul(xf, gain, w_main, 1024, 1024).reshape(B, T, n_main)
    gates = rms_matmul(xf, gain, w_gate, 512, G * LANES).reshape(B, T, G * LANES)

    N = T // CMP_STRIDE

    def blocks(slab):
        t = proj[:, :, qw + slab * kvw: qw + (slab + 1) * kvw]
        return t.reshape(B, N, CMP_STRIDE, G, HEAD_DIM).transpose(0, 3, 1, 2, 4).reshape(
            B, G, N, CMP_STRIDE * HEAD_DIM)

    flat = lambda w1: w1.reshape(CMP_BLOCK * HEAD_DIM, -1).astype(BF16)
    k_cmp = compress(blocks(0), pe_k.reshape(1, -1), flat(w1_k), w2_k.astype(BF16), k_gain[0], True)
    v_cmp = compress(blocks(1), pe_v.reshape(1, -1), flat(w1_v), w2_v.astype(BF16), k_gain[0], False)
    o = nsa_attn(proj, gates, k_cmp, v_cmp, q_gain, k_gain, 512, 512)
    out = out_proj([o.reshape(B * T, qw)], [w_out.astype(BF16)], xf, 256)
    return out.reshape(B, T, D)


def kernel(x, mem, norm_mix, norm_mem, norm_ffn, mem_in_gain, even_w_in, even_w_out, fox_f_bias,
           fox_q_gain, fox_k_gain, hgrn_lb_logits, hgrn_o_gain, odd_w_in, odd_w_out, nsa_q_gain,
           nsa_k_gain, cmp_pe_k, cmp_w1_k, cmp_w2_k, cmp_pe_v, cmp_w1_v, cmp_w2_v, mem_wq, mem_wkv,
           mem_wo, mem_q_gain, mem_k_gain, ffn_w1, ffn_w3, ffn_w2):
    B, T, D = x.shape
    depth = norm_mix.shape[0]
    L = mem.shape[1]
    lb_cum = jnp.cumsum(jax.nn.softmax(hgrn_lb_logits.astype(F32), axis=0), axis=0)
    hgrn_lb = lb_cum - lb_cum[0:1]
    memf = mem.reshape(B * L, D)
    for layer in range(depth):
        if layer % 2 == 0:
            e = layer // 2
            x = even_mixer(x, norm_mix[layer], even_w_in[e], even_w_out[e], fox_f_bias[e],
                           fox_q_gain[e], fox_k_gain[e], hgrn_lb[e], hgrn_o_gain[e])
        else:
            o = layer // 2
            x = nsa_mixer(x, norm_mix[layer], odd_w_in[o], odd_w_out[o], nsa_q_gain[o], nsa_k_gain[o],
                          cmp_pe_k[o], cmp_w1_k[o], cmp_w2_k[o], cmp_pe_v[o], cmp_w1_v[o], cmp_w2_v[o])
        kv = rms_matmul(memf, mem_in_gain[layer], mem_wkv[layer].astype(BF16), 256, 1024)
        x = mem_attn(x, norm_mem[layer], mem_wq[layer].astype(BF16), kv.reshape(B, L, -1),
                     mem_q_gain[layer], mem_k_gain[layer], mem_wo[layer].astype(BF16), 256)
        x = ffn(x.reshape(B * T, D), norm_ffn[layer], ffn_w1[layer].astype(BF16),
                ffn_w3[layer].astype(BF16), ffn_w2[layer].astype(BF16), 512, 512).reshape(B, T, D)
    return x
```

```python
import functools

import jax
import jax.numpy as jnp
from jax import lax
from jax.experimental import pallas as pl
from jax.experimental.pallas import tpu as pltpu

F32 = jnp.float32
BF16 = jnp.bfloat16
NEG_INF = float("-inf")

EPS = 1e-6
HEAD_DIM = 128
ATTN_SCALE = HEAD_DIM ** -0.5
LANES = 128
FOX_HEADS = 8
HGRN_HEADS = 8
HGRN_BLOCK = 16
HGRN_CHUNK = 64
HGRN_SPAN = 512
HGRN_SAFE_LOG_DECAY = -80.0
NSA_HEADS = 16
NSA_KV_HEADS = 4
NSA_GROUP = NSA_HEADS // NSA_KV_HEADS
CMP_BLOCK = 32
CMP_STRIDE = 16
SLC_BLOCK = 64
SLC_TOPK = 16
WINDOW = 512
MEM_HEADS = 4
V7X_VMEM_LIMIT = 56 * 1024 * 1024
LOG2E = 1.4426950408889634
MASK_BIG = 1e30
AUG = 2 * LANES
VT_ROWS = HEAD_DIM + 16


def _params(*sem):
    return pltpu.CompilerParams(dimension_semantics=sem, vmem_limit_bytes=V7X_VMEM_LIMIT)


def _rms(x, gain):
    return x * lax.rsqrt(jnp.mean(x * x, axis=-1, keepdims=True) + EPS) * gain


def _dot(a, b):
    return jnp.dot(a, b, preferred_element_type=F32)


def _dot_nt(a, b):
    return lax.dot_general(a, b, (((1,), (1,)), ((), ())), preferred_element_type=F32)


def _dot_tn(a, b):
    return lax.dot_general(a, b, (((0,), (0,)), ((), ())), preferred_element_type=F32)


def _split3(x):
    x1 = x.astype(BF16)
    r1 = x - x1.astype(F32)
    x2 = r1.astype(BF16)
    return x1, x2, (r1 - x2.astype(F32)).astype(BF16)


def _exact_dot(sel, x):
    x1, x2, x3 = _split3(x)
    return _dot(sel, x1) + _dot(sel, x2) + _dot(sel, x3)


def _log_sigmoid(z):
    return jnp.minimum(z, 0.0) - jnp.log1p(jnp.exp(-jnp.abs(z)))


def _rms_matmul_kernel(x_ref, g_ref, w_ref, o_ref, hn_ref):
    @pl.when(pl.program_id(1) == 0)
    def _():
        hn_ref[...] = _rms(x_ref[...], g_ref[...]).astype(BF16)

    o_ref[...] = _dot(hn_ref[...], w_ref[...])


def rms_matmul(x, gain, w, layer, tm, tn):
    M, K = x.shape
    N = w.shape[2]
    tm, tn = min(tm, M), min(tn, N)
    assert M % tm == 0 and N % tn == 0
    return pl.pallas_call(
        _rms_matmul_kernel,
        name="rms_matmul",
        grid=(M // tm, N // tn),
        in_specs=[pl.BlockSpec((tm, K), lambda i, j: (i, 0)),
                  pl.BlockSpec((1, K), lambda i, j: (0, 0)),
                  pl.BlockSpec((None, K, tn), lambda i, j: (layer, 0, j))],
        out_specs=pl.BlockSpec((tm, tn), lambda i, j: (i, j)),
        out_shape=jax.ShapeDtypeStruct((M, N), F32),
        scratch_shapes=[pltpu.VMEM((tm, K), BF16)],
        compiler_params=_params("parallel", "arbitrary"),
    )(x, gain.reshape(1, K), w)


def _out_proj_kernel(*refs, n_in):
    a_refs, w_refs = refs[:n_in], refs[n_in:2 * n_in]
    res_ref, o_ref = refs[2 * n_in], refs[2 * n_in + 1]
    acc = res_ref[...]
    for a_ref, w_ref in zip(a_refs, w_refs):
        acc = acc + _dot(a_ref[...], w_ref[...])
    o_ref[...] = acc


def out_proj(acts, w, layer, res, tm):
    M, N = res.shape
    tm = min(tm, M)
    assert M % tm == 0
    n_in = len(acts)
    ka = acts[0].shape[1]
    assert all(a.shape[1] == ka for a in acts) and n_in * ka == w.shape[1]
    row_block = lambda r: pl.BlockSpec((None, ka, N), lambda i: (layer, r, 0))
    weights = [w] * n_in
    in_specs = ([pl.BlockSpec((tm, ka), lambda i: (i, 0)) for _ in acts]
                + [row_block(r) for r in range(n_in)]
                + [pl.BlockSpec((tm, N), lambda i: (i, 0))])
    return pl.pallas_call(
        functools.partial(_out_proj_kernel, n_in=n_in),
        name="out_proj",
        grid=(M // tm,),
        in_specs=in_specs,
        out_specs=pl.BlockSpec((tm, N), lambda i: (i, 0)),
        out_shape=jax.ShapeDtypeStruct((M, N), F32),
        compiler_params=_params("parallel"),
    )(*acts, *weights, res)


def _ffn_kernel(x_ref, g_ref, w1_ref, w3_ref, w2_ref, o_ref, hn_ref):
    f = pl.program_id(1)

    @pl.when(f == 0)
    def _():
        x = x_ref[...]
        hn_ref[...] = _rms(x, g_ref[...]).astype(BF16)
        o_ref[...] = x

    h = hn_ref[...]
    a = _dot(h, w1_ref[...].astype(BF16))
    b = _dot(h, w3_ref[...].astype(BF16))
    u = (a * jax.nn.sigmoid(a) * b).astype(BF16)
    o_ref[...] += _dot(u, w2_ref[...].astype(BF16))


def ffn(x, gain, w1, w3, w2, layer, tm, tf):
    M, D = x.shape
    Fh = w1.shape[2]
    tm, tf = min(tm, M), min(tf, Fh)
    assert M % tm == 0 and Fh % tf == 0
    return pl.pallas_call(
        _ffn_kernel,
        name="ffn",
        grid=(M // tm, Fh // tf),
        in_specs=[pl.BlockSpec((tm, D), lambda i, f: (i, 0)),
                  pl.BlockSpec((1, D), lambda i, f: (0, 0)),
                  pl.BlockSpec((None, D, tf), lambda i, f: (layer, 0, f)),
                  pl.BlockSpec((None, D, tf), lambda i, f: (layer, 0, f)),
                  pl.BlockSpec((None, tf, D), lambda i, f: (layer, f, 0))],
        out_specs=pl.BlockSpec((tm, D), lambda i, f: (i, 0)),
        out_shape=jax.ShapeDtypeStruct((M, D), F32),
        scratch_shapes=[pltpu.VMEM((tm, D), BF16)],
        compiler_params=_params("parallel", "arbitrary"),
    )(x, gain.reshape(1, D), w1, w3, w2)


def _mem_attn_kernel(x_ref, g_ref, wq_ref, kv_ref, qg_ref, kg_ref, wo_ref, o_ref, *, heads):
    x = x_ref[0]
    h = _rms(x, g_ref[...]).astype(BF16)
    q = _dot(h, wq_ref[...])
    kv = kv_ref[0]
    w = heads * HEAD_DIM
    outs = []
    for hh in range(heads):
        sl = slice(hh * HEAD_DIM, (hh + 1) * HEAD_DIM)
        qh = (_rms(q[:, sl], qg_ref[...]) * ATTN_SCALE).astype(BF16)
        kh = _rms(kv[:, sl], kg_ref[...]).astype(BF16)
        vh = kv[:, w + hh * HEAD_DIM: w + (hh + 1) * HEAD_DIM].astype(BF16)
        s = _dot_nt(qh, kh)
        p = jnp.exp(s - jnp.max(s, axis=-1, keepdims=True))
        z = jnp.sum(p, axis=-1, keepdims=True)
        outs.append((_dot(p.astype(BF16), vh) / z).astype(BF16))
    o = jnp.concatenate(outs, axis=-1)
    o_ref[0] = x + _dot(o, wo_ref[...])


def mem_attn(x, gain, wq, kv, q_gain, k_gain, wo, layer, tm):
    B, T, D = x.shape
    L, W2 = kv.shape[1], kv.shape[2]
    W = W2 // 2
    tm = min(tm, T)
    assert T % tm == 0
    return pl.pallas_call(
        functools.partial(_mem_attn_kernel, heads=W // HEAD_DIM),
        name="mem_attn",
        grid=(B, T // tm),
        in_specs=[pl.BlockSpec((1, tm, D), lambda b, i: (b, i, 0)),
                  pl.BlockSpec((1, D), lambda b, i: (0, 0)),
                  pl.BlockSpec((None, D, W), lambda b, i: (layer, 0, 0)),
                  pl.BlockSpec((1, L, W2), lambda b, i: (b, 0, 0)),
                  pl.BlockSpec((1, HEAD_DIM), lambda b, i: (0, 0)),
                  pl.BlockSpec((1, HEAD_DIM), lambda b, i: (0, 0)),
                  pl.BlockSpec((None, W, D), lambda b, i: (layer, 0, 0))],
        out_specs=pl.BlockSpec((1, tm, D), lambda b, i: (b, i, 0)),
        out_shape=jax.ShapeDtypeStruct((B, T, D), F32),
        compiler_params=_params("parallel", "parallel"),
    )(x, gain.reshape(1, D), wq, kv, q_gain.reshape(1, HEAD_DIM), k_gain.reshape(1, HEAD_DIM), wo)


def _eye(n):
    return (lax.broadcasted_iota(jnp.int32, (n, n), 0)
            == lax.broadcasted_iota(jnp.int32, (n, n), 1)).astype(BF16)


def _ones_row(rows, n):
    return (lax.broadcasted_iota(jnp.int32, (rows, n), 0) == 0).astype(BF16)


def _split3_f32(x):
    return tuple(t.astype(F32) for t in _split3(x))


COL_GROUPS = 1


def _col_groups(n):
    w = n // COL_GROUPS
    return [slice(i * w, (i + 1) * w) for i in range(COL_GROUPS)]


def _attn_init(n):
    return tuple((jnp.full((1, c.stop - c.start), NEG_INF, F32), jnp.zeros((VT_ROWS, c.stop - c.start), F32))
                 for c in _col_groups(n))


def _attn_tile(k_tile, vt_tile, qa, carry, score_fn=None):
    out = []
    for (m, acc), c in zip(carry, _col_groups(qa.shape[1])):
        s = _dot(k_tile, qa[:, c])
        if score_fn is not None:
            s = score_fn(s, c)
        m_new = jnp.maximum(m, jnp.max(s, axis=0, keepdims=True))
        p = jnp.exp2(s - m_new).astype(BF16)
        out.append((m_new, jnp.exp2(m - m_new) * acc + _dot(vt_tile, p)))
    return tuple(out)


def _attn_finish(carry):
    return jnp.concatenate([acc for _, acc in carry], axis=1)


def _fox_gate_kernel(g_ref, b_ref, c_ref, kf_ref, *, heads):
    T = g_ref.shape[1]
    r = lax.broadcasted_iota(jnp.int32, (LANES, LANES), 0)
    lane = lax.broadcasted_iota(jnp.int32, (LANES, LANES), 1)
    tri = (r >= lane).astype(BF16)

    def body(i, carry):
        rows = pl.ds(pl.multiple_of(i * LANES, LANES), LANES)
        cs = _exact_dot(tri, _log_sigmoid(g_ref[0, rows, :] + b_ref[...])) + carry
        c_ref[0, rows, :] = cs
        c2 = cs * LOG2E
        for h in range(heads):
            d1, d2, d3 = _split3_f32(c2[:, h:h + 1])
            feat = jnp.where(lane < 3, 1.0,
                             jnp.where(lane == 3, -d1, jnp.where(lane == 4, -d2,
                                                                 jnp.where(lane == 5, -d3, 0.0))))
            kf_ref[0, h, rows, :] = feat.astype(BF16)
        return cs[LANES - 1:LANES, :]

    lax.fori_loop(0, T // LANES, body, jnp.zeros((1, LANES), F32))


def fox_gate(g, bias, heads):
    B, T, _ = g.shape
    return pl.pallas_call(
        functools.partial(_fox_gate_kernel, heads=heads),
        name="fox_gate",
        grid=(B,),
        in_specs=[pl.BlockSpec((1, T, LANES), lambda b: (b, 0, 0)),
                  pl.BlockSpec((1, LANES), lambda b: (0, 0))],
        out_specs=[pl.BlockSpec((1, T, LANES), lambda b: (b, 0, 0)),
                   pl.BlockSpec((1, heads, T, LANES), lambda b: (b, 0, 0, 0))],
        out_shape=[jax.ShapeDtypeStruct((B, T, LANES), F32),
                   jax.ShapeDtypeStruct((B, heads, T, LANES), BF16)],
        compiler_params=_params("parallel"),
    )(g, bias)


def _fox_attn_kernel(q_ref, k_ref, v_ref, kf_ref, c_ref, qg_ref, kg_ref, o_ref, ka_ref, vt_ref, *, tq):
    qi = pl.program_id(2)
    T = k_ref.shape[1]
    eye = _eye(HEAD_DIM)

    @pl.when(qi == 0)
    def _():
        ka_ref[:, :HEAD_DIM] = _rms(k_ref[0], kg_ref[...]).astype(BF16)
        ka_ref[:, HEAD_DIM:] = kf_ref[0, 0]
        vt_ref[:HEAD_DIM, :] = _dot_nt(eye, v_ref[0].astype(BF16)).astype(BF16)
        vt_ref[HEAD_DIM:, :] = _ones_row(VT_ROWS - HEAD_DIM, T)

    q0 = pl.multiple_of(qi * tq, tq)
    qn = (_rms(q_ref[0], qg_ref[...]) * (ATTN_SCALE * LOG2E)).astype(BF16)
    qt = _dot_nt(eye, qn).astype(BF16)
    c1, c2, c3 = _split3_f32(c_ref[0, :, pl.ds(q0, tq)] * LOG2E)
    r = lax.broadcasted_iota(jnp.int32, (16, tq), 0)
    feat = jnp.where(r == 0, c1, jnp.where(r == 1, c2, jnp.where(r == 2, c3,
                                                                 jnp.where(r < 6, 1.0, 0.0))))
    qa = jnp.concatenate([qt, feat.astype(BF16), jnp.zeros((AUG - HEAD_DIM - 16, tq), BF16)], axis=0)

    kpos = lax.broadcasted_iota(jnp.int32, (tq, 1), 0)
    tpos = lax.broadcasted_iota(jnp.int32, (1, tq), 1)

    def tile(j, carry, score_fn=None):
        k0 = pl.multiple_of(j * tq, tq)
        return _attn_tile(ka_ref[pl.ds(k0, tq), :], vt_ref[:, pl.ds(k0, tq)], qa, carry, score_fn)

    carry = tile(qi, _attn_init(tq), lambda s, c: jnp.where(kpos <= tpos[:, c], s, NEG_INF))
    acc = _attn_finish(lax.fori_loop(0, qi, tile, carry))
    ot = (acc[:HEAD_DIM] / acc[HEAD_DIM:HEAD_DIM + 1]).astype(BF16)
    for c in range(tq // LANES):
        cs = slice(c * LANES, (c + 1) * LANES)
        o_ref[0, cs, :] = _dot_nt(eye, ot[:, cs]).astype(o_ref.dtype)


def fox_attn(proj, kfeat, c_rows, q_gain, k_gain, tq):
    B, T, _ = proj.shape
    H = FOX_HEADS
    tq = min(tq, T)
    assert T % tq == 0 and tq % LANES == 0
    return pl.pallas_call(
        functools.partial(_fox_attn_kernel, tq=tq),
        name="fox_attn",
        grid=(B, H, T // tq),
        in_specs=[pl.BlockSpec((1, tq, HEAD_DIM), lambda b, h, i: (b, i, h)),
                  pl.BlockSpec((1, T, HEAD_DIM), lambda b, h, i: (b, 0, H + h)),
                  pl.BlockSpec((1, T, HEAD_DIM), lambda b, h, i: (b, 0, 2 * H + h)),
                  pl.BlockSpec((1, 1, T, LANES), lambda b, h, i: (b, h, 0, 0)),
                  pl.BlockSpec((1, 1, T), lambda b, h, i: (b * H + h, 0, 0)),
                  pl.BlockSpec((1, HEAD_DIM), lambda b, h, i: (0, 0)),
                  pl.BlockSpec((1, HEAD_DIM), lambda b, h, i: (0, 0))],
        out_specs=pl.BlockSpec((1, tq, HEAD_DIM), lambda b, h, i: (b, i, h)),
        out_shape=jax.ShapeDtypeStruct((B, T, H * HEAD_DIM), BF16),
        scratch_shapes=[pltpu.VMEM((T, AUG), BF16), pltpu.VMEM((VT_ROWS, T), BF16)],
        compiler_params=_params("parallel", "parallel", "arbitrary"),
    )(proj, proj, proj, kfeat, c_rows, q_gain.reshape(1, HEAD_DIM), k_gain.reshape(1, HEAD_DIM))


def _hgrn_kernel(q_ref, f_ref, i_ref, g_ref, la_ref, l1_ref, oml_ref, og_ref, o_ref, *, bs, chunk, span):
    T = q_ref.shape[1]
    R = LANES
    ri = lax.broadcasted_iota(jnp.int32, (R, R), 0)
    ci = lax.broadcasted_iota(jnp.int32, (R, R), 1)

    def prefix_sel(n):
        return (((ri // n) == (ci // n)) & (ri >= ci)).astype(BF16)

    s_idx = lax.broadcasted_iota(jnp.int32, (bs, HEAD_DIM), 0)
    tril = (lax.broadcasted_iota(jnp.int32, (chunk, chunk), 0)
            >= lax.broadcasted_iota(jnp.int32, (chunk, chunk), 1))

    def gates(row0, n):
        rows = pl.ds(row0, n)
        z = f_ref[0, rows, :]
        a = la_ref[0]
        y = l1_ref[0] + _log_sigmoid(z)
        log_f = jnp.maximum(a, y) + jnp.log1p(jnp.exp(-jnp.abs(a - y)))
        hq = q_ref[0, rows, :]
        return log_f, oml_ref[0] * jax.nn.sigmoid(-z), hq * jax.nn.sigmoid(hq), i_ref[0, rows, :]

    def prefixes(log_f, n):
        sel = prefix_sel(n)
        pieces = _split3(log_f)
        b = jnp.concatenate([sum(_dot(sel, p[g * R:(g + 1) * R]) for p in pieces)
                             for g in range(log_f.shape[0] // R)], axis=0)
        bt = jnp.concatenate([jnp.broadcast_to(b[(u + 1) * n - 1:(u + 1) * n, :], (n, HEAD_DIM))
                              for u in range(log_f.shape[0] // n)], axis=0)
        return b, bt

    def finish(row0, n, o):
        rows = pl.ds(row0, n)
        hg = g_ref[0, rows, :]
        o_ref[0, rows, :] = (_rms(o, og_ref[...]) * (hg * jax.nn.sigmoid(hg))).astype(o_ref.dtype)

    def matmul_path(row0, st, kk, qq, vv, b, bt):
        qe = (qq * jnp.exp(b)).astype(BF16)
        k_inv = (kk * jnp.exp(-b)).astype(BF16)
        ke = (kk * jnp.exp(bt - b)).astype(BF16)
        dec = jnp.exp(bt)
        vb = vv.astype(BF16)
        sls = [slice(u * chunk, (u + 1) * chunk) for u in range(span // chunk)]
        scores = [jnp.where(tril, _dot_nt(qe[sl], k_inv[sl]), 0.0).astype(BF16) for sl in sls]
        intra = [_dot(a, vb[sl]) for a, sl in zip(scores, sls)]
        update = [_dot_tn(vb[sl], ke[sl]) for sl in sls]
        outs = []
        for sl, o_in, upd in zip(sls, intra, update):
            outs.append(_dot_nt(qe[sl], st.astype(BF16)) + o_in)
            st = dec[sl.start:sl.start + 1, :] * st + upd
        finish(row0, span, jnp.concatenate(outs, axis=0))
        return st

    def pairwise_step(i, st, base):
        row0 = pl.multiple_of(base + i * R, R)
        log_f, kk, qq, vv = gates(row0, R)
        b, bt = prefixes(log_f, bs)
        qe = (qq * jnp.exp(b)).astype(BF16)
        ke = (kk * jnp.exp(bt - b)).astype(BF16)
        dec = jnp.exp(bt)
        vb = vv.astype(BF16)
        outs = []
        for u in range(R // bs):
            sl = slice(u * bs, (u + 1) * bs)
            bu, vu, qk, ku = b[sl], vv[sl], qq[sl], kk[sl]
            intra = []
            for t in range(bs):
                w = jnp.exp(jnp.where(s_idx <= t, bu[t:t + 1, :] - bu, NEG_INF))
                a_t = jnp.sum(w * (qk[t:t + 1, :] * ku), axis=-1, keepdims=True)
                intra.append(jnp.sum(a_t * vu, axis=0, keepdims=True))
            outs.append(_dot_nt(qe[sl], st.astype(BF16)) + jnp.concatenate(intra, axis=0))
            st = dec[u * bs:u * bs + 1, :] * st + _dot_tn(vb[sl], ke[sl])
        finish(row0, R, jnp.concatenate(outs, axis=0))
        return st

    def step(i, st):
        row0 = pl.multiple_of(i * span, span)
        log_f, kk, qq, vv = gates(row0, span)
        b, bt = prefixes(log_f, chunk)
        return lax.cond(
            jnp.min(bt) >= HGRN_SAFE_LOG_DECAY,
            lambda st: matmul_path(row0, st, kk, qq, vv, b, bt),
            lambda st: lax.fori_loop(0, span // R, functools.partial(pairwise_step, base=row0), st),
            st)

    lax.fori_loop(0, T // span, step, jnp.zeros((HEAD_DIM, HEAD_DIM), F32))


def hgrn(proj, col0, log_lb, log1m_lb, one_m_lb, o_gain):
    B, T, _ = proj.shape
    H = HGRN_HEADS
    row = lambda off: pl.BlockSpec((1, T, HEAD_DIM), lambda b, h: (b, 0, col0 + off * H + h))
    par = pl.BlockSpec((1, 1, HEAD_DIM), lambda b, h: (h, 0, 0))
    log_lb, log1m_lb, one_m_lb = (p.reshape(H, 1, HEAD_DIM) for p in (log_lb, log1m_lb, one_m_lb))
    return pl.pallas_call(
        functools.partial(_hgrn_kernel, bs=HGRN_BLOCK, chunk=HGRN_CHUNK, span=min(HGRN_SPAN, T)),
        name="hgrn",
        grid=(B, H),
        in_specs=[row(0), row(1), row(2), row(3), par, par, par,
                  pl.BlockSpec((1, HEAD_DIM), lambda b, h: (0, 0))],
        out_specs=pl.BlockSpec((1, T, HEAD_DIM), lambda b, h: (b, 0, h)),
        out_shape=jax.ShapeDtypeStruct((B, T, H * HEAD_DIM), BF16),
        compiler_params=_params("parallel", "parallel"),
    )(proj, proj, proj, proj, log_lb, log1m_lb, one_m_lb, o_gain.reshape(1, HEAD_DIM))


def _compress_kernel(t_ref, pe_ref, w1_ref, w2_ref, g_ref, o_ref, *, normalize):
    N = t_ref.shape[1] // CMP_STRIDE
    top = bot = jnp.zeros((N, w1_ref.shape[2]), F32)
    for l in range(CMP_STRIDE):
        xl = t_ref[0, pl.ds(l, N, stride=CMP_STRIDE), :]
        top = top + _dot((xl + pe_ref[l:l + 1, :]).astype(BF16), w1_ref[l])
        bot = bot + _dot((xl + pe_ref[CMP_STRIDE + l:CMP_STRIDE + l + 1, :]).astype(BF16),
                         w1_ref[CMP_STRIDE + l])
    hid = top + pltpu.roll(bot, N - 1, 0)
    out = _dot(jax.nn.gelu(hid).astype(BF16), w2_ref[...])
    if normalize:
        out = _rms(out, g_ref[...])
    valid = lax.broadcasted_iota(jnp.int32, out.shape, 0) < N - 1
    o_ref[0, 0] = jnp.where(valid, out, 0.0)


def compress(proj, col0, pe, w1, w2, layer, gain, normalize):
    B, T, _ = proj.shape
    G = NSA_KV_HEADS
    N = T // CMP_STRIDE
    Hc = w1.shape[3]
    assert CMP_BLOCK == 2 * CMP_STRIDE
    return pl.pallas_call(
        functools.partial(_compress_kernel, normalize=normalize),
        name="nsa_compress",
        grid=(B, G),
        in_specs=[pl.BlockSpec((1, T, HEAD_DIM), lambda b, g: (b, 0, col0 + g)),
                  pl.BlockSpec((None, CMP_BLOCK, HEAD_DIM), lambda b, g: (layer, 0, 0)),
                  pl.BlockSpec((None, CMP_BLOCK, HEAD_DIM, Hc), lambda b, g: (layer, 0, 0, 0)),
                  pl.BlockSpec((None, Hc, HEAD_DIM), lambda b, g: (layer, 0, 0)),
                  pl.BlockSpec((1, HEAD_DIM), lambda b, g: (0, 0))],
        out_specs=pl.BlockSpec((1, 1, N, HEAD_DIM), lambda b, g: (b, g, 0, 0)),
        out_shape=jax.ShapeDtypeStruct((B, G, N, HEAD_DIM), F32),
        compiler_params=_params("parallel", "parallel"),
    )(proj, pe, w1, w2, gain.reshape(1, HEAD_DIM))


_SEL_LANE0 = 64
_CMP_OV_ROW0 = VT_ROWS


def _nsa_kernel(q_ref, ks_ref, vs_ref, kw_ref, vw_ref, kc_ref, vc_ref, gt_ref, qg_ref, kg_ref,
                o_ref, ksa_ref, kwa_ref, vst_ref, vwt_ref, kca_ref, vcov_ref, *, tq, tk, n_sel):
    g = pl.program_id(1)
    qi = pl.program_id(2)
    T = ks_ref.shape[1]
    J = NSA_GROUP
    n_slc = T // SLC_BLOCK
    nc = kc_ref.shape[2]
    n_cmp = nc - 1
    cols = J * tq
    eye = _eye(HEAD_DIM)

    @pl.when(qi == 0)
    def _():
        row = lax.broadcasted_iota(jnp.int32, (T, LANES), 0)
        lane = lax.broadcasted_iota(jnp.int32, (T, LANES), 1)
        blk = row // SLC_BLOCK
        feats = jnp.where(lane < 3, blk,
                          jnp.where(lane < 6, row % SLC_BLOCK,
                                    jnp.where(lane < 9, 1,
                                              jnp.where(lane - _SEL_LANE0 == blk, 1, 0))))
        feats = feats.astype(F32).astype(BF16)
        ksa_ref[:, :HEAD_DIM] = _rms(ks_ref[0], kg_ref[1:2, :]).astype(BF16)
        ksa_ref[:, HEAD_DIM:] = feats
        kwa_ref[:, :HEAD_DIM] = _rms(kw_ref[0], kg_ref[2:3, :]).astype(BF16)
        kwa_ref[:, HEAD_DIM:] = feats
        vst_ref[:HEAD_DIM, :] = _dot_nt(eye, vs_ref[0].astype(BF16)).astype(BF16)
        vst_ref[HEAD_DIM:, :] = _ones_row(VT_ROWS - HEAD_DIM, T)
        vwt_ref[:HEAD_DIM, :] = _dot_nt(eye, vw_ref[0].astype(BF16)).astype(BF16)
        vwt_ref[HEAD_DIM:, :] = _ones_row(VT_ROWS - HEAD_DIM, T)
        n_r = lax.broadcasted_iota(jnp.int32, (nc, LANES), 0)
        n_l = lax.broadcasted_iota(jnp.int32, (nc, LANES), 1)
        per = SLC_BLOCK // CMP_STRIDE
        cfe = jnp.where(n_l < 3, (n_r // per).astype(F32),
                        jnp.where(n_l < 6, (CMP_STRIDE * (n_r % per)).astype(F32) + 0.5 * (CMP_BLOCK - 1),
                                  jnp.where(n_l < 9, 1.0, 0.0)))
        kca_ref[:, :HEAD_DIM] = kc_ref[0, 0].astype(BF16)
        kca_ref[:, HEAD_DIM:] = cfe.astype(BF16)
        vcov_ref[:HEAD_DIM, :] = _dot_nt(eye, vc_ref[0, 0].astype(BF16)).astype(BF16)
        vcov_ref[HEAD_DIM:_CMP_OV_ROW0, :] = _ones_row(VT_ROWS - HEAD_DIM, nc)
        m_r = lax.broadcasted_iota(jnp.int32, (SLC_BLOCK, nc), 0)
        n_c = lax.broadcasted_iota(jnp.int32, (SLC_BLOCK, nc), 1)
        vcov_ref[_CMP_OV_ROW0:, :] = ((n_c * CMP_STRIDE <= m_r * SLC_BLOCK + (SLC_BLOCK - 1))
                                      & (n_c * CMP_STRIDE + (CMP_BLOCK - 1) >= m_r * SLC_BLOCK)
                                      & (n_c < n_cmp) & (m_r < n_slc)).astype(BF16)

    q0 = pl.multiple_of(qi * tq, tq)
    qraw = q_ref[0]
    qt = jnp.concatenate(
        [_dot_nt(eye, (_rms(qraw[:, j * HEAD_DIM:(j + 1) * HEAD_DIM], qg_ref[...])
                       * (ATTN_SCALE * LOG2E)).astype(BF16)).astype(BF16)
         for j in range(J)], axis=1)

    col = lax.broadcasted_iota(jnp.int32, (1, cols), 1)
    tt = q0 + col % tq
    slope2 = jnp.exp2(-0.5 * (g * J + col // tq + 1).astype(F32)) * LOG2E
    s1, s2, s3 = _split3_f32(slope2)
    r1, r2, r3 = _split3_f32(-slope2 * tt.astype(F32))
    r16 = lax.broadcasted_iota(jnp.int32, (16, cols), 0)
    feat = jnp.zeros((16, cols), F32)
    for i, v in enumerate((s1 * SLC_BLOCK, s2 * SLC_BLOCK, s3 * SLC_BLOCK, s1, s2, s3, r1, r2, r3)):
        feat = jnp.where(r16 == i, v, feat)
    feat = feat.astype(BF16)
    qa_base = jnp.concatenate([qt, feat, jnp.zeros((AUG - HEAD_DIM - 16, cols), BF16)], axis=0)

    n_col = lax.broadcasted_iota(jnp.int32, (nc, 1), 0)
    cmp_end = n_col * CMP_STRIDE + (CMP_BLOCK - 1)
    kca = kca_ref[...]
    r_hi, r_lo = [], []
    for c in _col_groups(cols):
        s = _dot(kca, qa_base[:, c])
        s = jnp.where((cmp_end <= tt[:, c]) & (n_col < n_cmp), s, NEG_INF)
        m = jnp.max(s, axis=0, keepdims=True)
        p = jnp.exp2(s - jnp.where(m == NEG_INF, 0.0, m))
        p_hi = p.astype(BF16)
        r_hi.append(_dot(vcov_ref[...], p_hi))
        r_lo.append(_dot(vcov_ref[HEAD_DIM:, :], (p - p_hi.astype(F32)).astype(BF16)))
    r_hi = jnp.concatenate(r_hi, axis=1)
    r_lo = jnp.concatenate(r_lo, axis=1)
    z = r_hi[HEAD_DIM:HEAD_DIM + 1] + r_lo[0:1]
    inv_z = 1.0 / jnp.where(z > 0, z, 1.0)

    span = WINDOW + tq
    k0w = pl.multiple_of(jnp.maximum(q0 - WINDOW, 0), tq)
    d = (lax.broadcasted_iota(jnp.int32, (span, 1), 0)
         - lax.broadcasted_iota(jnp.int32, (1, tq), 1))
    off = q0 - k0w
    bias = jnp.where((d <= off) & (d > off - WINDOW), 0.0, NEG_INF)
    acc_w = _attn_finish(_attn_tile(
        kwa_ref[pl.ds(k0w, span), :], vwt_ref[:, pl.ds(k0w, span)], qa_base, _attn_init(cols),
        lambda s, c: s + jnp.concatenate([bias] * ((c.stop - c.start) // tq), axis=1)))

    imp4 = (r_hi[_CMP_OV_ROW0:] + r_lo[_CMP_OV_ROW0 - HEAD_DIM:]) * inv_z
    imp = imp4[:, 0:tq]
    for j in range(1, J):
        imp = imp + imp4[:, j * tq:(j + 1) * tq]
    blk = lax.broadcasted_iota(jnp.int32, (SLC_BLOCK, tq), 0)
    t_q = q0 + lax.broadcasted_iota(jnp.int32, (1, tq), 1)
    imp = jnp.where(blk * SLC_BLOCK > t_q, NEG_INF, imp)
    imp = jnp.where((blk == t_q // SLC_BLOCK) | (blk == 0), jnp.inf, imp)
    blk_f = blk.astype(F32)
    taken = blk >= n_slc
    for _ in range(n_sel):
        vals = jnp.where(taken, NEG_INF, imp)
        best = jnp.max(vals, axis=0, keepdims=True)
        cand = (vals == best) & jnp.logical_not(taken)
        first = jnp.min(jnp.where(cand, blk_f, float(SLC_BLOCK)), axis=0, keepdims=True)
        taken = taken | (blk_f == first)
    not_sel = jnp.where(taken & (blk < n_slc), 0.0, -MASK_BIG).astype(BF16)
    qa_slc = jnp.concatenate([qt, feat, jnp.zeros((_SEL_LANE0 - 16, cols), BF16)]
                             + [jnp.concatenate([not_sel] * J, axis=1)], axis=0)

    def slc_tile(jt, carry, score_fn=None):
        k0 = pl.multiple_of(jt * tk, tk)
        return _attn_tile(ksa_ref[pl.ds(k0, tk), :], vst_ref[:, pl.ds(k0, tk)], qa_slc, carry, score_fn)

    jd = q0 // tk
    kpos = jd * tk + lax.broadcasted_iota(jnp.int32, (tk, 1), 0)
    carry = slc_tile(jd, _attn_init(cols), lambda s, c: jnp.where(kpos <= tt[:, c], s, NEG_INF))
    acc_s = _attn_finish(lax.fori_loop(0, jd, slc_tile, carry))

    gates_t = jnp.transpose(jax.nn.sigmoid(gt_ref[0]))
    gate = lambda c: jnp.concatenate([gates_t[3 * j + c:3 * j + c + 1, :] for j in range(J)], axis=1)
    o_t = ((gate(0) * inv_z) * r_hi[:HEAD_DIM]
           + (gate(1) / acc_s[HEAD_DIM:HEAD_DIM + 1]) * acc_s[:HEAD_DIM]
           + (gate(2) / acc_w[HEAD_DIM:HEAD_DIM + 1]) * acc_w[:HEAD_DIM]).astype(BF16)
    for j in range(J):
        for c in range(tq // LANES):
            o_ref[0, c * LANES:(c + 1) * LANES, j * HEAD_DIM:(j + 1) * HEAD_DIM] = _dot_nt(
                eye, o_t[:, j * tq + c * LANES:j * tq + (c + 1) * LANES]).astype(o_ref.dtype)


def nsa_attn(proj, gates, k_cmp, v_cmp, q_gain, k_gain, tq, tk):
    B, T, _ = proj.shape
    G, J = NSA_KV_HEADS, NSA_GROUP
    tk = min(tk, T)
    nc = k_cmp.shape[2]
    assert T % tk == 0 and tk % tq == 0 and T >= WINDOW + tq
    assert T // SLC_BLOCK <= LANES - _SEL_LANE0 and SLC_BLOCK <= LANES - _SEL_LANE0
    qcols = NSA_HEADS
    kv = lambda slab: pl.BlockSpec((1, T, HEAD_DIM), lambda b, g, i: (b, 0, qcols + slab * G + g))
    cmp_spec = pl.BlockSpec((1, 1, nc, HEAD_DIM), lambda b, g, i: (b, g, 0, 0))
    return pl.pallas_call(
        functools.partial(_nsa_kernel, tq=tq, tk=tk, n_sel=min(SLC_TOPK, T // SLC_BLOCK)),
        name="nsa_attn",
        grid=(B, G, T // tq),
        in_specs=[pl.BlockSpec((1, tq, J * HEAD_DIM), lambda b, g, i: (b, i, g)),
                  kv(2), kv(3), kv(4), kv(5), cmp_spec, cmp_spec,
                  pl.BlockSpec((1, tq, LANES), lambda b, g, i: (b, i, g)),
                  pl.BlockSpec((1, HEAD_DIM), lambda b, g, i: (0, 0)),
                  pl.BlockSpec((3, HEAD_DIM), lambda b, g, i: (0, 0))],
        out_specs=pl.BlockSpec((1, tq, J * HEAD_DIM), lambda b, g, i: (b, i, g)),
        out_shape=jax.ShapeDtypeStruct((B, T, NSA_HEADS * HEAD_DIM), BF16),
        scratch_shapes=[pltpu.VMEM((T, AUG), BF16), pltpu.VMEM((T, AUG), BF16),
                        pltpu.VMEM((VT_ROWS, T), BF16), pltpu.VMEM((VT_ROWS, T), BF16),
                        pltpu.VMEM((nc, AUG), BF16),
                        pltpu.VMEM((VT_ROWS + SLC_BLOCK, nc), BF16)],
        compiler_params=_params("parallel", "parallel", "arbitrary"),
    )(proj, proj, proj, proj, proj, k_cmp, v_cmp, gates, q_gain.reshape(1, HEAD_DIM), k_gain)


def even_weights(w_in, w_out):
    fw = FOX_HEADS * HEAD_DIM
    w_main = jnp.concatenate([w_in[:, :, :3 * fw], w_in[:, :, 3 * fw + FOX_HEADS:]], axis=2).astype(BF16)
    w_gate = jnp.pad(w_in[:, :, 3 * fw:3 * fw + FOX_HEADS], ((0, 0), (0, 0), (0, LANES - FOX_HEADS)))
    return w_main, w_gate.astype(BF16), w_out.astype(BF16)


def even_mixer(x, gain, weights, e, f_bias, q_gain, k_gain, lb, o_gain):
    B, T, D = x.shape
    xf = x.reshape(B * T, D)
    w_main, w_gate, w_out = weights
    proj = rms_matmul(xf, gain, w_main, e, 1024, 1024).reshape(B, T, -1)
    gate = rms_matmul(xf, gain, w_gate, e, 512, LANES).reshape(B, T, LANES)
    f_bias = jnp.pad(f_bias.reshape(1, FOX_HEADS).astype(F32), ((0, 0), (0, LANES - FOX_HEADS)))
    c, kfeat = fox_gate(gate, f_bias, FOX_HEADS)
    c_rows = c[:, :, :FOX_HEADS].transpose(0, 2, 1).reshape(B * FOX_HEADS, 1, T)
    o_fox = fox_attn(proj, kfeat, c_rows, q_gain, k_gain, 1024)
    lb = lb.astype(F32).reshape(HGRN_HEADS, HEAD_DIM)
    o_h = hgrn(proj, 3 * FOX_HEADS, jnp.log(lb), jnp.log1p(-lb), 1.0 - lb, o_gain)
    out = out_proj([o_fox.reshape(B * T, -1), o_h.reshape(B * T, -1)], w_out, e, xf, 256)
    return out.reshape(B, T, D)


def nsa_weights(w_in, w_out, w1_k, w2_k, w1_v, w2_v):
    L, D, _ = w_in.shape
    G, J = NSA_KV_HEADS, NSA_GROUP
    n_main = (NSA_HEADS + 6 * G) * HEAD_DIM
    w_gate = w_in[:, :, n_main:].reshape(L, D, G, 3 * J)
    w_gate = jnp.pad(w_gate, ((0, 0), (0, 0), (0, 0), (0, LANES - 3 * J))).reshape(L, D, G * LANES)
    return (w_in[:, :, :n_main].astype(BF16), w_gate.astype(BF16), w_out.astype(BF16),
            w1_k.astype(BF16), w2_k.astype(BF16), w1_v.astype(BF16), w2_v.astype(BF16))


def nsa_mixer(x, gain, weights, o, q_gain, k_gain, pe_k, pe_v):
    B, T, D = x.shape
    G = NSA_KV_HEADS
    xf = x.reshape(B * T, D)
    w_main, w_gate, w_out, w1_k, w2_k, w1_v, w2_v = weights
    proj = rms_matmul(xf, gain, w_main, o, 1024, 1024).reshape(B, T, -1)
    gates = rms_matmul(xf, gain, w_gate, o, 512, G * LANES).reshape(B, T, G * LANES)
    k_cmp = compress(proj, NSA_HEADS, pe_k, w1_k, w2_k, o, k_gain[0], True)
    v_cmp = compress(proj, NSA_HEADS + G, pe_v, w1_v, w2_v, o, k_gain[0], False)
    att = nsa_attn(proj, gates, k_cmp, v_cmp, q_gain, k_gain, 512, 512)
    out = out_proj([att.reshape(B * T, -1)], w_out, o, xf, 256)
    return out.reshape(B, T, D)


def kernel(x, mem, norm_mix, norm_mem, norm_ffn, mem_in_gain, even_w_in, even_w_out, fox_f_bias,
           fox_q_gain, fox_k_gain, hgrn_lb_logits, hgrn_o_gain, odd_w_in, odd_w_out, nsa_q_gain,
           nsa_k_gain, cmp_pe_k, cmp_w1_k, cmp_w2_k, cmp_pe_v, cmp_w1_v, cmp_w2_v, mem_wq, mem_wkv,
           mem_wo, mem_q_gain, mem_k_gain, ffn_w1, ffn_w3, ffn_w2):
    B, T, D = x.shape
    depth = norm_mix.shape[0]
    L = mem.shape[1]
    lb_cum = jnp.cumsum(jax.nn.softmax(hgrn_lb_logits.astype(F32), axis=0), axis=0)
    hgrn_lb = lb_cum - lb_cum[0:1]
    memf = mem.reshape(B * L, D)
    even_w = even_weights(even_w_in, even_w_out)
    nsa_w = nsa_weights(odd_w_in, odd_w_out, cmp_w1_k, cmp_w2_k, cmp_w1_v, cmp_w2_v)
    mem_wq, mem_wkv, mem_wo = (w.astype(BF16) for w in (mem_wq, mem_wkv, mem_wo))
    for layer in range(depth):
        if layer % 2 == 0:
            e = layer // 2
            x = even_mixer(x, norm_mix[layer], even_w, e, fox_f_bias[e], fox_q_gain[e], fox_k_gain[e],
                           hgrn_lb[e], hgrn_o_gain[e])
        else:
            o = layer // 2
            x = nsa_mixer(x, norm_mix[layer], nsa_w, o, nsa_q_gain[o], nsa_k_gain[o], cmp_pe_k, cmp_pe_v)
        kv = rms_matmul(memf, mem_in_gain[layer], mem_wkv, layer, 256, 1024)
        x = mem_attn(x, norm_mem[layer], mem_wq, kv.reshape(B, L, -1), mem_q_gain[layer],
                     mem_k_gain[layer], mem_wo, layer, 256)
        x = ffn(x.reshape(B * T, D), norm_ffn[layer], ffn_w1, ffn_w3, ffn_w2, layer,
                1024, 256).reshape(B, T, D)
    return x
```

```python
import functools

import jax
import jax.numpy as jnp
from jax import lax
from jax.experimental import pallas as pl
from jax.experimental.pallas import tpu as pltpu

F32 = jnp.float32
BF16 = jnp.bfloat16
NEG_INF = float("-inf")

EPS = 1e-6
HEAD_DIM = 128
ATTN_SCALE = HEAD_DIM ** -0.5
LANES = 128
FOX_HEADS = 8
HGRN_HEADS = 8
HGRN_BLOCK = 16
HGRN_CHUNK = 64
HGRN_SPAN = 512
HGRN_SAFE_LOG_DECAY = -80.0
NSA_HEADS = 16
NSA_KV_HEADS = 4
NSA_GROUP = NSA_HEADS // NSA_KV_HEADS
CMP_BLOCK = 32
CMP_STRIDE = 16
SLC_BLOCK = 64
SLC_TOPK = 16
WINDOW = 512
WINDOW_Q = 256
MEM_HEADS = 4
V7X_VMEM_LIMIT = 56 * 1024 * 1024
LOG2E = 1.4426950408889634
MASK_BIG = 1e30
AUG = 2 * LANES
VT_ROWS = HEAD_DIM + 16


def _params(*sem):
    return pltpu.CompilerParams(dimension_semantics=sem, vmem_limit_bytes=V7X_VMEM_LIMIT)


def _rms(x, gain):
    return x * lax.rsqrt(jnp.mean(x * x, axis=-1, keepdims=True) + EPS) * gain


def _dot(a, b):
    return jnp.dot(a, b, preferred_element_type=F32)


def _dot_nt(a, b):
    return lax.dot_general(a, b, (((1,), (1,)), ((), ())), preferred_element_type=F32)


def _dot_tn(a, b):
    return lax.dot_general(a, b, (((0,), (0,)), ((), ())), preferred_element_type=F32)


def _split3(x):
    x1 = x.astype(BF16)
    r1 = x - x1.astype(F32)
    x2 = r1.astype(BF16)
    return x1, x2, (r1 - x2.astype(F32)).astype(BF16)


def _exact_dot(sel, x):
    x1, x2, x3 = _split3(x)
    return _dot(sel, x1) + _dot(sel, x2) + _dot(sel, x3)


def _log_sigmoid(z):
    return jnp.minimum(z, 0.0) - jnp.log1p(jnp.exp(-jnp.abs(z)))


def _rms_matmul_kernel(x_ref, g_ref, w_ref, *rest):
    side_w_ref, o_ref, side_o_ref, hn_ref = rest if len(rest) == 4 else (None, rest[0], None, rest[1])

    @pl.when(pl.program_id(1) == 0)
    def _():
        hn_ref[...] = _rms(x_ref[...], g_ref[...]).astype(BF16)
        if side_w_ref is not None:
            side_o_ref[...] = _dot(hn_ref[...], side_w_ref[...])

    o_ref[...] = _dot(hn_ref[...], w_ref[...])


def rms_matmul(x, gain, w, layer, tm, tn, side_w=None):
    M, K = x.shape
    N = w.shape[2]
    tm, tn = min(tm, M), min(tn, N)
    assert M % tm == 0 and N % tn == 0
    in_specs = [pl.BlockSpec((tm, K), lambda i, j: (i, 0)),
                pl.BlockSpec((1, K), lambda i, j: (0, 0)),
                pl.BlockSpec((None, K, tn), lambda i, j: (layer, 0, j))]
    out_specs = pl.BlockSpec((tm, tn), lambda i, j: (i, j))
    out_shape = jax.ShapeDtypeStruct((M, N), F32)
    args = (x, gain.reshape(1, K), w)
    if side_w is not None:
        ns = side_w.shape[2]
        in_specs.append(pl.BlockSpec((None, K, ns), lambda i, j: (layer, 0, 0)))
        out_specs = [out_specs, pl.BlockSpec((tm, ns), lambda i, j: (i, 0))]
        out_shape = [out_shape, jax.ShapeDtypeStruct((M, ns), F32)]
        args += (side_w,)
    return pl.pallas_call(
        _rms_matmul_kernel,
        name="rms_matmul",
        grid=(M // tm, N // tn),
        in_specs=in_specs,
        out_specs=out_specs,
        out_shape=out_shape,
        scratch_shapes=[pltpu.VMEM((tm, K), BF16)],
        compiler_params=_params("parallel", "arbitrary"),
    )(*args)


def _out_proj_kernel(*refs, n_in):
    a_refs, w_refs = refs[:n_in], refs[n_in:2 * n_in]
    res_ref, o_ref = refs[2 * n_in], refs[2 * n_in + 1]
    acc = res_ref[...]
    for a_ref, w_ref in zip(a_refs, w_refs):
        acc = acc + _dot(a_ref[...], w_ref[...])
    o_ref[...] = acc


def out_proj(acts, w, layer, res, tm):
    M, N = res.shape
    tm = min(tm, M)
    assert M % tm == 0
    n_in = len(acts)
    ka = acts[0].shape[1]
    assert all(a.shape[1] == ka for a in acts) and n_in * ka == w.shape[1]
    row_block = lambda r: pl.BlockSpec((None, ka, N), lambda i: (layer, r, 0))
    weights = [w] * n_in
    in_specs = ([pl.BlockSpec((tm, ka), lambda i: (i, 0)) for _ in acts]
                + [row_block(r) for r in range(n_in)]
                + [pl.BlockSpec((tm, N), lambda i: (i, 0))])
    return pl.pallas_call(
        functools.partial(_out_proj_kernel, n_in=n_in),
        name="out_proj",
        grid=(M // tm,),
        in_specs=in_specs,
        out_specs=pl.BlockSpec((tm, N), lambda i: (i, 0)),
        out_shape=jax.ShapeDtypeStruct((M, N), F32),
        compiler_params=_params("parallel"),
    )(*acts, *weights, res)


def _ffn_kernel(x_ref, g_ref, w1_ref, w3_ref, w2_ref, o_ref, hn_ref):
    f = pl.program_id(1)

    @pl.when(f == 0)
    def _():
        x = x_ref[...]
        hn_ref[...] = _rms(x, g_ref[...]).astype(BF16)
        o_ref[...] = x

    h = hn_ref[...]
    a = _dot(h, w1_ref[...].astype(BF16))
    b = _dot(h, w3_ref[...].astype(BF16))
    u = (a * jax.nn.sigmoid(a) * b).astype(BF16)
    o_ref[...] += _dot(u, w2_ref[...].astype(BF16))


def ffn(x, gain, w1, w3, w2, layer, tm, tf):
    M, D = x.shape
    Fh = w1.shape[2]
    tm, tf = min(tm, M), min(tf, Fh)
    assert M % tm == 0 and Fh % tf == 0
    return pl.pallas_call(
        _ffn_kernel,
        name="ffn",
        grid=(M // tm, Fh // tf),
        in_specs=[pl.BlockSpec((tm, D), lambda i, f: (i, 0)),
                  pl.BlockSpec((1, D), lambda i, f: (0, 0)),
                  pl.BlockSpec((None, D, tf), lambda i, f: (layer, 0, f)),
                  pl.BlockSpec((None, D, tf), lambda i, f: (layer, 0, f)),
                  pl.BlockSpec((None, tf, D), lambda i, f: (layer, f, 0))],
        out_specs=pl.BlockSpec((tm, D), lambda i, f: (i, 0)),
        out_shape=jax.ShapeDtypeStruct((M, D), F32),
        scratch_shapes=[pltpu.VMEM((tm, D), BF16)],
        compiler_params=_params("parallel", "arbitrary"),
    )(x, gain.reshape(1, D), w1, w3, w2)


def _mem_attn_kernel(x_ref, g_ref, wq_ref, kv_ref, qg_ref, kg_ref, wo_ref, o_ref, *, heads):
    x = x_ref[0]
    h = _rms(x, g_ref[...]).astype(BF16)
    q = _dot(h, wq_ref[...])
    kv = kv_ref[0]
    w = heads * HEAD_DIM
    outs = []
    for hh in range(heads):
        sl = slice(hh * HEAD_DIM, (hh + 1) * HEAD_DIM)
        qh = (_rms(q[:, sl], qg_ref[...]) * ATTN_SCALE).astype(BF16)
        kh = _rms(kv[:, sl], kg_ref[...]).astype(BF16)
        vh = kv[:, w + hh * HEAD_DIM: w + (hh + 1) * HEAD_DIM].astype(BF16)
        s = _dot_nt(qh, kh)
        p = jnp.exp(s - jnp.max(s, axis=-1, keepdims=True))
        z = jnp.sum(p, axis=-1, keepdims=True)
        outs.append((_dot(p.astype(BF16), vh) / z).astype(BF16))
    o = jnp.concatenate(outs, axis=-1)
    o_ref[0] = x + _dot(o, wo_ref[...])


def mem_attn(x, gain, wq, kv, q_gain, k_gain, wo, layer, tm):
    B, T, D = x.shape
    L, W2 = kv.shape[1], kv.shape[2]
    W = W2 // 2
    tm = min(tm, T)
    assert T % tm == 0
    return pl.pallas_call(
        functools.partial(_mem_attn_kernel, heads=W // HEAD_DIM),
        name="mem_attn",
        grid=(B, T // tm),
        in_specs=[pl.BlockSpec((1, tm, D), lambda b, i: (b, i, 0)),
                  pl.BlockSpec((1, D), lambda b, i: (0, 0)),
                  pl.BlockSpec((None, D, W), lambda b, i: (layer, 0, 0)),
                  pl.BlockSpec((1, L, W2), lambda b, i: (b, 0, 0)),
                  pl.BlockSpec((1, HEAD_DIM), lambda b, i: (0, 0)),
                  pl.BlockSpec((1, HEAD_DIM), lambda b, i: (0, 0)),
                  pl.BlockSpec((None, W, D), lambda b, i: (layer, 0, 0))],
        out_specs=pl.BlockSpec((1, tm, D), lambda b, i: (b, i, 0)),
        out_shape=jax.ShapeDtypeStruct((B, T, D), F32),
        compiler_params=_params("parallel", "parallel"),
    )(x, gain.reshape(1, D), wq, kv, q_gain.reshape(1, HEAD_DIM), k_gain.reshape(1, HEAD_DIM), wo)


def _eye(n):
    return (lax.broadcasted_iota(jnp.int32, (n, n), 0)
            == lax.broadcasted_iota(jnp.int32, (n, n), 1)).astype(BF16)


def _ones_row(rows, n):
    return (lax.broadcasted_iota(jnp.int32, (rows, n), 0) == 0).astype(BF16)


def _split3_f32(x):
    return tuple(t.astype(F32) for t in _split3(x))


COL_GROUPS = 1


def _col_groups(n):
    w = n // COL_GROUPS
    return [slice(i * w, (i + 1) * w) for i in range(COL_GROUPS)]


def _attn_init(n):
    return tuple((jnp.full((1, c.stop - c.start), NEG_INF, F32), jnp.zeros((VT_ROWS, c.stop - c.start), F32))
                 for c in _col_groups(n))


def _attn_tile(k_tile, vt_tile, qa, carry, score_fn=None):
    out = []
    for (m, acc), c in zip(carry, _col_groups(qa.shape[1])):
        s = _dot(k_tile, qa[:, c])
        if score_fn is not None:
            s = score_fn(s, c)
        m_new = jnp.maximum(m, jnp.max(s, axis=0, keepdims=True))
        p = jnp.exp2(s - m_new).astype(BF16)
        out.append((m_new, jnp.exp2(m - m_new) * acc + _dot(vt_tile, p)))
    return tuple(out)


def _attn_finish(carry):
    return jnp.concatenate([acc for _, acc in carry], axis=1)


def _fox_gate_kernel(g_ref, b_ref, c_ref, kf_ref, *, heads):
    T = g_ref.shape[1]
    r = lax.broadcasted_iota(jnp.int32, (LANES, LANES), 0)
    lane = lax.broadcasted_iota(jnp.int32, (LANES, LANES), 1)
    tri = (r >= lane).astype(BF16)

    def body(i, carry):
        rows = pl.ds(pl.multiple_of(i * LANES, LANES), LANES)
        cs = _exact_dot(tri, _log_sigmoid(g_ref[0, rows, :] + b_ref[...])) + carry
        c_ref[0, rows, :] = cs
        c2 = cs * LOG2E
        for h in range(heads):
            d1, d2, d3 = _split3_f32(c2[:, h:h + 1])
            feat = jnp.where(lane < 3, 1.0,
                             jnp.where(lane == 3, -d1, jnp.where(lane == 4, -d2,
                                                                 jnp.where(lane == 5, -d3, 0.0))))
            kf_ref[0, h, rows, :] = feat.astype(BF16)
        return cs[LANES - 1:LANES, :]

    lax.fori_loop(0, T // LANES, body, jnp.zeros((1, LANES), F32))


def fox_gate(g, bias, heads):
    B, T, _ = g.shape
    return pl.pallas_call(
        functools.partial(_fox_gate_kernel, heads=heads),
        name="fox_gate",
        grid=(B,),
        in_specs=[pl.BlockSpec((1, T, LANES), lambda b: (b, 0, 0)),
                  pl.BlockSpec((1, LANES), lambda b: (0, 0))],
        out_specs=[pl.BlockSpec((1, T, LANES), lambda b: (b, 0, 0)),
                   pl.BlockSpec((1, heads, T, LANES), lambda b: (b, 0, 0, 0))],
        out_shape=[jax.ShapeDtypeStruct((B, T, LANES), F32),
                   jax.ShapeDtypeStruct((B, heads, T, LANES), BF16)],
        compiler_params=_params("parallel"),
    )(g, bias)


def _fox_attn_kernel(q_ref, k_ref, v_ref, kf_ref, c_ref, qg_ref, kg_ref, o_ref, ka_ref, vt_ref, *, tq):
    qi = pl.program_id(2)
    T = k_ref.shape[1]
    eye = _eye(HEAD_DIM)

    @pl.when(qi == 0)
    def _():
        ka_ref[:, :HEAD_DIM] = _rms(k_ref[0], kg_ref[...]).astype(BF16)
        ka_ref[:, HEAD_DIM:] = kf_ref[0, 0]
        vt_ref[:HEAD_DIM, :] = _dot_nt(eye, v_ref[0].astype(BF16)).astype(BF16)
        vt_ref[HEAD_DIM:, :] = _ones_row(VT_ROWS - HEAD_DIM, T)

    q0 = pl.multiple_of(qi * tq, tq)
    qn = (_rms(q_ref[0], qg_ref[...]) * (ATTN_SCALE * LOG2E)).astype(BF16)
    qt = _dot_nt(eye, qn).astype(BF16)
    c1, c2, c3 = _split3_f32(c_ref[0, :, pl.ds(q0, tq)] * LOG2E)
    r = lax.broadcasted_iota(jnp.int32, (16, tq), 0)
    feat = jnp.where(r == 0, c1, jnp.where(r == 1, c2, jnp.where(r == 2, c3,
                                                                 jnp.where(r < 6, 1.0, 0.0))))
    qa = jnp.concatenate([qt, feat.astype(BF16), jnp.zeros((AUG - HEAD_DIM - 16, tq), BF16)], axis=0)

    kpos = lax.broadcasted_iota(jnp.int32, (tq, 1), 0)
    tpos = lax.broadcasted_iota(jnp.int32, (1, tq), 1)

    def tile(j, carry, score_fn=None):
        k0 = pl.multiple_of(j * tq, tq)
        return _attn_tile(ka_ref[pl.ds(k0, tq), :], vt_ref[:, pl.ds(k0, tq)], qa, carry, score_fn)

    carry = tile(qi, _attn_init(tq), lambda s, c: jnp.where(kpos <= tpos[:, c], s, NEG_INF))
    acc = _attn_finish(lax.fori_loop(0, qi, tile, carry))
    ot = (acc[:HEAD_DIM] / acc[HEAD_DIM:HEAD_DIM + 1]).astype(BF16)
    for c in range(tq // LANES):
        cs = slice(c * LANES, (c + 1) * LANES)
        o_ref[0, cs, :] = _dot_nt(eye, ot[:, cs]).astype(o_ref.dtype)


def fox_attn(proj, kfeat, c_rows, q_gain, k_gain, tq):
    B, T, _ = proj.shape
    H = FOX_HEADS
    tq = min(tq, T)
    assert T % tq == 0 and tq % LANES == 0
    return pl.pallas_call(
        functools.partial(_fox_attn_kernel, tq=tq),
        name="fox_attn",
        grid=(B, H, T // tq),
        in_specs=[pl.BlockSpec((1, tq, HEAD_DIM), lambda b, h, i: (b, i, h)),
                  pl.BlockSpec((1, T, HEAD_DIM), lambda b, h, i: (b, 0, H + h)),
                  pl.BlockSpec((1, T, HEAD_DIM), lambda b, h, i: (b, 0, 2 * H + h)),
                  pl.BlockSpec((1, 1, T, LANES), lambda b, h, i: (b, h, 0, 0)),
                  pl.BlockSpec((1, 1, T), lambda b, h, i: (b * H + h, 0, 0)),
                  pl.BlockSpec((1, HEAD_DIM), lambda b, h, i: (0, 0)),
                  pl.BlockSpec((1, HEAD_DIM), lambda b, h, i: (0, 0))],
        out_specs=pl.BlockSpec((1, tq, HEAD_DIM), lambda b, h, i: (b, i, h)),
        out_shape=jax.ShapeDtypeStruct((B, T, H * HEAD_DIM), BF16),
        scratch_shapes=[pltpu.VMEM((T, AUG), BF16), pltpu.VMEM((VT_ROWS, T), BF16)],
        compiler_params=_params("parallel", "parallel", "arbitrary"),
    )(proj, proj, proj, kfeat, c_rows, q_gain.reshape(1, HEAD_DIM), k_gain.reshape(1, HEAD_DIM))


def _hgrn_kernel(q_ref, f_ref, i_ref, g_ref, la_ref, l1_ref, oml_ref, og_ref, o_ref, *, bs, chunk, span):
    T = q_ref.shape[1]
    R = LANES
    ri = lax.broadcasted_iota(jnp.int32, (R, R), 0)
    ci = lax.broadcasted_iota(jnp.int32, (R, R), 1)

    def prefix_sel(n):
        return (((ri // n) == (ci // n)) & (ri >= ci)).astype(BF16)

    s_idx = lax.broadcasted_iota(jnp.int32, (bs, HEAD_DIM), 0)
    tril = (lax.broadcasted_iota(jnp.int32, (chunk, chunk), 0)
            >= lax.broadcasted_iota(jnp.int32, (chunk, chunk), 1))

    def gates(row0, n):
        rows = pl.ds(row0, n)
        z = f_ref[0, rows, :]
        a = la_ref[0]
        y = l1_ref[0] + _log_sigmoid(z)
        log_f = jnp.maximum(a, y) + jnp.log1p(jnp.exp(-jnp.abs(a - y)))
        hq = q_ref[0, rows, :]
        return log_f, oml_ref[0] * jax.nn.sigmoid(-z), hq * jax.nn.sigmoid(hq), i_ref[0, rows, :]

    def prefixes(log_f, n):
        sel = prefix_sel(n)
        pieces = _split3(log_f)
        b = jnp.concatenate([sum(_dot(sel, p[g * R:(g + 1) * R]) for p in pieces)
                             for g in range(log_f.shape[0] // R)], axis=0)
        bt = jnp.concatenate([jnp.broadcast_to(b[(u + 1) * n - 1:(u + 1) * n, :], (n, HEAD_DIM))
                              for u in range(log_f.shape[0] // n)], axis=0)
        return b, bt

    def finish(row0, n, o):
        rows = pl.ds(row0, n)
        hg = g_ref[0, rows, :]
        o_ref[0, rows, :] = (_rms(o, og_ref[...]) * (hg * jax.nn.sigmoid(hg))).astype(o_ref.dtype)

    def matmul_path(row0, st, kk, qq, vv, b, bt):
        qe = (qq * jnp.exp(b)).astype(BF16)
        k_inv = (kk * jnp.exp(-b)).astype(BF16)
        ke = (kk * jnp.exp(bt - b)).astype(BF16)
        dec = jnp.exp(bt)
        vb = vv.astype(BF16)
        sls = [slice(u * chunk, (u + 1) * chunk) for u in range(span // chunk)]
        scores = [jnp.where(tril, _dot_nt(qe[sl], k_inv[sl]), 0.0).astype(BF16) for sl in sls]
        intra = [_dot(a, vb[sl]) for a, sl in zip(scores, sls)]
        update = [_dot_tn(vb[sl], ke[sl]) for sl in sls]
        outs = []
        for sl, o_in, upd in zip(sls, intra, update):
            outs.append(_dot_nt(qe[sl], st.astype(BF16)) + o_in)
            st = dec[sl.start:sl.start + 1, :] * st + upd
        finish(row0, span, jnp.concatenate(outs, axis=0))
        return st

    def pairwise_step(i, st, base):
        row0 = pl.multiple_of(base + i * R, R)
        log_f, kk, qq, vv = gates(row0, R)
        b, bt = prefixes(log_f, bs)
        qe = (qq * jnp.exp(b)).astype(BF16)
        ke = (kk * jnp.exp(bt - b)).astype(BF16)
        dec = jnp.exp(bt)
        vb = vv.astype(BF16)
        outs = []
        for u in range(R // bs):
            sl = slice(u * bs, (u + 1) * bs)
            bu, vu, qk, ku = b[sl], vv[sl], qq[sl], kk[sl]
            intra = []
            for t in range(bs):
                w = jnp.exp(jnp.where(s_idx <= t, bu[t:t + 1, :] - bu, NEG_INF))
                a_t = jnp.sum(w * (qk[t:t + 1, :] * ku), axis=-1, keepdims=True)
                intra.append(jnp.sum(a_t * vu, axis=0, keepdims=True))
            outs.append(_dot_nt(qe[sl], st.astype(BF16)) + jnp.concatenate(intra, axis=0))
            st = dec[u * bs:u * bs + 1, :] * st + _dot_tn(vb[sl], ke[sl])
        finish(row0, R, jnp.concatenate(outs, axis=0))
        return st

    def step(i, st):
        row0 = pl.multiple_of(i * span, span)
        log_f, kk, qq, vv = gates(row0, span)
        b, bt = prefixes(log_f, chunk)
        return lax.cond(
            jnp.min(bt) >= HGRN_SAFE_LOG_DECAY,
            lambda st: matmul_path(row0, st, kk, qq, vv, b, bt),
            lambda st: lax.fori_loop(0, span // R, functools.partial(pairwise_step, base=row0), st),
            st)

    lax.fori_loop(0, T // span, step, jnp.zeros((HEAD_DIM, HEAD_DIM), F32))


def hgrn(proj, col0, log_lb, log1m_lb, one_m_lb, o_gain):
    B, T, _ = proj.shape
    H = HGRN_HEADS
    row = lambda off: pl.BlockSpec((1, T, HEAD_DIM), lambda b, h: (b, 0, col0 + off * H + h))
    par = pl.BlockSpec((1, 1, HEAD_DIM), lambda b, h: (h, 0, 0))
    log_lb, log1m_lb, one_m_lb = (p.reshape(H, 1, HEAD_DIM) for p in (log_lb, log1m_lb, one_m_lb))
    return pl.pallas_call(
        functools.partial(_hgrn_kernel, bs=HGRN_BLOCK, chunk=HGRN_CHUNK, span=min(HGRN_SPAN, T)),
        name="hgrn",
        grid=(B, H),
        in_specs=[row(0), row(1), row(2), row(3), par, par, par,
                  pl.BlockSpec((1, HEAD_DIM), lambda b, h: (0, 0))],
        out_specs=pl.BlockSpec((1, T, HEAD_DIM), lambda b, h: (b, 0, h)),
        out_shape=jax.ShapeDtypeStruct((B, T, H * HEAD_DIM), BF16),
        compiler_params=_params("parallel", "parallel"),
    )(proj, proj, proj, proj, log_lb, log1m_lb, one_m_lb, o_gain.reshape(1, HEAD_DIM))


def _compress_kernel(t_ref, pe_ref, w1_ref, w2_ref, g_ref, o_ref, *, normalize):
    N = t_ref.shape[1] // CMP_STRIDE
    top = bot = jnp.zeros((N, w1_ref.shape[2]), F32)
    for l in range(CMP_STRIDE):
        xl = t_ref[0, pl.ds(l, N, stride=CMP_STRIDE), :]
        top = top + _dot((xl + pe_ref[l:l + 1, :]).astype(BF16), w1_ref[l])
        bot = bot + _dot((xl + pe_ref[CMP_STRIDE + l:CMP_STRIDE + l + 1, :]).astype(BF16),
                         w1_ref[CMP_STRIDE + l])
    hid = top + pltpu.roll(bot, N - 1, 0)
    out = _dot(jax.nn.gelu(hid).astype(BF16), w2_ref[...])
    if normalize:
        out = _rms(out, g_ref[...])
    valid = lax.broadcasted_iota(jnp.int32, out.shape, 0) < N - 1
    o_ref[0, 0] = jnp.where(valid, out, 0.0)


def compress(proj, col0, pe, w1, w2, layer, gain, normalize):
    B, T, _ = proj.shape
    G = NSA_KV_HEADS
    N = T // CMP_STRIDE
    Hc = w1.shape[3]
    assert CMP_BLOCK == 2 * CMP_STRIDE
    return pl.pallas_call(
        functools.partial(_compress_kernel, normalize=normalize),
        name="nsa_compress",
        grid=(B, G),
        in_specs=[pl.BlockSpec((1, T, HEAD_DIM), lambda b, g: (b, 0, col0 + g)),
                  pl.BlockSpec((None, CMP_BLOCK, HEAD_DIM), lambda b, g: (layer, 0, 0)),
                  pl.BlockSpec((None, CMP_BLOCK, HEAD_DIM, Hc), lambda b, g: (layer, 0, 0, 0)),
                  pl.BlockSpec((None, Hc, HEAD_DIM), lambda b, g: (layer, 0, 0)),
                  pl.BlockSpec((1, HEAD_DIM), lambda b, g: (0, 0))],
        out_specs=pl.BlockSpec((1, 1, N, HEAD_DIM), lambda b, g: (b, g, 0, 0)),
        out_shape=jax.ShapeDtypeStruct((B, G, N, HEAD_DIM), F32),
        compiler_params=_params("parallel", "parallel"),
    )(proj, pe, w1, w2, gain.reshape(1, HEAD_DIM))


_SEL_LANE0 = 64
_CMP_OV_ROW0 = VT_ROWS


def _nsa_kernel(q_ref, ks_ref, vs_ref, kw_ref, vw_ref, kc_ref, vc_ref, gt_ref, qg_ref, kg_ref,
                o_ref, ksa_ref, kwa_ref, vst_ref, vwt_ref, kca_ref, vcov_ref, *, tq, tk, n_sel):
    g = pl.program_id(1)
    qi = pl.program_id(2)
    T = ks_ref.shape[1]
    J = NSA_GROUP
    n_slc = T // SLC_BLOCK
    nc = kc_ref.shape[2]
    n_cmp = nc - 1
    cols = J * tq
    eye = _eye(HEAD_DIM)

    @pl.when(qi == 0)
    def _():
        row = lax.broadcasted_iota(jnp.int32, (T, LANES), 0)
        lane = lax.broadcasted_iota(jnp.int32, (T, LANES), 1)
        blk = row // SLC_BLOCK
        feats = jnp.where(lane < 3, blk,
                          jnp.where(lane < 6, row % SLC_BLOCK,
                                    jnp.where(lane < 9, 1,
                                              jnp.where(lane - _SEL_LANE0 == blk, 1, 0))))
        feats = feats.astype(F32).astype(BF16)
        ksa_ref[:, :HEAD_DIM] = _rms(ks_ref[0], kg_ref[1:2, :]).astype(BF16)
        ksa_ref[:, HEAD_DIM:] = feats
        kwa_ref[:, :HEAD_DIM] = _rms(kw_ref[0], kg_ref[2:3, :]).astype(BF16)
        kwa_ref[:, HEAD_DIM:] = feats
        vst_ref[:HEAD_DIM, :] = _dot_nt(eye, vs_ref[0].astype(BF16)).astype(BF16)
        vst_ref[HEAD_DIM:, :] = _ones_row(VT_ROWS - HEAD_DIM, T)
        vwt_ref[:HEAD_DIM, :] = _dot_nt(eye, vw_ref[0].astype(BF16)).astype(BF16)
        vwt_ref[HEAD_DIM:, :] = _ones_row(VT_ROWS - HEAD_DIM, T)
        n_r = lax.broadcasted_iota(jnp.int32, (nc, LANES), 0)
        n_l = lax.broadcasted_iota(jnp.int32, (nc, LANES), 1)
        per = SLC_BLOCK // CMP_STRIDE
        cfe = jnp.where(n_l < 3, (n_r // per).astype(F32),
                        jnp.where(n_l < 6, (CMP_STRIDE * (n_r % per)).astype(F32) + 0.5 * (CMP_BLOCK - 1),
                                  jnp.where(n_l < 9, 1.0, 0.0)))
        kca_ref[:, :HEAD_DIM] = kc_ref[0, 0].astype(BF16)
        kca_ref[:, HEAD_DIM:] = cfe.astype(BF16)
        vcov_ref[:HEAD_DIM, :] = _dot_nt(eye, vc_ref[0, 0].astype(BF16)).astype(BF16)
        vcov_ref[HEAD_DIM:_CMP_OV_ROW0, :] = _ones_row(VT_ROWS - HEAD_DIM, nc)
        m_r = lax.broadcasted_iota(jnp.int32, (SLC_BLOCK, nc), 0)
        n_c = lax.broadcasted_iota(jnp.int32, (SLC_BLOCK, nc), 1)
        vcov_ref[_CMP_OV_ROW0:, :] = ((n_c * CMP_STRIDE <= m_r * SLC_BLOCK + (SLC_BLOCK - 1))
                                      & (n_c * CMP_STRIDE + (CMP_BLOCK - 1) >= m_r * SLC_BLOCK)
                                      & (n_c < n_cmp) & (m_r < n_slc)).astype(BF16)

    q0 = pl.multiple_of(qi * tq, tq)
    qraw = q_ref[0]
    qt = jnp.concatenate(
        [_dot_nt(eye, (_rms(qraw[:, j * HEAD_DIM:(j + 1) * HEAD_DIM], qg_ref[...])
                       * (ATTN_SCALE * LOG2E)).astype(BF16)).astype(BF16)
         for j in range(J)], axis=1)

    col = lax.broadcasted_iota(jnp.int32, (1, cols), 1)
    tt = q0 + col % tq
    slope2 = jnp.exp2(-0.5 * (g * J + col // tq + 1).astype(F32)) * LOG2E
    s1, s2, s3 = _split3_f32(slope2)
    r1, r2, r3 = _split3_f32(-slope2 * tt.astype(F32))
    r16 = lax.broadcasted_iota(jnp.int32, (16, cols), 0)
    feat = jnp.zeros((16, cols), F32)
    for i, v in enumerate((s1 * SLC_BLOCK, s2 * SLC_BLOCK, s3 * SLC_BLOCK, s1, s2, s3, r1, r2, r3)):
        feat = jnp.where(r16 == i, v, feat)
    feat = feat.astype(BF16)
    qa_base = jnp.concatenate([qt, feat, jnp.zeros((AUG - HEAD_DIM - 16, cols), BF16)], axis=0)

    n_col = lax.broadcasted_iota(jnp.int32, (nc, 1), 0)
    cmp_end = n_col * CMP_STRIDE + (CMP_BLOCK - 1)
    kca = kca_ref[...]
    r_hi, r_lo = [], []
    for c in _col_groups(cols):
        s = _dot(kca, qa_base[:, c])
        s = jnp.where((cmp_end <= tt[:, c]) & (n_col < n_cmp), s, NEG_INF)
        m = jnp.max(s, axis=0, keepdims=True)
        p = jnp.exp2(s - jnp.where(m == NEG_INF, 0.0, m))
        p_hi = p.astype(BF16)
        r_hi.append(_dot(vcov_ref[...], p_hi))
        r_lo.append(_dot(vcov_ref[HEAD_DIM:, :], (p - p_hi.astype(F32)).astype(BF16)))
    r_hi = jnp.concatenate(r_hi, axis=1)
    r_lo = jnp.concatenate(r_lo, axis=1)
    z = r_hi[HEAD_DIM:HEAD_DIM + 1] + r_lo[0:1]
    inv_z = 1.0 / jnp.where(z > 0, z, 1.0)

    wq = min(tq, WINDOW_Q)
    span = WINDOW + wq
    d = (lax.broadcasted_iota(jnp.int32, (span, 1), 0)
         - lax.broadcasted_iota(jnp.int32, (1, wq), 1))
    parts = []
    for c in range(tq // wq):
        k0w = pl.multiple_of(jnp.maximum(q0 + c * wq - WINDOW, 0), wq)
        off = q0 + c * wq - k0w
        bias = jnp.where((d <= off) & (d > off - WINDOW), 0.0, NEG_INF)
        bias = jnp.concatenate([bias] * J, axis=1)
        qa_c = jnp.concatenate([qa_base[:, j * tq + c * wq:j * tq + (c + 1) * wq] for j in range(J)], axis=1)
        parts.append(_attn_finish(_attn_tile(
            kwa_ref[pl.ds(k0w, span), :], vwt_ref[:, pl.ds(k0w, span)], qa_c, _attn_init(J * wq),
            lambda s, sub, bias=bias: s + bias[:, sub])))
    acc_w = jnp.concatenate([p[:, j * wq:(j + 1) * wq] for j in range(J) for p in parts], axis=1)

    imp4 = (r_hi[_CMP_OV_ROW0:] + r_lo[_CMP_OV_ROW0 - HEAD_DIM:]) * inv_z
    imp = imp4[:, 0:tq]
    for j in range(1, J):
        imp = imp + imp4[:, j * tq:(j + 1) * tq]
    blk = lax.broadcasted_iota(jnp.int32, (SLC_BLOCK, tq), 0)
    t_q = q0 + lax.broadcasted_iota(jnp.int32, (1, tq), 1)
    imp = jnp.where(blk * SLC_BLOCK > t_q, NEG_INF, imp)
    imp = jnp.where((blk == t_q // SLC_BLOCK) | (blk == 0), jnp.inf, imp)
    blk_f = blk.astype(F32)
    taken = blk >= n_slc
    for _ in range(n_sel):
        vals = jnp.where(taken, NEG_INF, imp)
        best = jnp.max(vals, axis=0, keepdims=True)
        cand = (vals == best) & jnp.logical_not(taken)
        first = jnp.min(jnp.where(cand, blk_f, float(SLC_BLOCK)), axis=0, keepdims=True)
        taken = taken | (blk_f == first)
    not_sel = jnp.where(taken & (blk < n_slc), 0.0, -MASK_BIG).astype(BF16)
    qa_slc = jnp.concatenate([qt, feat, jnp.zeros((_SEL_LANE0 - 16, cols), BF16)]
                             + [jnp.concatenate([not_sel] * J, axis=1)], axis=0)

    def slc_tile(jt, carry, score_fn=None):
        k0 = pl.multiple_of(jt * tk, tk)
        return _attn_tile(ksa_ref[pl.ds(k0, tk), :], vst_ref[:, pl.ds(k0, tk)], qa_slc, carry, score_fn)

    jd = q0 // tk
    kpos = jd * tk + lax.broadcasted_iota(jnp.int32, (tk, 1), 0)
    carry = slc_tile(jd, _attn_init(cols), lambda s, c: jnp.where(kpos <= tt[:, c], s, NEG_INF))
    acc_s = _attn_finish(lax.fori_loop(0, jd, slc_tile, carry))

    gates_t = jnp.transpose(jax.nn.sigmoid(gt_ref[0]))
    gate = lambda c: jnp.concatenate([gates_t[3 * j + c:3 * j + c + 1, :] for j in range(J)], axis=1)
    o_t = ((gate(0) * inv_z) * r_hi[:HEAD_DIM]
           + (gate(1) / acc_s[HEAD_DIM:HEAD_DIM + 1]) * acc_s[:HEAD_DIM]
           + (gate(2) / acc_w[HEAD_DIM:HEAD_DIM + 1]) * acc_w[:HEAD_DIM]).astype(BF16)
    for j in range(J):
        for c in range(tq // LANES):
            o_ref[0, c * LANES:(c + 1) * LANES, j * HEAD_DIM:(j + 1) * HEAD_DIM] = _dot_nt(
                eye, o_t[:, j * tq + c * LANES:j * tq + (c + 1) * LANES]).astype(o_ref.dtype)


def nsa_attn(proj, gates, k_cmp, v_cmp, q_gain, k_gain, tq, tk):
    B, T, _ = proj.shape
    G, J = NSA_KV_HEADS, NSA_GROUP
    tk = min(tk, T)
    nc = k_cmp.shape[2]
    assert T % tk == 0 and tk % tq == 0 and T >= WINDOW + tq
    assert T // SLC_BLOCK <= LANES - _SEL_LANE0 and SLC_BLOCK <= LANES - _SEL_LANE0
    qcols = NSA_HEADS
    kv = lambda slab: pl.BlockSpec((1, T, HEAD_DIM), lambda b, g, i: (b, 0, qcols + slab * G + g))
    cmp_spec = pl.BlockSpec((1, 1, nc, HEAD_DIM), lambda b, g, i: (b, g, 0, 0))
    return pl.pallas_call(
        functools.partial(_nsa_kernel, tq=tq, tk=tk, n_sel=min(SLC_TOPK, T // SLC_BLOCK)),
        name="nsa_attn",
        grid=(B, G, T // tq),
        in_specs=[pl.BlockSpec((1, tq, J * HEAD_DIM), lambda b, g, i: (b, i, g)),
                  kv(2), kv(3), kv(4), kv(5), cmp_spec, cmp_spec,
                  pl.BlockSpec((1, tq, LANES), lambda b, g, i: (b, i, g)),
                  pl.BlockSpec((1, HEAD_DIM), lambda b, g, i: (0, 0)),
                  pl.BlockSpec((3, HEAD_DIM), lambda b, g, i: (0, 0))],
        out_specs=pl.BlockSpec((1, tq, J * HEAD_DIM), lambda b, g, i: (b, i, g)),
        out_shape=jax.ShapeDtypeStruct((B, T, NSA_HEADS * HEAD_DIM), BF16),
        scratch_shapes=[pltpu.VMEM((T, AUG), BF16), pltpu.VMEM((T, AUG), BF16),
                        pltpu.VMEM((VT_ROWS, T), BF16), pltpu.VMEM((VT_ROWS, T), BF16),
                        pltpu.VMEM((nc, AUG), BF16),
                        pltpu.VMEM((VT_ROWS + SLC_BLOCK, nc), BF16)],
        compiler_params=_params("parallel", "parallel", "arbitrary"),
    )(proj, proj, proj, proj, proj, k_cmp, v_cmp, gates, q_gain.reshape(1, HEAD_DIM), k_gain)


def _pack_cols_kernel(a_ref, b_ref, o_ref, *, first_shifted, shift):
    if shift == 0:
        o_ref[...] = a_ref[...].astype(o_ref.dtype)
        return
    k = pl.program_id(1)

    @pl.when(k < first_shifted)
    def _():
        o_ref[...] = a_ref[...].astype(o_ref.dtype)

    @pl.when(k >= first_shifted)
    def _():
        o_ref[...] = jnp.concatenate([a_ref[:, shift:], b_ref[:, :shift]], axis=1).astype(o_ref.dtype)


def pack_cols(w, n_out, gap_at, gap, bw):
    L, K, N = w.shape
    assert gap_at % bw == 0 and n_out % bw == 0 and 0 <= gap < bw and n_out + gap <= N
    first_shifted = gap_at // bw
    last = (N - 1) // bw
    return pl.pallas_call(
        functools.partial(_pack_cols_kernel, first_shifted=first_shifted, shift=gap),
        name="pack_cols",
        grid=(L, n_out // bw),
        in_specs=[pl.BlockSpec((None, K, bw), lambda l, k: (l, 0, k)),
                  pl.BlockSpec((None, K, bw), lambda l, k: (l, 0, jnp.minimum(k + 1, last)))],
        out_specs=pl.BlockSpec((None, K, bw), lambda l, k: (l, 0, k)),
        out_shape=jax.ShapeDtypeStruct((L, K, n_out), BF16),
        compiler_params=_params("parallel", "parallel"),
    )(w, w)


def even_weights(w_in, w_out):
    fw = FOX_HEADS * HEAD_DIM
    w_main = pack_cols(w_in, w_in.shape[2] - FOX_HEADS, 3 * fw, FOX_HEADS, 512)
    w_gate = jnp.pad(w_in[:, :, 3 * fw:3 * fw + FOX_HEADS], ((0, 0), (0, 0), (0, LANES - FOX_HEADS)))
    return w_main, w_gate.astype(BF16), w_out.astype(BF16)


def even_mixer(x, gain, weights, e, f_bias, q_gain, k_gain, lb, o_gain):
    B, T, D = x.shape
    xf = x.reshape(B * T, D)
    w_main, w_gate, w_out = weights
    proj, gate = rms_matmul(xf, gain, w_main, e, 1024, 1024, side_w=w_gate)
    proj, gate = proj.reshape(B, T, -1), gate.reshape(B, T, LANES)
    f_bias = jnp.pad(f_bias.reshape(1, FOX_HEADS).astype(F32), ((0, 0), (0, LANES - FOX_HEADS)))
    c, kfeat = fox_gate(gate, f_bias, FOX_HEADS)
    c_rows = c[:, :, :FOX_HEADS].transpose(0, 2, 1).reshape(B * FOX_HEADS, 1, T)
    o_fox = fox_attn(proj, kfeat, c_rows, q_gain, k_gain, 1024)
    lb = lb.astype(F32).reshape(HGRN_HEADS, HEAD_DIM)
    o_h = hgrn(proj, 3 * FOX_HEADS, jnp.log(lb), jnp.log1p(-lb), 1.0 - lb, o_gain)
    out = out_proj([o_fox.reshape(B * T, -1), o_h.reshape(B * T, -1)], w_out, e, xf, 256)
    return out.reshape(B, T, D)


def nsa_weights(w_in, w_out, w1_k, w2_k, w1_v, w2_v):
    L, D, _ = w_in.shape
    G, J = NSA_KV_HEADS, NSA_GROUP
    n_main = (NSA_HEADS + 6 * G) * HEAD_DIM
    w_gate = w_in[:, :, n_main:].reshape(L, D, G, 3 * J)
    w_gate = jnp.pad(w_gate, ((0, 0), (0, 0), (0, 0), (0, LANES - 3 * J))).reshape(L, D, G * LANES)
    return (pack_cols(w_in, n_main, n_main, 0, 512), w_gate.astype(BF16), w_out.astype(BF16),
            w1_k.astype(BF16), w2_k.astype(BF16), w1_v.astype(BF16), w2_v.astype(BF16))


def nsa_mixer(x, gain, weights, o, q_gain, k_gain, pe_k, pe_v):
    B, T, D = x.shape
    G = NSA_KV_HEADS
    xf = x.reshape(B * T, D)
    w_main, w_gate, w_out, w1_k, w2_k, w1_v, w2_v = weights
    proj, gates = rms_matmul(xf, gain, w_main, o, 1024, 1024, side_w=w_gate)
    proj, gates = proj.reshape(B, T, -1), gates.reshape(B, T, G * LANES)
    k_cmp = compress(proj, NSA_HEADS, pe_k, w1_k, w2_k, o, k_gain[0], True)
    v_cmp = compress(proj, NSA_HEADS + G, pe_v, w1_v, w2_v, o, k_gain[0], False)
    att = nsa_attn(proj, gates, k_cmp, v_cmp, q_gain, k_gain, 512, 512)
    out = out_proj([att.reshape(B * T, -1)], w_out, o, xf, 256)
    return out.reshape(B, T, D)


def kernel(x, mem, norm_mix, norm_mem, norm_ffn, mem_in_gain, even_w_in, even_w_out, fox_f_bias,
           fox_q_gain, fox_k_gain, hgrn_lb_logits, hgrn_o_gain, odd_w_in, odd_w_out, nsa_q_gain,
           nsa_k_gain, cmp_pe_k, cmp_w1_k, cmp_w2_k, cmp_pe_v, cmp_w1_v, cmp_w2_v, mem_wq, mem_wkv,
           mem_wo, mem_q_gain, mem_k_gain, ffn_w1, ffn_w3, ffn_w2):
    B, T, D = x.shape
    depth = norm_mix.shape[0]
    L = mem.shape[1]
    lb_cum = jnp.cumsum(jax.nn.softmax(hgrn_lb_logits.astype(F32), axis=0), axis=0)
    hgrn_lb = lb_cum - lb_cum[0:1]
    memf = mem.reshape(B * L, D)
    even_w = even_weights(even_w_in, even_w_out)
    nsa_w = nsa_weights(odd_w_in, odd_w_out, cmp_w1_k, cmp_w2_k, cmp_w1_v, cmp_w2_v)
    mem_wq, mem_wkv, mem_wo = (w.astype(BF16) for w in (mem_wq, mem_wkv, mem_wo))
    for layer in range(depth):
        if layer % 2 == 0:
            e = layer // 2
            x = even_mixer(x, norm_mix[layer], even_w, e, fox_f_bias[e], fox_q_gain[e], fox_k_gain[e],
                           hgrn_lb[e], hgrn_o_gain[e])
        else:
            o = layer // 2
            x = nsa_mixer(x, norm_mix[layer], nsa_w, o, nsa_q_gain[o], nsa_k_gain[o], cmp_pe_k, cmp_pe_v)
        kv = rms_matmul(memf, mem_in_gain[layer], mem_wkv, layer, 256, 1024)
        x = mem_attn(x, norm_mem[layer], mem_wq, kv.reshape(B, L, -1), mem_q_gain[layer],
                     mem_k_gain[layer], mem_wo, layer, 256)
        x = ffn(x.reshape(B * T, D), norm_ffn[layer], ffn_w1, ffn_w3, ffn_w2, layer,
                1024, 256).reshape(B, T, D)
    return x
```

```python
import functools

import jax
import jax.numpy as jnp
from jax import lax
from jax.experimental import pallas as pl
from jax.experimental.pallas import tpu as pltpu

F32 = jnp.float32
BF16 = jnp.bfloat16
NEG_INF = float("-inf")

EPS = 1e-6
HEAD_DIM = 128
ATTN_SCALE = HEAD_DIM ** -0.5
LANES = 128
FOX_HEADS = 8
HGRN_HEADS = 8
HGRN_BLOCK = 16
HGRN_CHUNK = 64
HGRN_SPAN = 512
HGRN_SAFE_LOG_DECAY = -80.0
NSA_HEADS = 16
NSA_KV_HEADS = 4
NSA_GROUP = NSA_HEADS // NSA_KV_HEADS
CMP_BLOCK = 32
CMP_STRIDE = 16
SLC_BLOCK = 64
SLC_TOPK = 16
WINDOW = 512
WINDOW_Q = 256
MEM_HEADS = 4
V7X_VMEM_LIMIT = 60 * 1024 * 1024
LOG2E = 1.4426950408889634
MASK_BIG = 1e30
AUG = 2 * LANES
VT_ROWS = HEAD_DIM + 16


def _params(*sem):
    return pltpu.CompilerParams(dimension_semantics=sem, vmem_limit_bytes=V7X_VMEM_LIMIT)


def _rms(x, gain):
    return x * lax.rsqrt(jnp.mean(x * x, axis=-1, keepdims=True) + EPS) * gain


def _dot(a, b):
    return jnp.dot(a, b, preferred_element_type=F32)


def _dot_nt(a, b):
    return lax.dot_general(a, b, (((1,), (1,)), ((), ())), preferred_element_type=F32)


def _dot_tn(a, b):
    return lax.dot_general(a, b, (((0,), (0,)), ((), ())), preferred_element_type=F32)


def _split3(x):
    x1 = x.astype(BF16)
    r1 = x - x1.astype(F32)
    x2 = r1.astype(BF16)
    return x1, x2, (r1 - x2.astype(F32)).astype(BF16)


def _exact_dot(sel, x):
    x1, x2, x3 = _split3(x)
    return _dot(sel, x1) + _dot(sel, x2) + _dot(sel, x3)


def _log_sigmoid(z):
    return jnp.minimum(z, 0.0) - jnp.log1p(jnp.exp(-jnp.abs(z)))


def _rms_matmul_kernel(x_ref, g_ref, w_ref, *rest):
    side_w_ref, o_ref, side_o_ref, hn_ref = rest if len(rest) == 4 else (None, rest[0], None, rest[1])

    @pl.when(pl.program_id(1) == 0)
    def _():
        hn_ref[...] = _rms(x_ref[...], g_ref[...]).astype(BF16)
        if side_w_ref is not None:
            side_o_ref[...] = _dot(hn_ref[...], side_w_ref[...])

    o_ref[...] = _dot(hn_ref[...], w_ref[...])


def rms_matmul(x, gain, w, layer, tm, tn, side_w=None):
    M, K = x.shape
    N = w.shape[2]
    tm, tn = min(tm, M), min(tn, N)
    assert M % tm == 0 and N % tn == 0
    in_specs = [pl.BlockSpec((tm, K), lambda i, j: (i, 0), pipeline_mode=pl.Buffered(1)),
                pl.BlockSpec((1, K), lambda i, j: (0, 0)),
                pl.BlockSpec((None, K, tn), lambda i, j: (layer, 0, j))]
    out_specs = pl.BlockSpec((tm, tn), lambda i, j: (i, j))
    out_shape = jax.ShapeDtypeStruct((M, N), F32)
    args = (x, gain.reshape(1, K), w)
    if side_w is not None:
        ns = side_w.shape[2]
        in_specs.append(pl.BlockSpec((None, K, ns), lambda i, j: (layer, 0, 0)))
        out_specs = [out_specs, pl.BlockSpec((tm, ns), lambda i, j: (i, 0))]
        out_shape = [out_shape, jax.ShapeDtypeStruct((M, ns), F32)]
        args += (side_w,)
    return pl.pallas_call(
        _rms_matmul_kernel,
        name="rms_matmul",
        grid=(M // tm, N // tn),
        in_specs=in_specs,
        out_specs=out_specs,
        out_shape=out_shape,
        scratch_shapes=[pltpu.VMEM((tm, K), BF16)],
        compiler_params=_params("parallel", "arbitrary"),
    )(*args)


def _out_proj_kernel(*refs, n_in):
    a_refs, w_refs = refs[:n_in], refs[n_in:2 * n_in]
    res_ref, o_ref = refs[2 * n_in], refs[2 * n_in + 1]
    acc = res_ref[...]
    for a_ref, w_ref in zip(a_refs, w_refs):
        acc = acc + _dot(a_ref[...], w_ref[...])
    o_ref[...] = acc


def out_proj(acts, w, layer, res, tm):
    M, N = res.shape
    tm = min(tm, M)
    assert M % tm == 0
    n_in = len(acts)
    ka = acts[0].shape[1]
    assert all(a.shape[1] == ka for a in acts) and n_in * ka == w.shape[1]
    row_block = lambda r: pl.BlockSpec((None, ka, N), lambda i: (layer, r, 0))
    weights = [w] * n_in
    in_specs = ([pl.BlockSpec((tm, ka), lambda i: (i, 0)) for _ in acts]
                + [row_block(r) for r in range(n_in)]
                + [pl.BlockSpec((tm, N), lambda i: (i, 0))])
    return pl.pallas_call(
        functools.partial(_out_proj_kernel, n_in=n_in),
        name="out_proj",
        grid=(M // tm,),
        in_specs=in_specs,
        out_specs=pl.BlockSpec((tm, N), lambda i: (i, 0)),
        out_shape=jax.ShapeDtypeStruct((M, N), F32),
        compiler_params=_params("parallel"),
    )(*acts, *weights, res)


def _ffn_kernel(x_ref, g_ref, w1_ref, w3_ref, w2_ref, o_ref, hn_ref):
    f = pl.program_id(1)

    @pl.when(f == 0)
    def _():
        x = x_ref[...]
        hn_ref[...] = _rms(x, g_ref[...]).astype(BF16)
        o_ref[...] = x

    h = hn_ref[...]
    a = _dot(h, w1_ref[...].astype(BF16))
    b = _dot(h, w3_ref[...].astype(BF16))
    u = (a * jax.nn.sigmoid(a) * b).astype(BF16)
    o_ref[...] += _dot(u, w2_ref[...].astype(BF16))


def ffn(x, gain, w1, w3, w2, layer, tm, tf):
    M, D = x.shape
    Fh = w1.shape[2]
    tm, tf = min(tm, M), min(tf, Fh)
    assert M % tm == 0 and Fh % tf == 0
    return pl.pallas_call(
        _ffn_kernel,
        name="ffn",
        grid=(M // tm, Fh // tf),
        in_specs=[pl.BlockSpec((tm, D), lambda i, f: (i, 0)),
                  pl.BlockSpec((1, D), lambda i, f: (0, 0)),
                  pl.BlockSpec((None, D, tf), lambda i, f: (layer, 0, f)),
                  pl.BlockSpec((None, D, tf), lambda i, f: (layer, 0, f)),
                  pl.BlockSpec((None, tf, D), lambda i, f: (layer, f, 0))],
        out_specs=pl.BlockSpec((tm, D), lambda i, f: (i, 0)),
        out_shape=jax.ShapeDtypeStruct((M, D), F32),
        scratch_shapes=[pltpu.VMEM((tm, D), BF16)],
        compiler_params=_params("parallel", "arbitrary"),
    )(x, gain.reshape(1, D), w1, w3, w2)


def _mem_attn_kernel(x_ref, g_ref, wq_ref, kv_ref, qg_ref, kg_ref, wo_ref, o_ref, *, heads):
    x = x_ref[0]
    h = _rms(x, g_ref[...]).astype(BF16)
    q = _dot(h, wq_ref[...])
    kv = kv_ref[0]
    w = heads * HEAD_DIM
    outs = []
    for hh in range(heads):
        sl = slice(hh * HEAD_DIM, (hh + 1) * HEAD_DIM)
        qh = (_rms(q[:, sl], qg_ref[...]) * ATTN_SCALE).astype(BF16)
        kh = _rms(kv[:, sl], kg_ref[...]).astype(BF16)
        vh = kv[:, w + hh * HEAD_DIM: w + (hh + 1) * HEAD_DIM].astype(BF16)
        s = _dot_nt(qh, kh)
        p = jnp.exp(s - jnp.max(s, axis=-1, keepdims=True))
        z = jnp.sum(p, axis=-1, keepdims=True)
        outs.append((_dot(p.astype(BF16), vh) / z).astype(BF16))
    o = jnp.concatenate(outs, axis=-1)
    o_ref[0] = x + _dot(o, wo_ref[...])


def mem_attn(x, gain, wq, kv, q_gain, k_gain, wo, layer, tm):
    B, T, D = x.shape
    L, W2 = kv.shape[1], kv.shape[2]
    W = W2 // 2
    tm = min(tm, T)
    assert T % tm == 0
    return pl.pallas_call(
        functools.partial(_mem_attn_kernel, heads=W // HEAD_DIM),
        name="mem_attn",
        grid=(B, T // tm),
        in_specs=[pl.BlockSpec((1, tm, D), lambda b, i: (b, i, 0)),
                  pl.BlockSpec((1, D), lambda b, i: (0, 0)),
                  pl.BlockSpec((None, D, W), lambda b, i: (layer, 0, 0)),
                  pl.BlockSpec((1, L, W2), lambda b, i: (b, 0, 0)),
                  pl.BlockSpec((1, HEAD_DIM), lambda b, i: (0, 0)),
                  pl.BlockSpec((1, HEAD_DIM), lambda b, i: (0, 0)),
                  pl.BlockSpec((None, W, D), lambda b, i: (layer, 0, 0))],
        out_specs=pl.BlockSpec((1, tm, D), lambda b, i: (b, i, 0)),
        out_shape=jax.ShapeDtypeStruct((B, T, D), F32),
        compiler_params=_params("parallel", "parallel"),
    )(x, gain.reshape(1, D), wq, kv, q_gain.reshape(1, HEAD_DIM), k_gain.reshape(1, HEAD_DIM), wo)


def _eye(n):
    return (lax.broadcasted_iota(jnp.int32, (n, n), 0)
            == lax.broadcasted_iota(jnp.int32, (n, n), 1)).astype(BF16)


def _ones_row(rows, n):
    return (lax.broadcasted_iota(jnp.int32, (rows, n), 0) == 0).astype(BF16)


def _split3_f32(x):
    return tuple(t.astype(F32) for t in _split3(x))


COL_GROUPS = 1


def _col_groups(n):
    w = n // COL_GROUPS
    return [slice(i * w, (i + 1) * w) for i in range(COL_GROUPS)]


def _attn_init(n):
    return tuple((jnp.full((1, c.stop - c.start), NEG_INF, F32), jnp.zeros((VT_ROWS, c.stop - c.start), F32))
                 for c in _col_groups(n))


def _attn_tile(k_tile, vt_tile, qa, carry, score_fn=None):
    out = []
    for (m, acc), c in zip(carry, _col_groups(qa.shape[1])):
        s = _dot(k_tile, qa[:, c])
        if score_fn is not None:
            s = score_fn(s, c)
        m_new = jnp.maximum(m, jnp.max(s, axis=0, keepdims=True))
        p = jnp.exp2(s - m_new).astype(BF16)
        out.append((m_new, jnp.exp2(m - m_new) * acc + _dot(vt_tile, p)))
    return tuple(out)


def _attn_finish(carry):
    return jnp.concatenate([acc for _, acc in carry], axis=1)


def _fox_gate_kernel(g_ref, b_ref, c_ref, kf_ref, *, heads):
    T = g_ref.shape[1]
    r = lax.broadcasted_iota(jnp.int32, (LANES, LANES), 0)
    lane = lax.broadcasted_iota(jnp.int32, (LANES, LANES), 1)
    tri = (r >= lane).astype(BF16)

    def body(i, carry):
        rows = pl.ds(pl.multiple_of(i * LANES, LANES), LANES)
        cs = _exact_dot(tri, _log_sigmoid(g_ref[0, rows, :] + b_ref[...])) + carry
        c_ref[0, rows, :] = cs
        c2 = cs * LOG2E
        for h in range(heads):
            d1, d2, d3 = _split3_f32(c2[:, h:h + 1])
            feat = jnp.where(lane < 3, 1.0,
                             jnp.where(lane == 3, -d1, jnp.where(lane == 4, -d2,
                                                                 jnp.where(lane == 5, -d3, 0.0))))
            kf_ref[0, h, rows, :] = feat.astype(BF16)
        return cs[LANES - 1:LANES, :]

    lax.fori_loop(0, T // LANES, body, jnp.zeros((1, LANES), F32))


def fox_gate(g, bias, heads):
    B, T, _ = g.shape
    return pl.pallas_call(
        functools.partial(_fox_gate_kernel, heads=heads),
        name="fox_gate",
        grid=(B,),
        in_specs=[pl.BlockSpec((1, T, LANES), lambda b: (b, 0, 0)),
                  pl.BlockSpec((1, LANES), lambda b: (0, 0))],
        out_specs=[pl.BlockSpec((1, T, LANES), lambda b: (b, 0, 0)),
                   pl.BlockSpec((1, heads, T, LANES), lambda b: (b, 0, 0, 0))],
        out_shape=[jax.ShapeDtypeStruct((B, T, LANES), F32),
                   jax.ShapeDtypeStruct((B, heads, T, LANES), BF16)],
        compiler_params=_params("parallel"),
    )(g, bias)


def _fox_attn_kernel(q_ref, k_ref, v_ref, kf_ref, c_ref, qg_ref, kg_ref, o_ref, ka_ref, vt_ref, *, tq):
    qi = pl.program_id(2)
    T = k_ref.shape[1]
    eye = _eye(HEAD_DIM)

    @pl.when(qi == 0)
    def _():
        ka_ref[:, :HEAD_DIM] = _rms(k_ref[0], kg_ref[...]).astype(BF16)
        ka_ref[:, HEAD_DIM:] = kf_ref[0, 0]
        vt_ref[:HEAD_DIM, :] = _dot_nt(eye, v_ref[0].astype(BF16)).astype(BF16)
        vt_ref[HEAD_DIM:, :] = _ones_row(VT_ROWS - HEAD_DIM, T)

    q0 = pl.multiple_of(qi * tq, tq)
    qn = (_rms(q_ref[0], qg_ref[...]) * (ATTN_SCALE * LOG2E)).astype(BF16)
    qt = _dot_nt(eye, qn).astype(BF16)
    c1, c2, c3 = _split3_f32(c_ref[0, :, pl.ds(q0, tq)] * LOG2E)
    r = lax.broadcasted_iota(jnp.int32, (16, tq), 0)
    feat = jnp.where(r == 0, c1, jnp.where(r == 1, c2, jnp.where(r == 2, c3,
                                                                 jnp.where(r < 6, 1.0, 0.0))))
    qa = jnp.concatenate([qt, feat.astype(BF16), jnp.zeros((AUG - HEAD_DIM - 16, tq), BF16)], axis=0)

    kpos = lax.broadcasted_iota(jnp.int32, (tq, 1), 0)
    tpos = lax.broadcasted_iota(jnp.int32, (1, tq), 1)

    def tile(j, carry, score_fn=None):
        k0 = pl.multiple_of(j * tq, tq)
        return _attn_tile(ka_ref[pl.ds(k0, tq), :], vt_ref[:, pl.ds(k0, tq)], qa, carry, score_fn)

    carry = tile(qi, _attn_init(tq), lambda s, c: jnp.where(kpos <= tpos[:, c], s, NEG_INF))
    acc = _attn_finish(lax.fori_loop(0, qi, tile, carry))
    ot = (acc[:HEAD_DIM] / acc[HEAD_DIM:HEAD_DIM + 1]).astype(BF16)
    for c in range(tq // LANES):
        cs = slice(c * LANES, (c + 1) * LANES)
        o_ref[0, cs, :] = _dot_nt(eye, ot[:, cs]).astype(o_ref.dtype)


def fox_attn(proj, kfeat, c_rows, q_gain, k_gain, tq):
    B, T, _ = proj.shape
    H = FOX_HEADS
    tq = min(tq, T)
    assert T % tq == 0 and tq % LANES == 0
    return pl.pallas_call(
        functools.partial(_fox_attn_kernel, tq=tq),
        name="fox_attn",
        grid=(B, H, T // tq),
        in_specs=[pl.BlockSpec((1, tq, HEAD_DIM), lambda b, h, i: (b, i, h)),
                  pl.BlockSpec((1, T, HEAD_DIM), lambda b, h, i: (b, 0, H + h)),
                  pl.BlockSpec((1, T, HEAD_DIM), lambda b, h, i: (b, 0, 2 * H + h)),
                  pl.BlockSpec((1, 1, T, LANES), lambda b, h, i: (b, h, 0, 0)),
                  pl.BlockSpec((1, 1, T), lambda b, h, i: (b * H + h, 0, 0)),
                  pl.BlockSpec((1, HEAD_DIM), lambda b, h, i: (0, 0)),
                  pl.BlockSpec((1, HEAD_DIM), lambda b, h, i: (0, 0))],
        out_specs=pl.BlockSpec((1, tq, HEAD_DIM), lambda b, h, i: (b, i, h)),
        out_shape=jax.ShapeDtypeStruct((B, T, H * HEAD_DIM), BF16),
        scratch_shapes=[pltpu.VMEM((T, AUG), BF16), pltpu.VMEM((VT_ROWS, T), BF16)],
        compiler_params=_params("parallel", "parallel", "arbitrary"),
    )(proj, proj, proj, kfeat, c_rows, q_gain.reshape(1, HEAD_DIM), k_gain.reshape(1, HEAD_DIM))


def _hgrn_kernel(q_ref, f_ref, i_ref, g_ref, la_ref, l1_ref, oml_ref, og_ref, o_ref, *, bs, chunk, span):
    T = q_ref.shape[1]
    R = LANES
    ri = lax.broadcasted_iota(jnp.int32, (R, R), 0)
    ci = lax.broadcasted_iota(jnp.int32, (R, R), 1)

    def prefix_sel(n):
        return (((ri // n) == (ci // n)) & (ri >= ci)).astype(BF16)

    s_idx = lax.broadcasted_iota(jnp.int32, (bs, HEAD_DIM), 0)
    tril = (lax.broadcasted_iota(jnp.int32, (chunk, chunk), 0)
            >= lax.broadcasted_iota(jnp.int32, (chunk, chunk), 1))

    def gates(row0, n):
        rows = pl.ds(row0, n)
        z = f_ref[0, rows, :]
        a = la_ref[0]
        y = l1_ref[0] + _log_sigmoid(z)
        log_f = jnp.maximum(a, y) + jnp.log1p(jnp.exp(-jnp.abs(a - y)))
        hq = q_ref[0, rows, :]
        return log_f, oml_ref[0] * jax.nn.sigmoid(-z), hq * jax.nn.sigmoid(hq), i_ref[0, rows, :]

    def prefixes(log_f, n):
        sel = prefix_sel(n)
        pieces = _split3(log_f)
        b = jnp.concatenate([sum(_dot(sel, p[g * R:(g + 1) * R]) for p in pieces)
                             for g in range(log_f.shape[0] // R)], axis=0)
        bt = jnp.concatenate([jnp.broadcast_to(b[(u + 1) * n - 1:(u + 1) * n, :], (n, HEAD_DIM))
                              for u in range(log_f.shape[0] // n)], axis=0)
        return b, bt

    def finish(row0, n, o):
        rows = pl.ds(row0, n)
        hg = g_ref[0, rows, :]
        o_ref[0, rows, :] = (_rms(o, og_ref[...]) * (hg * jax.nn.sigmoid(hg))).astype(o_ref.dtype)

    def matmul_path(row0, st, kk, qq, vv, b, bt):
        qe = (qq * jnp.exp(b)).astype(BF16)
        k_inv = (kk * jnp.exp(-b)).astype(BF16)
        ke = (kk * jnp.exp(bt - b)).astype(BF16)
        dec = jnp.exp(bt)
        vb = vv.astype(BF16)
        sls = [slice(u * chunk, (u + 1) * chunk) for u in range(span // chunk)]
        scores = [jnp.where(tril, _dot_nt(qe[sl], k_inv[sl]), 0.0).astype(BF16) for sl in sls]
        intra = [_dot(a, vb[sl]) for a, sl in zip(scores, sls)]
        update = [_dot_tn(vb[sl], ke[sl]) for sl in sls]
        outs = []
        for sl, o_in, upd in zip(sls, intra, update):
            outs.append(_dot_nt(qe[sl], st.astype(BF16)) + o_in)
            st = dec[sl.start:sl.start + 1, :] * st + upd
        finish(row0, span, jnp.concatenate(outs, axis=0))
        return st

    def pairwise_step(i, st, base):
        row0 = pl.multiple_of(base + i * R, R)
        log_f, kk, qq, vv = gates(row0, R)
        b, bt = prefixes(log_f, bs)
        qe = (qq * jnp.exp(b)).astype(BF16)
        ke = (kk * jnp.exp(bt - b)).astype(BF16)
        dec = jnp.exp(bt)
        vb = vv.astype(BF16)
        outs = []
        for u in range(R // bs):
            sl = slice(u * bs, (u + 1) * bs)
            bu, vu, qk, ku = b[sl], vv[sl], qq[sl], kk[sl]
            intra = []
            for t in range(bs):
                w = jnp.exp(jnp.where(s_idx <= t, bu[t:t + 1, :] - bu, NEG_INF))
                a_t = jnp.sum(w * (qk[t:t + 1, :] * ku), axis=-1, keepdims=True)
                intra.append(jnp.sum(a_t * vu, axis=0, keepdims=True))
            outs.append(_dot_nt(qe[sl], st.astype(BF16)) + jnp.concatenate(intra, axis=0))
            st = dec[u * bs:u * bs + 1, :] * st + _dot_tn(vb[sl], ke[sl])
        finish(row0, R, jnp.concatenate(outs, axis=0))
        return st

    def step(i, st):
        row0 = pl.multiple_of(i * span, span)
        log_f, kk, qq, vv = gates(row0, span)
        b, bt = prefixes(log_f, chunk)
        return lax.cond(
            jnp.min(bt) >= HGRN_SAFE_LOG_DECAY,
            lambda st: matmul_path(row0, st, kk, qq, vv, b, bt),
            lambda st: lax.fori_loop(0, span // R, functools.partial(pairwise_step, base=row0), st),
            st)

    lax.fori_loop(0, T // span, step, jnp.zeros((HEAD_DIM, HEAD_DIM), F32))


def hgrn(proj, col0, log_lb, log1m_lb, one_m_lb, o_gain):
    B, T, _ = proj.shape
    H = HGRN_HEADS
    row = lambda off: pl.BlockSpec((1, T, HEAD_DIM), lambda b, h: (b, 0, col0 + off * H + h))
    par = pl.BlockSpec((1, 1, HEAD_DIM), lambda b, h: (h, 0, 0))
    log_lb, log1m_lb, one_m_lb = (p.reshape(H, 1, HEAD_DIM) for p in (log_lb, log1m_lb, one_m_lb))
    return pl.pallas_call(
        functools.partial(_hgrn_kernel, bs=HGRN_BLOCK, chunk=HGRN_CHUNK, span=min(HGRN_SPAN, T)),
        name="hgrn",
        grid=(B, H),
        in_specs=[row(0), row(1), row(2), row(3), par, par, par,
                  pl.BlockSpec((1, HEAD_DIM), lambda b, h: (0, 0))],
        out_specs=pl.BlockSpec((1, T, HEAD_DIM), lambda b, h: (b, 0, h)),
        out_shape=jax.ShapeDtypeStruct((B, T, H * HEAD_DIM), BF16),
        compiler_params=_params("parallel", "parallel"),
    )(proj, proj, proj, proj, log_lb, log1m_lb, one_m_lb, o_gain.reshape(1, HEAD_DIM))


def _compress_kernel(t_ref, pe_ref, w1_ref, w2_ref, g_ref, o_ref, *, normalize):
    N = t_ref.shape[1] // CMP_STRIDE
    top = bot = jnp.zeros((N, w1_ref.shape[2]), F32)
    for l in range(CMP_STRIDE):
        xl = t_ref[0, pl.ds(l, N, stride=CMP_STRIDE), :]
        top = top + _dot((xl + pe_ref[l:l + 1, :]).astype(BF16), w1_ref[l])
        bot = bot + _dot((xl + pe_ref[CMP_STRIDE + l:CMP_STRIDE + l + 1, :]).astype(BF16),
                         w1_ref[CMP_STRIDE + l])
    hid = top + pltpu.roll(bot, N - 1, 0)
    out = _dot(jax.nn.gelu(hid).astype(BF16), w2_ref[...])
    if normalize:
        out = _rms(out, g_ref[...])
    valid = lax.broadcasted_iota(jnp.int32, out.shape, 0) < N - 1
    o_ref[0, 0] = jnp.where(valid, out, 0.0)


def compress(proj, col0, pe, w1, w2, layer, gain, normalize):
    B, T, _ = proj.shape
    G = NSA_KV_HEADS
    N = T // CMP_STRIDE
    Hc = w1.shape[3]
    assert CMP_BLOCK == 2 * CMP_STRIDE
    return pl.pallas_call(
        functools.partial(_compress_kernel, normalize=normalize),
        name="nsa_compress",
        grid=(B, G),
        in_specs=[pl.BlockSpec((1, T, HEAD_DIM), lambda b, g: (b, 0, col0 + g)),
                  pl.BlockSpec((None, CMP_BLOCK, HEAD_DIM), lambda b, g: (layer, 0, 0)),
                  pl.BlockSpec((None, CMP_BLOCK, HEAD_DIM, Hc), lambda b, g: (layer, 0, 0, 0)),
                  pl.BlockSpec((None, Hc, HEAD_DIM), lambda b, g: (layer, 0, 0)),
                  pl.BlockSpec((1, HEAD_DIM), lambda b, g: (0, 0))],
        out_specs=pl.BlockSpec((1, 1, N, HEAD_DIM), lambda b, g: (b, g, 0, 0)),
        out_shape=jax.ShapeDtypeStruct((B, G, N, HEAD_DIM), F32),
        compiler_params=_params("parallel", "parallel"),
    )(proj, pe, w1, w2, gain.reshape(1, HEAD_DIM))


_SEL_LANE0 = 64
_CMP_OV_ROW0 = VT_ROWS


def _nsa_kernel(q_ref, ks_ref, vs_ref, kw_ref, vw_ref, kc_ref, vc_ref, gt_ref, qg_ref, kg_ref,
                o_ref, ksa_ref, kwa_ref, vst_ref, vwt_ref, kca_ref, vcov_ref, *, tq, tk, n_sel):
    g = pl.program_id(1)
    qi = pl.program_id(2)
    T = ks_ref.shape[1]
    J = NSA_GROUP
    n_slc = T // SLC_BLOCK
    nc = kc_ref.shape[2]
    n_cmp = nc - 1
    cols = J * tq
    eye = _eye(HEAD_DIM)

    @pl.when(qi == 0)
    def _():
        row = lax.broadcasted_iota(jnp.int32, (T, LANES), 0)
        lane = lax.broadcasted_iota(jnp.int32, (T, LANES), 1)
        blk = row // SLC_BLOCK
        feats = jnp.where(lane < 3, blk,
                          jnp.where(lane < 6, row % SLC_BLOCK,
                                    jnp.where(lane < 9, 1,
                                              jnp.where(lane - _SEL_LANE0 == blk, 1, 0))))
        feats = feats.astype(F32).astype(BF16)
        ksa_ref[:, :HEAD_DIM] = _rms(ks_ref[0], kg_ref[1:2, :]).astype(BF16)
        ksa_ref[:, HEAD_DIM:] = feats
        kwa_ref[:, :HEAD_DIM] = _rms(kw_ref[0], kg_ref[2:3, :]).astype(BF16)
        kwa_ref[:, HEAD_DIM:] = feats
        vst_ref[:HEAD_DIM, :] = _dot_nt(eye, vs_ref[0].astype(BF16)).astype(BF16)
        vst_ref[HEAD_DIM:, :] = _ones_row(VT_ROWS - HEAD_DIM, T)
        vwt_ref[:HEAD_DIM, :] = _dot_nt(eye, vw_ref[0].astype(BF16)).astype(BF16)
        vwt_ref[HEAD_DIM:, :] = _ones_row(VT_ROWS - HEAD_DIM, T)
        n_r = lax.broadcasted_iota(jnp.int32, (nc, LANES), 0)
        n_l = lax.broadcasted_iota(jnp.int32, (nc, LANES), 1)
        per = SLC_BLOCK // CMP_STRIDE
        cfe = jnp.where(n_l < 3, (n_r // per).astype(F32),
                        jnp.where(n_l < 6, (CMP_STRIDE * (n_r % per)).astype(F32) + 0.5 * (CMP_BLOCK - 1),
                                  jnp.where(n_l < 9, 1.0, 0.0)))
        kca_ref[:, :HEAD_DIM] = kc_ref[0, 0].astype(BF16)
        kca_ref[:, HEAD_DIM:] = cfe.astype(BF16)
        vcov_ref[:HEAD_DIM, :] = _dot_nt(eye, vc_ref[0, 0].astype(BF16)).astype(BF16)
        vcov_ref[HEAD_DIM:_CMP_OV_ROW0, :] = _ones_row(VT_ROWS - HEAD_DIM, nc)
        m_r = lax.broadcasted_iota(jnp.int32, (SLC_BLOCK, nc), 0)
        n_c = lax.broadcasted_iota(jnp.int32, (SLC_BLOCK, nc), 1)
        vcov_ref[_CMP_OV_ROW0:, :] = ((n_c * CMP_STRIDE <= m_r * SLC_BLOCK + (SLC_BLOCK - 1))
                                      & (n_c * CMP_STRIDE + (CMP_BLOCK - 1) >= m_r * SLC_BLOCK)
                                      & (n_c < n_cmp) & (m_r < n_slc)).astype(BF16)

    q0 = pl.multiple_of(qi * tq, tq)
    qraw = q_ref[0]
    qt = jnp.concatenate(
        [_dot_nt(eye, (_rms(qraw[:, j * HEAD_DIM:(j + 1) * HEAD_DIM], qg_ref[...])
                       * (ATTN_SCALE * LOG2E)).astype(BF16)).astype(BF16)
         for j in range(J)], axis=1)

    col = lax.broadcasted_iota(jnp.int32, (1, cols), 1)
    tt = q0 + col % tq
    slope2 = jnp.exp2(-0.5 * (g * J + col // tq + 1).astype(F32)) * LOG2E
    s1, s2, s3 = _split3_f32(slope2)
    r1, r2, r3 = _split3_f32(-slope2 * tt.astype(F32))
    r16 = lax.broadcasted_iota(jnp.int32, (16, cols), 0)
    feat = jnp.zeros((16, cols), F32)
    for i, v in enumerate((s1 * SLC_BLOCK, s2 * SLC_BLOCK, s3 * SLC_BLOCK, s1, s2, s3, r1, r2, r3)):
        feat = jnp.where(r16 == i, v, feat)
    feat = feat.astype(BF16)
    qa_base = jnp.concatenate([qt, feat, jnp.zeros((AUG - HEAD_DIM - 16, cols), BF16)], axis=0)

    n_col = lax.broadcasted_iota(jnp.int32, (nc, 1), 0)
    cmp_end = n_col * CMP_STRIDE + (CMP_BLOCK - 1)
    kca = kca_ref[...]
    r_hi, r_lo = [], []
    for c in _col_groups(cols):
        s = _dot(kca, qa_base[:, c])
        s = jnp.where((cmp_end <= tt[:, c]) & (n_col < n_cmp), s, NEG_INF)
        m = jnp.max(s, axis=0, keepdims=True)
        p = jnp.exp2(s - jnp.where(m == NEG_INF, 0.0, m))
        p_hi = p.astype(BF16)
        r_hi.append(_dot(vcov_ref[...], p_hi))
        r_lo.append(_dot(vcov_ref[HEAD_DIM:, :], (p - p_hi.astype(F32)).astype(BF16)))
    r_hi = jnp.concatenate(r_hi, axis=1)
    r_lo = jnp.concatenate(r_lo, axis=1)
    z = r_hi[HEAD_DIM:HEAD_DIM + 1] + r_lo[0:1]
    inv_z = 1.0 / jnp.where(z > 0, z, 1.0)

    wq = min(tq, WINDOW_Q)
    span = WINDOW + wq
    d = (lax.broadcasted_iota(jnp.int32, (span, 1), 0)
         - lax.broadcasted_iota(jnp.int32, (1, wq), 1))
    parts = []
    for c in range(tq // wq):
        k0w = pl.multiple_of(jnp.maximum(q0 + c * wq - WINDOW, 0), wq)
        off = q0 + c * wq - k0w
        bias = jnp.where((d <= off) & (d > off - WINDOW), 0.0, NEG_INF)
        bias = jnp.concatenate([bias] * J, axis=1)
        qa_c = jnp.concatenate([qa_base[:, j * tq + c * wq:j * tq + (c + 1) * wq] for j in range(J)], axis=1)
        parts.append(_attn_finish(_attn_tile(
            kwa_ref[pl.ds(k0w, span), :], vwt_ref[:, pl.ds(k0w, span)], qa_c, _attn_init(J * wq),
            lambda s, sub, bias=bias: s + bias[:, sub])))
    acc_w = jnp.concatenate([p[:, j * wq:(j + 1) * wq] for j in range(J) for p in parts], axis=1)

    imp4 = (r_hi[_CMP_OV_ROW0:] + r_lo[_CMP_OV_ROW0 - HEAD_DIM:]) * inv_z
    imp = imp4[:, 0:tq]
    for j in range(1, J):
        imp = imp + imp4[:, j * tq:(j + 1) * tq]
    blk = lax.broadcasted_iota(jnp.int32, (SLC_BLOCK, tq), 0)
    t_q = q0 + lax.broadcasted_iota(jnp.int32, (1, tq), 1)
    imp = jnp.where(blk * SLC_BLOCK > t_q, NEG_INF, imp)
    imp = jnp.where((blk == t_q // SLC_BLOCK) | (blk == 0), jnp.inf, imp)
    blk_f = blk.astype(F32)
    taken = blk >= n_slc
    for _ in range(n_sel):
        vals = jnp.where(taken, NEG_INF, imp)
        best = jnp.max(vals, axis=0, keepdims=True)
        cand = (vals == best) & jnp.logical_not(taken)
        first = jnp.min(jnp.where(cand, blk_f, float(SLC_BLOCK)), axis=0, keepdims=True)
        taken = taken | (blk_f == first)
    not_sel = jnp.where(taken & (blk < n_slc), 0.0, -MASK_BIG).astype(BF16)
    qa_slc = jnp.concatenate([qt, feat, jnp.zeros((_SEL_LANE0 - 16, cols), BF16)]
                             + [jnp.concatenate([not_sel] * J, axis=1)], axis=0)

    def slc_tile(jt, carry, score_fn=None):
        k0 = pl.multiple_of(jt * tk, tk)
        return _attn_tile(ksa_ref[pl.ds(k0, tk), :], vst_ref[:, pl.ds(k0, tk)], qa_slc, carry, score_fn)

    jd = q0 // tk
    kpos = jd * tk + lax.broadcasted_iota(jnp.int32, (tk, 1), 0)
    carry = slc_tile(jd, _attn_init(cols), lambda s, c: jnp.where(kpos <= tt[:, c], s, NEG_INF))
    acc_s = _attn_finish(lax.fori_loop(0, jd, slc_tile, carry))

    gates_t = jnp.transpose(jax.nn.sigmoid(gt_ref[0]))
    gate = lambda c: jnp.concatenate([gates_t[3 * j + c:3 * j + c + 1, :] for j in range(J)], axis=1)
    o_t = ((gate(0) * inv_z) * r_hi[:HEAD_DIM]
           + (gate(1) / acc_s[HEAD_DIM:HEAD_DIM + 1]) * acc_s[:HEAD_DIM]
           + (gate(2) / acc_w[HEAD_DIM:HEAD_DIM + 1]) * acc_w[:HEAD_DIM]).astype(BF16)
    for j in range(J):
        for c in range(tq // LANES):
            o_ref[0, c * LANES:(c + 1) * LANES, j * HEAD_DIM:(j + 1) * HEAD_DIM] = _dot_nt(
                eye, o_t[:, j * tq + c * LANES:j * tq + (c + 1) * LANES]).astype(o_ref.dtype)


def nsa_attn(proj, gates, k_cmp, v_cmp, q_gain, k_gain, tq, tk):
    B, T, _ = proj.shape
    G, J = NSA_KV_HEADS, NSA_GROUP
    tk = min(tk, T)
    nc = k_cmp.shape[2]
    assert T % tk == 0 and tk % tq == 0 and T >= WINDOW + tq
    assert T // SLC_BLOCK <= LANES - _SEL_LANE0 and SLC_BLOCK <= LANES - _SEL_LANE0
    qcols = NSA_HEADS
    kv = lambda slab: pl.BlockSpec((1, T, HEAD_DIM), lambda b, g, i: (b, 0, qcols + slab * G + g))
    cmp_spec = pl.BlockSpec((1, 1, nc, HEAD_DIM), lambda b, g, i: (b, g, 0, 0))
    return pl.pallas_call(
        functools.partial(_nsa_kernel, tq=tq, tk=tk, n_sel=min(SLC_TOPK, T // SLC_BLOCK)),
        name="nsa_attn",
        grid=(B, G, T // tq),
        in_specs=[pl.BlockSpec((1, tq, J * HEAD_DIM), lambda b, g, i: (b, i, g)),
                  kv(2), kv(3), kv(4), kv(5), cmp_spec, cmp_spec,
                  pl.BlockSpec((1, tq, LANES), lambda b, g, i: (b, i, g)),
                  pl.BlockSpec((1, HEAD_DIM), lambda b, g, i: (0, 0)),
                  pl.BlockSpec((3, HEAD_DIM), lambda b, g, i: (0, 0))],
        out_specs=pl.BlockSpec((1, tq, J * HEAD_DIM), lambda b, g, i: (b, i, g)),
        out_shape=jax.ShapeDtypeStruct((B, T, NSA_HEADS * HEAD_DIM), BF16),
        scratch_shapes=[pltpu.VMEM((T, AUG), BF16), pltpu.VMEM((T, AUG), BF16),
                        pltpu.VMEM((VT_ROWS, T), BF16), pltpu.VMEM((VT_ROWS, T), BF16),
                        pltpu.VMEM((nc, AUG), BF16),
                        pltpu.VMEM((VT_ROWS + SLC_BLOCK, nc), BF16)],
        compiler_params=_params("parallel", "parallel", "arbitrary"),
    )(proj, proj, proj, proj, proj, k_cmp, v_cmp, gates, q_gain.reshape(1, HEAD_DIM), k_gain)


def even_weights(w_in, w_out):
    fw = FOX_HEADS * HEAD_DIM
    w_main = jnp.concatenate([w_in[:, :, :3 * fw], w_in[:, :, 3 * fw + FOX_HEADS:]], axis=2).astype(BF16)
    w_gate = jnp.pad(w_in[:, :, 3 * fw:3 * fw + FOX_HEADS], ((0, 0), (0, 0), (0, LANES - FOX_HEADS)))
    return w_main, w_gate.astype(BF16), w_out.astype(BF16)


def even_mixer(x, gain, weights, e, f_bias, q_gain, k_gain, lb, o_gain):
    B, T, D = x.shape
    xf = x.reshape(B * T, D)
    w_main, w_gate, w_out = weights
    proj, gate = rms_matmul(xf, gain, w_main, e, 2048, 512, side_w=w_gate)
    proj, gate = proj.reshape(B, T, -1), gate.reshape(B, T, LANES)
    f_bias = jnp.pad(f_bias.reshape(1, FOX_HEADS).astype(F32), ((0, 0), (0, LANES - FOX_HEADS)))
    c, kfeat = fox_gate(gate, f_bias, FOX_HEADS)
    c_rows = c[:, :, :FOX_HEADS].transpose(0, 2, 1).reshape(B * FOX_HEADS, 1, T)
    o_fox = fox_attn(proj, kfeat, c_rows, q_gain, k_gain, 1024)
    lb = lb.astype(F32).reshape(HGRN_HEADS, HEAD_DIM)
    o_h = hgrn(proj, 3 * FOX_HEADS, jnp.log(lb), jnp.log1p(-lb), 1.0 - lb, o_gain)
    out = out_proj([o_fox.reshape(B * T, -1), o_h.reshape(B * T, -1)], w_out, e, xf, 256)
    return out.reshape(B, T, D)


def nsa_weights(w_in, w_out, w1_k, w2_k, w1_v, w2_v):
    L, D, _ = w_in.shape
    G, J = NSA_KV_HEADS, NSA_GROUP
    n_main = (NSA_HEADS + 6 * G) * HEAD_DIM
    w_gate = w_in[:, :, n_main:].reshape(L, D, G, 3 * J)
    w_gate = jnp.pad(w_gate, ((0, 0), (0, 0), (0, 0), (0, LANES - 3 * J))).reshape(L, D, G * LANES)
    return (w_in[:, :, :n_main].astype(BF16), w_gate.astype(BF16), w_out.astype(BF16),
            w1_k.astype(BF16), w2_k.astype(BF16), w1_v.astype(BF16), w2_v.astype(BF16))


def nsa_mixer(x, gain, weights, o, q_gain, k_gain, pe_k, pe_v):
    B, T, D = x.shape
    G = NSA_KV_HEADS
    xf = x.reshape(B * T, D)
    w_main, w_gate, w_out, w1_k, w2_k, w1_v, w2_v = weights
    proj, gates = rms_matmul(xf, gain, w_main, o, 1024, 1024, side_w=w_gate)
    proj, gates = proj.reshape(B, T, -1), gates.reshape(B, T, G * LANES)
    k_cmp = compress(proj, NSA_HEADS, pe_k, w1_k, w2_k, o, k_gain[0], True)
    v_cmp = compress(proj, NSA_HEADS + G, pe_v, w1_v, w2_v, o, k_gain[0], False)
    att = nsa_attn(proj, gates, k_cmp, v_cmp, q_gain, k_gain, 512, 512)
    out = out_proj([att.reshape(B * T, -1)], w_out, o, xf, 256)
    return out.reshape(B, T, D)


def kernel(x, mem, norm_mix, norm_mem, norm_ffn, mem_in_gain, even_w_in, even_w_out, fox_f_bias,
           fox_q_gain, fox_k_gain, hgrn_lb_logits, hgrn_o_gain, odd_w_in, odd_w_out, nsa_q_gain,
           nsa_k_gain, cmp_pe_k, cmp_w1_k, cmp_w2_k, cmp_pe_v, cmp_w1_v, cmp_w2_v, mem_wq, mem_wkv,
           mem_wo, mem_q_gain, mem_k_gain, ffn_w1, ffn_w3, ffn_w2):
    B, T, D = x.shape
    depth = norm_mix.shape[0]
    L = mem.shape[1]
    lb_cum = jnp.cumsum(jax.nn.softmax(hgrn_lb_logits.astype(F32), axis=0), axis=0)
    hgrn_lb = lb_cum - lb_cum[0:1]
    memf = mem.reshape(B * L, D)
    even_w = even_weights(even_w_in, even_w_out)
    nsa_w = nsa_weights(odd_w_in, odd_w_out, cmp_w1_k, cmp_w2_k, cmp_w1_v, cmp_w2_v)
    mem_wq, mem_wkv, mem_wo = (w.astype(BF16) for w in (mem_wq, mem_wkv, mem_wo))
    for layer in range(depth):
        if layer % 2 == 0:
            e = layer // 2
            x = even_mixer(x, norm_mix[layer], even_w, e, fox_f_bias[e], fox_q_gain[e], fox_k_gain[e],
                           hgrn_lb[e], hgrn_o_gain[e])
        else:
            o = layer // 2
            x = nsa_mixer(x, norm_mix[layer], nsa_w, o, nsa_q_gain[o], nsa_k_gain[o], cmp_pe_k, cmp_pe_v)
        kv = rms_matmul(memf, mem_in_gain[layer], mem_wkv, layer, 256, 1024)
        x = mem_attn(x, norm_mem[layer], mem_wq, kv.reshape(B, L, -1), mem_q_gain[layer],
                     mem_k_gain[layer], mem_wo, layer, 512)
        x = ffn(x.reshape(B * T, D), norm_ffn[layer], ffn_w1, ffn_w3, ffn_w2, layer,
                1024, 256).reshape(B, T, D)
    return x
```

```python
import functools

import jax
import jax.numpy as jnp
from jax import lax
from jax.experimental import pallas as pl
from jax.experimental.pallas import tpu as pltpu

F32 = jnp.float32
BF16 = jnp.bfloat16
NEG_INF = float("-inf")

EPS = 1e-6
HEAD_DIM = 128
ATTN_SCALE = HEAD_DIM ** -0.5
LANES = 128
FOX_HEADS = 8
HGRN_HEADS = 8
HGRN_BLOCK = 16
HGRN_CHUNK = 64
HGRN_SPAN = 512
HGRN_SAFE_LOG_DECAY = -80.0
NSA_HEADS = 16
NSA_KV_HEADS = 4
NSA_GROUP = NSA_HEADS // NSA_KV_HEADS
CMP_BLOCK = 32
CMP_STRIDE = 16
SLC_BLOCK = 64
SLC_TOPK = 16
WINDOW = 512
WINDOW_Q = 256
MEM_HEADS = 4
V7X_VMEM_LIMIT = 56 * 1024 * 1024
LOG2E = 1.4426950408889634
MASK_BIG = 1e30
AUG = 2 * LANES
VT_ROWS = HEAD_DIM + 16


def _params(*sem):
    return pltpu.CompilerParams(dimension_semantics=sem, vmem_limit_bytes=V7X_VMEM_LIMIT)


def _rms(x, gain):
    return x * lax.rsqrt(jnp.mean(x * x, axis=-1, keepdims=True) + EPS) * gain


def _dot(a, b):
    return jnp.dot(a, b, preferred_element_type=F32)


def _dot_nt(a, b):
    return lax.dot_general(a, b, (((1,), (1,)), ((), ())), preferred_element_type=F32)


def _dot_tn(a, b):
    return lax.dot_general(a, b, (((0,), (0,)), ((), ())), preferred_element_type=F32)


def _split3(x):
    x1 = x.astype(BF16)
    r1 = x - x1.astype(F32)
    x2 = r1.astype(BF16)
    return x1, x2, (r1 - x2.astype(F32)).astype(BF16)


def _exact_dot(sel, x):
    x1, x2, x3 = _split3(x)
    return _dot(sel, x1) + _dot(sel, x2) + _dot(sel, x3)


def _log_sigmoid(z):
    return jnp.minimum(z, 0.0) - jnp.log1p(jnp.exp(-jnp.abs(z)))


def _rms_matmul_kernel(x_ref, g_ref, w_ref, *rest):
    side_w_ref, o_ref, side_o_ref, hn_ref = rest if len(rest) == 4 else (None, rest[0], None, rest[1])

    @pl.when(pl.program_id(1) == 0)
    def _():
        hn_ref[...] = _rms(x_ref[...], g_ref[...]).astype(BF16)
        if side_w_ref is not None:
            side_o_ref[...] = _dot(hn_ref[...], side_w_ref[...])

    o_ref[...] = _dot(hn_ref[...], w_ref[...])


def rms_matmul(x, gain, w, layer, tm, tn, side_w=None):
    M, K = x.shape
    N = w.shape[2]
    tm, tn = min(tm, M), min(tn, N)
    assert M % tm == 0 and N % tn == 0
    in_specs = [pl.BlockSpec((tm, K), lambda i, j: (i, 0)),
                pl.BlockSpec((1, K), lambda i, j: (0, 0)),
                pl.BlockSpec((None, K, tn), lambda i, j: (layer, 0, j))]
    out_specs = pl.BlockSpec((tm, tn), lambda i, j: (i, j))
    out_shape = jax.ShapeDtypeStruct((M, N), F32)
    args = (x, gain.reshape(1, K), w)
    if side_w is not None:
        ns = side_w.shape[2]
        in_specs.append(pl.BlockSpec((None, K, ns), lambda i, j: (layer, 0, 0)))
        out_specs = [out_specs, pl.BlockSpec((tm, ns), lambda i, j: (i, 0))]
        out_shape = [out_shape, jax.ShapeDtypeStruct((M, ns), F32)]
        args += (side_w,)
    return pl.pallas_call(
        _rms_matmul_kernel,
        name="rms_matmul",
        grid=(M // tm, N // tn),
        in_specs=in_specs,
        out_specs=out_specs,
        out_shape=out_shape,
        scratch_shapes=[pltpu.VMEM((tm, K), BF16)],
        compiler_params=_params("parallel", "arbitrary"),
    )(*args)


def _out_proj_kernel(*refs, n_in):
    a_refs, w_refs = refs[:n_in], refs[n_in:2 * n_in]
    res_ref, o_ref = refs[2 * n_in], refs[2 * n_in + 1]
    acc = res_ref[...]
    for a_ref, w_ref in zip(a_refs, w_refs):
        acc = acc + _dot(a_ref[...], w_ref[...])
    o_ref[...] = acc


def out_proj(acts, w, layer, res, tm):
    M, N = res.shape
    tm = min(tm, M)
    assert M % tm == 0
    n_in = len(acts)
    ka = acts[0].shape[1]
    assert all(a.shape[1] == ka for a in acts) and n_in * ka == w.shape[1]
    row_block = lambda r: pl.BlockSpec((None, ka, N), lambda i: (layer, r, 0))
    weights = [w] * n_in
    in_specs = ([pl.BlockSpec((tm, ka), lambda i: (i, 0)) for _ in acts]
                + [row_block(r) for r in range(n_in)]
                + [pl.BlockSpec((tm, N), lambda i: (i, 0))])
    return pl.pallas_call(
        functools.partial(_out_proj_kernel, n_in=n_in),
        name="out_proj",
        grid=(M // tm,),
        in_specs=in_specs,
        out_specs=pl.BlockSpec((tm, N), lambda i: (i, 0)),
        out_shape=jax.ShapeDtypeStruct((M, N), F32),
        compiler_params=_params("parallel"),
    )(*acts, *weights, res)


def _ffn_kernel(x_ref, g_ref, w1_ref, w3_ref, w2_ref, o_ref, hn_ref):
    f = pl.program_id(1)

    @pl.when(f == 0)
    def _():
        x = x_ref[...]
        hn_ref[...] = _rms(x, g_ref[...]).astype(BF16)
        o_ref[...] = x

    h = hn_ref[...]
    a = _dot(h, w1_ref[...].astype(BF16))
    b = _dot(h, w3_ref[...].astype(BF16))
    u = (a * jax.nn.sigmoid(a) * b).astype(BF16)
    o_ref[...] += _dot(u, w2_ref[...].astype(BF16))


def ffn(x, gain, w1, w3, w2, layer, tm, tf):
    M, D = x.shape
    Fh = w1.shape[2]
    tm, tf = min(tm, M), min(tf, Fh)
    assert M % tm == 0 and Fh % tf == 0
    return pl.pallas_call(
        _ffn_kernel,
        name="ffn",
        grid=(M // tm, Fh // tf),
        in_specs=[pl.BlockSpec((tm, D), lambda i, f: (i, 0)),
                  pl.BlockSpec((1, D), lambda i, f: (0, 0)),
                  pl.BlockSpec((None, D, tf), lambda i, f: (layer, 0, f)),
                  pl.BlockSpec((None, D, tf), lambda i, f: (layer, 0, f)),
                  pl.BlockSpec((None, tf, D), lambda i, f: (layer, f, 0))],
        out_specs=pl.BlockSpec((tm, D), lambda i, f: (i, 0)),
        out_shape=jax.ShapeDtypeStruct((M, D), F32),
        scratch_shapes=[pltpu.VMEM((tm, D), BF16)],
        compiler_params=_params("parallel", "arbitrary"),
    )(x, gain.reshape(1, D), w1, w3, w2)


def _mem_attn_kernel(x_ref, g_ref, wq_ref, kv_ref, qg_ref, kg_ref, wo_ref, o_ref, *, heads):
    x = x_ref[0]
    h = _rms(x, g_ref[...]).astype(BF16)
    q = _dot(h, wq_ref[...])
    kv = kv_ref[0]
    w = heads * HEAD_DIM
    outs = []
    for hh in range(heads):
        sl = slice(hh * HEAD_DIM, (hh + 1) * HEAD_DIM)
        qh = (_rms(q[:, sl], qg_ref[...]) * ATTN_SCALE).astype(BF16)
        kh = _rms(kv[:, sl], kg_ref[...]).astype(BF16)
        vh = kv[:, w + hh * HEAD_DIM: w + (hh + 1) * HEAD_DIM].astype(BF16)
        s = _dot_nt(qh, kh)
        p = jnp.exp(s - jnp.max(s, axis=-1, keepdims=True))
        z = jnp.sum(p, axis=-1, keepdims=True)
        outs.append((_dot(p.astype(BF16), vh) / z).astype(BF16))
    o = jnp.concatenate(outs, axis=-1)
    o_ref[0] = x + _dot(o, wo_ref[...])


def mem_attn(x, gain, wq, kv, q_gain, k_gain, wo, layer, tm):
    B, T, D = x.shape
    L, W2 = kv.shape[1], kv.shape[2]
    W = W2 // 2
    tm = min(tm, T)
    assert T % tm == 0
    return pl.pallas_call(
        functools.partial(_mem_attn_kernel, heads=W // HEAD_DIM),
        name="mem_attn",
        grid=(B, T // tm),
        in_specs=[pl.BlockSpec((1, tm, D), lambda b, i: (b, i, 0)),
                  pl.BlockSpec((1, D), lambda b, i: (0, 0)),
                  pl.BlockSpec((None, D, W), lambda b, i: (layer, 0, 0)),
                  pl.BlockSpec((1, L, W2), lambda b, i: (b, 0, 0)),
                  pl.BlockSpec((1, HEAD_DIM), lambda b, i: (0, 0)),
                  pl.BlockSpec((1, HEAD_DIM), lambda b, i: (0, 0)),
                  pl.BlockSpec((None, W, D), lambda b, i: (layer, 0, 0))],
        out_specs=pl.BlockSpec((1, tm, D), lambda b, i: (b, i, 0)),
        out_shape=jax.ShapeDtypeStruct((B, T, D), F32),
        compiler_params=_params("parallel", "parallel"),
    )(x, gain.reshape(1, D), wq, kv, q_gain.reshape(1, HEAD_DIM), k_gain.reshape(1, HEAD_DIM), wo)


def _eye(n):
    return (lax.broadcasted_iota(jnp.int32, (n, n), 0)
            == lax.broadcasted_iota(jnp.int32, (n, n), 1)).astype(BF16)


def _ones_row(rows, n):
    return (lax.broadcasted_iota(jnp.int32, (rows, n), 0) == 0).astype(BF16)


def _split3_f32(x):
    return tuple(t.astype(F32) for t in _split3(x))


COL_GROUPS = 1


def _col_groups(n):
    w = n // COL_GROUPS
    return [slice(i * w, (i + 1) * w) for i in range(COL_GROUPS)]


def _attn_init(n):
    return tuple((jnp.full((1, c.stop - c.start), NEG_INF, F32), jnp.zeros((VT_ROWS, c.stop - c.start), F32))
                 for c in _col_groups(n))


def _attn_tile(k_tile, vt_tile, qa, carry, score_fn=None):
    out = []
    for (m, acc), c in zip(carry, _col_groups(qa.shape[1])):
        s = _dot(k_tile, qa[:, c])
        if score_fn is not None:
            s = score_fn(s, c)
        m_new = jnp.maximum(m, jnp.max(s, axis=0, keepdims=True))
        p = jnp.exp2(s - m_new).astype(BF16)
        out.append((m_new, jnp.exp2(m - m_new) * acc + _dot(vt_tile, p)))
    return tuple(out)


def _attn_finish(carry):
    return jnp.concatenate([acc for _, acc in carry], axis=1)


def _fox_gate_kernel(g_ref, b_ref, c_ref, kf_ref, *, heads):
    T = g_ref.shape[1]
    r = lax.broadcasted_iota(jnp.int32, (LANES, LANES), 0)
    lane = lax.broadcasted_iota(jnp.int32, (LANES, LANES), 1)
    tri = (r >= lane).astype(BF16)

    def body(i, carry):
        rows = pl.ds(pl.multiple_of(i * LANES, LANES), LANES)
        cs = _exact_dot(tri, _log_sigmoid(g_ref[0, rows, :] + b_ref[...])) + carry
        c_ref[0, rows, :] = cs
        c2 = cs * LOG2E
        for h in range(heads):
            d1, d2, d3 = _split3_f32(c2[:, h:h + 1])
            feat = jnp.where(lane < 3, 1.0,
                             jnp.where(lane == 3, -d1, jnp.where(lane == 4, -d2,
                                                                 jnp.where(lane == 5, -d3, 0.0))))
            kf_ref[0, h, rows, :] = feat.astype(BF16)
        return cs[LANES - 1:LANES, :]

    lax.fori_loop(0, T // LANES, body, jnp.zeros((1, LANES), F32))


def fox_gate(g, bias, heads):
    B, T, _ = g.shape
    return pl.pallas_call(
        functools.partial(_fox_gate_kernel, heads=heads),
        name="fox_gate",
        grid=(B,),
        in_specs=[pl.BlockSpec((1, T, LANES), lambda b: (b, 0, 0)),
                  pl.BlockSpec((1, LANES), lambda b: (0, 0))],
        out_specs=[pl.BlockSpec((1, T, LANES), lambda b: (b, 0, 0)),
                   pl.BlockSpec((1, heads, T, LANES), lambda b: (b, 0, 0, 0))],
        out_shape=[jax.ShapeDtypeStruct((B, T, LANES), F32),
                   jax.ShapeDtypeStruct((B, heads, T, LANES), BF16)],
        compiler_params=_params("parallel"),
    )(g, bias)


def _fox_attn_kernel(q_ref, k_ref, v_ref, kf_ref, c_ref, qg_ref, kg_ref, o_ref, ka_ref, vt_ref, *, tq):
    qi = pl.program_id(2)
    T = k_ref.shape[1]
    eye = _eye(HEAD_DIM)

    @pl.when(qi == 0)
    def _():
        ka_ref[:, :HEAD_DIM] = _rms(k_ref[0], kg_ref[...]).astype(BF16)
        ka_ref[:, HEAD_DIM:] = kf_ref[0, 0]
        vt_ref[:HEAD_DIM, :] = _dot_nt(eye, v_ref[0].astype(BF16)).astype(BF16)
        vt_ref[HEAD_DIM:, :] = _ones_row(VT_ROWS - HEAD_DIM, T)

    q0 = pl.multiple_of(qi * tq, tq)
    qn = (_rms(q_ref[0], qg_ref[...]) * (ATTN_SCALE * LOG2E)).astype(BF16)
    qt = _dot_nt(eye, qn).astype(BF16)
    c1, c2, c3 = _split3_f32(c_ref[0, :, pl.ds(q0, tq)] * LOG2E)
    r = lax.broadcasted_iota(jnp.int32, (16, tq), 0)
    feat = jnp.where(r == 0, c1, jnp.where(r == 1, c2, jnp.where(r == 2, c3,
                                                                 jnp.where(r < 6, 1.0, 0.0))))
    qa = jnp.concatenate([qt, feat.astype(BF16), jnp.zeros((AUG - HEAD_DIM - 16, tq), BF16)], axis=0)

    kpos = lax.broadcasted_iota(jnp.int32, (tq, 1), 0)
    tpos = lax.broadcasted_iota(jnp.int32, (1, tq), 1)

    def tile(j, carry, score_fn=None):
        k0 = pl.multiple_of(j * tq, tq)
        return _attn_tile(ka_ref[pl.ds(k0, tq), :], vt_ref[:, pl.ds(k0, tq)], qa, carry, score_fn)

    carry = tile(qi, _attn_init(tq), lambda s, c: jnp.where(kpos <= tpos[:, c], s, NEG_INF))
    acc = _attn_finish(lax.fori_loop(0, qi, tile, carry))
    ot = (acc[:HEAD_DIM] / acc[HEAD_DIM:HEAD_DIM + 1]).astype(BF16)
    for c in range(tq // LANES):
        cs = slice(c * LANES, (c + 1) * LANES)
        o_ref[0, cs, :] = _dot_nt(eye, ot[:, cs]).astype(o_ref.dtype)


def fox_attn(proj, kfeat, c_rows, q_gain, k_gain, tq):
    B, T, _ = proj.shape
    H = FOX_HEADS
    tq = min(tq, T)
    assert T % tq == 0 and tq % LANES == 0
    return pl.pallas_call(
        functools.partial(_fox_attn_kernel, tq=tq),
        name="fox_attn",
        grid=(B, H, T // tq),
        in_specs=[pl.BlockSpec((1, tq, HEAD_DIM), lambda b, h, i: (b, i, h)),
                  pl.BlockSpec((1, T, HEAD_DIM), lambda b, h, i: (b, 0, H + h)),
                  pl.BlockSpec((1, T, HEAD_DIM), lambda b, h, i: (b, 0, 2 * H + h)),
                  pl.BlockSpec((1, 1, T, LANES), lambda b, h, i: (b, h, 0, 0)),
                  pl.BlockSpec((1, 1, T), lambda b, h, i: (b * H + h, 0, 0)),
                  pl.BlockSpec((1, HEAD_DIM), lambda b, h, i: (0, 0)),
                  pl.BlockSpec((1, HEAD_DIM), lambda b, h, i: (0, 0))],
        out_specs=pl.BlockSpec((1, tq, HEAD_DIM), lambda b, h, i: (b, i, h)),
        out_shape=jax.ShapeDtypeStruct((B, T, H * HEAD_DIM), BF16),
        scratch_shapes=[pltpu.VMEM((T, AUG), BF16), pltpu.VMEM((VT_ROWS, T), BF16)],
        compiler_params=_params("parallel", "parallel", "arbitrary"),
    )(proj, proj, proj, kfeat, c_rows, q_gain.reshape(1, HEAD_DIM), k_gain.reshape(1, HEAD_DIM))


def _hgrn_kernel(q_ref, f_ref, i_ref, g_ref, la_ref, l1_ref, oml_ref, og_ref, o_ref, *, bs, chunk, span):
    T = q_ref.shape[1]
    R = LANES
    ri = lax.broadcasted_iota(jnp.int32, (R, R), 0)
    ci = lax.broadcasted_iota(jnp.int32, (R, R), 1)

    def prefix_sel(n):
        return (((ri // n) == (ci // n)) & (ri >= ci)).astype(BF16)

    s_idx = lax.broadcasted_iota(jnp.int32, (bs, HEAD_DIM), 0)
    tril = (lax.broadcasted_iota(jnp.int32, (chunk, chunk), 0)
            >= lax.broadcasted_iota(jnp.int32, (chunk, chunk), 1))

    def gates(row0, n):
        rows = pl.ds(row0, n)
        z = f_ref[0, rows, :]
        a = la_ref[0]
        y = l1_ref[0] + _log_sigmoid(z)
        log_f = jnp.maximum(a, y) + jnp.log1p(jnp.exp(-jnp.abs(a - y)))
        hq = q_ref[0, rows, :]
        return log_f, oml_ref[0] * jax.nn.sigmoid(-z), hq * jax.nn.sigmoid(hq), i_ref[0, rows, :]

    def prefixes(log_f, n):
        sel = prefix_sel(n)
        pieces = _split3(log_f)
        b = jnp.concatenate([sum(_dot(sel, p[g * R:(g + 1) * R]) for p in pieces)
                             for g in range(log_f.shape[0] // R)], axis=0)
        bt = jnp.concatenate([jnp.broadcast_to(b[(u + 1) * n - 1:(u + 1) * n, :], (n, HEAD_DIM))
                              for u in range(log_f.shape[0] // n)], axis=0)
        return b, bt

    def finish(row0, n, o):
        rows = pl.ds(row0, n)
        hg = g_ref[0, rows, :]
        o_ref[0, rows, :] = (_rms(o, og_ref[...]) * (hg * jax.nn.sigmoid(hg))).astype(o_ref.dtype)

    def matmul_path(row0, st, kk, qq, vv, b, bt):
        qe = (qq * jnp.exp(b)).astype(BF16)
        k_inv = (kk * jnp.exp(-b)).astype(BF16)
        ke = (kk * jnp.exp(bt - b)).astype(BF16)
        dec = jnp.exp(bt)
        vb = vv.astype(BF16)
        sls = [slice(u * chunk, (u + 1) * chunk) for u in range(span // chunk)]
        scores = [jnp.where(tril, _dot_nt(qe[sl], k_inv[sl]), 0.0).astype(BF16) for sl in sls]
        intra = [_dot(a, vb[sl]) for a, sl in zip(scores, sls)]
        update = [_dot_tn(vb[sl], ke[sl]) for sl in sls]
        outs = []
        for sl, o_in, upd in zip(sls, intra, update):
            outs.append(_dot_nt(qe[sl], st.astype(BF16)) + o_in)
            st = dec[sl.start:sl.start + 1, :] * st + upd
        finish(row0, span, jnp.concatenate(outs, axis=0))
        return st

    def pairwise_step(i, st, base):
        row0 = pl.multiple_of(base + i * R, R)
        log_f, kk, qq, vv = gates(row0, R)
        b, bt = prefixes(log_f, bs)
        qe = (qq * jnp.exp(b)).astype(BF16)
        ke = (kk * jnp.exp(bt - b)).astype(BF16)
        dec = jnp.exp(bt)
        vb = vv.astype(BF16)
        outs = []
        for u in range(R // bs):
            sl = slice(u * bs, (u + 1) * bs)
            bu, vu, qk, ku = b[sl], vv[sl], qq[sl], kk[sl]
            intra = []
            for t in range(bs):
                w = jnp.exp(jnp.where(s_idx <= t, bu[t:t + 1, :] - bu, NEG_INF))
                a_t = jnp.sum(w * (qk[t:t + 1, :] * ku), axis=-1, keepdims=True)
                intra.append(jnp.sum(a_t * vu, axis=0, keepdims=True))
            outs.append(_dot_nt(qe[sl], st.astype(BF16)) + jnp.concatenate(intra, axis=0))
            st = dec[u * bs:u * bs + 1, :] * st + _dot_tn(vb[sl], ke[sl])
        finish(row0, R, jnp.concatenate(outs, axis=0))
        return st

    def step(i, st):
        row0 = pl.multiple_of(i * span, span)
        log_f, kk, qq, vv = gates(row0, span)
        b, bt = prefixes(log_f, chunk)
        return lax.cond(
            jnp.min(bt) >= HGRN_SAFE_LOG_DECAY,
            lambda st: matmul_path(row0, st, kk, qq, vv, b, bt),
            lambda st: lax.fori_loop(0, span // R, functools.partial(pairwise_step, base=row0), st),
            st)

    lax.fori_loop(0, T // span, step, jnp.zeros((HEAD_DIM, HEAD_DIM), F32))


def hgrn(proj, col0, log_lb, log1m_lb, one_m_lb, o_gain):
    B, T, _ = proj.shape
    H = HGRN_HEADS
    row = lambda off: pl.BlockSpec((1, T, HEAD_DIM), lambda b, h: (b, 0, col0 + off * H + h))
    par = pl.BlockSpec((1, 1, HEAD_DIM), lambda b, h: (h, 0, 0))
    log_lb, log1m_lb, one_m_lb = (p.reshape(H, 1, HEAD_DIM) for p in (log_lb, log1m_lb, one_m_lb))
    return pl.pallas_call(
        functools.partial(_hgrn_kernel, bs=HGRN_BLOCK, chunk=HGRN_CHUNK, span=min(HGRN_SPAN, T)),
        name="hgrn",
        grid=(B, H),
        in_specs=[row(0), row(1), row(2), row(3), par, par, par,
                  pl.BlockSpec((1, HEAD_DIM), lambda b, h: (0, 0))],
        out_specs=pl.BlockSpec((1, T, HEAD_DIM), lambda b, h: (b, 0, h)),
        out_shape=jax.ShapeDtypeStruct((B, T, H * HEAD_DIM), BF16),
        compiler_params=_params("parallel", "parallel"),
    )(proj, proj, proj, proj, log_lb, log1m_lb, one_m_lb, o_gain.reshape(1, HEAD_DIM))


def _compress_kernel(t_ref, pe_ref, w1_ref, w2_ref, g_ref, o_ref, *, normalize):
    N = t_ref.shape[1] // CMP_STRIDE
    top = bot = jnp.zeros((N, w1_ref.shape[2]), F32)
    for l in range(CMP_STRIDE):
        xl = t_ref[0, pl.ds(l, N, stride=CMP_STRIDE), :]
        top = top + _dot((xl + pe_ref[l:l + 1, :]).astype(BF16), w1_ref[l])
        bot = bot + _dot((xl + pe_ref[CMP_STRIDE + l:CMP_STRIDE + l + 1, :]).astype(BF16),
                         w1_ref[CMP_STRIDE + l])
    hid = top + pltpu.roll(bot, N - 1, 0)
    out = _dot(jax.nn.gelu(hid).astype(BF16), w2_ref[...])
    if normalize:
        out = _rms(out, g_ref[...])
    valid = lax.broadcasted_iota(jnp.int32, out.shape, 0) < N - 1
    o_ref[0, 0] = jnp.where(valid, out, 0.0)


def compress(proj, col0, pe, w1, w2, layer, gain, normalize):
    B, T, _ = proj.shape
    G = NSA_KV_HEADS
    N = T // CMP_STRIDE
    Hc = w1.shape[3]
    assert CMP_BLOCK == 2 * CMP_STRIDE
    return pl.pallas_call(
        functools.partial(_compress_kernel, normalize=normalize),
        name="nsa_compress",
        grid=(B, G),
        in_specs=[pl.BlockSpec((1, T, HEAD_DIM), lambda b, g: (b, 0, col0 + g)),
                  pl.BlockSpec((None, CMP_BLOCK, HEAD_DIM), lambda b, g: (layer, 0, 0)),
                  pl.BlockSpec((None, CMP_BLOCK, HEAD_DIM, Hc), lambda b, g: (layer, 0, 0, 0)),
                  pl.BlockSpec((None, Hc, HEAD_DIM), lambda b, g: (layer, 0, 0)),
                  pl.BlockSpec((1, HEAD_DIM), lambda b, g: (0, 0))],
        out_specs=pl.BlockSpec((1, 1, N, HEAD_DIM), lambda b, g: (b, g, 0, 0)),
        out_shape=jax.ShapeDtypeStruct((B, G, N, HEAD_DIM), F32),
        compiler_params=_params("parallel", "parallel"),
    )(proj, pe, w1, w2, gain.reshape(1, HEAD_DIM))


_SEL_LANE0 = 64
_CMP_OV_ROW0 = VT_ROWS


def _nsa_kernel(q_ref, ks_ref, vs_ref, kw_ref, vw_ref, kc_ref, vc_ref, gt_ref, qg_ref, kg_ref,
                o_ref, ksa_ref, kwa_ref, vst_ref, vwt_ref, kca_ref, vcov_ref, tiles_ref, *, tq, tk, n_sel):
    g = pl.program_id(1)
    qi = pl.program_id(2)
    T = ks_ref.shape[1]
    J = NSA_GROUP
    n_slc = T // SLC_BLOCK
    nc = kc_ref.shape[2]
    n_cmp = nc - 1
    cols = J * tq
    eye = _eye(HEAD_DIM)

    @pl.when(qi == 0)
    def _():
        row = lax.broadcasted_iota(jnp.int32, (T, LANES), 0)
        lane = lax.broadcasted_iota(jnp.int32, (T, LANES), 1)
        blk = row // SLC_BLOCK
        feats = jnp.where(lane < 3, blk,
                          jnp.where(lane < 6, row % SLC_BLOCK,
                                    jnp.where(lane < 9, 1,
                                              jnp.where(lane - _SEL_LANE0 == blk, 1, 0))))
        feats = feats.astype(F32).astype(BF16)
        ksa_ref[:, :HEAD_DIM] = _rms(ks_ref[0], kg_ref[1:2, :]).astype(BF16)
        ksa_ref[:, HEAD_DIM:] = feats
        kwa_ref[:, :HEAD_DIM] = _rms(kw_ref[0], kg_ref[2:3, :]).astype(BF16)
        kwa_ref[:, HEAD_DIM:] = feats
        vst_ref[:HEAD_DIM, :] = _dot_nt(eye, vs_ref[0].astype(BF16)).astype(BF16)
        vst_ref[HEAD_DIM:, :] = _ones_row(VT_ROWS - HEAD_DIM, T)
        vwt_ref[:HEAD_DIM, :] = _dot_nt(eye, vw_ref[0].astype(BF16)).astype(BF16)
        vwt_ref[HEAD_DIM:, :] = _ones_row(VT_ROWS - HEAD_DIM, T)
        n_r = lax.broadcasted_iota(jnp.int32, (nc, LANES), 0)
        n_l = lax.broadcasted_iota(jnp.int32, (nc, LANES), 1)
        per = SLC_BLOCK // CMP_STRIDE
        cfe = jnp.where(n_l < 3, (n_r // per).astype(F32),
                        jnp.where(n_l < 6, (CMP_STRIDE * (n_r % per)).astype(F32) + 0.5 * (CMP_BLOCK - 1),
                                  jnp.where(n_l < 9, 1.0, 0.0)))
        kca_ref[:, :HEAD_DIM] = kc_ref[0, 0].astype(BF16)
        kca_ref[:, HEAD_DIM:] = cfe.astype(BF16)
        vcov_ref[:HEAD_DIM, :] = _dot_nt(eye, vc_ref[0, 0].astype(BF16)).astype(BF16)
        vcov_ref[HEAD_DIM:_CMP_OV_ROW0, :] = _ones_row(VT_ROWS - HEAD_DIM, nc)
        m_r = lax.broadcasted_iota(jnp.int32, (SLC_BLOCK, nc), 0)
        n_c = lax.broadcasted_iota(jnp.int32, (SLC_BLOCK, nc), 1)
        vcov_ref[_CMP_OV_ROW0:, :] = ((n_c * CMP_STRIDE <= m_r * SLC_BLOCK + (SLC_BLOCK - 1))
                                      & (n_c * CMP_STRIDE + (CMP_BLOCK - 1) >= m_r * SLC_BLOCK)
                                      & (n_c < n_cmp) & (m_r < n_slc)).astype(BF16)

    q0 = pl.multiple_of(qi * tq, tq)
    qraw = q_ref[0]
    qt = jnp.concatenate(
        [_dot_nt(eye, (_rms(qraw[:, j * HEAD_DIM:(j + 1) * HEAD_DIM], qg_ref[...])
                       * (ATTN_SCALE * LOG2E)).astype(BF16)).astype(BF16)
         for j in range(J)], axis=1)

    col = lax.broadcasted_iota(jnp.int32, (1, cols), 1)
    tt = q0 + col % tq
    slope2 = jnp.exp2(-0.5 * (g * J + col // tq + 1).astype(F32)) * LOG2E
    s1, s2, s3 = _split3_f32(slope2)
    r1, r2, r3 = _split3_f32(-slope2 * tt.astype(F32))
    r16 = lax.broadcasted_iota(jnp.int32, (16, cols), 0)
    feat = jnp.zeros((16, cols), F32)
    for i, v in enumerate((s1 * SLC_BLOCK, s2 * SLC_BLOCK, s3 * SLC_BLOCK, s1, s2, s3, r1, r2, r3)):
        feat = jnp.where(r16 == i, v, feat)
    feat = feat.astype(BF16)
    qa_base = jnp.concatenate([qt, feat, jnp.zeros((AUG - HEAD_DIM - 16, cols), BF16)], axis=0)

    n_col = lax.broadcasted_iota(jnp.int32, (nc, 1), 0)
    cmp_end = n_col * CMP_STRIDE + (CMP_BLOCK - 1)
    kca = kca_ref[...]
    r_hi, r_lo = [], []
    for c in _col_groups(cols):
        s = _dot(kca, qa_base[:, c])
        s = jnp.where((cmp_end <= tt[:, c]) & (n_col < n_cmp), s, NEG_INF)
        m = jnp.max(s, axis=0, keepdims=True)
        p = jnp.exp2(s - jnp.where(m == NEG_INF, 0.0, m))
        p_hi = p.astype(BF16)
        r_hi.append(_dot(vcov_ref[...], p_hi))
        r_lo.append(_dot(vcov_ref[HEAD_DIM:, :], (p - p_hi.astype(F32)).astype(BF16)))
    r_hi = jnp.concatenate(r_hi, axis=1)
    r_lo = jnp.concatenate(r_lo, axis=1)
    z = r_hi[HEAD_DIM:HEAD_DIM + 1] + r_lo[0:1]
    inv_z = 1.0 / jnp.where(z > 0, z, 1.0)

    wq = min(tq, WINDOW_Q)
    span = WINDOW + wq
    d = (lax.broadcasted_iota(jnp.int32, (span, 1), 0)
         - lax.broadcasted_iota(jnp.int32, (1, wq), 1))
    parts = []
    for c in range(tq // wq):
        k0w = pl.multiple_of(jnp.maximum(q0 + c * wq - WINDOW, 0), wq)
        off = q0 + c * wq - k0w
        bias = jnp.where((d <= off) & (d > off - WINDOW), 0.0, NEG_INF)
        bias = jnp.concatenate([bias] * J, axis=1)
        qa_c = jnp.concatenate([qa_base[:, j * tq + c * wq:j * tq + (c + 1) * wq] for j in range(J)], axis=1)
        parts.append(_attn_finish(_attn_tile(
            kwa_ref[pl.ds(k0w, span), :], vwt_ref[:, pl.ds(k0w, span)], qa_c, _attn_init(J * wq),
            lambda s, sub, bias=bias: s + bias[:, sub])))
    acc_w = jnp.concatenate([p[:, j * wq:(j + 1) * wq] for j in range(J) for p in parts], axis=1)

    imp4 = (r_hi[_CMP_OV_ROW0:] + r_lo[_CMP_OV_ROW0 - HEAD_DIM:]) * inv_z
    imp = imp4[:, 0:tq]
    for j in range(1, J):
        imp = imp + imp4[:, j * tq:(j + 1) * tq]
    blk = lax.broadcasted_iota(jnp.int32, (SLC_BLOCK, tq), 0)
    t_q = q0 + lax.broadcasted_iota(jnp.int32, (1, tq), 1)
    imp = jnp.where(blk * SLC_BLOCK > t_q, NEG_INF, imp)
    imp = jnp.where((blk == t_q // SLC_BLOCK) | (blk == 0), jnp.inf, imp)
    blk_f = blk.astype(F32)
    taken = blk >= n_slc
    for _ in range(n_sel):
        vals = jnp.where(taken, NEG_INF, imp)
        best = jnp.max(vals, axis=0, keepdims=True)
        cand = (vals == best) & jnp.logical_not(taken)
        first = jnp.min(jnp.where(cand, blk_f, float(SLC_BLOCK)), axis=0, keepdims=True)
        taken = taken | (blk_f == first)
    sel = taken & (blk < n_slc)
    not_sel = jnp.where(sel, 0.0, -MASK_BIG).astype(BF16)
    qa_slc = jnp.concatenate([qt, feat, jnp.zeros((_SEL_LANE0 - 16, cols), BF16)]
                             + [jnp.concatenate([not_sel] * J, axis=1)], axis=0)

    def slc_tile(jt, carry, score_fn=None):
        k0 = pl.multiple_of(jt * tk, tk)
        return _attn_tile(ksa_ref[pl.ds(k0, tk), :], vst_ref[:, pl.ds(k0, tk)], qa_slc, carry, score_fn)

    jd = q0 // tk
    kpos = jd * tk + lax.broadcasted_iota(jnp.int32, (tk, 1), 0)
    carry = slc_tile(jd, _attn_init(cols), lambda s, c: jnp.where(kpos <= tt[:, c], s, NEG_INF))
    sel_f = jnp.where(sel, 1.0, 0.0)
    per_tile = tk // SLC_BLOCK
    n_live = jnp.int32(0)
    for jt in range(T // tk - 1):
        live = (jnp.max(sel_f[jt * per_tile:(jt + 1) * per_tile, :]) > 0.5) & (jt < jd)
        tiles_ref[n_live] = jt
        n_live = n_live + live.astype(jnp.int32)
    acc_s = _attn_finish(lax.fori_loop(0, n_live, lambda i, c: slc_tile(tiles_ref[i], c), carry))

    gates_t = jnp.transpose(jax.nn.sigmoid(gt_ref[0]))
    gate = lambda c: jnp.concatenate([gates_t[3 * j + c:3 * j + c + 1, :] for j in range(J)], axis=1)
    o_t = ((gate(0) * inv_z) * r_hi[:HEAD_DIM]
           + (gate(1) / acc_s[HEAD_DIM:HEAD_DIM + 1]) * acc_s[:HEAD_DIM]
           + (gate(2) / acc_w[HEAD_DIM:HEAD_DIM + 1]) * acc_w[:HEAD_DIM]).astype(BF16)
    for j in range(J):
        for c in range(tq // LANES):
            o_ref[0, c * LANES:(c + 1) * LANES, j * HEAD_DIM:(j + 1) * HEAD_DIM] = _dot_nt(
                eye, o_t[:, j * tq + c * LANES:j * tq + (c + 1) * LANES]).astype(o_ref.dtype)


def nsa_attn(proj, gates, k_cmp, v_cmp, q_gain, k_gain, tq, tk):
    B, T, _ = proj.shape
    G, J = NSA_KV_HEADS, NSA_GROUP
    tk = min(tk, T)
    nc = k_cmp.shape[2]
    assert T % tk == 0 and tk % tq == 0 and T >= WINDOW + tq
    assert T // SLC_BLOCK <= LANES - _SEL_LANE0 and SLC_BLOCK <= LANES - _SEL_LANE0
    qcols = NSA_HEADS
    kv = lambda slab: pl.BlockSpec((1, T, HEAD_DIM), lambda b, g, i: (b, 0, qcols + slab * G + g))
    cmp_spec = pl.BlockSpec((1, 1, nc, HEAD_DIM), lambda b, g, i: (b, g, 0, 0))
    return pl.pallas_call(
        functools.partial(_nsa_kernel, tq=tq, tk=tk, n_sel=min(SLC_TOPK, T // SLC_BLOCK)),
        name="nsa_attn",
        grid=(B, G, T // tq),
        in_specs=[pl.BlockSpec((1, tq, J * HEAD_DIM), lambda b, g, i: (b, i, g)),
                  kv(2), kv(3), kv(4), kv(5), cmp_spec, cmp_spec,
                  pl.BlockSpec((1, tq, LANES), lambda b, g, i: (b, i, g)),
                  pl.BlockSpec((1, HEAD_DIM), lambda b, g, i: (0, 0)),
                  pl.BlockSpec((3, HEAD_DIM), lambda b, g, i: (0, 0))],
        out_specs=pl.BlockSpec((1, tq, J * HEAD_DIM), lambda b, g, i: (b, i, g)),
        out_shape=jax.ShapeDtypeStruct((B, T, NSA_HEADS * HEAD_DIM), BF16),
        scratch_shapes=[pltpu.VMEM((T, AUG), BF16), pltpu.VMEM((T, AUG), BF16),
                        pltpu.VMEM((VT_ROWS, T), BF16), pltpu.VMEM((VT_ROWS, T), BF16),
                        pltpu.VMEM((nc, AUG), BF16),
                        pltpu.VMEM((VT_ROWS + SLC_BLOCK, nc), BF16),
                        pltpu.SMEM((T // tk,), jnp.int32)],
        compiler_params=_params("parallel", "parallel", "arbitrary"),
    )(proj, proj, proj, proj, proj, k_cmp, v_cmp, gates, q_gain.reshape(1, HEAD_DIM), k_gain)


def even_weights(w_in, w_out):
    fw = FOX_HEADS * HEAD_DIM
    w_main = jnp.concatenate([w_in[:, :, :3 * fw], w_in[:, :, 3 * fw + FOX_HEADS:]], axis=2).astype(BF16)
    w_gate = jnp.pad(w_in[:, :, 3 * fw:3 * fw + FOX_HEADS], ((0, 0), (0, 0), (0, LANES - FOX_HEADS)))
    return w_main, w_gate.astype(BF16), w_out.astype(BF16)


def even_mixer(x, gain, weights, e, f_bias, q_gain, k_gain, lb, o_gain):
    B, T, D = x.shape
    xf = x.reshape(B * T, D)
    w_main, w_gate, w_out = weights
    proj, gate = rms_matmul(xf, gain, w_main, e, 1024, 1024, side_w=w_gate)
    proj, gate = proj.reshape(B, T, -1), gate.reshape(B, T, LANES)
    f_bias = jnp.pad(f_bias.reshape(1, FOX_HEADS).astype(F32), ((0, 0), (0, LANES - FOX_HEADS)))
    c, kfeat = fox_gate(gate, f_bias, FOX_HEADS)
    c_rows = c[:, :, :FOX_HEADS].transpose(0, 2, 1).reshape(B * FOX_HEADS, 1, T)
    o_fox = fox_attn(proj, kfeat, c_rows, q_gain, k_gain, 1024)
    lb = lb.astype(F32).reshape(HGRN_HEADS, HEAD_DIM)
    o_h = hgrn(proj, 3 * FOX_HEADS, jnp.log(lb), jnp.log1p(-lb), 1.0 - lb, o_gain)
    out = out_proj([o_fox.reshape(B * T, -1), o_h.reshape(B * T, -1)], w_out, e, xf, 256)
    return out.reshape(B, T, D)


def nsa_weights(w_in, w_out, w1_k, w2_k, w1_v, w2_v):
    L, D, _ = w_in.shape
    G, J = NSA_KV_HEADS, NSA_GROUP
    n_main = (NSA_HEADS + 6 * G) * HEAD_DIM
    w_gate = w_in[:, :, n_main:].reshape(L, D, G, 3 * J)
    w_gate = jnp.pad(w_gate, ((0, 0), (0, 0), (0, 0), (0, LANES - 3 * J))).reshape(L, D, G * LANES)
    return (w_in[:, :, :n_main].astype(BF16), w_gate.astype(BF16), w_out.astype(BF16),
            w1_k.astype(BF16), w2_k.astype(BF16), w1_v.astype(BF16), w2_v.astype(BF16))


def nsa_mixer(x, gain, weights, o, q_gain, k_gain, pe_k, pe_v):
    B, T, D = x.shape
    G = NSA_KV_HEADS
    xf = x.reshape(B * T, D)
    w_main, w_gate, w_out, w1_k, w2_k, w1_v, w2_v = weights
    proj, gates = rms_matmul(xf, gain, w_main, o, 1024, 1024, side_w=w_gate)
    proj, gates = proj.reshape(B, T, -1), gates.reshape(B, T, G * LANES)
    k_cmp = compress(proj, NSA_HEADS, pe_k, w1_k, w2_k, o, k_gain[0], True)
    v_cmp = compress(proj, NSA_HEADS + G, pe_v, w1_v, w2_v, o, k_gain[0], False)
    att = nsa_attn(proj, gates, k_cmp, v_cmp, q_gain, k_gain, 512, 512)
    out = out_proj([att.reshape(B * T, -1)], w_out, o, xf, 256)
    return out.reshape(B, T, D)


def kernel(x, mem, norm_mix, norm_mem, norm_ffn, mem_in_gain, even_w_in, even_w_out, fox_f_bias,
           fox_q_gain, fox_k_gain, hgrn_lb_logits, hgrn_o_gain, odd_w_in, odd_w_out, nsa_q_gain,
           nsa_k_gain, cmp_pe_k, cmp_w1_k, cmp_w2_k, cmp_pe_v, cmp_w1_v, cmp_w2_v, mem_wq, mem_wkv,
           mem_wo, mem_q_gain, mem_k_gain, ffn_w1, ffn_w3, ffn_w2):
    B, T, D = x.shape
    depth = norm_mix.shape[0]
    L = mem.shape[1]
    lb_cum = jnp.cumsum(jax.nn.softmax(hgrn_lb_logits.astype(F32), axis=0), axis=0)
    hgrn_lb = lb_cum - lb_cum[0:1]
    memf = mem.reshape(B * L, D)
    even_w = even_weights(even_w_in, even_w_out)
    nsa_w = nsa_weights(odd_w_in, odd_w_out, cmp_w1_k, cmp_w2_k, cmp_w1_v, cmp_w2_v)
    mem_wq, mem_wkv, mem_wo = (w.astype(BF16) for w in (mem_wq, mem_wkv, mem_wo))
    for layer in range(depth):
        if layer % 2 == 0:
            e = layer // 2
            x = even_mixer(x, norm_mix[layer], even_w, e, fox_f_bias[e], fox_q_gain[e], fox_k_gain[e],
                           hgrn_lb[e], hgrn_o_gain[e])
        else:
            o = layer // 2
            x = nsa_mixer(x, norm_mix[layer], nsa_w, o, nsa_q_gain[o], nsa_k_gain[o], cmp_pe_k, cmp_pe_v)
        kv = rms_matmul(memf, mem_in_gain[layer], mem_wkv, layer, 256, 1024)
        x = mem_attn(x, norm_mem[layer], mem_wq, kv.reshape(B, L, -1), mem_q_gain[layer],
                     mem_k_gain[layer], mem_wo, layer, 512)
        x = ffn(x.reshape(B * T, D), norm_ffn[layer], ffn_w1, ffn_w3, ffn_w2, layer,
                1024, 256).reshape(B, T, D)
    return x
```

```python
import functools

import jax
import jax.numpy as jnp
from jax import lax
from jax.experimental import pallas as pl
from jax.experimental.pallas import tpu as pltpu

F32 = jnp.float32
BF16 = jnp.bfloat16
NEG_INF = float("-inf")

EPS = 1e-6
HEAD_DIM = 128
ATTN_SCALE = HEAD_DIM ** -0.5
LANES = 128
FOX_HEADS = 8
HGRN_HEADS = 8
HGRN_BLOCK = 16
HGRN_CHUNK = 64
HGRN_SPAN = 512
HGRN_SAFE_LOG_DECAY = -80.0
NSA_HEADS = 16
NSA_KV_HEADS = 4
NSA_GROUP = NSA_HEADS // NSA_KV_HEADS
CMP_BLOCK = 32
CMP_STRIDE = 16
SLC_BLOCK = 64
SLC_TOPK = 16
WINDOW = 512
WINDOW_Q = 256
MEM_HEADS = 4
V7X_VMEM_LIMIT = 56 * 1024 * 1024
LOG2E = 1.4426950408889634
MASK_BIG = 1e30
AUG = 2 * LANES
VT_ROWS = HEAD_DIM + 16


def _params(*sem):
    return pltpu.CompilerParams(dimension_semantics=sem, vmem_limit_bytes=V7X_VMEM_LIMIT)


def _rms(x, gain):
    return x * lax.rsqrt(jnp.mean(x * x, axis=-1, keepdims=True) + EPS) * gain


def _dot(a, b):
    return jnp.dot(a, b, preferred_element_type=F32)


def _dot_nt(a, b):
    return lax.dot_general(a, b, (((1,), (1,)), ((), ())), preferred_element_type=F32)


def _dot_tn(a, b):
    return lax.dot_general(a, b, (((0,), (0,)), ((), ())), preferred_element_type=F32)


def _split3(x):
    x1 = x.astype(BF16)
    r1 = x - x1.astype(F32)
    x2 = r1.astype(BF16)
    return x1, x2, (r1 - x2.astype(F32)).astype(BF16)


def _exact_dot(sel, x):
    x1, x2, x3 = _split3(x)
    return _dot(sel, x1) + _dot(sel, x2) + _dot(sel, x3)


def _log_sigmoid(z):
    return jnp.minimum(z, 0.0) - jnp.log1p(jnp.exp(-jnp.abs(z)))


def _rms_matmul_kernel(x_ref, g_ref, w_ref, *rest, w_transposed):
    side_w_ref, o_ref, side_o_ref, hn_ref = rest if len(rest) == 4 else (None, rest[0], None, rest[1])

    @pl.when(pl.program_id(1) == 0)
    def _():
        hn_ref[...] = _rms(x_ref[...], g_ref[...]).astype(BF16)
        if side_w_ref is not None:
            side_o_ref[...] = _dot(hn_ref[...], side_w_ref[...])

    o_ref[...] = (_dot_nt if w_transposed else _dot)(hn_ref[...], w_ref[...])


def rms_matmul(x, gain, w, layer, tm, tn, side_w=None, w_transposed=False):
    M, K = x.shape
    N = w.shape[1] if w_transposed else w.shape[2]
    tm, tn = min(tm, M), min(tn, N)
    assert M % tm == 0 and N % tn == 0
    w_spec = (pl.BlockSpec((None, tn, K), lambda i, j: (layer, j, 0)) if w_transposed
              else pl.BlockSpec((None, K, tn), lambda i, j: (layer, 0, j)))
    in_specs = [pl.BlockSpec((tm, K), lambda i, j: (i, 0)),
                pl.BlockSpec((1, K), lambda i, j: (0, 0)),
                w_spec]
    out_specs = pl.BlockSpec((tm, tn), lambda i, j: (i, j))
    out_shape = jax.ShapeDtypeStruct((M, N), F32)
    args = (x, gain.reshape(1, K), w)
    if side_w is not None:
        ns = side_w.shape[2]
        in_specs.append(pl.BlockSpec((None, K, ns), lambda i, j: (layer, 0, 0)))
        out_specs = [out_specs, pl.BlockSpec((tm, ns), lambda i, j: (i, 0))]
        out_shape = [out_shape, jax.ShapeDtypeStruct((M, ns), F32)]
        args += (side_w,)
    return pl.pallas_call(
        functools.partial(_rms_matmul_kernel, w_transposed=w_transposed),
        name="rms_matmul",
        grid=(M // tm, N // tn),
        in_specs=in_specs,
        out_specs=out_specs,
        out_shape=out_shape,
        scratch_shapes=[pltpu.VMEM((tm, K), BF16)],
        compiler_params=_params("parallel", "arbitrary"),
    )(*args)


def _out_proj_kernel(*refs, n_in):
    a_refs, w_refs = refs[:n_in], refs[n_in:2 * n_in]
    res_ref, o_ref = refs[2 * n_in], refs[2 * n_in + 1]
    acc = res_ref[...]
    for a_ref, w_ref in zip(a_refs, w_refs):
        acc = acc + _dot(a_ref[...], w_ref[...])
    o_ref[...] = acc


def out_proj(acts, w, layer, res, tm):
    M, N = res.shape
    tm = min(tm, M)
    assert M % tm == 0
    n_in = len(acts)
    ka = acts[0].shape[1]
    assert all(a.shape[1] == ka for a in acts) and n_in * ka == w.shape[1]
    row_block = lambda r: pl.BlockSpec((None, ka, N), lambda i: (layer, r, 0))
    weights = [w] * n_in
    in_specs = ([pl.BlockSpec((tm, ka), lambda i: (i, 0)) for _ in acts]
                + [row_block(r) for r in range(n_in)]
                + [pl.BlockSpec((tm, N), lambda i: (i, 0))])
    return pl.pallas_call(
        functools.partial(_out_proj_kernel, n_in=n_in),
        name="out_proj",
        grid=(M // tm,),
        in_specs=in_specs,
        out_specs=pl.BlockSpec((tm, N), lambda i: (i, 0)),
        out_shape=jax.ShapeDtypeStruct((M, N), F32),
        compiler_params=_params("parallel"),
    )(*acts, *weights, res)


def _ffn_kernel(x_ref, g_ref, w1_ref, w3_ref, w2_ref, o_ref, hn_ref):
    f = pl.program_id(1)

    @pl.when(f == 0)
    def _():
        x = x_ref[...]
        hn_ref[...] = _rms(x, g_ref[...]).astype(BF16)
        o_ref[...] = x

    h = hn_ref[...]
    a = _dot(h, w1_ref[...].astype(BF16))
    b = _dot(h, w3_ref[...].astype(BF16))
    u = (a * jax.nn.sigmoid(a) * b).astype(BF16)
    o_ref[...] += _dot(u, w2_ref[...].astype(BF16))


def ffn(x, gain, w1, w3, w2, layer, tm, tf):
    M, D = x.shape
    Fh = w1.shape[2]
    tm, tf = min(tm, M), min(tf, Fh)
    assert M % tm == 0 and Fh % tf == 0
    return pl.pallas_call(
        _ffn_kernel,
        name="ffn",
        grid=(M // tm, Fh // tf),
        in_specs=[pl.BlockSpec((tm, D), lambda i, f: (i, 0)),
                  pl.BlockSpec((1, D), lambda i, f: (0, 0)),
                  pl.BlockSpec((None, D, tf), lambda i, f: (layer, 0, f)),
                  pl.BlockSpec((None, D, tf), lambda i, f: (layer, 0, f)),
                  pl.BlockSpec((None, tf, D), lambda i, f: (layer, f, 0))],
        out_specs=pl.BlockSpec((tm, D), lambda i, f: (i, 0)),
        out_shape=jax.ShapeDtypeStruct((M, D), F32),
        scratch_shapes=[pltpu.VMEM((tm, D), BF16)],
        compiler_params=_params("parallel", "arbitrary"),
    )(x, gain.reshape(1, D), w1, w3, w2)


def _mem_attn_kernel(x_ref, g_ref, wq_ref, kv_ref, qg_ref, kg_ref, wo_ref, o_ref, *, heads):
    x = x_ref[0]
    h = _rms(x, g_ref[...]).astype(BF16)
    q = _dot(h, wq_ref[...])
    kv = kv_ref[0]
    w = heads * HEAD_DIM
    outs = []
    for hh in range(heads):
        sl = slice(hh * HEAD_DIM, (hh + 1) * HEAD_DIM)
        qh = (_rms(q[:, sl], qg_ref[...]) * ATTN_SCALE).astype(BF16)
        kh = _rms(kv[:, sl], kg_ref[...]).astype(BF16)
        vh = kv[:, w + hh * HEAD_DIM: w + (hh + 1) * HEAD_DIM].astype(BF16)
        s = _dot_nt(qh, kh)
        p = jnp.exp(s - jnp.max(s, axis=-1, keepdims=True))
        z = jnp.sum(p, axis=-1, keepdims=True)
        outs.append((_dot(p.astype(BF16), vh) / z).astype(BF16))
    o = jnp.concatenate(outs, axis=-1)
    o_ref[0] = x + _dot(o, wo_ref[...])


def mem_attn(x, gain, wq, kv, q_gain, k_gain, wo, layer, tm):
    B, T, D = x.shape
    L, W2 = kv.shape[1], kv.shape[2]
    W = W2 // 2
    tm = min(tm, T)
    assert T % tm == 0
    return pl.pallas_call(
        functools.partial(_mem_attn_kernel, heads=W // HEAD_DIM),
        name="mem_attn",
        grid=(B, T // tm),
        in_specs=[pl.BlockSpec((1, tm, D), lambda b, i: (b, i, 0)),
                  pl.BlockSpec((1, D), lambda b, i: (0, 0)),
                  pl.BlockSpec((None, D, W), lambda b, i: (layer, 0, 0)),
                  pl.BlockSpec((1, L, W2), lambda b, i: (b, 0, 0)),
                  pl.BlockSpec((1, HEAD_DIM), lambda b, i: (0, 0)),
                  pl.BlockSpec((1, HEAD_DIM), lambda b, i: (0, 0)),
                  pl.BlockSpec((None, W, D), lambda b, i: (layer, 0, 0))],
        out_specs=pl.BlockSpec((1, tm, D), lambda b, i: (b, i, 0)),
        out_shape=jax.ShapeDtypeStruct((B, T, D), F32),
        compiler_params=_params("parallel", "parallel"),
    )(x, gain.reshape(1, D), wq, kv, q_gain.reshape(1, HEAD_DIM), k_gain.reshape(1, HEAD_DIM), wo)


def _eye(n):
    return (lax.broadcasted_iota(jnp.int32, (n, n), 0)
            == lax.broadcasted_iota(jnp.int32, (n, n), 1)).astype(BF16)


def _ones_row(rows, n):
    return (lax.broadcasted_iota(jnp.int32, (rows, n), 0) == 0).astype(BF16)


def _split3_f32(x):
    return tuple(t.astype(F32) for t in _split3(x))


COL_GROUPS = 1


def _col_groups(n):
    w = n // COL_GROUPS
    return [slice(i * w, (i + 1) * w) for i in range(COL_GROUPS)]


def _attn_init(n):
    return tuple((jnp.full((1, c.stop - c.start), NEG_INF, F32), jnp.zeros((VT_ROWS, c.stop - c.start), F32))
                 for c in _col_groups(n))


def _attn_tile(k_tile, vt_tile, qa, carry, score_fn=None):
    out = []
    for (m, acc), c in zip(carry, _col_groups(qa.shape[1])):
        s = _dot(k_tile, qa[:, c])
        if score_fn is not None:
            s = score_fn(s, c)
        m_new = jnp.maximum(m, jnp.max(s, axis=0, keepdims=True))
        p = jnp.exp2(s - m_new).astype(BF16)
        out.append((m_new, jnp.exp2(m - m_new) * acc + _dot(vt_tile, p)))
    return tuple(out)


def _attn_finish(carry):
    return jnp.concatenate([acc for _, acc in carry], axis=1)


def _fox_gate_kernel(g_ref, b_ref, c_ref, kf_ref, *, heads):
    T = g_ref.shape[1]
    r = lax.broadcasted_iota(jnp.int32, (LANES, LANES), 0)
    lane = lax.broadcasted_iota(jnp.int32, (LANES, LANES), 1)
    tri = (r >= lane).astype(BF16)

    def body(i, carry):
        rows = pl.ds(pl.multiple_of(i * LANES, LANES), LANES)
        cs = _exact_dot(tri, _log_sigmoid(g_ref[0, rows, :] + b_ref[...])) + carry
        c_ref[0, rows, :] = cs
        c2 = cs * LOG2E
        for h in range(heads):
            d1, d2, d3 = _split3_f32(c2[:, h:h + 1])
            feat = jnp.where(lane < 3, 1.0,
                             jnp.where(lane == 3, -d1, jnp.where(lane == 4, -d2,
                                                                 jnp.where(lane == 5, -d3, 0.0))))
            kf_ref[0, h, rows, :] = feat.astype(BF16)
        return cs[LANES - 1:LANES, :]

    lax.fori_loop(0, T // LANES, body, jnp.zeros((1, LANES), F32))


def fox_gate(g, bias, heads):
    B, T, _ = g.shape
    return pl.pallas_call(
        functools.partial(_fox_gate_kernel, heads=heads),
        name="fox_gate",
        grid=(B,),
        in_specs=[pl.BlockSpec((1, T, LANES), lambda b: (b, 0, 0)),
                  pl.BlockSpec((1, LANES), lambda b: (0, 0))],
        out_specs=[pl.BlockSpec((1, T, LANES), lambda b: (b, 0, 0)),
                   pl.BlockSpec((1, heads, T, LANES), lambda b: (b, 0, 0, 0))],
        out_shape=[jax.ShapeDtypeStruct((B, T, LANES), F32),
                   jax.ShapeDtypeStruct((B, heads, T, LANES), BF16)],
        compiler_params=_params("parallel"),
    )(g, bias)


def _fox_attn_kernel(q_ref, k_ref, v_ref, kf_ref, c_ref, qg_ref, kg_ref, o_ref, ka_ref, vt_ref, *, tq):
    qi = pl.program_id(2)
    T = k_ref.shape[1]
    eye = _eye(HEAD_DIM)

    @pl.when(qi == 0)
    def _():
        ka_ref[:, :HEAD_DIM] = _rms(k_ref[0], kg_ref[...]).astype(BF16)
        ka_ref[:, HEAD_DIM:] = kf_ref[0, 0]
        vt_ref[:HEAD_DIM, :] = _dot_nt(eye, v_ref[0].astype(BF16)).astype(BF16)
        vt_ref[HEAD_DIM:, :] = _ones_row(VT_ROWS - HEAD_DIM, T)

    q0 = pl.multiple_of(qi * tq, tq)
    qn = (_rms(q_ref[0], qg_ref[...]) * (ATTN_SCALE * LOG2E)).astype(BF16)
    qt = _dot_nt(eye, qn).astype(BF16)
    c1, c2, c3 = _split3_f32(c_ref[0, :, pl.ds(q0, tq)] * LOG2E)
    r = lax.broadcasted_iota(jnp.int32, (16, tq), 0)
    feat = jnp.where(r == 0, c1, jnp.where(r == 1, c2, jnp.where(r == 2, c3,
                                                                 jnp.where(r < 6, 1.0, 0.0))))
    qa = jnp.concatenate([qt, feat.astype(BF16), jnp.zeros((AUG - HEAD_DIM - 16, tq), BF16)], axis=0)

    kpos = lax.broadcasted_iota(jnp.int32, (tq, 1), 0)
    tpos = lax.broadcasted_iota(jnp.int32, (1, tq), 1)

    def tile(j, carry, score_fn=None):
        k0 = pl.multiple_of(j * tq, tq)
        return _attn_tile(ka_ref[pl.ds(k0, tq), :], vt_ref[:, pl.ds(k0, tq)], qa, carry, score_fn)

    carry = tile(qi, _attn_init(tq), lambda s, c: jnp.where(kpos <= tpos[:, c], s, NEG_INF))
    acc = _attn_finish(lax.fori_loop(0, qi, tile, carry))
    ot = (acc[:HEAD_DIM] / acc[HEAD_DIM:HEAD_DIM + 1]).astype(BF16)
    for c in range(tq // LANES):
        cs = slice(c * LANES, (c + 1) * LANES)
        o_ref[0, cs, :] = _dot_nt(eye, ot[:, cs]).astype(o_ref.dtype)


def fox_attn(proj, kfeat, c_rows, q_gain, k_gain, tq):
    B, T, _ = proj.shape
    H = FOX_HEADS
    tq = min(tq, T)
    assert T % tq == 0 and tq % LANES == 0
    return pl.pallas_call(
        functools.partial(_fox_attn_kernel, tq=tq),
        name="fox_attn",
        grid=(B, H, T // tq),
        in_specs=[pl.BlockSpec((1, tq, HEAD_DIM), lambda b, h, i: (b, i, h)),
                  pl.BlockSpec((1, T, HEAD_DIM), lambda b, h, i: (b, 0, H + h)),
                  pl.BlockSpec((1, T, HEAD_DIM), lambda b, h, i: (b, 0, 2 * H + h)),
                  pl.BlockSpec((1, 1, T, LANES), lambda b, h, i: (b, h, 0, 0)),
                  pl.BlockSpec((1, 1, T), lambda b, h, i: (b * H + h, 0, 0)),
                  pl.BlockSpec((1, HEAD_DIM), lambda b, h, i: (0, 0)),
                  pl.BlockSpec((1, HEAD_DIM), lambda b, h, i: (0, 0))],
        out_specs=pl.BlockSpec((1, tq, HEAD_DIM), lambda b, h, i: (b, i, h)),
        out_shape=jax.ShapeDtypeStruct((B, T, H * HEAD_DIM), BF16),
        scratch_shapes=[pltpu.VMEM((T, AUG), BF16), pltpu.VMEM((VT_ROWS, T), BF16)],
        compiler_params=_params("parallel", "parallel", "arbitrary"),
    )(proj, proj, proj, kfeat, c_rows, q_gain.reshape(1, HEAD_DIM), k_gain.reshape(1, HEAD_DIM))


def _hgrn_kernel(q_ref, f_ref, i_ref, g_ref, la_ref, l1_ref, oml_ref, og_ref, o_ref, *, bs, chunk, span):
    T = q_ref.shape[1]
    R = LANES
    ri = lax.broadcasted_iota(jnp.int32, (R, R), 0)
    ci = lax.broadcasted_iota(jnp.int32, (R, R), 1)

    def prefix_sel(n):
        return (((ri // n) == (ci // n)) & (ri >= ci)).astype(BF16)

    s_idx = lax.broadcasted_iota(jnp.int32, (bs, HEAD_DIM), 0)
    tril = (lax.broadcasted_iota(jnp.int32, (chunk, chunk), 0)
            >= lax.broadcasted_iota(jnp.int32, (chunk, chunk), 1))

    def gates(row0, n):
        rows = pl.ds(row0, n)
        z = f_ref[0, rows, :]
        a = la_ref[0]
        y = l1_ref[0] + _log_sigmoid(z)
        log_f = jnp.maximum(a, y) + jnp.log1p(jnp.exp(-jnp.abs(a - y)))
        hq = q_ref[0, rows, :]
        return log_f, oml_ref[0] * jax.nn.sigmoid(-z), hq * jax.nn.sigmoid(hq), i_ref[0, rows, :]

    def prefixes(log_f, n):
        sel = prefix_sel(n)
        pieces = _split3(log_f)
        b = jnp.concatenate([sum(_dot(sel, p[g * R:(g + 1) * R]) for p in pieces)
                             for g in range(log_f.shape[0] // R)], axis=0)
        bt = jnp.concatenate([jnp.broadcast_to(b[(u + 1) * n - 1:(u + 1) * n, :], (n, HEAD_DIM))
                              for u in range(log_f.shape[0] // n)], axis=0)
        return b, bt

    def finish(row0, n, o):
        rows = pl.ds(row0, n)
        hg = g_ref[0, rows, :]
        o_ref[0, rows, :] = (_rms(o, og_ref[...]) * (hg * jax.nn.sigmoid(hg))).astype(o_ref.dtype)

    def matmul_path(row0, st, kk, qq, vv, b, bt):
        qe = (qq * jnp.exp(b)).astype(BF16)
        k_inv = (kk * jnp.exp(-b)).astype(BF16)
        ke = (kk * jnp.exp(bt - b)).astype(BF16)
        dec = jnp.exp(bt)
        vb = vv.astype(BF16)
        sls = [slice(u * chunk, (u + 1) * chunk) for u in range(span // chunk)]
        scores = [jnp.where(tril, _dot_nt(qe[sl], k_inv[sl]), 0.0).astype(BF16) for sl in sls]
        intra = [_dot(a, vb[sl]) for a, sl in zip(scores, sls)]
        update = [_dot_tn(vb[sl], ke[sl]) for sl in sls]
        outs = []
        for sl, o_in, upd in zip(sls, intra, update):
            outs.append(_dot_nt(qe[sl], st.astype(BF16)) + o_in)
            st = dec[sl.start:sl.start + 1, :] * st + upd
        finish(row0, span, jnp.concatenate(outs, axis=0))
        return st

    def pairwise_step(i, st, base):
        row0 = pl.multiple_of(base + i * R, R)
        log_f, kk, qq, vv = gates(row0, R)
        b, bt = prefixes(log_f, bs)
        qe = (qq * jnp.exp(b)).astype(BF16)
        ke = (kk * jnp.exp(bt - b)).astype(BF16)
        dec = jnp.exp(bt)
        vb = vv.astype(BF16)
        outs = []
        for u in range(R // bs):
            sl = slice(u * bs, (u + 1) * bs)
            bu, vu, qk, ku = b[sl], vv[sl], qq[sl], kk[sl]
            intra = []
            for t in range(bs):
                w = jnp.exp(jnp.where(s_idx <= t, bu[t:t + 1, :] - bu, NEG_INF))
                a_t = jnp.sum(w * (qk[t:t + 1, :] * ku), axis=-1, keepdims=True)
                intra.append(jnp.sum(a_t * vu, axis=0, keepdims=True))
            outs.append(_dot_nt(qe[sl], st.astype(BF16)) + jnp.concatenate(intra, axis=0))
            st = dec[u * bs:u * bs + 1, :] * st + _dot_tn(vb[sl], ke[sl])
        finish(row0, R, jnp.concatenate(outs, axis=0))
        return st

    def step(i, st):
        row0 = pl.multiple_of(i * span, span)
        log_f, kk, qq, vv = gates(row0, span)
        b, bt = prefixes(log_f, chunk)
        return lax.cond(
            jnp.min(bt) >= HGRN_SAFE_LOG_DECAY,
            lambda st: matmul_path(row0, st, kk, qq, vv, b, bt),
            lambda st: lax.fori_loop(0, span // R, functools.partial(pairwise_step, base=row0), st),
            st)

    lax.fori_loop(0, T // span, step, jnp.zeros((HEAD_DIM, HEAD_DIM), F32))


def hgrn(proj, col0, log_lb, log1m_lb, one_m_lb, o_gain):
    B, T, _ = proj.shape
    H = HGRN_HEADS
    row = lambda off: pl.BlockSpec((1, T, HEAD_DIM), lambda b, h: (b, 0, col0 + off * H + h))
    par = pl.BlockSpec((1, 1, HEAD_DIM), lambda b, h: (h, 0, 0))
    log_lb, log1m_lb, one_m_lb = (p.reshape(H, 1, HEAD_DIM) for p in (log_lb, log1m_lb, one_m_lb))
    return pl.pallas_call(
        functools.partial(_hgrn_kernel, bs=HGRN_BLOCK, chunk=HGRN_CHUNK, span=min(HGRN_SPAN, T)),
        name="hgrn",
        grid=(B, H),
        in_specs=[row(0), row(1), row(2), row(3), par, par, par,
                  pl.BlockSpec((1, HEAD_DIM), lambda b, h: (0, 0))],
        out_specs=pl.BlockSpec((1, T, HEAD_DIM), lambda b, h: (b, 0, h)),
        out_shape=jax.ShapeDtypeStruct((B, T, H * HEAD_DIM), BF16),
        compiler_params=_params("parallel", "parallel"),
    )(proj, proj, proj, proj, log_lb, log1m_lb, one_m_lb, o_gain.reshape(1, HEAD_DIM))


def _compress_kernel(t_ref, pe_ref, w1_ref, w2_ref, g_ref, o_ref, *, normalize):
    N = t_ref.shape[1] // CMP_STRIDE
    top = bot = jnp.zeros((N, w1_ref.shape[2]), F32)
    for l in range(CMP_STRIDE):
        xl = t_ref[0, pl.ds(l, N, stride=CMP_STRIDE), :]
        top = top + _dot((xl + pe_ref[l:l + 1, :]).astype(BF16), w1_ref[l])
        bot = bot + _dot((xl + pe_ref[CMP_STRIDE + l:CMP_STRIDE + l + 1, :]).astype(BF16),
                         w1_ref[CMP_STRIDE + l])
    hid = top + pltpu.roll(bot, N - 1, 0)
    out = _dot(jax.nn.gelu(hid).astype(BF16), w2_ref[...])
    if normalize:
        out = _rms(out, g_ref[...])
    valid = lax.broadcasted_iota(jnp.int32, out.shape, 0) < N - 1
    o_ref[0, 0] = jnp.where(valid, out, 0.0)


def compress(proj, col0, pe, w1, w2, layer, gain, normalize):
    B, T, _ = proj.shape
    G = NSA_KV_HEADS
    N = T // CMP_STRIDE
    Hc = w1.shape[3]
    assert CMP_BLOCK == 2 * CMP_STRIDE
    return pl.pallas_call(
        functools.partial(_compress_kernel, normalize=normalize),
        name="nsa_compress",
        grid=(B, G),
        in_specs=[pl.BlockSpec((1, T, HEAD_DIM), lambda b, g: (b, 0, col0 + g)),
                  pl.BlockSpec((None, CMP_BLOCK, HEAD_DIM), lambda b, g: (layer, 0, 0)),
                  pl.BlockSpec((None, CMP_BLOCK, HEAD_DIM, Hc), lambda b, g: (layer, 0, 0, 0)),
                  pl.BlockSpec((None, Hc, HEAD_DIM), lambda b, g: (layer, 0, 0)),
                  pl.BlockSpec((1, HEAD_DIM), lambda b, g: (0, 0))],
        out_specs=pl.BlockSpec((1, 1, N, HEAD_DIM), lambda b, g: (b, g, 0, 0)),
        out_shape=jax.ShapeDtypeStruct((B, G, N, HEAD_DIM), F32),
        compiler_params=_params("parallel", "parallel"),
    )(proj, pe, w1, w2, gain.reshape(1, HEAD_DIM))


_SEL_LANE0 = 64
_CMP_OV_ROW0 = VT_ROWS


def _nsa_kernel(q_ref, ks_ref, vs_ref, kw_ref, vw_ref, kc_ref, vc_ref, gt_ref, qg_ref, kg_ref,
                o_ref, ksa_ref, kwa_ref, vst_ref, vwt_ref, kca_ref, vcov_ref, tiles_ref, *, tq, tk, n_sel):
    g = pl.program_id(1)
    qi = pl.program_id(2)
    T = ks_ref.shape[1]
    J = NSA_GROUP
    n_slc = T // SLC_BLOCK
    nc = kc_ref.shape[2]
    n_cmp = nc - 1
    cols = J * tq
    eye = _eye(HEAD_DIM)

    @pl.when(qi == 0)
    def _():
        row = lax.broadcasted_iota(jnp.int32, (T, LANES), 0)
        lane = lax.broadcasted_iota(jnp.int32, (T, LANES), 1)
        blk = row // SLC_BLOCK
        feats = jnp.where(lane < 3, blk,
                          jnp.where(lane < 6, row % SLC_BLOCK,
                                    jnp.where(lane < 9, 1,
                                              jnp.where(lane - _SEL_LANE0 == blk, 1, 0))))
        feats = feats.astype(F32).astype(BF16)
        ksa_ref[:, :HEAD_DIM] = _rms(ks_ref[0], kg_ref[1:2, :]).astype(BF16)
        ksa_ref[:, HEAD_DIM:] = feats
        kwa_ref[:, :HEAD_DIM] = _rms(kw_ref[0], kg_ref[2:3, :]).astype(BF16)
        kwa_ref[:, HEAD_DIM:] = feats
        vst_ref[:HEAD_DIM, :] = _dot_nt(eye, vs_ref[0].astype(BF16)).astype(BF16)
        vst_ref[HEAD_DIM:, :] = _ones_row(VT_ROWS - HEAD_DIM, T)
        vwt_ref[:HEAD_DIM, :] = _dot_nt(eye, vw_ref[0].astype(BF16)).astype(BF16)
        vwt_ref[HEAD_DIM:, :] = _ones_row(VT_ROWS - HEAD_DIM, T)
        n_r = lax.broadcasted_iota(jnp.int32, (nc, LANES), 0)
        n_l = lax.broadcasted_iota(jnp.int32, (nc, LANES), 1)
        per = SLC_BLOCK // CMP_STRIDE
        cfe = jnp.where(n_l < 3, (n_r // per).astype(F32),
                        jnp.where(n_l < 6, (CMP_STRIDE * (n_r % per)).astype(F32) + 0.5 * (CMP_BLOCK - 1),
                                  jnp.where(n_l < 9, 1.0, 0.0)))
        kca_ref[:, :HEAD_DIM] = kc_ref[0, 0].astype(BF16)
        kca_ref[:, HEAD_DIM:] = cfe.astype(BF16)
        vcov_ref[:HEAD_DIM, :] = _dot_nt(eye, vc_ref[0, 0].astype(BF16)).astype(BF16)
        vcov_ref[HEAD_DIM:_CMP_OV_ROW0, :] = _ones_row(VT_ROWS - HEAD_DIM, nc)
        m_r = lax.broadcasted_iota(jnp.int32, (SLC_BLOCK, nc), 0)
        n_c = lax.broadcasted_iota(jnp.int32, (SLC_BLOCK, nc), 1)
        vcov_ref[_CMP_OV_ROW0:, :] = ((n_c * CMP_STRIDE <= m_r * SLC_BLOCK + (SLC_BLOCK - 1))
                                      & (n_c * CMP_STRIDE + (CMP_BLOCK - 1) >= m_r * SLC_BLOCK)
                                      & (n_c < n_cmp) & (m_r < n_slc)).astype(BF16)

    q0 = pl.multiple_of(qi * tq, tq)
    qraw = q_ref[0]
    qt = jnp.concatenate(
        [_dot_nt(eye, (_rms(qraw[:, j * HEAD_DIM:(j + 1) * HEAD_DIM], qg_ref[...])
                       * (ATTN_SCALE * LOG2E)).astype(BF16)).astype(BF16)
         for j in range(J)], axis=1)

    col = lax.broadcasted_iota(jnp.int32, (1, cols), 1)
    tt = q0 + col % tq
    slope2 = jnp.exp2(-0.5 * (g * J + col // tq + 1).astype(F32)) * LOG2E
    s1, s2, s3 = _split3_f32(slope2)
    r1, r2, r3 = _split3_f32(-slope2 * tt.astype(F32))
    r16 = lax.broadcasted_iota(jnp.int32, (16, cols), 0)
    feat = jnp.zeros((16, cols), F32)
    for i, v in enumerate((s1 * SLC_BLOCK, s2 * SLC_BLOCK, s3 * SLC_BLOCK, s1, s2, s3, r1, r2, r3)):
        feat = jnp.where(r16 == i, v, feat)
    feat = feat.astype(BF16)
    qa_base = jnp.concatenate([qt, feat, jnp.zeros((AUG - HEAD_DIM - 16, cols), BF16)], axis=0)

    n_col = lax.broadcasted_iota(jnp.int32, (nc, 1), 0)
    cmp_end = n_col * CMP_STRIDE + (CMP_BLOCK - 1)
    kca = kca_ref[...]
    r_hi, r_lo = [], []
    for c in _col_groups(cols):
        s = _dot(kca, qa_base[:, c])
        s = jnp.where((cmp_end <= tt[:, c]) & (n_col < n_cmp), s, NEG_INF)
        m = jnp.max(s, axis=0, keepdims=True)
        p = jnp.exp2(s - jnp.where(m == NEG_INF, 0.0, m))
        p_hi = p.astype(BF16)
        r_hi.append(_dot(vcov_ref[...], p_hi))
        r_lo.append(_dot(vcov_ref[HEAD_DIM:, :], (p - p_hi.astype(F32)).astype(BF16)))
    r_hi = jnp.concatenate(r_hi, axis=1)
    r_lo = jnp.concatenate(r_lo, axis=1)
    z = r_hi[HEAD_DIM:HEAD_DIM + 1] + r_lo[0:1]
    inv_z = 1.0 / jnp.where(z > 0, z, 1.0)

    wq = min(tq, WINDOW_Q)
    span = WINDOW + wq
    d = (lax.broadcasted_iota(jnp.int32, (span, 1), 0)
         - lax.broadcasted_iota(jnp.int32, (1, wq), 1))
    parts = []
    for c in range(tq // wq):
        k0w = pl.multiple_of(jnp.maximum(q0 + c * wq - WINDOW, 0), wq)
        off = q0 + c * wq - k0w
        bias = jnp.where((d <= off) & (d > off - WINDOW), 0.0, NEG_INF)
        bias = jnp.concatenate([bias] * J, axis=1)
        qa_c = jnp.concatenate([qa_base[:, j * tq + c * wq:j * tq + (c + 1) * wq] for j in range(J)], axis=1)
        parts.append(_attn_finish(_attn_tile(
            kwa_ref[pl.ds(k0w, span), :], vwt_ref[:, pl.ds(k0w, span)], qa_c, _attn_init(J * wq),
            lambda s, sub, bias=bias: s + bias[:, sub])))
    acc_w = jnp.concatenate([p[:, j * wq:(j + 1) * wq] for j in range(J) for p in parts], axis=1)

    imp4 = (r_hi[_CMP_OV_ROW0:] + r_lo[_CMP_OV_ROW0 - HEAD_DIM:]) * inv_z
    imp = imp4[:, 0:tq]
    for j in range(1, J):
        imp = imp + imp4[:, j * tq:(j + 1) * tq]
    blk = lax.broadcasted_iota(jnp.int32, (SLC_BLOCK, tq), 0)
    t_q = q0 + lax.broadcasted_iota(jnp.int32, (1, tq), 1)
    imp = jnp.where(blk * SLC_BLOCK > t_q, NEG_INF, imp)
    imp = jnp.where((blk == t_q // SLC_BLOCK) | (blk == 0), jnp.inf, imp)
    blk_f = blk.astype(F32)
    taken = blk >= n_slc
    for _ in range(n_sel):
        vals = jnp.where(taken, NEG_INF, imp)
        best = jnp.max(vals, axis=0, keepdims=True)
        cand = (vals == best) & jnp.logical_not(taken)
        first = jnp.min(jnp.where(cand, blk_f, float(SLC_BLOCK)), axis=0, keepdims=True)
        taken = taken | (blk_f == first)
    sel = taken & (blk < n_slc)
    not_sel = jnp.where(sel, 0.0, -MASK_BIG).astype(BF16)
    qa_slc = jnp.concatenate([qt, feat, jnp.zeros((_SEL_LANE0 - 16, cols), BF16)]
                             + [jnp.concatenate([not_sel] * J, axis=1)], axis=0)

    def slc_tile(jt, carry, score_fn=None):
        k0 = pl.multiple_of(jt * tk, tk)
        return _attn_tile(ksa_ref[pl.ds(k0, tk), :], vst_ref[:, pl.ds(k0, tk)], qa_slc, carry, score_fn)

    jd = q0 // tk
    kpos = jd * tk + lax.broadcasted_iota(jnp.int32, (tk, 1), 0)
    carry = slc_tile(jd, _attn_init(cols), lambda s, c: jnp.where(kpos <= tt[:, c], s, NEG_INF))
    sel_f = jnp.where(sel, 1.0, 0.0)
    per_tile = tk // SLC_BLOCK
    n_live = jnp.int32(0)
    for jt in range(T // tk - 1):
        live = (jnp.max(sel_f[jt * per_tile:(jt + 1) * per_tile, :]) > 0.5) & (jt < jd)
        tiles_ref[n_live] = jt
        n_live = n_live + live.astype(jnp.int32)
    acc_s = _attn_finish(lax.fori_loop(0, n_live, lambda i, c: slc_tile(tiles_ref[i], c), carry))

    gates_t = jnp.transpose(jax.nn.sigmoid(gt_ref[0]))
    gate = lambda c: jnp.concatenate([gates_t[3 * j + c:3 * j + c + 1, :] for j in range(J)], axis=1)
    o_t = ((gate(0) * inv_z) * r_hi[:HEAD_DIM]
           + (gate(1) / acc_s[HEAD_DIM:HEAD_DIM + 1]) * acc_s[:HEAD_DIM]
           + (gate(2) / acc_w[HEAD_DIM:HEAD_DIM + 1]) * acc_w[:HEAD_DIM]).astype(BF16)
    for j in range(J):
        for c in range(tq // LANES):
            o_ref[0, c * LANES:(c + 1) * LANES, j * HEAD_DIM:(j + 1) * HEAD_DIM] = _dot_nt(
                eye, o_t[:, j * tq + c * LANES:j * tq + (c + 1) * LANES]).astype(o_ref.dtype)


def nsa_attn(proj, gates, k_cmp, v_cmp, q_gain, k_gain, tq, tk):
    B, T, _ = proj.shape
    G, J = NSA_KV_HEADS, NSA_GROUP
    tk = min(tk, T)
    nc = k_cmp.shape[2]
    assert T % tk == 0 and tk % tq == 0 and T >= WINDOW + tq
    assert T // SLC_BLOCK <= LANES - _SEL_LANE0 and SLC_BLOCK <= LANES - _SEL_LANE0
    qcols = NSA_HEADS
    kv = lambda slab: pl.BlockSpec((1, T, HEAD_DIM), lambda b, g, i: (b, 0, qcols + slab * G + g))
    cmp_spec = pl.BlockSpec((1, 1, nc, HEAD_DIM), lambda b, g, i: (b, g, 0, 0))
    return pl.pallas_call(
        functools.partial(_nsa_kernel, tq=tq, tk=tk, n_sel=min(SLC_TOPK, T // SLC_BLOCK)),
        name="nsa_attn",
        grid=(B, G, T // tq),
        in_specs=[pl.BlockSpec((1, tq, J * HEAD_DIM), lambda b, g, i: (b, i, g)),
                  kv(2), kv(3), kv(4), kv(5), cmp_spec, cmp_spec,
                  pl.BlockSpec((1, tq, LANES), lambda b, g, i: (b, i, g)),
                  pl.BlockSpec((1, HEAD_DIM), lambda b, g, i: (0, 0)),
                  pl.BlockSpec((3, HEAD_DIM), lambda b, g, i: (0, 0))],
        out_specs=pl.BlockSpec((1, tq, J * HEAD_DIM), lambda b, g, i: (b, i, g)),
        out_shape=jax.ShapeDtypeStruct((B, T, NSA_HEADS * HEAD_DIM), BF16),
        scratch_shapes=[pltpu.VMEM((T, AUG), BF16), pltpu.VMEM((T, AUG), BF16),
                        pltpu.VMEM((VT_ROWS, T), BF16), pltpu.VMEM((VT_ROWS, T), BF16),
                        pltpu.VMEM((nc, AUG), BF16),
                        pltpu.VMEM((VT_ROWS + SLC_BLOCK, nc), BF16),
                        pltpu.SMEM((T // tk,), jnp.int32)],
        compiler_params=_params("parallel", "parallel", "arbitrary"),
    )(proj, proj, proj, proj, proj, k_cmp, v_cmp, gates, q_gain.reshape(1, HEAD_DIM), k_gain)


def even_weights(w_in, w_out):
    fw = FOX_HEADS * HEAD_DIM
    w_main = jnp.concatenate([w_in[:, :, :3 * fw], w_in[:, :, 3 * fw + FOX_HEADS:]], axis=2).astype(BF16)
    w_gate = jnp.pad(w_in[:, :, 3 * fw:3 * fw + FOX_HEADS], ((0, 0), (0, 0), (0, LANES - FOX_HEADS)))
    return w_main, w_gate.astype(BF16), w_out.astype(BF16)


def even_mixer(x, gain, weights, e, f_bias, q_gain, k_gain, lb, o_gain):
    B, T, D = x.shape
    xf = x.reshape(B * T, D)
    w_main, w_gate, w_out = weights
    proj, gate = rms_matmul(xf, gain, w_main, e, 1024, 1024, side_w=w_gate)
    proj, gate = proj.reshape(B, T, -1), gate.reshape(B, T, LANES)
    f_bias = jnp.pad(f_bias.reshape(1, FOX_HEADS).astype(F32), ((0, 0), (0, LANES - FOX_HEADS)))
    c, kfeat = fox_gate(gate, f_bias, FOX_HEADS)
    c_rows = c[:, :, :FOX_HEADS].transpose(0, 2, 1).reshape(B * FOX_HEADS, 1, T)
    o_fox = fox_attn(proj, kfeat, c_rows, q_gain, k_gain, 1024)
    lb = lb.astype(F32).reshape(HGRN_HEADS, HEAD_DIM)
    o_h = hgrn(proj, 3 * FOX_HEADS, jnp.log(lb), jnp.log1p(-lb), 1.0 - lb, o_gain)
    out = out_proj([o_fox.reshape(B * T, -1), o_h.reshape(B * T, -1)], w_out, e, xf, 256)
    return out.reshape(B, T, D)


def nsa_weights(w_in, w_out, w1_k, w2_k, w1_v, w2_v):
    L, D, _ = w_in.shape
    G, J = NSA_KV_HEADS, NSA_GROUP
    n_main = (NSA_HEADS + 6 * G) * HEAD_DIM
    w_gate = w_in[:, :, n_main:].reshape(L, D, G, 3 * J)
    w_gate = jnp.pad(w_gate, ((0, 0), (0, 0), (0, 0), (0, LANES - 3 * J))).reshape(L, D, G * LANES)
    return (jnp.swapaxes(w_in[:, :, :n_main], 1, 2).astype(BF16), w_gate.astype(BF16), w_out.astype(BF16),
            w1_k.astype(BF16), w2_k.astype(BF16), w1_v.astype(BF16), w2_v.astype(BF16))


def nsa_mixer(x, gain, weights, o, q_gain, k_gain, pe_k, pe_v):
    B, T, D = x.shape
    G = NSA_KV_HEADS
    xf = x.reshape(B * T, D)
    w_main, w_gate, w_out, w1_k, w2_k, w1_v, w2_v = weights
    proj, gates = rms_matmul(xf, gain, w_main, o, 1024, 1024, side_w=w_gate, w_transposed=True)
    proj, gates = proj.reshape(B, T, -1), gates.reshape(B, T, G * LANES)
    k_cmp = compress(proj, NSA_HEADS, pe_k, w1_k, w2_k, o, k_gain[0], True)
    v_cmp = compress(proj, NSA_HEADS + G, pe_v, w1_v, w2_v, o, k_gain[0], False)
    att = nsa_attn(proj, gates, k_cmp, v_cmp, q_gain, k_gain, 512, 512)
    out = out_proj([att.reshape(B * T, -1)], w_out, o, xf, 256)
    return out.reshape(B, T, D)


def kernel(x, mem, norm_mix, norm_mem, norm_ffn, mem_in_gain, even_w_in, even_w_out, fox_f_bias,
           fox_q_gain, fox_k_gain, hgrn_lb_logits, hgrn_o_gain, odd_w_in, odd_w_out, nsa_q_gain,
           nsa_k_gain, cmp_pe_k, cmp_w1_k, cmp_w2_k, cmp_pe_v, cmp_w1_v, cmp_w2_v, mem_wq, mem_wkv,
           mem_wo, mem_q_gain, mem_k_gain, ffn_w1, ffn_w3, ffn_w2):
    B, T, D = x.shape
    depth = norm_mix.shape[0]
    L = mem.shape[1]
    lb_cum = jnp.cumsum(jax.nn.softmax(hgrn_lb_logits.astype(F32), axis=0), axis=0)
    hgrn_lb = lb_cum - lb_cum[0:1]
    memf = mem.reshape(B * L, D)
    even_w = even_weights(even_w_in, even_w_out)
    nsa_w = nsa_weights(odd_w_in, odd_w_out, cmp_w1_k, cmp_w2_k, cmp_w1_v, cmp_w2_v)
    mem_wq, mem_wkv, mem_wo = (w.astype(BF16) for w in (mem_wq, mem_wkv, mem_wo))
    for layer in range(depth):
        if layer % 2 == 0:
            e = layer // 2
            x = even_mixer(x, norm_mix[layer], even_w, e, fox_f_bias[e], fox_q_gain[e], fox_k_gain[e],
                           hgrn_lb[e], hgrn_o_gain[e])
        else:
            o = layer // 2
            x = nsa_mixer(x, norm_mix[layer], nsa_w, o, nsa_q_gain[o], nsa_k_gain[o], cmp_pe_k, cmp_pe_v)
        kv = rms_matmul(memf, mem_in_gain[layer], mem_wkv, layer, 256, 1024)
        x = mem_attn(x, norm_mem[layer], mem_wq, kv.reshape(B, L, -1), mem_q_gain[layer],
                     mem_k_gain[layer], mem_wo, layer, 512)
        x = ffn(x.reshape(B * T, D), norm_ffn[layer], ffn_w1, ffn_w3, ffn_w2, layer,
                1024, 256).reshape(B, T, D)
    return x
```

```python
import functools

import jax
import jax.numpy as jnp
from jax import lax
from jax.experimental import pallas as pl
from jax.experimental.pallas import tpu as pltpu

F32 = jnp.float32
BF16 = jnp.bfloat16
NEG_INF = float("-inf")

EPS = 1e-6
HEAD_DIM = 128
ATTN_SCALE = HEAD_DIM ** -0.5
LANES = 128
FOX_HEADS = 8
HGRN_HEADS = 8
HGRN_BLOCK = 16
HGRN_CHUNK = 64
HGRN_SPAN = 512
HGRN_SAFE_LOG_DECAY = -80.0
NSA_HEADS = 16
NSA_KV_HEADS = 4
NSA_GROUP = NSA_HEADS // NSA_KV_HEADS
CMP_BLOCK = 32
CMP_STRIDE = 16
SLC_BLOCK = 64
SLC_TOPK = 16
WINDOW = 512
WINDOW_Q = 256
MEM_HEADS = 4
V7X_VMEM_LIMIT = 56 * 1024 * 1024
LOG2E = 1.4426950408889634
MASK_BIG = 1e30
AUG = 2 * LANES
VT_ROWS = HEAD_DIM + 16


def _params(*sem):
    return pltpu.CompilerParams(dimension_semantics=sem, vmem_limit_bytes=V7X_VMEM_LIMIT)


def _rms(x, gain):
    return x * lax.rsqrt(jnp.mean(x * x, axis=-1, keepdims=True) + EPS) * gain


def _dot(a, b):
    return jnp.dot(a, b, preferred_element_type=F32)


def _dot_nt(a, b):
    return lax.dot_general(a, b, (((1,), (1,)), ((), ())), preferred_element_type=F32)


def _dot_tn(a, b):
    return lax.dot_general(a, b, (((0,), (0,)), ((), ())), preferred_element_type=F32)


def _split3(x):
    x1 = x.astype(BF16)
    r1 = x - x1.astype(F32)
    x2 = r1.astype(BF16)
    return x1, x2, (r1 - x2.astype(F32)).astype(BF16)


def _exact_dot(sel, x):
    x1, x2, x3 = _split3(x)
    return _dot(sel, x1) + _dot(sel, x2) + _dot(sel, x3)


def _log_sigmoid(z):
    return jnp.minimum(z, 0.0) - jnp.log1p(jnp.exp(-jnp.abs(z)))


def _rms_matmul_kernel(x_ref, g_ref, w_ref, *rest, w_transposed):
    side_w_ref, o_ref, side_o_ref, hn_ref = rest if len(rest) == 4 else (None, rest[0], None, rest[1])

    @pl.when(pl.program_id(1) == 0)
    def _():
        hn_ref[...] = _rms(x_ref[...], g_ref[...]).astype(BF16)
        if side_w_ref is not None:
            side_o_ref[...] = _dot(hn_ref[...], side_w_ref[...])

    o_ref[...] = (_dot_nt if w_transposed else _dot)(hn_ref[...], w_ref[...])


def rms_matmul(x, gain, w, layer, tm, tn, side_w=None, w_transposed=False):
    M, K = x.shape
    N = w.shape[1] if w_transposed else w.shape[2]
    tm, tn = min(tm, M), min(tn, N)
    assert M % tm == 0 and N % tn == 0
    w_spec = (pl.BlockSpec((None, tn, K), lambda i, j: (layer, j, 0)) if w_transposed
              else pl.BlockSpec((None, K, tn), lambda i, j: (layer, 0, j)))
    in_specs = [pl.BlockSpec((tm, K), lambda i, j: (i, 0)),
                pl.BlockSpec((1, K), lambda i, j: (0, 0)),
                w_spec]
    out_specs = pl.BlockSpec((tm, tn), lambda i, j: (i, j))
    out_shape = jax.ShapeDtypeStruct((M, N), F32)
    args = (x, gain.reshape(1, K), w)
    if side_w is not None:
        ns = side_w.shape[2]
        in_specs.append(pl.BlockSpec((None, K, ns), lambda i, j: (layer, 0, 0)))
        out_specs = [out_specs, pl.BlockSpec((tm, ns), lambda i, j: (i, 0))]
        out_shape = [out_shape, jax.ShapeDtypeStruct((M, ns), F32)]
        args += (side_w,)
    return pl.pallas_call(
        functools.partial(_rms_matmul_kernel, w_transposed=w_transposed),
        name="rms_matmul",
        grid=(M // tm, N // tn),
        in_specs=in_specs,
        out_specs=out_specs,
        out_shape=out_shape,
        scratch_shapes=[pltpu.VMEM((tm, K), BF16)],
        compiler_params=_params("parallel", "arbitrary"),
    )(*args)


def _out_proj_kernel(*refs, n_in):
    a_refs, w_refs = refs[:n_in], refs[n_in:2 * n_in]
    res_ref, o_ref = refs[2 * n_in], refs[2 * n_in + 1]
    acc = res_ref[...]
    for a_ref, w_ref in zip(a_refs, w_refs):
        acc = acc + _dot(a_ref[...], w_ref[...])
    o_ref[...] = acc


def out_proj(acts, w, layer, res, tm):
    M, N = res.shape
    tm = min(tm, M)
    assert M % tm == 0
    n_in = len(acts)
    ka = acts[0].shape[1]
    assert all(a.shape[1] == ka for a in acts) and n_in * ka == w.shape[1]
    row_block = lambda r: pl.BlockSpec((None, ka, N), lambda i: (layer, r, 0))
    weights = [w] * n_in
    in_specs = ([pl.BlockSpec((tm, ka), lambda i: (i, 0)) for _ in acts]
                + [row_block(r) for r in range(n_in)]
                + [pl.BlockSpec((tm, N), lambda i: (i, 0))])
    return pl.pallas_call(
        functools.partial(_out_proj_kernel, n_in=n_in),
        name="out_proj",
        grid=(M // tm,),
        in_specs=in_specs,
        out_specs=pl.BlockSpec((tm, N), lambda i: (i, 0)),
        out_shape=jax.ShapeDtypeStruct((M, N), F32),
        compiler_params=_params("parallel"),
    )(*acts, *weights, res)


def _ffn_kernel(x_ref, g_ref, w1_ref, w3_ref, w2_ref, o_ref, hn_ref):
    f = pl.program_id(1)

    @pl.when(f == 0)
    def _():
        x = x_ref[...]
        hn_ref[...] = _rms(x, g_ref[...]).astype(BF16)
        o_ref[...] = x

    h = hn_ref[...]
    a = _dot(h, w1_ref[...].astype(BF16))
    b = _dot(h, w3_ref[...].astype(BF16))
    u = (a * jax.nn.sigmoid(a) * b).astype(BF16)
    o_ref[...] += _dot(u, w2_ref[...].astype(BF16))


def ffn(x, gain, w1, w3, w2, layer, tm, tf):
    M, D = x.shape
    Fh = w1.shape[2]
    tm, tf = min(tm, M), min(tf, Fh)
    assert M % tm == 0 and Fh % tf == 0
    return pl.pallas_call(
        _ffn_kernel,
        name="ffn",
        grid=(M // tm, Fh // tf),
        in_specs=[pl.BlockSpec((tm, D), lambda i, f: (i, 0)),
                  pl.BlockSpec((1, D), lambda i, f: (0, 0)),
                  pl.BlockSpec((None, D, tf), lambda i, f: (layer, 0, f)),
                  pl.BlockSpec((None, D, tf), lambda i, f: (layer, 0, f)),
                  pl.BlockSpec((None, tf, D), lambda i, f: (layer, f, 0))],
        out_specs=pl.BlockSpec((tm, D), lambda i, f: (i, 0)),
        out_shape=jax.ShapeDtypeStruct((M, D), F32),
        scratch_shapes=[pltpu.VMEM((tm, D), BF16)],
        compiler_params=_params("parallel", "arbitrary"),
    )(x, gain.reshape(1, D), w1, w3, w2)


def _mem_attn_kernel(x_ref, g_ref, wq_ref, kv_ref, qg_ref, kg_ref, wo_ref, o_ref, *, heads):
    x = x_ref[0]
    h = _rms(x, g_ref[...]).astype(BF16)
    q = _dot(h, wq_ref[...])
    kv = kv_ref[0]
    w = heads * HEAD_DIM
    outs = []
    for hh in range(heads):
        sl = slice(hh * HEAD_DIM, (hh + 1) * HEAD_DIM)
        qh = (_rms(q[:, sl], qg_ref[...]) * ATTN_SCALE).astype(BF16)
        kh = _rms(kv[:, sl], kg_ref[...]).astype(BF16)
        vh = kv[:, w + hh * HEAD_DIM: w + (hh + 1) * HEAD_DIM].astype(BF16)
        s = _dot_nt(qh, kh)
        p = jnp.exp(s - jnp.max(s, axis=-1, keepdims=True))
        z = jnp.sum(p, axis=-1, keepdims=True)
        outs.append((_dot(p.astype(BF16), vh) / z).astype(BF16))
    o = jnp.concatenate(outs, axis=-1)
    o_ref[0] = x + _dot(o, wo_ref[...])


def mem_attn(x, gain, wq, kv, q_gain, k_gain, wo, layer, tm):
    B, T, D = x.shape
    L, W2 = kv.shape[1], kv.shape[2]
    W = W2 // 2
    tm = min(tm, T)
    assert T % tm == 0
    return pl.pallas_call(
        functools.partial(_mem_attn_kernel, heads=W // HEAD_DIM),
        name="mem_attn",
        grid=(B, T // tm),
        in_specs=[pl.BlockSpec((1, tm, D), lambda b, i: (b, i, 0)),
                  pl.BlockSpec((1, D), lambda b, i: (0, 0)),
                  pl.BlockSpec((None, D, W), lambda b, i: (layer, 0, 0)),
                  pl.BlockSpec((1, L, W2), lambda b, i: (b, 0, 0)),
                  pl.BlockSpec((1, HEAD_DIM), lambda b, i: (0, 0)),
                  pl.BlockSpec((1, HEAD_DIM), lambda b, i: (0, 0)),
                  pl.BlockSpec((None, W, D), lambda b, i: (layer, 0, 0))],
        out_specs=pl.BlockSpec((1, tm, D), lambda b, i: (b, i, 0)),
        out_shape=jax.ShapeDtypeStruct((B, T, D), F32),
        compiler_params=_params("parallel", "parallel"),
    )(x, gain.reshape(1, D), wq, kv, q_gain.reshape(1, HEAD_DIM), k_gain.reshape(1, HEAD_DIM), wo)


def _eye(n):
    return (lax.broadcasted_iota(jnp.int32, (n, n), 0)
            == lax.broadcasted_iota(jnp.int32, (n, n), 1)).astype(BF16)


def _ones_row(rows, n):
    return (lax.broadcasted_iota(jnp.int32, (rows, n), 0) == 0).astype(BF16)


def _split3_f32(x):
    return tuple(t.astype(F32) for t in _split3(x))


COL_GROUPS = 1


def _col_groups(n):
    w = n // COL_GROUPS
    return [slice(i * w, (i + 1) * w) for i in range(COL_GROUPS)]


def _attn_init(n):
    return tuple((jnp.full((1, c.stop - c.start), NEG_INF, F32), jnp.zeros((VT_ROWS, c.stop - c.start), F32))
                 for c in _col_groups(n))


def _attn_tile(k_tile, vt_tile, qa, carry, score_fn=None):
    out = []
    for (m, acc), c in zip(carry, _col_groups(qa.shape[1])):
        s = _dot(k_tile, qa[:, c])
        if score_fn is not None:
            s = score_fn(s, c)
        m_new = jnp.maximum(m, jnp.max(s, axis=0, keepdims=True))
        p = jnp.exp2(s - m_new).astype(BF16)
        out.append((m_new, jnp.exp2(m - m_new) * acc + _dot(vt_tile, p)))
    return tuple(out)


def _attn_finish(carry):
    return jnp.concatenate([acc for _, acc in carry], axis=1)


def _fox_gate_kernel(g_ref, b_ref, c_ref, kf_ref, *, heads):
    T = g_ref.shape[1]
    r = lax.broadcasted_iota(jnp.int32, (LANES, LANES), 0)
    lane = lax.broadcasted_iota(jnp.int32, (LANES, LANES), 1)
    tri = (r >= lane).astype(BF16)

    def body(i, carry):
        rows = pl.ds(pl.multiple_of(i * LANES, LANES), LANES)
        cs = _exact_dot(tri, _log_sigmoid(g_ref[0, rows, :] + b_ref[...])) + carry
        c_ref[0, rows, :] = cs
        c2 = cs * LOG2E
        for h in range(heads):
            d1, d2, d3 = _split3_f32(c2[:, h:h + 1])
            feat = jnp.where(lane < 3, 1.0,
                             jnp.where(lane == 3, -d1, jnp.where(lane == 4, -d2,
                                                                 jnp.where(lane == 5, -d3, 0.0))))
            kf_ref[0, h, rows, :] = feat.astype(BF16)
        return cs[LANES - 1:LANES, :]

    lax.fori_loop(0, T // LANES, body, jnp.zeros((1, LANES), F32))


def fox_gate(g, bias, heads):
    B, T, _ = g.shape
    return pl.pallas_call(
        functools.partial(_fox_gate_kernel, heads=heads),
        name="fox_gate",
        grid=(B,),
        in_specs=[pl.BlockSpec((1, T, LANES), lambda b: (b, 0, 0)),
                  pl.BlockSpec((1, LANES), lambda b: (0, 0))],
        out_specs=[pl.BlockSpec((1, T, LANES), lambda b: (b, 0, 0)),
                   pl.BlockSpec((1, heads, T, LANES), lambda b: (b, 0, 0, 0))],
        out_shape=[jax.ShapeDtypeStruct((B, T, LANES), F32),
                   jax.ShapeDtypeStruct((B, heads, T, LANES), BF16)],
        compiler_params=_params("parallel"),
    )(g, bias)


def _fox_attn_kernel(q_ref, k_ref, v_ref, kf_ref, c_ref, qg_ref, kg_ref, o_ref, ka_ref, vt_ref, *, tq):
    qi = pl.program_id(2)
    T = k_ref.shape[1]
    eye = _eye(HEAD_DIM)

    @pl.when(qi == 0)
    def _():
        ka_ref[:, :HEAD_DIM] = _rms(k_ref[0], kg_ref[...]).astype(BF16)
        ka_ref[:, HEAD_DIM:] = kf_ref[0, 0]
        vt_ref[:HEAD_DIM, :] = _dot_nt(eye, v_ref[0].astype(BF16)).astype(BF16)
        vt_ref[HEAD_DIM:, :] = _ones_row(VT_ROWS - HEAD_DIM, T)

    q0 = pl.multiple_of(qi * tq, tq)
    qn = (_rms(q_ref[0], qg_ref[...]) * (ATTN_SCALE * LOG2E)).astype(BF16)
    qt = _dot_nt(eye, qn).astype(BF16)
    c1, c2, c3 = _split3_f32(c_ref[0, :, pl.ds(q0, tq)] * LOG2E)
    r = lax.broadcasted_iota(jnp.int32, (16, tq), 0)
    feat = jnp.where(r == 0, c1, jnp.where(r == 1, c2, jnp.where(r == 2, c3,
                                                                 jnp.where(r < 6, 1.0, 0.0))))
    qa = jnp.concatenate([qt, feat.astype(BF16), jnp.zeros((AUG - HEAD_DIM - 16, tq), BF16)], axis=0)

    kpos = lax.broadcasted_iota(jnp.int32, (tq, 1), 0)
    tpos = lax.broadcasted_iota(jnp.int32, (1, tq), 1)

    def tile(j, carry, score_fn=None):
        k0 = pl.multiple_of(j * tq, tq)
        return _attn_tile(ka_ref[pl.ds(k0, tq), :], vt_ref[:, pl.ds(k0, tq)], qa, carry, score_fn)

    carry = tile(qi, _attn_init(tq), lambda s, c: jnp.where(kpos <= tpos[:, c], s, NEG_INF))
    acc = _attn_finish(lax.fori_loop(0, qi, tile, carry))
    ot = (acc[:HEAD_DIM] / acc[HEAD_DIM:HEAD_DIM + 1]).astype(BF16)
    for c in range(tq // LANES):
        cs = slice(c * LANES, (c + 1) * LANES)
        o_ref[0, cs, :] = _dot_nt(eye, ot[:, cs]).astype(o_ref.dtype)


def fox_attn(proj, kfeat, c_rows, q_gain, k_gain, tq):
    B, T, _ = proj.shape
    H = FOX_HEADS
    tq = min(tq, T)
    assert T % tq == 0 and tq % LANES == 0
    return pl.pallas_call(
        functools.partial(_fox_attn_kernel, tq=tq),
        name="fox_attn",
        grid=(B, H, T // tq),
        in_specs=[pl.BlockSpec((1, tq, HEAD_DIM), lambda b, h, i: (b, i, h)),
                  pl.BlockSpec((1, T, HEAD_DIM), lambda b, h, i: (b, 0, H + h)),
                  pl.BlockSpec((1, T, HEAD_DIM), lambda b, h, i: (b, 0, 2 * H + h)),
                  pl.BlockSpec((1, 1, T, LANES), lambda b, h, i: (b, h, 0, 0)),
                  pl.BlockSpec((1, 1, T), lambda b, h, i: (b * H + h, 0, 0)),
                  pl.BlockSpec((1, HEAD_DIM), lambda b, h, i: (0, 0)),
                  pl.BlockSpec((1, HEAD_DIM), lambda b, h, i: (0, 0))],
        out_specs=pl.BlockSpec((1, tq, HEAD_DIM), lambda b, h, i: (b, i, h)),
        out_shape=jax.ShapeDtypeStruct((B, T, H * HEAD_DIM), BF16),
        scratch_shapes=[pltpu.VMEM((T, AUG), BF16), pltpu.VMEM((VT_ROWS, T), BF16)],
        compiler_params=_params("parallel", "parallel", "arbitrary"),
    )(proj, proj, proj, kfeat, c_rows, q_gain.reshape(1, HEAD_DIM), k_gain.reshape(1, HEAD_DIM))


def _hgrn_kernel(q_ref, f_ref, i_ref, g_ref, la_ref, l1_ref, oml_ref, og_ref, o_ref, *, bs, chunk, span):
    T = q_ref.shape[1]
    R = LANES
    ri = lax.broadcasted_iota(jnp.int32, (R, R), 0)
    ci = lax.broadcasted_iota(jnp.int32, (R, R), 1)

    def prefix_sel(n):
        return (((ri // n) == (ci // n)) & (ri >= ci)).astype(BF16)

    s_idx = lax.broadcasted_iota(jnp.int32, (bs, HEAD_DIM), 0)
    tril = (lax.broadcasted_iota(jnp.int32, (chunk, chunk), 0)
            >= lax.broadcasted_iota(jnp.int32, (chunk, chunk), 1))

    def gates(row0, n):
        rows = pl.ds(row0, n)
        z = f_ref[0, rows, :]
        a = la_ref[0]
        y = l1_ref[0] + _log_sigmoid(z)
        log_f = jnp.maximum(a, y) + jnp.log1p(jnp.exp(-jnp.abs(a - y)))
        hq = q_ref[0, rows, :]
        return log_f, oml_ref[0] * jax.nn.sigmoid(-z), hq * jax.nn.sigmoid(hq), i_ref[0, rows, :]

    def prefixes(log_f, n):
        sel = prefix_sel(n)
        pieces = _split3(log_f)
        b = jnp.concatenate([sum(_dot(sel, p[g * R:(g + 1) * R]) for p in pieces)
                             for g in range(log_f.shape[0] // R)], axis=0)
        bt = jnp.concatenate([jnp.broadcast_to(b[(u + 1) * n - 1:(u + 1) * n, :], (n, HEAD_DIM))
                              for u in range(log_f.shape[0] // n)], axis=0)
        return b, bt

    def finish(row0, n, o):
        rows = pl.ds(row0, n)
        hg = g_ref[0, rows, :]
        o_ref[0, rows, :] = (_rms(o, og_ref[...]) * (hg * jax.nn.sigmoid(hg))).astype(o_ref.dtype)

    def matmul_path(row0, st, kk, qq, vv, b, bt):
        qe = (qq * jnp.exp(b)).astype(BF16)
        k_inv = (kk * jnp.exp(-b)).astype(BF16)
        ke = (kk * jnp.exp(bt - b)).astype(BF16)
        dec = jnp.exp(bt)
        vb = vv.astype(BF16)
        sls = [slice(u * chunk, (u + 1) * chunk) for u in range(span // chunk)]
        scores = [jnp.where(tril, _dot_nt(qe[sl], k_inv[sl]), 0.0).astype(BF16) for sl in sls]
        intra = [_dot(a, vb[sl]) for a, sl in zip(scores, sls)]
        update = [_dot_tn(vb[sl], ke[sl]) for sl in sls]
        outs = []
        for sl, o_in, upd in zip(sls, intra, update):
            outs.append(_dot_nt(qe[sl], st.astype(BF16)) + o_in)
            st = dec[sl.start:sl.start + 1, :] * st + upd
        finish(row0, span, jnp.concatenate(outs, axis=0))
        return st

    def pairwise_step(i, st, base):
        row0 = pl.multiple_of(base + i * R, R)
        log_f, kk, qq, vv = gates(row0, R)
        b, bt = prefixes(log_f, bs)
        qe = (qq * jnp.exp(b)).astype(BF16)
        ke = (kk * jnp.exp(bt - b)).astype(BF16)
        dec = jnp.exp(bt)
        vb = vv.astype(BF16)
        outs = []
        for u in range(R // bs):
            sl = slice(u * bs, (u + 1) * bs)
            bu, vu, qk, ku = b[sl], vv[sl], qq[sl], kk[sl]
            intra = []
            for t in range(bs):
                w = jnp.exp(jnp.where(s_idx <= t, bu[t:t + 1, :] - bu, NEG_INF))
                a_t = jnp.sum(w * (qk[t:t + 1, :] * ku), axis=-1, keepdims=True)
                intra.append(jnp.sum(a_t * vu, axis=0, keepdims=True))
            outs.append(_dot_nt(qe[sl], st.astype(BF16)) + jnp.concatenate(intra, axis=0))
            st = dec[u * bs:u * bs + 1, :] * st + _dot_tn(vb[sl], ke[sl])
        finish(row0, R, jnp.concatenate(outs, axis=0))
        return st

    def step(i, st):
        row0 = pl.multiple_of(i * span, span)
        log_f, kk, qq, vv = gates(row0, span)
        b, bt = prefixes(log_f, chunk)
        return lax.cond(
            jnp.min(bt) >= HGRN_SAFE_LOG_DECAY,
            lambda st: matmul_path(row0, st, kk, qq, vv, b, bt),
            lambda st: lax.fori_loop(0, span // R, functools.partial(pairwise_step, base=row0), st),
            st)

    lax.fori_loop(0, T // span, step, jnp.zeros((HEAD_DIM, HEAD_DIM), F32))


def hgrn(proj, col0, log_lb, log1m_lb, one_m_lb, o_gain):
    B, T, _ = proj.shape
    H = HGRN_HEADS
    row = lambda off: pl.BlockSpec((1, T, HEAD_DIM), lambda b, h: (b, 0, col0 + off * H + h))
    par = pl.BlockSpec((1, 1, HEAD_DIM), lambda b, h: (h, 0, 0))
    log_lb, log1m_lb, one_m_lb = (p.reshape(H, 1, HEAD_DIM) for p in (log_lb, log1m_lb, one_m_lb))
    return pl.pallas_call(
        functools.partial(_hgrn_kernel, bs=HGRN_BLOCK, chunk=HGRN_CHUNK, span=min(HGRN_SPAN, T)),
        name="hgrn",
        grid=(B, H),
        in_specs=[row(0), row(1), row(2), row(3), par, par, par,
                  pl.BlockSpec((1, HEAD_DIM), lambda b, h: (0, 0))],
        out_specs=pl.BlockSpec((1, T, HEAD_DIM), lambda b, h: (b, 0, h)),
        out_shape=jax.ShapeDtypeStruct((B, T, H * HEAD_DIM), BF16),
        compiler_params=_params("parallel", "parallel"),
    )(proj, proj, proj, proj, log_lb, log1m_lb, one_m_lb, o_gain.reshape(1, HEAD_DIM))


def _compress_kernel(t_ref, pe_ref, w1_ref, w2_ref, g_ref, o_ref, *, normalize):
    N = t_ref.shape[1] // CMP_STRIDE
    top = bot = jnp.zeros((N, w1_ref.shape[2]), F32)
    for l in range(CMP_STRIDE):
        xl = t_ref[0, pl.ds(l, N, stride=CMP_STRIDE), :]
        top = top + _dot((xl + pe_ref[l:l + 1, :]).astype(BF16), w1_ref[l])
        bot = bot + _dot((xl + pe_ref[CMP_STRIDE + l:CMP_STRIDE + l + 1, :]).astype(BF16),
                         w1_ref[CMP_STRIDE + l])
    hid = top + pltpu.roll(bot, N - 1, 0)
    out = _dot(jax.nn.gelu(hid).astype(BF16), w2_ref[...])
    if normalize:
        out = _rms(out, g_ref[...])
    valid = lax.broadcasted_iota(jnp.int32, out.shape, 0) < N - 1
    o_ref[0, 0] = jnp.where(valid, out, 0.0)


def compress(proj, col0, pe, w1, w2, layer, gain, normalize):
    B, T, _ = proj.shape
    G = NSA_KV_HEADS
    N = T // CMP_STRIDE
    Hc = w1.shape[3]
    assert CMP_BLOCK == 2 * CMP_STRIDE
    return pl.pallas_call(
        functools.partial(_compress_kernel, normalize=normalize),
        name="nsa_compress",
        grid=(B, G),
        in_specs=[pl.BlockSpec((1, T, HEAD_DIM), lambda b, g: (b, 0, col0 + g)),
                  pl.BlockSpec((None, CMP_BLOCK, HEAD_DIM), lambda b, g: (layer, 0, 0)),
                  pl.BlockSpec((None, CMP_BLOCK, HEAD_DIM, Hc), lambda b, g: (layer, 0, 0, 0)),
                  pl.BlockSpec((None, Hc, HEAD_DIM), lambda b, g: (layer, 0, 0)),
                  pl.BlockSpec((1, HEAD_DIM), lambda b, g: (0, 0))],
        out_specs=pl.BlockSpec((1, 1, N, HEAD_DIM), lambda b, g: (b, g, 0, 0)),
        out_shape=jax.ShapeDtypeStruct((B, G, N, HEAD_DIM), F32),
        compiler_params=_params("parallel", "parallel"),
    )(proj, pe, w1, w2, gain.reshape(1, HEAD_DIM))


_SEL_LANE0 = 64
_CMP_OV_ROW0 = VT_ROWS


def _nsa_kernel(q_ref, ks_ref, vs_ref, kw_ref, vw_ref, kc_ref, vc_ref, gt_ref, qg_ref, kg_ref,
                o_ref, ksa_ref, kwa_ref, vst_ref, vwt_ref, kca_ref, vcov_ref, tiles_ref, *, tq, tk, n_sel):
    g = pl.program_id(1)
    qi = pl.program_id(2)
    T = ks_ref.shape[1]
    J = NSA_GROUP
    n_slc = T // SLC_BLOCK
    nc = kc_ref.shape[2]
    n_cmp = nc - 1
    cols = J * tq
    eye = _eye(HEAD_DIM)

    @pl.when(qi == 0)
    def _():
        row = lax.broadcasted_iota(jnp.int32, (T, LANES), 0)
        lane = lax.broadcasted_iota(jnp.int32, (T, LANES), 1)
        blk = row // SLC_BLOCK
        feats = jnp.where(lane < 3, blk,
                          jnp.where(lane < 6, row % SLC_BLOCK,
                                    jnp.where(lane < 9, 1,
                                              jnp.where(lane - _SEL_LANE0 == blk, 1, 0))))
        feats = feats.astype(F32).astype(BF16)
        ksa_ref[:, :HEAD_DIM] = _rms(ks_ref[0], kg_ref[1:2, :]).astype(BF16)
        ksa_ref[:, HEAD_DIM:] = feats
        kwa_ref[:, :HEAD_DIM] = _rms(kw_ref[0], kg_ref[2:3, :]).astype(BF16)
        kwa_ref[:, HEAD_DIM:] = feats
        vst_ref[:HEAD_DIM, :] = _dot_nt(eye, vs_ref[0].astype(BF16)).astype(BF16)
        vst_ref[HEAD_DIM:, :] = _ones_row(VT_ROWS - HEAD_DIM, T)
        vwt_ref[:HEAD_DIM, :] = _dot_nt(eye, vw_ref[0].astype(BF16)).astype(BF16)
        vwt_ref[HEAD_DIM:, :] = _ones_row(VT_ROWS - HEAD_DIM, T)
        n_r = lax.broadcasted_iota(jnp.int32, (nc, LANES), 0)
        n_l = lax.broadcasted_iota(jnp.int32, (nc, LANES), 1)
        per = SLC_BLOCK // CMP_STRIDE
        cfe = jnp.where(n_l < 3, (n_r // per).astype(F32),
                        jnp.where(n_l < 6, (CMP_STRIDE * (n_r % per)).astype(F32) + 0.5 * (CMP_BLOCK - 1),
                                  jnp.where(n_l < 9, 1.0, 0.0)))
        kca_ref[:, :HEAD_DIM] = kc_ref[0, 0].astype(BF16)
        kca_ref[:, HEAD_DIM:] = cfe.astype(BF16)
        vcov_ref[:HEAD_DIM, :] = _dot_nt(eye, vc_ref[0, 0].astype(BF16)).astype(BF16)
        vcov_ref[HEAD_DIM:_CMP_OV_ROW0, :] = _ones_row(VT_ROWS - HEAD_DIM, nc)
        m_r = lax.broadcasted_iota(jnp.int32, (SLC_BLOCK, nc), 0)
        n_c = lax.broadcasted_iota(jnp.int32, (SLC_BLOCK, nc), 1)
        vcov_ref[_CMP_OV_ROW0:, :] = ((n_c * CMP_STRIDE <= m_r * SLC_BLOCK + (SLC_BLOCK - 1))
                                      & (n_c * CMP_STRIDE + (CMP_BLOCK - 1) >= m_r * SLC_BLOCK)
                                      & (n_c < n_cmp) & (m_r < n_slc)).astype(BF16)

    q0 = pl.multiple_of(qi * tq, tq)
    qraw = q_ref[0]
    qt = jnp.concatenate(
        [_dot_nt(eye, (_rms(qraw[:, j * HEAD_DIM:(j + 1) * HEAD_DIM], qg_ref[...])
                       * (ATTN_SCALE * LOG2E)).astype(BF16)).astype(BF16)
         for j in range(J)], axis=1)

    col = lax.broadcasted_iota(jnp.int32, (1, cols), 1)
    tt = q0 + col % tq
    slope2 = jnp.exp2(-0.5 * (g * J + col // tq + 1).astype(F32)) * LOG2E
    s1, s2, s3 = _split3_f32(slope2)
    r1, r2, r3 = _split3_f32(-slope2 * tt.astype(F32))
    r16 = lax.broadcasted_iota(jnp.int32, (16, cols), 0)
    feat = jnp.zeros((16, cols), F32)
    for i, v in enumerate((s1 * SLC_BLOCK, s2 * SLC_BLOCK, s3 * SLC_BLOCK, s1, s2, s3, r1, r2, r3)):
        feat = jnp.where(r16 == i, v, feat)
    feat = feat.astype(BF16)
    qa_base = jnp.concatenate([qt, feat, jnp.zeros((AUG - HEAD_DIM - 16, cols), BF16)], axis=0)

    n_col = lax.broadcasted_iota(jnp.int32, (nc, 1), 0)
    cmp_end = n_col * CMP_STRIDE + (CMP_BLOCK - 1)
    kca = kca_ref[...]
    r_hi, r_lo = [], []
    for c in _col_groups(cols):
        s = _dot(kca, qa_base[:, c])
        s = jnp.where((cmp_end <= tt[:, c]) & (n_col < n_cmp), s, NEG_INF)
        m = jnp.max(s, axis=0, keepdims=True)
        p = jnp.exp2(s - jnp.where(m == NEG_INF, 0.0, m))
        p_hi = p.astype(BF16)
        r_hi.append(_dot(vcov_ref[...], p_hi))
        r_lo.append(_dot(vcov_ref[HEAD_DIM:, :], (p - p_hi.astype(F32)).astype(BF16)))
    r_hi = jnp.concatenate(r_hi, axis=1)
    r_lo = jnp.concatenate(r_lo, axis=1)
    z = r_hi[HEAD_DIM:HEAD_DIM + 1] + r_lo[0:1]
    inv_z = 1.0 / jnp.where(z > 0, z, 1.0)

    wq = min(tq, WINDOW_Q)
    span = WINDOW + wq
    d = (lax.broadcasted_iota(jnp.int32, (span, 1), 0)
         - lax.broadcasted_iota(jnp.int32, (1, wq), 1))
    parts = []
    for c in range(tq // wq):
        k0w = pl.multiple_of(jnp.maximum(q0 + c * wq - WINDOW, 0), wq)
        off = q0 + c * wq - k0w
        bias = jnp.where((d <= off) & (d > off - WINDOW), 0.0, NEG_INF)
        bias = jnp.concatenate([bias] * J, axis=1)
        qa_c = jnp.concatenate([qa_base[:, j * tq + c * wq:j * tq + (c + 1) * wq] for j in range(J)], axis=1)
        parts.append(_attn_finish(_attn_tile(
            kwa_ref[pl.ds(k0w, span), :], vwt_ref[:, pl.ds(k0w, span)], qa_c, _attn_init(J * wq),
            lambda s, sub, bias=bias: s + bias[:, sub])))
    acc_w = jnp.concatenate([p[:, j * wq:(j + 1) * wq] for j in range(J) for p in parts], axis=1)

    imp4 = (r_hi[_CMP_OV_ROW0:] + r_lo[_CMP_OV_ROW0 - HEAD_DIM:]) * inv_z
    imp = imp4[:, 0:tq]
    for j in range(1, J):
        imp = imp + imp4[:, j * tq:(j + 1) * tq]
    blk = lax.broadcasted_iota(jnp.int32, (SLC_BLOCK, tq), 0)
    t_q = q0 + lax.broadcasted_iota(jnp.int32, (1, tq), 1)
    imp = jnp.where(blk * SLC_BLOCK > t_q, NEG_INF, imp)
    imp = jnp.where((blk == t_q // SLC_BLOCK) | (blk == 0), jnp.inf, imp)
    blk_f = blk.astype(F32)
    taken = blk >= n_slc
    for _ in range(n_sel):
        vals = jnp.where(taken, NEG_INF, imp)
        best = jnp.max(vals, axis=0, keepdims=True)
        cand = (vals == best) & jnp.logical_not(taken)
        first = jnp.min(jnp.where(cand, blk_f, float(SLC_BLOCK)), axis=0, keepdims=True)
        taken = taken | (blk_f == first)
    sel = taken & (blk < n_slc)
    not_sel = jnp.where(sel, 0.0, -MASK_BIG).astype(BF16)
    qa_slc = jnp.concatenate([qt, feat, jnp.zeros((_SEL_LANE0 - 16, cols), BF16)]
                             + [jnp.concatenate([not_sel] * J, axis=1)], axis=0)

    def slc_tile(jt, carry, score_fn=None):
        k0 = pl.multiple_of(jt * tk, tk)
        return _attn_tile(ksa_ref[pl.ds(k0, tk), :], vst_ref[:, pl.ds(k0, tk)], qa_slc, carry, score_fn)

    jd = q0 // tk
    kpos = jd * tk + lax.broadcasted_iota(jnp.int32, (tk, 1), 0)
    carry = slc_tile(jd, _attn_init(cols), lambda s, c: jnp.where(kpos <= tt[:, c], s, NEG_INF))
    sel_f = jnp.where(sel, 1.0, 0.0)
    per_tile = tk // SLC_BLOCK
    n_live = jnp.int32(0)
    for jt in range(T // tk - 1):
        live = (jnp.max(sel_f[jt * per_tile:(jt + 1) * per_tile, :]) > 0.5) & (jt < jd)
        tiles_ref[n_live] = jt
        n_live = n_live + live.astype(jnp.int32)
    acc_s = _attn_finish(lax.fori_loop(0, n_live, lambda i, c: slc_tile(tiles_ref[i], c), carry))

    gates_t = jnp.transpose(jax.nn.sigmoid(gt_ref[0]))
    gate = lambda c: jnp.concatenate([gates_t[3 * j + c:3 * j + c + 1, :] for j in range(J)], axis=1)
    o_t = ((gate(0) * inv_z) * r_hi[:HEAD_DIM]
           + (gate(1) / acc_s[HEAD_DIM:HEAD_DIM + 1]) * acc_s[:HEAD_DIM]
           + (gate(2) / acc_w[HEAD_DIM:HEAD_DIM + 1]) * acc_w[:HEAD_DIM]).astype(BF16)
    for j in range(J):
        for c in range(tq // LANES):
            o_ref[0, c * LANES:(c + 1) * LANES, j * HEAD_DIM:(j + 1) * HEAD_DIM] = _dot_nt(
                eye, o_t[:, j * tq + c * LANES:j * tq + (c + 1) * LANES]).astype(o_ref.dtype)


def nsa_attn(proj, gates, k_cmp, v_cmp, q_gain, k_gain, tq, tk):
    B, T, _ = proj.shape
    G, J = NSA_KV_HEADS, NSA_GROUP
    tk = min(tk, T)
    nc = k_cmp.shape[2]
    assert T % tk == 0 and tk % tq == 0 and T >= WINDOW + tq
    assert T // SLC_BLOCK <= LANES - _SEL_LANE0 and SLC_BLOCK <= LANES - _SEL_LANE0
    qcols = NSA_HEADS
    kv = lambda slab: pl.BlockSpec((1, T, HEAD_DIM), lambda b, g, i: (b, 0, qcols + slab * G + g))
    cmp_spec = pl.BlockSpec((1, 1, nc, HEAD_DIM), lambda b, g, i: (b, g, 0, 0))
    return pl.pallas_call(
        functools.partial(_nsa_kernel, tq=tq, tk=tk, n_sel=min(SLC_TOPK, T // SLC_BLOCK)),
        name="nsa_attn",
        grid=(B, G, T // tq),
        in_specs=[pl.BlockSpec((1, tq, J * HEAD_DIM), lambda b, g, i: (b, i, g)),
                  kv(2), kv(3), kv(4), kv(5), cmp_spec, cmp_spec,
                  pl.BlockSpec((1, tq, LANES), lambda b, g, i: (b, i, g)),
                  pl.BlockSpec((1, HEAD_DIM), lambda b, g, i: (0, 0)),
                  pl.BlockSpec((3, HEAD_DIM), lambda b, g, i: (0, 0))],
        out_specs=pl.BlockSpec((1, tq, J * HEAD_DIM), lambda b, g, i: (b, i, g)),
        out_shape=jax.ShapeDtypeStruct((B, T, NSA_HEADS * HEAD_DIM), BF16),
        scratch_shapes=[pltpu.VMEM((T, AUG), BF16), pltpu.VMEM((T, AUG), BF16),
                        pltpu.VMEM((VT_ROWS, T), BF16), pltpu.VMEM((VT_ROWS, T), BF16),
                        pltpu.VMEM((nc, AUG), BF16),
                        pltpu.VMEM((VT_ROWS + SLC_BLOCK, nc), BF16),
                        pltpu.SMEM((T // tk,), jnp.int32)],
        compiler_params=_params("parallel", "parallel", "arbitrary"),
    )(proj, proj, proj, proj, proj, k_cmp, v_cmp, gates, q_gain.reshape(1, HEAD_DIM), k_gain)


def even_weights(w_in, w_out):
    fw = FOX_HEADS * HEAD_DIM
    w_main = jnp.concatenate([w_in[:, :, :3 * fw], w_in[:, :, 3 * fw + FOX_HEADS:]], axis=2).astype(BF16)
    w_gate = jnp.pad(w_in[:, :, 3 * fw:3 * fw + FOX_HEADS], ((0, 0), (0, 0), (0, LANES - FOX_HEADS)))
    return w_main, w_gate.astype(BF16), w_out.astype(BF16)


def even_mixer(x, gain, weights, e, f_bias, q_gain, k_gain, lb, o_gain):
    B, T, D = x.shape
    xf = x.reshape(B * T, D)
    w_main, w_gate, w_out = weights
    proj, gate = rms_matmul(xf, gain, w_main, e, 1024, 1024, side_w=w_gate)
    proj, gate = proj.reshape(B, T, -1), gate.reshape(B, T, LANES)
    f_bias = jnp.pad(f_bias.reshape(1, FOX_HEADS).astype(F32), ((0, 0), (0, LANES - FOX_HEADS)))
    c, kfeat = fox_gate(gate, f_bias, FOX_HEADS)
    c_rows = c[:, :, :FOX_HEADS].transpose(0, 2, 1).reshape(B * FOX_HEADS, 1, T)
    o_fox = fox_attn(proj, kfeat, c_rows, q_gain, k_gain, 1024)
    lb = lb.astype(F32).reshape(HGRN_HEADS, HEAD_DIM)
    o_h = hgrn(proj, 3 * FOX_HEADS, jnp.log(lb), jnp.log1p(-lb), 1.0 - lb, o_gain)
    out = out_proj([o_fox.reshape(B * T, -1), o_h.reshape(B * T, -1)], w_out, e, xf, 512)
    return out.reshape(B, T, D)


def nsa_weights(w_in, w_out, w1_k, w2_k, w1_v, w2_v):
    L, D, _ = w_in.shape
    G, J = NSA_KV_HEADS, NSA_GROUP
    n_main = (NSA_HEADS + 6 * G) * HEAD_DIM
    w_gate = w_in[:, :, n_main:].reshape(L, D, G, 3 * J)
    w_gate = jnp.pad(w_gate, ((0, 0), (0, 0), (0, 0), (0, LANES - 3 * J))).reshape(L, D, G * LANES)
    return (jnp.swapaxes(w_in[:, :, :n_main], 1, 2).astype(BF16), w_gate.astype(BF16), w_out.astype(BF16),
            w1_k.astype(BF16), w2_k.astype(BF16), w1_v.astype(BF16), w2_v.astype(BF16))


def nsa_mixer(x, gain, weights, o, q_gain, k_gain, pe_k, pe_v):
    B, T, D = x.shape
    G = NSA_KV_HEADS
    xf = x.reshape(B * T, D)
    w_main, w_gate, w_out, w1_k, w2_k, w1_v, w2_v = weights
    proj, gates = rms_matmul(xf, gain, w_main, o, 1024, 1024, side_w=w_gate, w_transposed=True)
    proj, gates = proj.reshape(B, T, -1), gates.reshape(B, T, G * LANES)
    k_cmp = compress(proj, NSA_HEADS, pe_k, w1_k, w2_k, o, k_gain[0], True)
    v_cmp = compress(proj, NSA_HEADS + G, pe_v, w1_v, w2_v, o, k_gain[0], False)
    att = nsa_attn(proj, gates, k_cmp, v_cmp, q_gain, k_gain, 512, 512)
    out = out_proj([att.reshape(B * T, -1)], w_out, o, xf, 512)
    return out.reshape(B, T, D)


def kernel(x, mem, norm_mix, norm_mem, norm_ffn, mem_in_gain, even_w_in, even_w_out, fox_f_bias,
           fox_q_gain, fox_k_gain, hgrn_lb_logits, hgrn_o_gain, odd_w_in, odd_w_out, nsa_q_gain,
           nsa_k_gain, cmp_pe_k, cmp_w1_k, cmp_w2_k, cmp_pe_v, cmp_w1_v, cmp_w2_v, mem_wq, mem_wkv,
           mem_wo, mem_q_gain, mem_k_gain, ffn_w1, ffn_w3, ffn_w2):
    B, T, D = x.shape
    depth = norm_mix.shape[0]
    L = mem.shape[1]
    lb_cum = jnp.cumsum(jax.nn.softmax(hgrn_lb_logits.astype(F32), axis=0), axis=0)
    hgrn_lb = lb_cum - lb_cum[0:1]
    memf = mem.reshape(B * L, D)
    even_w = even_weights(even_w_in, even_w_out)
    nsa_w = nsa_weights(odd_w_in, odd_w_out, cmp_w1_k, cmp_w2_k, cmp_w1_v, cmp_w2_v)
    mem_wq, mem_wkv, mem_wo = (w.astype(BF16) for w in (mem_wq, mem_wkv, mem_wo))
    for layer in range(depth):
        if layer % 2 == 0:
            e = layer // 2
            x = even_mixer(x, norm_mix[layer], even_w, e, fox_f_bias[e], fox_q_gain[e], fox_k_gain[e],
                           hgrn_lb[e], hgrn_o_gain[e])
        else:
            o = layer // 2
            x = nsa_mixer(x, norm_mix[layer], nsa_w, o, nsa_q_gain[o], nsa_k_gain[o], cmp_pe_k, cmp_pe_v)
        kv = rms_matmul(memf, mem_in_gain[layer], mem_wkv, layer, 256, 1024)
        x = mem_attn(x, norm_mem[layer], mem_wq, kv.reshape(B, L, -1), mem_q_gain[layer],
                     mem_k_gain[layer], mem_wo, layer, 1024)
        x = ffn(x.reshape(B * T, D), norm_ffn[layer], ffn_w1, ffn_w3, ffn_w2, layer,
                1024, 256).reshape(B, T, D)
    return x
```

```python
import functools

import jax
import jax.numpy as jnp
from jax import lax
from jax.experimental import pallas as pl
from jax.experimental.pallas import tpu as pltpu

F32 = jnp.float32
BF16 = jnp.bfloat16
NEG_INF = float("-inf")

EPS = 1e-6
HEAD_DIM = 128
ATTN_SCALE = HEAD_DIM ** -0.5
LANES = 128
FOX_HEADS = 8
HGRN_HEADS = 8
HGRN_BLOCK = 16
HGRN_CHUNK = 64
HGRN_SPAN = 2048
GATE_ROWS = 512
HGRN_SAFE_LOG_DECAY = -80.0
NSA_HEADS = 16
NSA_KV_HEADS = 4
NSA_GROUP = NSA_HEADS // NSA_KV_HEADS
CMP_BLOCK = 32
CMP_STRIDE = 16
SLC_BLOCK = 64
SLC_TOPK = 16
WINDOW = 512
WINDOW_Q = 256
MEM_HEADS = 4
V7X_VMEM_LIMIT = 56 * 1024 * 1024
LOG2E = 1.4426950408889634
MASK_BIG = 1e30
AUG = 2 * LANES
VT_ROWS = HEAD_DIM + 16


def _params(*sem):
    return pltpu.CompilerParams(dimension_semantics=sem, vmem_limit_bytes=V7X_VMEM_LIMIT)


def _rms(x, gain):
    return x * lax.rsqrt(jnp.mean(x * x, axis=-1, keepdims=True) + EPS) * gain


def _dot(a, b):
    return jnp.dot(a, b, preferred_element_type=F32)


def _dot_nt(a, b):
    return lax.dot_general(a, b, (((1,), (1,)), ((), ())), preferred_element_type=F32)


def _dot_tn(a, b):
    return lax.dot_general(a, b, (((0,), (0,)), ((), ())), preferred_element_type=F32)


def _split3(x):
    x1 = x.astype(BF16)
    r1 = x - x1.astype(F32)
    x2 = r1.astype(BF16)
    return x1, x2, (r1 - x2.astype(F32)).astype(BF16)


def _exact_dot(sel, x):
    x1, x2, x3 = _split3(x)
    return _dot(sel, x1) + _dot(sel, x2) + _dot(sel, x3)


def _log_sigmoid(z):
    return jnp.minimum(z, 0.0) - jnp.log1p(jnp.exp(-jnp.abs(z)))


def _rms_matmul_kernel(x_ref, g_ref, w_ref, *rest, w_transposed):
    side_w_ref, o_ref, side_o_ref, hn_ref = rest if len(rest) == 4 else (None, rest[0], None, rest[1])

    @pl.when(pl.program_id(1) == 0)
    def _():
        hn_ref[...] = _rms(x_ref[...], g_ref[...]).astype(BF16)
        if side_w_ref is not None:
            side_o_ref[...] = _dot(hn_ref[...], side_w_ref[...])

    o_ref[...] = (_dot_nt if w_transposed else _dot)(hn_ref[...], w_ref[...])


def rms_matmul(x, gain, w, layer, tm, tn, side_w=None, w_transposed=False):
    M, K = x.shape
    N = w.shape[1] if w_transposed else w.shape[2]
    tm, tn = min(tm, M), min(tn, N)
    assert M % tm == 0 and N % tn == 0
    w_spec = (pl.BlockSpec((None, tn, K), lambda i, j: (layer, j, 0)) if w_transposed
              else pl.BlockSpec((None, K, tn), lambda i, j: (layer, 0, j)))
    in_specs = [pl.BlockSpec((tm, K), lambda i, j: (i, 0)),
                pl.BlockSpec((1, K), lambda i, j: (0, 0)),
                w_spec]
    out_specs = pl.BlockSpec((tm, tn), lambda i, j: (i, j))
    out_shape = jax.ShapeDtypeStruct((M, N), F32)
    args = (x, gain.reshape(1, K), w)
    if side_w is not None:
        ns = side_w.shape[2]
        in_specs.append(pl.BlockSpec((None, K, ns), lambda i, j: (layer, 0, 0)))
        out_specs = [out_specs, pl.BlockSpec((tm, ns), lambda i, j: (i, 0))]
        out_shape = [out_shape, jax.ShapeDtypeStruct((M, ns), F32)]
        args += (side_w,)
    return pl.pallas_call(
        functools.partial(_rms_matmul_kernel, w_transposed=w_transposed),
        name="rms_matmul",
        grid=(M // tm, N // tn),
        in_specs=in_specs,
        out_specs=out_specs,
        out_shape=out_shape,
        scratch_shapes=[pltpu.VMEM((tm, K), BF16)],
        compiler_params=_params("parallel", "arbitrary"),
    )(*args)


def _out_proj_kernel(*refs, n_in):
    a_refs, w_refs = refs[:n_in], refs[n_in:2 * n_in]
    res_ref, o_ref = refs[2 * n_in], refs[2 * n_in + 1]
    acc = res_ref[...]
    for a_ref, w_ref in zip(a_refs, w_refs):
        acc = acc + _dot(a_ref[...], w_ref[...])
    o_ref[...] = acc


def out_proj(acts, w, layer, res, tm):
    M, N = res.shape
    tm = min(tm, M)
    assert M % tm == 0
    n_in = len(acts)
    ka = acts[0].shape[1]
    assert all(a.shape[1] == ka for a in acts) and n_in * ka == w.shape[1]
    row_block = lambda r: pl.BlockSpec((None, ka, N), lambda i: (layer, r, 0))
    weights = [w] * n_in
    in_specs = ([pl.BlockSpec((tm, ka), lambda i: (i, 0)) for _ in acts]
                + [row_block(r) for r in range(n_in)]
                + [pl.BlockSpec((tm, N), lambda i: (i, 0))])
    return pl.pallas_call(
        functools.partial(_out_proj_kernel, n_in=n_in),
        name="out_proj",
        grid=(M // tm,),
        in_specs=in_specs,
        out_specs=pl.BlockSpec((tm, N), lambda i: (i, 0)),
        out_shape=jax.ShapeDtypeStruct((M, N), F32),
        compiler_params=_params("parallel"),
    )(*acts, *weights, res)


def _ffn_kernel(x_ref, g_ref, w1_ref, w3_ref, w2_ref, o_ref, hn_ref):
    f = pl.program_id(1)

    @pl.when(f == 0)
    def _():
        x = x_ref[...]
        hn_ref[...] = _rms(x, g_ref[...]).astype(BF16)
        o_ref[...] = x

    h = hn_ref[...]
    a = _dot(h, w1_ref[...].astype(BF16))
    b = _dot(h, w3_ref[...].astype(BF16))
    u = (a * jax.nn.sigmoid(a) * b).astype(BF16)
    o_ref[...] += _dot(u, w2_ref[...].astype(BF16))


def ffn(x, gain, w1, w3, w2, layer, tm, tf):
    M, D = x.shape
    Fh = w1.shape[2]
    tm, tf = min(tm, M), min(tf, Fh)
    assert M % tm == 0 and Fh % tf == 0
    return pl.pallas_call(
        _ffn_kernel,
        name="ffn",
        grid=(M // tm, Fh // tf),
        in_specs=[pl.BlockSpec((tm, D), lambda i, f: (i, 0)),
                  pl.BlockSpec((1, D), lambda i, f: (0, 0)),
                  pl.BlockSpec((None, D, tf), lambda i, f: (layer, 0, f)),
                  pl.BlockSpec((None, D, tf), lambda i, f: (layer, 0, f)),
                  pl.BlockSpec((None, tf, D), lambda i, f: (layer, f, 0))],
        out_specs=pl.BlockSpec((tm, D), lambda i, f: (i, 0)),
        out_shape=jax.ShapeDtypeStruct((M, D), F32),
        scratch_shapes=[pltpu.VMEM((tm, D), BF16)],
        compiler_params=_params("parallel", "arbitrary"),
    )(x, gain.reshape(1, D), w1, w3, w2)


def _mem_attn_kernel(x_ref, g_ref, wq_ref, kv_ref, qg_ref, kg_ref, wo_ref, o_ref, *, heads):
    x = x_ref[0]
    h = _rms(x, g_ref[...]).astype(BF16)
    q = _dot(h, wq_ref[...])
    kv = kv_ref[0]
    w = heads * HEAD_DIM
    outs = []
    for hh in range(heads):
        sl = slice(hh * HEAD_DIM, (hh + 1) * HEAD_DIM)
        qh = (_rms(q[:, sl], qg_ref[...]) * ATTN_SCALE).astype(BF16)
        kh = _rms(kv[:, sl], kg_ref[...]).astype(BF16)
        vh = kv[:, w + hh * HEAD_DIM: w + (hh + 1) * HEAD_DIM].astype(BF16)
        s = _dot_nt(qh, kh)
        p = jnp.exp(s - jnp.max(s, axis=-1, keepdims=True))
        z = jnp.sum(p, axis=-1, keepdims=True)
        outs.append((_dot(p.astype(BF16), vh) / z).astype(BF16))
    o = jnp.concatenate(outs, axis=-1)
    o_ref[0] = x + _dot(o, wo_ref[...])


def mem_attn(x, gain, wq, kv, q_gain, k_gain, wo, layer, tm):
    B, T, D = x.shape
    L, W2 = kv.shape[1], kv.shape[2]
    W = W2 // 2
    tm = min(tm, T)
    assert T % tm == 0
    return pl.pallas_call(
        functools.partial(_mem_attn_kernel, heads=W // HEAD_DIM),
        name="mem_attn",
        grid=(B, T // tm),
        in_specs=[pl.BlockSpec((1, tm, D), lambda b, i: (b, i, 0)),
                  pl.BlockSpec((1, D), lambda b, i: (0, 0)),
                  pl.BlockSpec((None, D, W), lambda b, i: (layer, 0, 0)),
                  pl.BlockSpec((1, L, W2), lambda b, i: (b, 0, 0)),
                  pl.BlockSpec((1, HEAD_DIM), lambda b, i: (0, 0)),
                  pl.BlockSpec((1, HEAD_DIM), lambda b, i: (0, 0)),
                  pl.BlockSpec((None, W, D), lambda b, i: (layer, 0, 0))],
        out_specs=pl.BlockSpec((1, tm, D), lambda b, i: (b, i, 0)),
        out_shape=jax.ShapeDtypeStruct((B, T, D), F32),
        compiler_params=_params("parallel", "parallel"),
    )(x, gain.reshape(1, D), wq, kv, q_gain.reshape(1, HEAD_DIM), k_gain.reshape(1, HEAD_DIM), wo)


def _eye(n):
    return (lax.broadcasted_iota(jnp.int32, (n, n), 0)
            == lax.broadcasted_iota(jnp.int32, (n, n), 1)).astype(BF16)


def _ones_row(rows, n):
    return (lax.broadcasted_iota(jnp.int32, (rows, n), 0) == 0).astype(BF16)


def _split3_f32(x):
    return tuple(t.astype(F32) for t in _split3(x))


COL_GROUPS = 1


def _col_groups(n):
    w = n // COL_GROUPS
    return [slice(i * w, (i + 1) * w) for i in range(COL_GROUPS)]


def _attn_init(n):
    return tuple((jnp.full((1, c.stop - c.start), NEG_INF, F32), jnp.zeros((VT_ROWS, c.stop - c.start), F32))
                 for c in _col_groups(n))


def _attn_tile(k_tile, vt_tile, qa, carry, score_fn=None):
    out = []
    for (m, acc), c in zip(carry, _col_groups(qa.shape[1])):
        s = _dot(k_tile, qa[:, c])
        if score_fn is not None:
            s = score_fn(s, c)
        m_new = jnp.maximum(m, jnp.max(s, axis=0, keepdims=True))
        p = jnp.exp2(s - m_new).astype(BF16)
        out.append((m_new, jnp.exp2(m - m_new) * acc + _dot(vt_tile, p)))
    return tuple(out)


def _attn_finish(carry):
    return jnp.concatenate([acc for _, acc in carry], axis=1)


def _fox_gate_kernel(g_ref, b_ref, c_ref, kf_ref, *, heads):
    T = g_ref.shape[1]
    n = min(GATE_ROWS, T)
    tri = (lax.broadcasted_iota(jnp.int32, (n, n), 0)
           >= lax.broadcasted_iota(jnp.int32, (n, n), 1)).astype(BF16)
    lane = lax.broadcasted_iota(jnp.int32, (n, LANES), 1)

    def body(i, carry):
        rows = pl.ds(pl.multiple_of(i * n, n), n)
        cs = _exact_dot(tri, _log_sigmoid(g_ref[0, rows, :] + b_ref[...])) + carry
        c_ref[0, rows, :] = cs
        c2 = cs * LOG2E
        for h in range(heads):
            d1, d2, d3 = _split3_f32(c2[:, h:h + 1])
            feat = jnp.where(lane < 3, 1.0,
                             jnp.where(lane == 3, -d1, jnp.where(lane == 4, -d2,
                                                                 jnp.where(lane == 5, -d3, 0.0))))
            kf_ref[0, h, rows, :] = feat.astype(BF16)
        return cs[n - 1:n, :]

    lax.fori_loop(0, T // n, body, jnp.zeros((1, LANES), F32))


def fox_gate(g, bias, heads):
    B, T, _ = g.shape
    return pl.pallas_call(
        functools.partial(_fox_gate_kernel, heads=heads),
        name="fox_gate",
        grid=(B,),
        in_specs=[pl.BlockSpec((1, T, LANES), lambda b: (b, 0, 0)),
                  pl.BlockSpec((1, LANES), lambda b: (0, 0))],
        out_specs=[pl.BlockSpec((1, T, LANES), lambda b: (b, 0, 0)),
                   pl.BlockSpec((1, heads, T, LANES), lambda b: (b, 0, 0, 0))],
        out_shape=[jax.ShapeDtypeStruct((B, T, LANES), F32),
                   jax.ShapeDtypeStruct((B, heads, T, LANES), BF16)],
        compiler_params=_params("parallel"),
    )(g, bias)


def _fox_attn_kernel(q_ref, k_ref, v_ref, kf_ref, c_ref, qg_ref, kg_ref, o_ref, ka_ref, vt_ref, *, tq):
    qi = pl.program_id(2)
    T = k_ref.shape[1]
    eye = _eye(HEAD_DIM)

    @pl.when(qi == 0)
    def _():
        ka_ref[:, :HEAD_DIM] = _rms(k_ref[0], kg_ref[...]).astype(BF16)
        ka_ref[:, HEAD_DIM:] = kf_ref[0, 0]
        vt_ref[:HEAD_DIM, :] = _dot_nt(eye, v_ref[0].astype(BF16)).astype(BF16)
        vt_ref[HEAD_DIM:, :] = _ones_row(VT_ROWS - HEAD_DIM, T)

    q0 = pl.multiple_of(qi * tq, tq)
    qn = (_rms(q_ref[0], qg_ref[...]) * (ATTN_SCALE * LOG2E)).astype(BF16)
    qt = _dot_nt(eye, qn).astype(BF16)
    c1, c2, c3 = _split3_f32(c_ref[0, :, pl.ds(q0, tq)] * LOG2E)
    r = lax.broadcasted_iota(jnp.int32, (16, tq), 0)
    feat = jnp.where(r == 0, c1, jnp.where(r == 1, c2, jnp.where(r == 2, c3,
                                                                 jnp.where(r < 6, 1.0, 0.0))))
    qa = jnp.concatenate([qt, feat.astype(BF16), jnp.zeros((AUG - HEAD_DIM - 16, tq), BF16)], axis=0)

    kpos = lax.broadcasted_iota(jnp.int32, (tq, 1), 0)
    tpos = lax.broadcasted_iota(jnp.int32, (1, tq), 1)

    def tile(j, carry, score_fn=None):
        k0 = pl.multiple_of(j * tq, tq)
        return _attn_tile(ka_ref[pl.ds(k0, tq), :], vt_ref[:, pl.ds(k0, tq)], qa, carry, score_fn)

    carry = tile(qi, _attn_init(tq), lambda s, c: jnp.where(kpos <= tpos[:, c], s, NEG_INF))
    acc = _attn_finish(lax.fori_loop(0, qi, tile, carry))
    ot = (acc[:HEAD_DIM] / acc[HEAD_DIM:HEAD_DIM + 1]).astype(BF16)
    for c in range(tq // LANES):
        cs = slice(c * LANES, (c + 1) * LANES)
        o_ref[0, cs, :] = _dot_nt(eye, ot[:, cs]).astype(o_ref.dtype)


def fox_attn(proj, kfeat, c_rows, q_gain, k_gain, tq):
    B, T, _ = proj.shape
    H = FOX_HEADS
    tq = min(tq, T)
    assert T % tq == 0 and tq % LANES == 0
    return pl.pallas_call(
        functools.partial(_fox_attn_kernel, tq=tq),
        name="fox_attn",
        grid=(B, H, T // tq),
        in_specs=[pl.BlockSpec((1, tq, HEAD_DIM), lambda b, h, i: (b, i, h)),
                  pl.BlockSpec((1, T, HEAD_DIM), lambda b, h, i: (b, 0, H + h)),
                  pl.BlockSpec((1, T, HEAD_DIM), lambda b, h, i: (b, 0, 2 * H + h)),
                  pl.BlockSpec((1, 1, T, LANES), lambda b, h, i: (b, h, 0, 0)),
                  pl.BlockSpec((1, 1, T), lambda b, h, i: (b * H + h, 0, 0)),
                  pl.BlockSpec((1, HEAD_DIM), lambda b, h, i: (0, 0)),
                  pl.BlockSpec((1, HEAD_DIM), lambda b, h, i: (0, 0))],
        out_specs=pl.BlockSpec((1, tq, HEAD_DIM), lambda b, h, i: (b, i, h)),
        out_shape=jax.ShapeDtypeStruct((B, T, H * HEAD_DIM), BF16),
        scratch_shapes=[pltpu.VMEM((T, AUG), BF16), pltpu.VMEM((VT_ROWS, T), BF16)],
        compiler_params=_params("parallel", "parallel", "arbitrary"),
    )(proj, proj, proj, kfeat, c_rows, q_gain.reshape(1, HEAD_DIM), k_gain.reshape(1, HEAD_DIM))


def _hgrn_kernel(q_ref, f_ref, i_ref, g_ref, la_ref, l1_ref, oml_ref, og_ref, o_ref, *, bs, chunk, span):
    T = q_ref.shape[1]
    R = LANES
    ri = lax.broadcasted_iota(jnp.int32, (R, R), 0)
    ci = lax.broadcasted_iota(jnp.int32, (R, R), 1)

    def prefix_sel(n):
        return (((ri // n) == (ci // n)) & (ri >= ci)).astype(BF16)

    s_idx = lax.broadcasted_iota(jnp.int32, (bs, HEAD_DIM), 0)
    tril = (lax.broadcasted_iota(jnp.int32, (chunk, chunk), 0)
            >= lax.broadcasted_iota(jnp.int32, (chunk, chunk), 1))

    def gates(row0, n):
        rows = pl.ds(row0, n)
        z = f_ref[0, rows, :]
        a = la_ref[0]
        y = l1_ref[0] + _log_sigmoid(z)
        log_f = jnp.maximum(a, y) + jnp.log1p(jnp.exp(-jnp.abs(a - y)))
        hq = q_ref[0, rows, :]
        return log_f, oml_ref[0] * jax.nn.sigmoid(-z), hq * jax.nn.sigmoid(hq), i_ref[0, rows, :]

    def prefixes(log_f, n):
        sel = prefix_sel(n)
        pieces = _split3(log_f)
        b = jnp.concatenate([sum(_dot(sel, p[g * R:(g + 1) * R]) for p in pieces)
                             for g in range(log_f.shape[0] // R)], axis=0)
        bt = jnp.concatenate([jnp.broadcast_to(b[(u + 1) * n - 1:(u + 1) * n, :], (n, HEAD_DIM))
                              for u in range(log_f.shape[0] // n)], axis=0)
        return b, bt

    def finish(row0, n, o):
        rows = pl.ds(row0, n)
        hg = g_ref[0, rows, :]
        o_ref[0, rows, :] = (_rms(o, og_ref[...]) * (hg * jax.nn.sigmoid(hg))).astype(o_ref.dtype)

    def matmul_path(row0, st, kk, qq, vv, b, bt):
        qe = (qq * jnp.exp(b)).astype(BF16)
        k_inv = (kk * jnp.exp(-b)).astype(BF16)
        ke = (kk * jnp.exp(bt - b)).astype(BF16)
        dec = jnp.exp(bt)
        vb = vv.astype(BF16)
        sls = [slice(u * chunk, (u + 1) * chunk) for u in range(span // chunk)]
        scores = [jnp.where(tril, _dot_nt(qe[sl], k_inv[sl]), 0.0).astype(BF16) for sl in sls]
        intra = [_dot(a, vb[sl]) for a, sl in zip(scores, sls)]
        update = [_dot_tn(vb[sl], ke[sl]) for sl in sls]
        outs = []
        for sl, o_in, upd in zip(sls, intra, update):
            outs.append(_dot_nt(qe[sl], st.astype(BF16)) + o_in)
            st = dec[sl.start:sl.start + 1, :] * st + upd
        finish(row0, span, jnp.concatenate(outs, axis=0))
        return st

    def pairwise_step(i, st, base):
        row0 = pl.multiple_of(base + i * R, R)
        log_f, kk, qq, vv = gates(row0, R)
        b, bt = prefixes(log_f, bs)
        qe = (qq * jnp.exp(b)).astype(BF16)
        ke = (kk * jnp.exp(bt - b)).astype(BF16)
        dec = jnp.exp(bt)
        vb = vv.astype(BF16)
        outs = []
        for u in range(R // bs):
            sl = slice(u * bs, (u + 1) * bs)
            bu, vu, qk, ku = b[sl], vv[sl], qq[sl], kk[sl]
            intra = []
            for t in range(bs):
                w = jnp.exp(jnp.where(s_idx <= t, bu[t:t + 1, :] - bu, NEG_INF))
                a_t = jnp.sum(w * (qk[t:t + 1, :] * ku), axis=-1, keepdims=True)
                intra.append(jnp.sum(a_t * vu, axis=0, keepdims=True))
            outs.append(_dot_nt(qe[sl], st.astype(BF16)) + jnp.concatenate(intra, axis=0))
            st = dec[u * bs:u * bs + 1, :] * st + _dot_tn(vb[sl], ke[sl])
        finish(row0, R, jnp.concatenate(outs, axis=0))
        return st

    def step(i, st):
        row0 = pl.multiple_of(i * span, span)
        log_f, kk, qq, vv = gates(row0, span)
        b, bt = prefixes(log_f, chunk)
        return lax.cond(
            jnp.min(bt) >= HGRN_SAFE_LOG_DECAY,
            lambda st: matmul_path(row0, st, kk, qq, vv, b, bt),
            lambda st: lax.fori_loop(0, span // R, functools.partial(pairwise_step, base=row0), st),
            st)

    lax.fori_loop(0, T // span, step, jnp.zeros((HEAD_DIM, HEAD_DIM), F32))


def hgrn(proj, col0, log_lb, log1m_lb, one_m_lb, o_gain):
    B, T, _ = proj.shape
    H = HGRN_HEADS
    row = lambda off: pl.BlockSpec((1, T, HEAD_DIM), lambda b, h: (b, 0, col0 + off * H + h))
    par = pl.BlockSpec((1, 1, HEAD_DIM), lambda b, h: (h, 0, 0))
    log_lb, log1m_lb, one_m_lb = (p.reshape(H, 1, HEAD_DIM) for p in (log_lb, log1m_lb, one_m_lb))
    return pl.pallas_call(
        functools.partial(_hgrn_kernel, bs=HGRN_BLOCK, chunk=HGRN_CHUNK, span=min(HGRN_SPAN, T)),
        name="hgrn",
        grid=(B, H),
        in_specs=[row(0), row(1), row(2), row(3), par, par, par,
                  pl.BlockSpec((1, HEAD_DIM), lambda b, h: (0, 0))],
        out_specs=pl.BlockSpec((1, T, HEAD_DIM), lambda b, h: (b, 0, h)),
        out_shape=jax.ShapeDtypeStruct((B, T, H * HEAD_DIM), BF16),
        compiler_params=_params("parallel", "parallel"),
    )(proj, proj, proj, proj, log_lb, log1m_lb, one_m_lb, o_gain.reshape(1, HEAD_DIM))


def _compress_kernel(t_ref, pe_ref, w1_ref, w2_ref, g_ref, o_ref, *, normalize):
    N = t_ref.shape[1] // CMP_STRIDE
    top = bot = jnp.zeros((N, w1_ref.shape[2]), F32)
    for l in range(CMP_STRIDE):
        xl = t_ref[0, pl.ds(l, N, stride=CMP_STRIDE), :]
        top = top + _dot((xl + pe_ref[l:l + 1, :]).astype(BF16), w1_ref[l])
        bot = bot + _dot((xl + pe_ref[CMP_STRIDE + l:CMP_STRIDE + l + 1, :]).astype(BF16),
                         w1_ref[CMP_STRIDE + l])
    hid = top + pltpu.roll(bot, N - 1, 0)
    out = _dot(jax.nn.gelu(hid).astype(BF16), w2_ref[...])
    if normalize:
        out = _rms(out, g_ref[...])
    valid = lax.broadcasted_iota(jnp.int32, out.shape, 0) < N - 1
    o_ref[0, 0] = jnp.where(valid, out, 0.0)


def compress(proj, col0, pe, w1, w2, layer, gain, normalize):
    B, T, _ = proj.shape
    G = NSA_KV_HEADS
    N = T // CMP_STRIDE
    Hc = w1.shape[3]
    assert CMP_BLOCK == 2 * CMP_STRIDE
    return pl.pallas_call(
        functools.partial(_compress_kernel, normalize=normalize),
        name="nsa_compress",
        grid=(B, G),
        in_specs=[pl.BlockSpec((1, T, HEAD_DIM), lambda b, g: (b, 0, col0 + g)),
                  pl.BlockSpec((None, CMP_BLOCK, HEAD_DIM), lambda b, g: (layer, 0, 0)),
                  pl.BlockSpec((None, CMP_BLOCK, HEAD_DIM, Hc), lambda b, g: (layer, 0, 0, 0)),
                  pl.BlockSpec((None, Hc, HEAD_DIM), lambda b, g: (layer, 0, 0)),
                  pl.BlockSpec((1, HEAD_DIM), lambda b, g: (0, 0))],
        out_specs=pl.BlockSpec((1, 1, N, HEAD_DIM), lambda b, g: (b, g, 0, 0)),
        out_shape=jax.ShapeDtypeStruct((B, G, N, HEAD_DIM), F32),
        compiler_params=_params("parallel", "parallel"),
    )(proj, pe, w1, w2, gain.reshape(1, HEAD_DIM))


_SEL_LANE0 = 64
_CMP_OV_ROW0 = VT_ROWS


def _nsa_kernel(q_ref, ks_ref, vs_ref, kw_ref, vw_ref, kc_ref, vc_ref, gt_ref, qg_ref, kg_ref,
                o_ref, ksa_ref, kwa_ref, vst_ref, vwt_ref, kca_ref, vcov_ref, tiles_ref, *, tq, tk, n_sel):
    g = pl.program_id(1)
    qi = pl.program_id(2)
    T = ks_ref.shape[1]
    J = NSA_GROUP
    n_slc = T // SLC_BLOCK
    nc = kc_ref.shape[2]
    n_cmp = nc - 1
    cols = J * tq
    eye = _eye(HEAD_DIM)

    @pl.when(qi == 0)
    def _():
        row = lax.broadcasted_iota(jnp.int32, (T, LANES), 0)
        lane = lax.broadcasted_iota(jnp.int32, (T, LANES), 1)
        blk = row // SLC_BLOCK
        feats = jnp.where(lane < 3, blk,
                          jnp.where(lane < 6, row % SLC_BLOCK,
                                    jnp.where(lane < 9, 1,
                                              jnp.where(lane - _SEL_LANE0 == blk, 1, 0))))
        feats = feats.astype(F32).astype(BF16)
        ksa_ref[:, :HEAD_DIM] = _rms(ks_ref[0], kg_ref[1:2, :]).astype(BF16)
        ksa_ref[:, HEAD_DIM:] = feats
        kwa_ref[:, :HEAD_DIM] = _rms(kw_ref[0], kg_ref[2:3, :]).astype(BF16)
        kwa_ref[:, HEAD_DIM:] = feats
        vst_ref[:HEAD_DIM, :] = _dot_nt(eye, vs_ref[0].astype(BF16)).astype(BF16)
        vst_ref[HEAD_DIM:, :] = _ones_row(VT_ROWS - HEAD_DIM, T)
        vwt_ref[:HEAD_DIM, :] = _dot_nt(eye, vw_ref[0].astype(BF16)).astype(BF16)
        vwt_ref[HEAD_DIM:, :] = _ones_row(VT_ROWS - HEAD_DIM, T)
        n_r = lax.broadcasted_iota(jnp.int32, (nc, LANES), 0)
        n_l = lax.broadcasted_iota(jnp.int32, (nc, LANES), 1)
        per = SLC_BLOCK // CMP_STRIDE
        cfe = jnp.where(n_l < 3, (n_r // per).astype(F32),
                        jnp.where(n_l < 6, (CMP_STRIDE * (n_r % per)).astype(F32) + 0.5 * (CMP_BLOCK - 1),
                                  jnp.where(n_l < 9, 1.0, 0.0)))
        kca_ref[:, :HEAD_DIM] = kc_ref[0, 0].astype(BF16)
        kca_ref[:, HEAD_DIM:] = cfe.astype(BF16)
        vcov_ref[:HEAD_DIM, :] = _dot_nt(eye, vc_ref[0, 0].astype(BF16)).astype(BF16)
        vcov_ref[HEAD_DIM:_CMP_OV_ROW0, :] = _ones_row(VT_ROWS - HEAD_DIM, nc)
        m_r = lax.broadcasted_iota(jnp.int32, (SLC_BLOCK, nc), 0)
        n_c = lax.broadcasted_iota(jnp.int32, (SLC_BLOCK, nc), 1)
        vcov_ref[_CMP_OV_ROW0:, :] = ((n_c * CMP_STRIDE <= m_r * SLC_BLOCK + (SLC_BLOCK - 1))
                                      & (n_c * CMP_STRIDE + (CMP_BLOCK - 1) >= m_r * SLC_BLOCK)
                                      & (n_c < n_cmp) & (m_r < n_slc)).astype(BF16)

    q0 = pl.multiple_of(qi * tq, tq)
    qraw = q_ref[0]
    qt = jnp.concatenate(
        [_dot_nt(eye, (_rms(qraw[:, j * HEAD_DIM:(j + 1) * HEAD_DIM], qg_ref[...])
                       * (ATTN_SCALE * LOG2E)).astype(BF16)).astype(BF16)
         for j in range(J)], axis=1)

    col = lax.broadcasted_iota(jnp.int32, (1, cols), 1)
    tt = q0 + col % tq
    slope2 = jnp.exp2(-0.5 * (g * J + col // tq + 1).astype(F32)) * LOG2E
    s1, s2, s3 = _split3_f32(slope2)
    r1, r2, r3 = _split3_f32(-slope2 * tt.astype(F32))
    r16 = lax.broadcasted_iota(jnp.int32, (16, cols), 0)
    feat = jnp.zeros((16, cols), F32)
    for i, v in enumerate((s1 * SLC_BLOCK, s2 * SLC_BLOCK, s3 * SLC_BLOCK, s1, s2, s3, r1, r2, r3)):
        feat = jnp.where(r16 == i, v, feat)
    feat = feat.astype(BF16)
    qa_base = jnp.concatenate([qt, feat, jnp.zeros((AUG - HEAD_DIM - 16, cols), BF16)], axis=0)

    n_col = lax.broadcasted_iota(jnp.int32, (nc, 1), 0)
    cmp_end = n_col * CMP_STRIDE + (CMP_BLOCK - 1)
    kca = kca_ref[...]
    r_hi, r_lo = [], []
    for c in _col_groups(cols):
        s = _dot(kca, qa_base[:, c])
        s = jnp.where((cmp_end <= tt[:, c]) & (n_col < n_cmp), s, NEG_INF)
        m = jnp.max(s, axis=0, keepdims=True)
        p = jnp.exp2(s - jnp.where(m == NEG_INF, 0.0, m))
        p_hi = p.astype(BF16)
        r_hi.append(_dot(vcov_ref[...], p_hi))
        r_lo.append(_dot(vcov_ref[HEAD_DIM:, :], (p - p_hi.astype(F32)).astype(BF16)))
    r_hi = jnp.concatenate(r_hi, axis=1)
    r_lo = jnp.concatenate(r_lo, axis=1)
    z = r_hi[HEAD_DIM:HEAD_DIM + 1] + r_lo[0:1]
    inv_z = 1.0 / jnp.where(z > 0, z, 1.0)

    wq = min(tq, WINDOW_Q)
    span = WINDOW + wq
    d = (lax.broadcasted_iota(jnp.int32, (span, 1), 0)
         - lax.broadcasted_iota(jnp.int32, (1, wq), 1))
    parts = []
    for c in range(tq // wq):
        k0w = pl.multiple_of(jnp.maximum(q0 + c * wq - WINDOW, 0), wq)
        off = q0 + c * wq - k0w
        bias = jnp.where((d <= off) & (d > off - WINDOW), 0.0, NEG_INF)
        bias = jnp.concatenate([bias] * J, axis=1)
        qa_c = jnp.concatenate([qa_base[:, j * tq + c * wq:j * tq + (c + 1) * wq] for j in range(J)], axis=1)
        parts.append(_attn_finish(_attn_tile(
            kwa_ref[pl.ds(k0w, span), :], vwt_ref[:, pl.ds(k0w, span)], qa_c, _attn_init(J * wq),
            lambda s, sub, bias=bias: s + bias[:, sub])))
    acc_w = jnp.concatenate([p[:, j * wq:(j + 1) * wq] for j in range(J) for p in parts], axis=1)

    imp4 = (r_hi[_CMP_OV_ROW0:] + r_lo[_CMP_OV_ROW0 - HEAD_DIM:]) * inv_z
    imp = imp4[:, 0:tq]
    for j in range(1, J):
        imp = imp + imp4[:, j * tq:(j + 1) * tq]
    blk = lax.broadcasted_iota(jnp.int32, (SLC_BLOCK, tq), 0)
    t_q = q0 + lax.broadcasted_iota(jnp.int32, (1, tq), 1)
    imp = jnp.where(blk * SLC_BLOCK > t_q, NEG_INF, imp)
    imp = jnp.where((blk == t_q // SLC_BLOCK) | (blk == 0), jnp.inf, imp)
    blk_f = blk.astype(F32)
    taken = blk >= n_slc
    for _ in range(n_sel):
        vals = jnp.where(taken, NEG_INF, imp)
        best = jnp.max(vals, axis=0, keepdims=True)
        cand = (vals == best) & jnp.logical_not(taken)
        first = jnp.min(jnp.where(cand, blk_f, float(SLC_BLOCK)), axis=0, keepdims=True)
        taken = taken | (blk_f == first)
    sel = taken & (blk < n_slc)
    not_sel = jnp.where(sel, 0.0, -MASK_BIG).astype(BF16)
    qa_slc = jnp.concatenate([qt, feat, jnp.zeros((_SEL_LANE0 - 16, cols), BF16)]
                             + [jnp.concatenate([not_sel] * J, axis=1)], axis=0)

    def slc_tile(jt, carry, score_fn=None):
        k0 = pl.multiple_of(jt * tk, tk)
        return _attn_tile(ksa_ref[pl.ds(k0, tk), :], vst_ref[:, pl.ds(k0, tk)], qa_slc, carry, score_fn)

    jd = q0 // tk
    kpos = jd * tk + lax.broadcasted_iota(jnp.int32, (tk, 1), 0)
    carry = slc_tile(jd, _attn_init(cols), lambda s, c: jnp.where(kpos <= tt[:, c], s, NEG_INF))
    sel_f = jnp.where(sel, 1.0, 0.0)
    per_tile = tk // SLC_BLOCK
    n_live = jnp.int32(0)
    for jt in range(T // tk - 1):
        live = (jnp.max(sel_f[jt * per_tile:(jt + 1) * per_tile, :]) > 0.5) & (jt < jd)
        tiles_ref[n_live] = jt
        n_live = n_live + live.astype(jnp.int32)
    acc_s = _attn_finish(lax.fori_loop(0, n_live, lambda i, c: slc_tile(tiles_ref[i], c), carry))

    gates_t = jnp.transpose(jax.nn.sigmoid(gt_ref[0]))
    gate = lambda c: jnp.concatenate([gates_t[3 * j + c:3 * j + c + 1, :] for j in range(J)], axis=1)
    o_t = ((gate(0) * inv_z) * r_hi[:HEAD_DIM]
           + (gate(1) / acc_s[HEAD_DIM:HEAD_DIM + 1]) * acc_s[:HEAD_DIM]
           + (gate(2) / acc_w[HEAD_DIM:HEAD_DIM + 1]) * acc_w[:HEAD_DIM]).astype(BF16)
    for j in range(J):
        for c in range(tq // LANES):
            o_ref[0, c * LANES:(c + 1) * LANES, j * HEAD_DIM:(j + 1) * HEAD_DIM] = _dot_nt(
                eye, o_t[:, j * tq + c * LANES:j * tq + (c + 1) * LANES]).astype(o_ref.dtype)


def nsa_attn(proj, gates, k_cmp, v_cmp, q_gain, k_gain, tq, tk):
    B, T, _ = proj.shape
    G, J = NSA_KV_HEADS, NSA_GROUP
    tk = min(tk, T)
    nc = k_cmp.shape[2]
    assert T % tk == 0 and tk % tq == 0 and T >= WINDOW + tq
    assert T // SLC_BLOCK <= LANES - _SEL_LANE0 and SLC_BLOCK <= LANES - _SEL_LANE0
    qcols = NSA_HEADS
    kv = lambda slab: pl.BlockSpec((1, T, HEAD_DIM), lambda b, g, i: (b, 0, qcols + slab * G + g))
    cmp_spec = pl.BlockSpec((1, 1, nc, HEAD_DIM), lambda b, g, i: (b, g, 0, 0))
    return pl.pallas_call(
        functools.partial(_nsa_kernel, tq=tq, tk=tk, n_sel=min(SLC_TOPK, T // SLC_BLOCK)),
        name="nsa_attn",
        grid=(B, G, T // tq),
        in_specs=[pl.BlockSpec((1, tq, J * HEAD_DIM), lambda b, g, i: (b, i, g)),
                  kv(2), kv(3), kv(4), kv(5), cmp_spec, cmp_spec,
                  pl.BlockSpec((1, tq, LANES), lambda b, g, i: (b, i, g)),
                  pl.BlockSpec((1, HEAD_DIM), lambda b, g, i: (0, 0)),
                  pl.BlockSpec((3, HEAD_DIM), lambda b, g, i: (0, 0))],
        out_specs=pl.BlockSpec((1, tq, J * HEAD_DIM), lambda b, g, i: (b, i, g)),
        out_shape=jax.ShapeDtypeStruct((B, T, NSA_HEADS * HEAD_DIM), BF16),
        scratch_shapes=[pltpu.VMEM((T, AUG), BF16), pltpu.VMEM((T, AUG), BF16),
                        pltpu.VMEM((VT_ROWS, T), BF16), pltpu.VMEM((VT_ROWS, T), BF16),
                        pltpu.VMEM((nc, AUG), BF16),
                        pltpu.VMEM((VT_ROWS + SLC_BLOCK, nc), BF16),
                        pltpu.SMEM((T // tk,), jnp.int32)],
        compiler_params=_params("parallel", "parallel", "arbitrary"),
    )(proj, proj, proj, proj, proj, k_cmp, v_cmp, gates, q_gain.reshape(1, HEAD_DIM), k_gain)


def even_weights(w_in, w_out):
    fw = FOX_HEADS * HEAD_DIM
    w_main = jnp.concatenate([w_in[:, :, :3 * fw], w_in[:, :, 3 * fw + FOX_HEADS:]], axis=2).astype(BF16)
    w_gate = jnp.pad(w_in[:, :, 3 * fw:3 * fw + FOX_HEADS], ((0, 0), (0, 0), (0, LANES - FOX_HEADS)))
    return w_main, w_gate.astype(BF16), w_out.astype(BF16)


def even_mixer(x, gain, weights, e, f_bias, q_gain, k_gain, lb, o_gain):
    B, T, D = x.shape
    xf = x.reshape(B * T, D)
    w_main, w_gate, w_out = weights
    proj, gate = rms_matmul(xf, gain, w_main, e, 1024, 1024, side_w=w_gate)
    proj, gate = proj.reshape(B, T, -1), gate.reshape(B, T, LANES)
    f_bias = jnp.pad(f_bias.reshape(1, FOX_HEADS).astype(F32), ((0, 0), (0, LANES - FOX_HEADS)))
    c, kfeat = fox_gate(gate, f_bias, FOX_HEADS)
    c_rows = c[:, :, :FOX_HEADS].transpose(0, 2, 1).reshape(B * FOX_HEADS, 1, T)
    o_fox = fox_attn(proj, kfeat, c_rows, q_gain, k_gain, 1024)
    lb = lb.astype(F32).reshape(HGRN_HEADS, HEAD_DIM)
    o_h = hgrn(proj, 3 * FOX_HEADS, jnp.log(lb), jnp.log1p(-lb), 1.0 - lb, o_gain)
    out = out_proj([o_fox.reshape(B * T, -1), o_h.reshape(B * T, -1)], w_out, e, xf, 512)
    return out.reshape(B, T, D)


def nsa_weights(w_in, w_out, w1_k, w2_k, w1_v, w2_v):
    L, D, _ = w_in.shape
    G, J = NSA_KV_HEADS, NSA_GROUP
    n_main = (NSA_HEADS + 6 * G) * HEAD_DIM
    w_gate = w_in[:, :, n_main:].reshape(L, D, G, 3 * J)
    w_gate = jnp.pad(w_gate, ((0, 0), (0, 0), (0, 0), (0, LANES - 3 * J))).reshape(L, D, G * LANES)
    return (jnp.swapaxes(w_in[:, :, :n_main], 1, 2).astype(BF16), w_gate.astype(BF16), w_out.astype(BF16),
            w1_k.astype(BF16), w2_k.astype(BF16), w1_v.astype(BF16), w2_v.astype(BF16))


def nsa_mixer(x, gain, weights, o, q_gain, k_gain, pe_k, pe_v):
    B, T, D = x.shape
    G = NSA_KV_HEADS
    xf = x.reshape(B * T, D)
    w_main, w_gate, w_out, w1_k, w2_k, w1_v, w2_v = weights
    proj, gates = rms_matmul(xf, gain, w_main, o, 1024, 1024, side_w=w_gate, w_transposed=True)
    proj, gates = proj.reshape(B, T, -1), gates.reshape(B, T, G * LANES)
    k_cmp = compress(proj, NSA_HEADS, pe_k, w1_k, w2_k, o, k_gain[0], True)
    v_cmp = compress(proj, NSA_HEADS + G, pe_v, w1_v, w2_v, o, k_gain[0], False)
    att = nsa_attn(proj, gates, k_cmp, v_cmp, q_gain, k_gain, 512, 512)
    out = out_proj([att.reshape(B * T, -1)], w_out, o, xf, 512)
    return out.reshape(B, T, D)


def kernel(x, mem, norm_mix, norm_mem, norm_ffn, mem_in_gain, even_w_in, even_w_out, fox_f_bias,
           fox_q_gain, fox_k_gain, hgrn_lb_logits, hgrn_o_gain, odd_w_in, odd_w_out, nsa_q_gain,
           nsa_k_gain, cmp_pe_k, cmp_w1_k, cmp_w2_k, cmp_pe_v, cmp_w1_v, cmp_w2_v, mem_wq, mem_wkv,
           mem_wo, mem_q_gain, mem_k_gain, ffn_w1, ffn_w3, ffn_w2):
    B, T, D = x.shape
    depth = norm_mix.shape[0]
    L = mem.shape[1]
    lb_cum = jnp.cumsum(jax.nn.softmax(hgrn_lb_logits.astype(F32), axis=0), axis=0)
    hgrn_lb = lb_cum - lb_cum[0:1]
    memf = mem.reshape(B * L, D)
    even_w = even_weights(even_w_in, even_w_out)
    nsa_w = nsa_weights(odd_w_in, odd_w_out, cmp_w1_k, cmp_w2_k, cmp_w1_v, cmp_w2_v)
    mem_wq, mem_wkv, mem_wo = (w.astype(BF16) for w in (mem_wq, mem_wkv, mem_wo))
    for layer in range(depth):
        if layer % 2 == 0:
            e = layer // 2
            x = even_mixer(x, norm_mix[layer], even_w, e, fox_f_bias[e], fox_q_gain[e], fox_k_gain[e],
                           hgrn_lb[e], hgrn_o_gain[e])
        else:
            o = layer // 2
            x = nsa_mixer(x, norm_mix[layer], nsa_w, o, nsa_q_gain[o], nsa_k_gain[o], cmp_pe_k, cmp_pe_v)
        kv = rms_matmul(memf, mem_in_gain[layer], mem_wkv, layer, 256, 1024)
        x = mem_attn(x, norm_mem[layer], mem_wq, kv.reshape(B, L, -1), mem_q_gain[layer],
                     mem_k_gain[layer], mem_wo, layer, 1024)
        x = ffn(x.reshape(B * T, D), norm_ffn[layer], ffn_w1, ffn_w3, ffn_w2, layer,
                1024, 256).reshape(B, T, D)
    return x
```

```python
import functools

import jax
import jax.numpy as jnp
from jax import lax
from jax.experimental import pallas as pl
from jax.experimental.pallas import tpu as pltpu

F32 = jnp.float32
BF16 = jnp.bfloat16
NEG_INF = float("-inf")

EPS = 1e-6
HEAD_DIM = 128
ATTN_SCALE = HEAD_DIM ** -0.5
LANES = 128
FOX_HEADS = 8
HGRN_HEADS = 8
HGRN_BLOCK = 16
HGRN_CHUNK = 64
HGRN_SPAN = 2048
GATE_ROWS = 512
HGRN_SAFE_LOG_DECAY = -80.0
NSA_HEADS = 16
NSA_KV_HEADS = 4
NSA_GROUP = NSA_HEADS // NSA_KV_HEADS
CMP_BLOCK = 32
CMP_STRIDE = 16
SLC_BLOCK = 64
SLC_TOPK = 16
WINDOW = 512
WINDOW_Q = 256
MEM_HEADS = 4
V7X_VMEM_LIMIT = 56 * 1024 * 1024
LOG2E = 1.4426950408889634
MASK_BIG = 1e30
AUG = 2 * LANES
VT_ROWS = HEAD_DIM + 16


def _params(*sem):
    return pltpu.CompilerParams(dimension_semantics=sem, vmem_limit_bytes=V7X_VMEM_LIMIT)


def _rms(x, gain):
    return x * lax.rsqrt(jnp.mean(x * x, axis=-1, keepdims=True) + EPS) * gain


def _dot(a, b):
    return jnp.dot(a, b, preferred_element_type=F32)


def _dot_nt(a, b):
    return lax.dot_general(a, b, (((1,), (1,)), ((), ())), preferred_element_type=F32)


def _dot_tn(a, b):
    return lax.dot_general(a, b, (((0,), (0,)), ((), ())), preferred_element_type=F32)


def _split3(x):
    x1 = x.astype(BF16)
    r1 = x - x1.astype(F32)
    x2 = r1.astype(BF16)
    return x1, x2, (r1 - x2.astype(F32)).astype(BF16)


def _exact_dot(sel, x):
    x1, x2, x3 = _split3(x)
    return _dot(sel, x1) + _dot(sel, x2) + _dot(sel, x3)


def _log_sigmoid(z):
    return jnp.minimum(z, 0.0) - jnp.log1p(jnp.exp(-jnp.abs(z)))


def _rms_matmul_kernel(x_ref, g_ref, w_ref, *rest, w_transposed):
    side_w_ref, o_ref, side_o_ref, hn_ref = rest if len(rest) == 4 else (None, rest[0], None, rest[1])

    @pl.when(pl.program_id(1) == 0)
    def _():
        hn_ref[...] = _rms(x_ref[...], g_ref[...]).astype(BF16)
        if side_w_ref is not None:
            side_o_ref[...] = _dot(hn_ref[...], side_w_ref[...])

    o_ref[...] = (_dot_nt if w_transposed else _dot)(hn_ref[...], w_ref[...])


def rms_matmul(x, gain, w, layer, tm, tn, side_w=None, w_transposed=False):
    M, K = x.shape
    N = w.shape[1] if w_transposed else w.shape[2]
    tm, tn = min(tm, M), min(tn, N)
    assert M % tm == 0 and N % tn == 0
    w_spec = (pl.BlockSpec((None, tn, K), lambda i, j: (layer, j, 0)) if w_transposed
              else pl.BlockSpec((None, K, tn), lambda i, j: (layer, 0, j)))
    in_specs = [pl.BlockSpec((tm, K), lambda i, j: (i, 0)),
                pl.BlockSpec((1, K), lambda i, j: (0, 0)),
                w_spec]
    out_specs = pl.BlockSpec((tm, tn), lambda i, j: (i, j))
    out_shape = jax.ShapeDtypeStruct((M, N), F32)
    args = (x, gain.reshape(1, K), w)
    if side_w is not None:
        ns = side_w.shape[2]
        in_specs.append(pl.BlockSpec((None, K, ns), lambda i, j: (layer, 0, 0)))
        out_specs = [out_specs, pl.BlockSpec((tm, ns), lambda i, j: (i, 0))]
        out_shape = [out_shape, jax.ShapeDtypeStruct((M, ns), F32)]
        args += (side_w,)
    return pl.pallas_call(
        functools.partial(_rms_matmul_kernel, w_transposed=w_transposed),
        name="rms_matmul",
        grid=(M // tm, N // tn),
        in_specs=in_specs,
        out_specs=out_specs,
        out_shape=out_shape,
        scratch_shapes=[pltpu.VMEM((tm, K), BF16)],
        compiler_params=_params("parallel", "arbitrary"),
    )(*args)


def _out_proj_kernel(*refs, n_in):
    a_refs, w_refs = refs[:n_in], refs[n_in:2 * n_in]
    res_ref, o_ref = refs[2 * n_in], refs[2 * n_in + 1]
    acc = res_ref[...]
    for a_ref, w_ref in zip(a_refs, w_refs):
        acc = acc + _dot(a_ref[...], w_ref[...])
    o_ref[...] = acc


def out_proj(acts, w, layer, res, tm):
    M, N = res.shape
    tm = min(tm, M)
    assert M % tm == 0
    n_in = len(acts)
    ka = acts[0].shape[1]
    assert all(a.shape[1] == ka for a in acts) and n_in * ka == w.shape[1]
    row_block = lambda r: pl.BlockSpec((None, ka, N), lambda i: (layer, r, 0))
    weights = [w] * n_in
    in_specs = ([pl.BlockSpec((tm, ka), lambda i: (i, 0)) for _ in acts]
                + [row_block(r) for r in range(n_in)]
                + [pl.BlockSpec((tm, N), lambda i: (i, 0))])
    return pl.pallas_call(
        functools.partial(_out_proj_kernel, n_in=n_in),
        name="out_proj",
        grid=(M // tm,),
        in_specs=in_specs,
        out_specs=pl.BlockSpec((tm, N), lambda i: (i, 0)),
        out_shape=jax.ShapeDtypeStruct((M, N), F32),
        compiler_params=_params("parallel"),
    )(*acts, *weights, res)


def _ffn_kernel(x_ref, g_ref, w1_ref, w3_ref, w2_ref, o_ref, hn_ref):
    f = pl.program_id(1)

    @pl.when(f == 0)
    def _():
        x = x_ref[...]
        hn_ref[...] = _rms(x, g_ref[...]).astype(BF16)
        o_ref[...] = x

    h = hn_ref[...]
    a = _dot(h, w1_ref[...].astype(BF16))
    b = _dot(h, w3_ref[...].astype(BF16))
    u = (a * jax.nn.sigmoid(a) * b).astype(BF16)
    o_ref[...] += _dot(u, w2_ref[...].astype(BF16))


def ffn(x, gain, w1, w3, w2, layer, tm, tf):
    M, D = x.shape
    Fh = w1.shape[2]
    tm, tf = min(tm, M), min(tf, Fh)
    assert M % tm == 0 and Fh % tf == 0
    return pl.pallas_call(
        _ffn_kernel,
        name="ffn",
        grid=(M // tm, Fh // tf),
        in_specs=[pl.BlockSpec((tm, D), lambda i, f: (i, 0)),
                  pl.BlockSpec((1, D), lambda i, f: (0, 0)),
                  pl.BlockSpec((None, D, tf), lambda i, f: (layer, 0, f)),
                  pl.BlockSpec((None, D, tf), lambda i, f: (layer, 0, f)),
                  pl.BlockSpec((None, tf, D), lambda i, f: (layer, f, 0))],
        out_specs=pl.BlockSpec((tm, D), lambda i, f: (i, 0)),
        out_shape=jax.ShapeDtypeStruct((M, D), F32),
        scratch_shapes=[pltpu.VMEM((tm, D), BF16)],
        compiler_params=_params("parallel", "arbitrary"),
    )(x, gain.reshape(1, D), w1, w3, w2)


def _mem_attn_kernel(x_ref, g_ref, wq_ref, kv_ref, qg_ref, kg_ref, wo_ref, o_ref, *, heads):
    x = x_ref[0]
    h = _rms(x, g_ref[...]).astype(BF16)
    q = _dot(h, wq_ref[...])
    kv = kv_ref[0]
    w = heads * HEAD_DIM
    outs = []
    for hh in range(heads):
        sl = slice(hh * HEAD_DIM, (hh + 1) * HEAD_DIM)
        qh = (_rms(q[:, sl], qg_ref[...]) * ATTN_SCALE).astype(BF16)
        kh = _rms(kv[:, sl], kg_ref[...]).astype(BF16)
        vh = kv[:, w + hh * HEAD_DIM: w + (hh + 1) * HEAD_DIM].astype(BF16)
        s = _dot_nt(qh, kh)
        p = jnp.exp(s - jnp.max(s, axis=-1, keepdims=True))
        z = jnp.sum(p, axis=-1, keepdims=True)
        outs.append((_dot(p.astype(BF16), vh) / z).astype(BF16))
    o = jnp.concatenate(outs, axis=-1)
    o_ref[0] = x + _dot(o, wo_ref[...])


def mem_attn(x, gain, wq, kv, q_gain, k_gain, wo, layer, tm):
    B, T, D = x.shape
    L, W2 = kv.shape[1], kv.shape[2]
    W = W2 // 2
    tm = min(tm, T)
    assert T % tm == 0
    return pl.pallas_call(
        functools.partial(_mem_attn_kernel, heads=W // HEAD_DIM),
        name="mem_attn",
        grid=(B, T // tm),
        in_specs=[pl.BlockSpec((1, tm, D), lambda b, i: (b, i, 0)),
                  pl.BlockSpec((1, D), lambda b, i: (0, 0)),
                  pl.BlockSpec((None, D, W), lambda b, i: (layer, 0, 0)),
                  pl.BlockSpec((1, L, W2), lambda b, i: (b, 0, 0)),
                  pl.BlockSpec((1, HEAD_DIM), lambda b, i: (0, 0)),
                  pl.BlockSpec((1, HEAD_DIM), lambda b, i: (0, 0)),
                  pl.BlockSpec((None, W, D), lambda b, i: (layer, 0, 0))],
        out_specs=pl.BlockSpec((1, tm, D), lambda b, i: (b, i, 0)),
        out_shape=jax.ShapeDtypeStruct((B, T, D), F32),
        compiler_params=_params("parallel", "parallel"),
    )(x, gain.reshape(1, D), wq, kv, q_gain.reshape(1, HEAD_DIM), k_gain.reshape(1, HEAD_DIM), wo)


def _eye(n):
    return (lax.broadcasted_iota(jnp.int32, (n, n), 0)
            == lax.broadcasted_iota(jnp.int32, (n, n), 1)).astype(BF16)


def _ones_row(rows, n):
    return (lax.broadcasted_iota(jnp.int32, (rows, n), 0) == 0).astype(BF16)


def _split3_f32(x):
    return tuple(t.astype(F32) for t in _split3(x))


COL_GROUPS = 1


def _col_groups(n):
    w = n // COL_GROUPS
    return [slice(i * w, (i + 1) * w) for i in range(COL_GROUPS)]


def _attn_init(n):
    return tuple((jnp.full((1, c.stop - c.start), NEG_INF, F32), jnp.zeros((VT_ROWS, c.stop - c.start), F32))
                 for c in _col_groups(n))


def _attn_tile(k_tile, vt_tile, qa, carry, score_fn=None):
    out = []
    for (m, acc), c in zip(carry, _col_groups(qa.shape[1])):
        s = _dot(k_tile, qa[:, c])
        if score_fn is not None:
            s = score_fn(s, c)
        m_new = jnp.maximum(m, jnp.max(s, axis=0, keepdims=True))
        p = jnp.exp2(s - m_new).astype(BF16)
        out.append((m_new, jnp.exp2(m - m_new) * acc + _dot(vt_tile, p)))
    return tuple(out)


def _attn_finish(carry):
    return jnp.concatenate([acc for _, acc in carry], axis=1)


def _fox_gate_kernel(g_ref, b_ref, c_ref, kf_ref, *, heads):
    T = g_ref.shape[1]
    n = min(GATE_ROWS, T)
    tri = (lax.broadcasted_iota(jnp.int32, (n, n), 0)
           >= lax.broadcasted_iota(jnp.int32, (n, n), 1)).astype(BF16)
    lane = lax.broadcasted_iota(jnp.int32, (n, LANES), 1)

    def body(i, carry):
        rows = pl.ds(pl.multiple_of(i * n, n), n)
        cs = _exact_dot(tri, _log_sigmoid(g_ref[0, rows, :] + b_ref[...])) + carry
        c_ref[0, rows, :] = cs
        c2 = cs * LOG2E
        for h in range(heads):
            d1, d2, d3 = _split3_f32(c2[:, h:h + 1])
            feat = jnp.where(lane < 3, 1.0,
                             jnp.where(lane == 3, -d1, jnp.where(lane == 4, -d2,
                                                                 jnp.where(lane == 5, -d3, 0.0))))
            kf_ref[0, h, rows, :] = feat.astype(BF16)
        return cs[n - 1:n, :]

    lax.fori_loop(0, T // n, body, jnp.zeros((1, LANES), F32))


def fox_gate(g, bias, heads):
    B, T, _ = g.shape
    return pl.pallas_call(
        functools.partial(_fox_gate_kernel, heads=heads),
        name="fox_gate",
        grid=(B,),
        in_specs=[pl.BlockSpec((1, T, LANES), lambda b: (b, 0, 0)),
                  pl.BlockSpec((1, LANES), lambda b: (0, 0))],
        out_specs=[pl.BlockSpec((1, T, LANES), lambda b: (b, 0, 0)),
                   pl.BlockSpec((1, heads, T, LANES), lambda b: (b, 0, 0, 0))],
        out_shape=[jax.ShapeDtypeStruct((B, T, LANES), F32),
                   jax.ShapeDtypeStruct((B, heads, T, LANES), BF16)],
        compiler_params=_params("parallel"),
    )(g, bias)


def _fox_attn_kernel(q_ref, k_ref, v_ref, kf_ref, c_ref, qg_ref, kg_ref, o_ref, ka_ref, vt_ref, *, tq):
    qi = pl.program_id(2)
    T = k_ref.shape[1]
    eye = _eye(HEAD_DIM)

    @pl.when(qi == 0)
    def _():
        ka_ref[:, :HEAD_DIM] = _rms(k_ref[0], kg_ref[...]).astype(BF16)
        ka_ref[:, HEAD_DIM:] = kf_ref[0, 0]
        vt_ref[:HEAD_DIM, :] = _dot_nt(eye, v_ref[0].astype(BF16)).astype(BF16)
        vt_ref[HEAD_DIM:, :] = _ones_row(VT_ROWS - HEAD_DIM, T)

    q0 = pl.multiple_of(qi * tq, tq)
    qn = (_rms(q_ref[0], qg_ref[...]) * (ATTN_SCALE * LOG2E)).astype(BF16)
    qt = _dot_nt(eye, qn).astype(BF16)
    c1, c2, c3 = _split3_f32(c_ref[0, :, pl.ds(q0, tq)] * LOG2E)
    r = lax.broadcasted_iota(jnp.int32, (16, tq), 0)
    feat = jnp.where(r == 0, c1, jnp.where(r == 1, c2, jnp.where(r == 2, c3,
                                                                 jnp.where(r < 6, 1.0, 0.0))))
    qa = jnp.concatenate([qt, feat.astype(BF16), jnp.zeros((AUG - HEAD_DIM - 16, tq), BF16)], axis=0)

    kpos = lax.broadcasted_iota(jnp.int32, (tq, 1), 0)
    tpos = lax.broadcasted_iota(jnp.int32, (1, tq), 1)

    def tile(j, carry, score_fn=None):
        k0 = pl.multiple_of(j * tq, tq)
        return _attn_tile(ka_ref[pl.ds(k0, tq), :], vt_ref[:, pl.ds(k0, tq)], qa, carry, score_fn)

    carry = tile(qi, _attn_init(tq), lambda s, c: jnp.where(kpos <= tpos[:, c], s, NEG_INF))
    acc = _attn_finish(lax.fori_loop(0, qi, tile, carry))
    ot = (acc[:HEAD_DIM] / acc[HEAD_DIM:HEAD_DIM + 1]).astype(BF16)
    for c in range(tq // LANES):
        cs = slice(c * LANES, (c + 1) * LANES)
        o_ref[0, cs, :] = _dot_nt(eye, ot[:, cs]).astype(o_ref.dtype)


def fox_attn(proj, kfeat, c_rows, q_gain, k_gain, tq):
    B, T, _ = proj.shape
    H = FOX_HEADS
    tq = min(tq, T)
    assert T % tq == 0 and tq % LANES == 0
    return pl.pallas_call(
        functools.partial(_fox_attn_kernel, tq=tq),
        name="fox_attn",
        grid=(B, H, T // tq),
        in_specs=[pl.BlockSpec((1, tq, HEAD_DIM), lambda b, h, i: (b, i, h)),
                  pl.BlockSpec((1, T, HEAD_DIM), lambda b, h, i: (b, 0, H + h)),
                  pl.BlockSpec((1, T, HEAD_DIM), lambda b, h, i: (b, 0, 2 * H + h)),
                  pl.BlockSpec((1, 1, T, LANES), lambda b, h, i: (b, h, 0, 0)),
                  pl.BlockSpec((1, 1, T), lambda b, h, i: (b * H + h, 0, 0)),
                  pl.BlockSpec((1, HEAD_DIM), lambda b, h, i: (0, 0)),
                  pl.BlockSpec((1, HEAD_DIM), lambda b, h, i: (0, 0))],
        out_specs=pl.BlockSpec((1, tq, HEAD_DIM), lambda b, h, i: (b, i, h)),
        out_shape=jax.ShapeDtypeStruct((B, T, H * HEAD_DIM), BF16),
        scratch_shapes=[pltpu.VMEM((T, AUG), BF16), pltpu.VMEM((VT_ROWS, T), BF16)],
        compiler_params=_params("parallel", "parallel", "arbitrary"),
    )(proj, proj, proj, kfeat, c_rows, q_gain.reshape(1, HEAD_DIM), k_gain.reshape(1, HEAD_DIM))


def _hgrn_kernel(q_ref, f_ref, i_ref, g_ref, la_ref, l1_ref, oml_ref, og_ref, o_ref, *, bs, chunk, span):
    T = q_ref.shape[1]
    R = LANES
    ri = lax.broadcasted_iota(jnp.int32, (R, R), 0)
    ci = lax.broadcasted_iota(jnp.int32, (R, R), 1)

    def prefix_sel(n):
        return (((ri // n) == (ci // n)) & (ri >= ci)).astype(BF16)

    s_idx = lax.broadcasted_iota(jnp.int32, (bs, HEAD_DIM), 0)
    tril = (lax.broadcasted_iota(jnp.int32, (chunk, chunk), 0)
            >= lax.broadcasted_iota(jnp.int32, (chunk, chunk), 1))

    def gates(row0, n):
        rows = pl.ds(row0, n)
        z = f_ref[0, rows, :]
        a = la_ref[0]
        y = l1_ref[0] + _log_sigmoid(z)
        log_f = jnp.maximum(a, y) + jnp.log1p(jnp.exp(-jnp.abs(a - y)))
        hq = q_ref[0, rows, :]
        return log_f, oml_ref[0] * jax.nn.sigmoid(-z), hq * jax.nn.sigmoid(hq), i_ref[0, rows, :]

    def prefixes(log_f, n):
        sel = prefix_sel(n)
        pieces = _split3(log_f)
        b = jnp.concatenate([sum(_dot(sel, p[g * R:(g + 1) * R]) for p in pieces)
                             for g in range(log_f.shape[0] // R)], axis=0)
        bt = jnp.concatenate([jnp.broadcast_to(b[(u + 1) * n - 1:(u + 1) * n, :], (n, HEAD_DIM))
                              for u in range(log_f.shape[0] // n)], axis=0)
        return b, bt

    def finish(row0, n, o):
        rows = pl.ds(row0, n)
        hg = g_ref[0, rows, :]
        o_ref[0, rows, :] = (_rms(o, og_ref[...]) * (hg * jax.nn.sigmoid(hg))).astype(o_ref.dtype)

    def matmul_path(row0, st, kk, qq, vv, b, bt):
        qe = (qq * jnp.exp(b)).astype(BF16)
        k_inv = (kk * jnp.exp(-b)).astype(BF16)
        ke = (kk * jnp.exp(bt - b)).astype(BF16)
        dec = jnp.exp(bt)
        vb = vv.astype(BF16)
        sls = [slice(u * chunk, (u + 1) * chunk) for u in range(span // chunk)]
        scores = [jnp.where(tril, _dot_nt(qe[sl], k_inv[sl]), 0.0).astype(BF16) for sl in sls]
        intra = [_dot(a, vb[sl]) for a, sl in zip(scores, sls)]
        update = [_dot_tn(vb[sl], ke[sl]) for sl in sls]
        outs = []
        for sl, o_in, upd in zip(sls, intra, update):
            outs.append(_dot_nt(qe[sl], st.astype(BF16)) + o_in)
            st = dec[sl.start:sl.start + 1, :] * st + upd
        finish(row0, span, jnp.concatenate(outs, axis=0))
        return st

    def pairwise_step(i, st, base):
        row0 = pl.multiple_of(base + i * R, R)
        log_f, kk, qq, vv = gates(row0, R)
        b, bt = prefixes(log_f, bs)
        qe = (qq * jnp.exp(b)).astype(BF16)
        ke = (kk * jnp.exp(bt - b)).astype(BF16)
        dec = jnp.exp(bt)
        vb = vv.astype(BF16)
        outs = []
        for u in range(R // bs):
            sl = slice(u * bs, (u + 1) * bs)
            bu, vu, qk, ku = b[sl], vv[sl], qq[sl], kk[sl]
            intra = []
            for t in range(bs):
                w = jnp.exp(jnp.where(s_idx <= t, bu[t:t + 1, :] - bu, NEG_INF))
                a_t = jnp.sum(w * (qk[t:t + 1, :] * ku), axis=-1, keepdims=True)
                intra.append(jnp.sum(a_t * vu, axis=0, keepdims=True))
            outs.append(_dot_nt(qe[sl], st.astype(BF16)) + jnp.concatenate(intra, axis=0))
            st = dec[u * bs:u * bs + 1, :] * st + _dot_tn(vb[sl], ke[sl])
        finish(row0, R, jnp.concatenate(outs, axis=0))
        return st

    def step(i, st):
        row0 = pl.multiple_of(i * span, span)
        log_f, kk, qq, vv = gates(row0, span)
        b, bt = prefixes(log_f, chunk)
        return lax.cond(
            jnp.min(bt) >= HGRN_SAFE_LOG_DECAY,
            lambda st: matmul_path(row0, st, kk, qq, vv, b, bt),
            lambda st: lax.fori_loop(0, span // R, functools.partial(pairwise_step, base=row0), st),
            st)

    lax.fori_loop(0, T // span, step, jnp.zeros((HEAD_DIM, HEAD_DIM), F32))


def hgrn(proj, col0, log_lb, log1m_lb, one_m_lb, o_gain):
    B, T, _ = proj.shape
    H = HGRN_HEADS
    row = lambda off: pl.BlockSpec((1, T, HEAD_DIM), lambda b, h: (b, 0, col0 + off * H + h))
    par = pl.BlockSpec((1, 1, HEAD_DIM), lambda b, h: (h, 0, 0))
    log_lb, log1m_lb, one_m_lb = (p.reshape(H, 1, HEAD_DIM) for p in (log_lb, log1m_lb, one_m_lb))
    return pl.pallas_call(
        functools.partial(_hgrn_kernel, bs=HGRN_BLOCK, chunk=HGRN_CHUNK, span=min(HGRN_SPAN, T)),
        name="hgrn",
        grid=(B, H),
        in_specs=[row(0), row(1), row(2), row(3), par, par, par,
                  pl.BlockSpec((1, HEAD_DIM), lambda b, h: (0, 0))],
        out_specs=pl.BlockSpec((1, T, HEAD_DIM), lambda b, h: (b, 0, h)),
        out_shape=jax.ShapeDtypeStruct((B, T, H * HEAD_DIM), BF16),
        compiler_params=_params("parallel", "parallel"),
    )(proj, proj, proj, proj, log_lb, log1m_lb, one_m_lb, o_gain.reshape(1, HEAD_DIM))


def _compress_kernel(t_ref, pe_ref, w1_ref, w2_ref, g_ref, o_ref, *, normalize):
    N = t_ref.shape[1] // CMP_STRIDE
    top = bot = jnp.zeros((N, w1_ref.shape[2]), F32)
    for l in range(CMP_STRIDE):
        xl = t_ref[0, pl.ds(l, N, stride=CMP_STRIDE), :]
        top = top + _dot((xl + pe_ref[l:l + 1, :]).astype(BF16), w1_ref[l])
        bot = bot + _dot((xl + pe_ref[CMP_STRIDE + l:CMP_STRIDE + l + 1, :]).astype(BF16),
                         w1_ref[CMP_STRIDE + l])
    hid = top + pltpu.roll(bot, N - 1, 0)
    out = _dot(jax.nn.gelu(hid).astype(BF16), w2_ref[...])
    if normalize:
        out = _rms(out, g_ref[...])
    valid = lax.broadcasted_iota(jnp.int32, out.shape, 0) < N - 1
    o_ref[0, 0] = jnp.where(valid, out, 0.0)


def compress(proj, col0, pe, w1, w2, layer, gain, normalize):
    B, T, _ = proj.shape
    G = NSA_KV_HEADS
    N = T // CMP_STRIDE
    Hc = w1.shape[3]
    assert CMP_BLOCK == 2 * CMP_STRIDE
    return pl.pallas_call(
        functools.partial(_compress_kernel, normalize=normalize),
        name="nsa_compress",
        grid=(B, G),
        in_specs=[pl.BlockSpec((1, T, HEAD_DIM), lambda b, g: (b, 0, col0 + g)),
                  pl.BlockSpec((None, CMP_BLOCK, HEAD_DIM), lambda b, g: (layer, 0, 0)),
                  pl.BlockSpec((None, CMP_BLOCK, HEAD_DIM, Hc), lambda b, g: (layer, 0, 0, 0)),
                  pl.BlockSpec((None, Hc, HEAD_DIM), lambda b, g: (layer, 0, 0)),
                  pl.BlockSpec((1, HEAD_DIM), lambda b, g: (0, 0))],
        out_specs=pl.BlockSpec((1, 1, N, HEAD_DIM), lambda b, g: (b, g, 0, 0)),
        out_shape=jax.ShapeDtypeStruct((B, G, N, HEAD_DIM), F32),
        compiler_params=_params("parallel", "parallel"),
    )(proj, pe, w1, w2, gain.reshape(1, HEAD_DIM))


_SEL_LANE0 = 64
_CMP_OV_ROW0 = VT_ROWS


def _nsa_kernel(q_ref, ks_ref, vs_ref, kw_ref, vw_ref, kc_ref, vc_ref, gt_ref, qg_ref, kg_ref,
                o_ref, ksa_ref, kwa_ref, vst_ref, vwt_ref, kca_ref, vcov_ref, tiles_ref, *, tq, tk, n_sel):
    g = pl.program_id(1)
    qi = pl.program_id(2)
    T = ks_ref.shape[1]
    J = NSA_GROUP
    n_slc = T // SLC_BLOCK
    nc = kc_ref.shape[2]
    n_cmp = nc - 1
    cols = J * tq
    eye = _eye(HEAD_DIM)

    @pl.when(qi == 0)
    def _():
        row = lax.broadcasted_iota(jnp.int32, (T, LANES), 0)
        lane = lax.broadcasted_iota(jnp.int32, (T, LANES), 1)
        blk = row // SLC_BLOCK
        feats = jnp.where(lane < 3, blk,
                          jnp.where(lane < 6, row % SLC_BLOCK,
                                    jnp.where(lane < 9, 1,
                                              jnp.where(lane - _SEL_LANE0 == blk, 1, 0))))
        feats = feats.astype(F32).astype(BF16)
        ksa_ref[:, :HEAD_DIM] = _rms(ks_ref[0], kg_ref[1:2, :]).astype(BF16)
        ksa_ref[:, HEAD_DIM:] = feats
        kwa_ref[:, :HEAD_DIM] = _rms(kw_ref[0], kg_ref[2:3, :]).astype(BF16)
        kwa_ref[:, HEAD_DIM:] = feats
        vst_ref[:HEAD_DIM, :] = _dot_nt(eye, vs_ref[0].astype(BF16)).astype(BF16)
        vst_ref[HEAD_DIM:, :] = _ones_row(VT_ROWS - HEAD_DIM, T)
        vwt_ref[:HEAD_DIM, :] = _dot_nt(eye, vw_ref[0].astype(BF16)).astype(BF16)
        vwt_ref[HEAD_DIM:, :] = _ones_row(VT_ROWS - HEAD_DIM, T)
        n_r = lax.broadcasted_iota(jnp.int32, (nc, LANES), 0)
        n_l = lax.broadcasted_iota(jnp.int32, (nc, LANES), 1)
        per = SLC_BLOCK // CMP_STRIDE
        cfe = jnp.where(n_l < 3, (n_r // per).astype(F32),
                        jnp.where(n_l < 6, (CMP_STRIDE * (n_r % per)).astype(F32) + 0.5 * (CMP_BLOCK - 1),
                                  jnp.where(n_l < 9, 1.0, 0.0)))
        kca_ref[:, :HEAD_DIM] = kc_ref[0, 0].astype(BF16)
        kca_ref[:, HEAD_DIM:] = cfe.astype(BF16)
        vcov_ref[:HEAD_DIM, :] = _dot_nt(eye, vc_ref[0, 0].astype(BF16)).astype(BF16)
        vcov_ref[HEAD_DIM:_CMP_OV_ROW0, :] = _ones_row(VT_ROWS - HEAD_DIM, nc)
        m_r = lax.broadcasted_iota(jnp.int32, (SLC_BLOCK, nc), 0)
        n_c = lax.broadcasted_iota(jnp.int32, (SLC_BLOCK, nc), 1)
        vcov_ref[_CMP_OV_ROW0:, :] = ((n_c * CMP_STRIDE <= m_r * SLC_BLOCK + (SLC_BLOCK - 1))
                                      & (n_c * CMP_STRIDE + (CMP_BLOCK - 1) >= m_r * SLC_BLOCK)
                                      & (n_c < n_cmp) & (m_r < n_slc)).astype(BF16)

    q0 = pl.multiple_of(qi * tq, tq)
    qraw = q_ref[0]
    qt = jnp.concatenate(
        [_dot_nt(eye, (_rms(qraw[:, j * HEAD_DIM:(j + 1) * HEAD_DIM], qg_ref[...])
                       * (ATTN_SCALE * LOG2E)).astype(BF16)).astype(BF16)
         for j in range(J)], axis=1)

    col = lax.broadcasted_iota(jnp.int32, (1, cols), 1)
    tt = q0 + col % tq
    slope2 = jnp.exp2(-0.5 * (g * J + col // tq + 1).astype(F32)) * LOG2E
    s1, s2, s3 = _split3_f32(slope2)
    r1, r2, r3 = _split3_f32(-slope2 * tt.astype(F32))
    r16 = lax.broadcasted_iota(jnp.int32, (16, cols), 0)
    feat = jnp.zeros((16, cols), F32)
    for i, v in enumerate((s1 * SLC_BLOCK, s2 * SLC_BLOCK, s3 * SLC_BLOCK, s1, s2, s3, r1, r2, r3)):
        feat = jnp.where(r16 == i, v, feat)
    feat = feat.astype(BF16)
    qa_base = jnp.concatenate([qt, feat, jnp.zeros((AUG - HEAD_DIM - 16, cols), BF16)], axis=0)

    n_col = lax.broadcasted_iota(jnp.int32, (nc, 1), 0)
    cmp_end = n_col * CMP_STRIDE + (CMP_BLOCK - 1)
    kca = kca_ref[...]
    r_hi, r_lo = [], []
    for c in _col_groups(cols):
        s = _dot(kca, qa_base[:, c])
        s = jnp.where(cmp_end <= tt[:, c], s, NEG_INF)
        m = jnp.max(s, axis=0, keepdims=True)
        p = jnp.exp2(s - jnp.where(m == NEG_INF, 0.0, m))
        p_hi = p.astype(BF16)
        r_hi.append(_dot(vcov_ref[...], p_hi))
        r_lo.append(_dot(vcov_ref[HEAD_DIM:, :], (p - p_hi.astype(F32)).astype(BF16)))
    r_hi = jnp.concatenate(r_hi, axis=1)
    r_lo = jnp.concatenate(r_lo, axis=1)
    z = r_hi[HEAD_DIM:HEAD_DIM + 1] + r_lo[0:1]
    inv_z = 1.0 / jnp.where(z > 0, z, 1.0)

    wq = min(tq, WINDOW_Q)
    span = WINDOW + wq
    d = (lax.broadcasted_iota(jnp.int32, (span, 1), 0)
         - lax.broadcasted_iota(jnp.int32, (1, wq), 1))
    parts = []
    for c in range(tq // wq):
        k0w = pl.multiple_of(jnp.maximum(q0 + c * wq - WINDOW, 0), wq)
        off = q0 + c * wq - k0w
        bias = jnp.where(d <= off, jnp.where(d > off - WINDOW, 0.0, NEG_INF), NEG_INF)
        bias = jnp.concatenate([bias] * J, axis=1)
        qa_c = jnp.concatenate([qa_base[:, j * tq + c * wq:j * tq + (c + 1) * wq] for j in range(J)], axis=1)
        parts.append(_attn_finish(_attn_tile(
            kwa_ref[pl.ds(k0w, span), :], vwt_ref[:, pl.ds(k0w, span)], qa_c, _attn_init(J * wq),
            lambda s, sub, bias=bias: s + bias[:, sub])))
    acc_w = jnp.concatenate([p[:, j * wq:(j + 1) * wq] for j in range(J) for p in parts], axis=1)

    imp4 = (r_hi[_CMP_OV_ROW0:] + r_lo[_CMP_OV_ROW0 - HEAD_DIM:]) * inv_z
    imp = imp4[:, 0:tq]
    for j in range(1, J):
        imp = imp + imp4[:, j * tq:(j + 1) * tq]
    blk = lax.broadcasted_iota(jnp.int32, (SLC_BLOCK, tq), 0)
    t_q = q0 + lax.broadcasted_iota(jnp.int32, (1, tq), 1)
    imp = jnp.where(blk * SLC_BLOCK > t_q, NEG_INF, imp)
    imp = jnp.where(blk == t_q // SLC_BLOCK, jnp.inf, jnp.where(blk == 0, jnp.inf, imp))
    blk_f = blk.astype(F32)
    vals = jnp.where(blk >= n_slc, NEG_INF, imp)
    picked = jnp.zeros((SLC_BLOCK, tq), F32)
    for _ in range(n_sel):
        best = jnp.max(vals, axis=0, keepdims=True)
        first = jnp.min(jnp.where(vals == best, blk_f, float(SLC_BLOCK)), axis=0, keepdims=True)
        hit = blk_f == first
        picked = jnp.where(hit, 1.0, picked)
        vals = jnp.where(hit, NEG_INF, vals)
    sel = (picked > 0.5) & (blk < n_slc)
    not_sel = jnp.where(sel, 0.0, -MASK_BIG).astype(BF16)
    qa_slc = jnp.concatenate([qt, feat, jnp.zeros((_SEL_LANE0 - 16, cols), BF16)]
                             + [jnp.concatenate([not_sel] * J, axis=1)], axis=0)

    def slc_tile(jt, carry, score_fn=None):
        k0 = pl.multiple_of(jt * tk, tk)
        return _attn_tile(ksa_ref[pl.ds(k0, tk), :], vst_ref[:, pl.ds(k0, tk)], qa_slc, carry, score_fn)

    jd = q0 // tk
    kpos = jd * tk + lax.broadcasted_iota(jnp.int32, (tk, 1), 0)
    carry = slc_tile(jd, _attn_init(cols), lambda s, c: jnp.where(kpos <= tt[:, c], s, NEG_INF))
    sel_f = jnp.where(sel, 1.0, 0.0)
    per_tile = tk // SLC_BLOCK
    n_live = jnp.int32(0)
    for jt in range(T // tk - 1):
        live = (jnp.max(sel_f[jt * per_tile:(jt + 1) * per_tile, :]) > 0.5) & (jt < jd)
        tiles_ref[n_live] = jt
        n_live = n_live + live.astype(jnp.int32)
    acc_s = _attn_finish(lax.fori_loop(0, n_live, lambda i, c: slc_tile(tiles_ref[i], c), carry))

    gates_t = jnp.transpose(jax.nn.sigmoid(gt_ref[0]))
    gate = lambda c: jnp.concatenate([gates_t[3 * j + c:3 * j + c + 1, :] for j in range(J)], axis=1)
    o_t = ((gate(0) * inv_z) * r_hi[:HEAD_DIM]
           + (gate(1) / acc_s[HEAD_DIM:HEAD_DIM + 1]) * acc_s[:HEAD_DIM]
           + (gate(2) / acc_w[HEAD_DIM:HEAD_DIM + 1]) * acc_w[:HEAD_DIM]).astype(BF16)
    for j in range(J):
        for c in range(tq // LANES):
            o_ref[0, c * LANES:(c + 1) * LANES, j * HEAD_DIM:(j + 1) * HEAD_DIM] = _dot_nt(
                eye, o_t[:, j * tq + c * LANES:j * tq + (c + 1) * LANES]).astype(o_ref.dtype)


def nsa_attn(proj, gates, k_cmp, v_cmp, q_gain, k_gain, tq, tk):
    B, T, _ = proj.shape
    G, J = NSA_KV_HEADS, NSA_GROUP
    tk = min(tk, T)
    nc = k_cmp.shape[2]
    assert T % tk == 0 and tk % tq == 0 and T >= WINDOW + tq
    assert T // SLC_BLOCK <= LANES - _SEL_LANE0 and SLC_BLOCK <= LANES - _SEL_LANE0
    qcols = NSA_HEADS
    kv = lambda slab: pl.BlockSpec((1, T, HEAD_DIM), lambda b, g, i: (b, 0, qcols + slab * G + g))
    cmp_spec = pl.BlockSpec((1, 1, nc, HEAD_DIM), lambda b, g, i: (b, g, 0, 0))
    return pl.pallas_call(
        functools.partial(_nsa_kernel, tq=tq, tk=tk, n_sel=min(SLC_TOPK, T // SLC_BLOCK)),
        name="nsa_attn",
        grid=(B, G, T // tq),
        in_specs=[pl.BlockSpec((1, tq, J * HEAD_DIM), lambda b, g, i: (b, i, g)),
                  kv(2), kv(3), kv(4), kv(5), cmp_spec, cmp_spec,
                  pl.BlockSpec((1, tq, LANES), lambda b, g, i: (b, i, g)),
                  pl.BlockSpec((1, HEAD_DIM), lambda b, g, i: (0, 0)),
                  pl.BlockSpec((3, HEAD_DIM), lambda b, g, i: (0, 0))],
        out_specs=pl.BlockSpec((1, tq, J * HEAD_DIM), lambda b, g, i: (b, i, g)),
        out_shape=jax.ShapeDtypeStruct((B, T, NSA_HEADS * HEAD_DIM), BF16),
        scratch_shapes=[pltpu.VMEM((T, AUG), BF16), pltpu.VMEM((T, AUG), BF16),
                        pltpu.VMEM((VT_ROWS, T), BF16), pltpu.VMEM((VT_ROWS, T), BF16),
                        pltpu.VMEM((nc, AUG), BF16),
                        pltpu.VMEM((VT_ROWS + SLC_BLOCK, nc), BF16),
                        pltpu.SMEM((T // tk,), jnp.int32)],
        compiler_params=_params("parallel", "parallel", "arbitrary"),
    )(proj, proj, proj, proj, proj, k_cmp, v_cmp, gates, q_gain.reshape(1, HEAD_DIM), k_gain)


def even_weights(w_in, w_out):
    fw = FOX_HEADS * HEAD_DIM
    w_main = jnp.concatenate([w_in[:, :, :3 * fw], w_in[:, :, 3 * fw + FOX_HEADS:]], axis=2).astype(BF16)
    w_gate = jnp.pad(w_in[:, :, 3 * fw:3 * fw + FOX_HEADS], ((0, 0), (0, 0), (0, LANES - FOX_HEADS)))
    return w_main, w_gate.astype(BF16), w_out.astype(BF16)


def even_mixer(x, gain, weights, e, f_bias, q_gain, k_gain, lb, o_gain):
    B, T, D = x.shape
    xf = x.reshape(B * T, D)
    w_main, w_gate, w_out = weights
    proj, gate = rms_matmul(xf, gain, w_main, e, 1024, 1024, side_w=w_gate)
    proj, gate = proj.reshape(B, T, -1), gate.reshape(B, T, LANES)
    f_bias = jnp.pad(f_bias.reshape(1, FOX_HEADS).astype(F32), ((0, 0), (0, LANES - FOX_HEADS)))
    c, kfeat = fox_gate(gate, f_bias, FOX_HEADS)
    c_rows = c[:, :, :FOX_HEADS].transpose(0, 2, 1).reshape(B * FOX_HEADS, 1, T)
    o_fox = fox_attn(proj, kfeat, c_rows, q_gain, k_gain, 1024)
    lb = lb.astype(F32).reshape(HGRN_HEADS, HEAD_DIM)
    o_h = hgrn(proj, 3 * FOX_HEADS, jnp.log(lb), jnp.log1p(-lb), 1.0 - lb, o_gain)
    out = out_proj([o_fox.reshape(B * T, -1), o_h.reshape(B * T, -1)], w_out, e, xf, 512)
    return out.reshape(B, T, D)


def nsa_weights(w_in, w_out, w1_k, w2_k, w1_v, w2_v):
    L, D, _ = w_in.shape
    G, J = NSA_KV_HEADS, NSA_GROUP
    n_main = (NSA_HEADS + 6 * G) * HEAD_DIM
    w_gate = w_in[:, :, n_main:].reshape(L, D, G, 3 * J)
    w_gate = jnp.pad(w_gate, ((0, 0), (0, 0), (0, 0), (0, LANES - 3 * J))).reshape(L, D, G * LANES)
    return (jnp.swapaxes(w_in[:, :, :n_main], 1, 2).astype(BF16), w_gate.astype(BF16), w_out.astype(BF16),
            w1_k.astype(BF16), w2_k.astype(BF16), w1_v.astype(BF16), w2_v.astype(BF16))


def nsa_mixer(x, gain, weights, o, q_gain, k_gain, pe_k, pe_v):
    B, T, D = x.shape
    G = NSA_KV_HEADS
    xf = x.reshape(B * T, D)
    w_main, w_gate, w_out, w1_k, w2_k, w1_v, w2_v = weights
    proj, gates = rms_matmul(xf, gain, w_main, o, 1024, 1024, side_w=w_gate, w_transposed=True)
    proj, gates = proj.reshape(B, T, -1), gates.reshape(B, T, G * LANES)
    k_cmp = compress(proj, NSA_HEADS, pe_k, w1_k, w2_k, o, k_gain[0], True)
    v_cmp = compress(proj, NSA_HEADS + G, pe_v, w1_v, w2_v, o, k_gain[0], False)
    att = nsa_attn(proj, gates, k_cmp, v_cmp, q_gain, k_gain, 512, 512)
    out = out_proj([att.reshape(B * T, -1)], w_out, o, xf, 512)
    return out.reshape(B, T, D)


def kernel(x, mem, norm_mix, norm_mem, norm_ffn, mem_in_gain, even_w_in, even_w_out, fox_f_bias,
           fox_q_gain, fox_k_gain, hgrn_lb_logits, hgrn_o_gain, odd_w_in, odd_w_out, nsa_q_gain,
           nsa_k_gain, cmp_pe_k, cmp_w1_k, cmp_w2_k, cmp_pe_v, cmp_w1_v, cmp_w2_v, mem_wq, mem_wkv,
           mem_wo, mem_q_gain, mem_k_gain, ffn_w1, ffn_w3, ffn_w2):
    B, T, D = x.shape
    depth = norm_mix.shape[0]
    L = mem.shape[1]
    lb_cum = jnp.cumsum(jax.nn.softmax(hgrn_lb_logits.astype(F32), axis=0), axis=0)
    hgrn_lb = lb_cum - lb_cum[0:1]
    memf = mem.reshape(B * L, D)
    even_w = even_weights(even_w_in, even_w_out)
    nsa_w = nsa_weights(odd_w_in, odd_w_out, cmp_w1_k, cmp_w2_k, cmp_w1_v, cmp_w2_v)
    mem_wq, mem_wkv, mem_wo = (w.astype(BF16) for w in (mem_wq, mem_wkv, mem_wo))
    for layer in range(depth):
        if layer % 2 == 0:
            e = layer // 2
            x = even_mixer(x, norm_mix[layer], even_w, e, fox_f_bias[e], fox_q_gain[e], fox_k_gain[e],
                           hgrn_lb[e], hgrn_o_gain[e])
        else:
            o = layer // 2
            x = nsa_mixer(x, norm_mix[layer], nsa_w, o, nsa_q_gain[o], nsa_k_gain[o], cmp_pe_k, cmp_pe_v)
        kv = rms_matmul(memf, mem_in_gain[layer], mem_wkv, layer, 256, 1024)
        x = mem_attn(x, norm_mem[layer], mem_wq, kv.reshape(B, L, -1), mem_q_gain[layer],
                     mem_k_gain[layer], mem_wo, layer, 1024)
        x = ffn(x.reshape(B * T, D), norm_ffn[layer], ffn_w1, ffn_w3, ffn_w2, layer,
                1024, 256).reshape(B, T, D)
    return x
```

```python
import functools

import jax
import jax.numpy as jnp
from jax import lax
from jax.experimental import pallas as pl
from jax.experimental.pallas import tpu as pltpu

F32 = jnp.float32
BF16 = jnp.bfloat16
NEG_INF = float("-inf")

EPS = 1e-6
HEAD_DIM = 128
ATTN_SCALE = HEAD_DIM ** -0.5
LANES = 128
FOX_HEADS = 8
HGRN_HEADS = 8
HGRN_BLOCK = 16
HGRN_CHUNK = 64
HGRN_SPAN = 2048
GATE_ROWS = 512
HGRN_SAFE_LOG_DECAY = -80.0
NSA_HEADS = 16
NSA_KV_HEADS = 4
NSA_GROUP = NSA_HEADS // NSA_KV_HEADS
CMP_BLOCK = 32
CMP_STRIDE = 16
SLC_BLOCK = 64
SLC_TOPK = 16
WINDOW = 512
WINDOW_Q = 256
MEM_HEADS = 4
V7X_VMEM_LIMIT = 56 * 1024 * 1024
LOG2E = 1.4426950408889634
MASK_BIG = 1e30
SOFTMAX_BOUND_LIMIT = 40.0
AUG = 2 * LANES
VT_ROWS = HEAD_DIM + 16


def _params(*sem):
    return pltpu.CompilerParams(dimension_semantics=sem, vmem_limit_bytes=V7X_VMEM_LIMIT)


def _rms(x, gain):
    return x * lax.rsqrt(jnp.mean(x * x, axis=-1, keepdims=True) + EPS) * gain


def _dot(a, b):
    return jnp.dot(a, b, preferred_element_type=F32)


def _dot_nt(a, b):
    return lax.dot_general(a, b, (((1,), (1,)), ((), ())), preferred_element_type=F32)


def _dot_tn(a, b):
    return lax.dot_general(a, b, (((0,), (0,)), ((), ())), preferred_element_type=F32)


def _split3(x):
    x1 = x.astype(BF16)
    r1 = x - x1.astype(F32)
    x2 = r1.astype(BF16)
    return x1, x2, (r1 - x2.astype(F32)).astype(BF16)


def _exact_dot(sel, x):
    x1, x2, x3 = _split3(x)
    return _dot(sel, x1) + _dot(sel, x2) + _dot(sel, x3)


def _log_sigmoid(z):
    return jnp.minimum(z, 0.0) - jnp.log1p(jnp.exp(-jnp.abs(z)))


def _rms_matmul_kernel(x_ref, g_ref, w_ref, *rest, w_transposed):
    side_w_ref, o_ref, side_o_ref, hn_ref = rest if len(rest) == 4 else (None, rest[0], None, rest[1])

    @pl.when(pl.program_id(1) == 0)
    def _():
        hn_ref[...] = _rms(x_ref[...], g_ref[...]).astype(BF16)
        if side_w_ref is not None:
            side_o_ref[...] = _dot(hn_ref[...], side_w_ref[...])

    o_ref[...] = (_dot_nt if w_transposed else _dot)(hn_ref[...], w_ref[...])


def rms_matmul(x, gain, w, layer, tm, tn, side_w=None, w_transposed=False):
    M, K = x.shape
    N = w.shape[1] if w_transposed else w.shape[2]
    tm, tn = min(tm, M), min(tn, N)
    assert M % tm == 0 and N % tn == 0
    w_spec = (pl.BlockSpec((None, tn, K), lambda i, j: (layer, j, 0)) if w_transposed
              else pl.BlockSpec((None, K, tn), lambda i, j: (layer, 0, j)))
    in_specs = [pl.BlockSpec((tm, K), lambda i, j: (i, 0)),
                pl.BlockSpec((1, K), lambda i, j: (0, 0)),
                w_spec]
    out_specs = pl.BlockSpec((tm, tn), lambda i, j: (i, j))
    out_shape = jax.ShapeDtypeStruct((M, N), F32)
    args = (x, gain.reshape(1, K), w)
    if side_w is not None:
        ns = side_w.shape[2]
        in_specs.append(pl.BlockSpec((None, K, ns), lambda i, j: (layer, 0, 0)))
        out_specs = [out_specs, pl.BlockSpec((tm, ns), lambda i, j: (i, 0))]
        out_shape = [out_shape, jax.ShapeDtypeStruct((M, ns), F32)]
        args += (side_w,)
    return pl.pallas_call(
        functools.partial(_rms_matmul_kernel, w_transposed=w_transposed),
        name="rms_matmul",
        grid=(M // tm, N // tn),
        in_specs=in_specs,
        out_specs=out_specs,
        out_shape=out_shape,
        scratch_shapes=[pltpu.VMEM((tm, K), BF16)],
        compiler_params=_params("parallel", "arbitrary"),
    )(*args)


def _out_proj_kernel(*refs, n_in):
    a_refs, w_refs = refs[:n_in], refs[n_in:2 * n_in]
    res_ref, o_ref = refs[2 * n_in], refs[2 * n_in + 1]
    acc = res_ref[...]
    for a_ref, w_ref in zip(a_refs, w_refs):
        acc = acc + _dot(a_ref[...], w_ref[...])
    o_ref[...] = acc


def out_proj(acts, w, layer, res, tm):
    M, N = res.shape
    tm = min(tm, M)
    assert M % tm == 0
    n_in = len(acts)
    ka = acts[0].shape[1]
    assert all(a.shape[1] == ka for a in acts) and n_in * ka == w.shape[1]
    row_block = lambda r: pl.BlockSpec((None, ka, N), lambda i: (layer, r, 0))
    weights = [w] * n_in
    in_specs = ([pl.BlockSpec((tm, ka), lambda i: (i, 0)) for _ in acts]
                + [row_block(r) for r in range(n_in)]
                + [pl.BlockSpec((tm, N), lambda i: (i, 0))])
    return pl.pallas_call(
        functools.partial(_out_proj_kernel, n_in=n_in),
        name="out_proj",
        grid=(M // tm,),
        in_specs=in_specs,
        out_specs=pl.BlockSpec((tm, N), lambda i: (i, 0)),
        out_shape=jax.ShapeDtypeStruct((M, N), F32),
        compiler_params=_params("parallel"),
    )(*acts, *weights, res)


def _ffn_kernel(x_ref, g_ref, w1_ref, w3_ref, w2_ref, o_ref, hn_ref):
    f = pl.program_id(1)

    @pl.when(f == 0)
    def _():
        x = x_ref[...]
        hn_ref[...] = _rms(x, g_ref[...]).astype(BF16)
        o_ref[...] = x

    h = hn_ref[...]
    a = _dot(h, w1_ref[...].astype(BF16))
    b = _dot(h, w3_ref[...].astype(BF16))
    u = (a * jax.nn.sigmoid(a) * b).astype(BF16)
    o_ref[...] += _dot(u, w2_ref[...].astype(BF16))


def ffn(x, gain, w1, w3, w2, layer, tm, tf):
    M, D = x.shape
    Fh = w1.shape[2]
    tm, tf = min(tm, M), min(tf, Fh)
    assert M % tm == 0 and Fh % tf == 0
    return pl.pallas_call(
        _ffn_kernel,
        name="ffn",
        grid=(M // tm, Fh // tf),
        in_specs=[pl.BlockSpec((tm, D), lambda i, f: (i, 0)),
                  pl.BlockSpec((1, D), lambda i, f: (0, 0)),
                  pl.BlockSpec((None, D, tf), lambda i, f: (layer, 0, f)),
                  pl.BlockSpec((None, D, tf), lambda i, f: (layer, 0, f)),
                  pl.BlockSpec((None, tf, D), lambda i, f: (layer, f, 0))],
        out_specs=pl.BlockSpec((tm, D), lambda i, f: (i, 0)),
        out_shape=jax.ShapeDtypeStruct((M, D), F32),
        scratch_shapes=[pltpu.VMEM((tm, D), BF16)],
        compiler_params=_params("parallel", "arbitrary"),
    )(x, gain.reshape(1, D), w1, w3, w2)


def _mem_attn_kernel(x_ref, g_ref, wq_ref, kv_ref, qg_ref, kg_ref, wo_ref, o_ref, *, heads):
    x = x_ref[0]
    h = _rms(x, g_ref[...]).astype(BF16)
    q = _dot(h, wq_ref[...])
    kv = kv_ref[0]
    w = heads * HEAD_DIM
    outs = []
    for hh in range(heads):
        sl = slice(hh * HEAD_DIM, (hh + 1) * HEAD_DIM)
        qh = (_rms(q[:, sl], qg_ref[...]) * ATTN_SCALE).astype(BF16)
        kh = _rms(kv[:, sl], kg_ref[...]).astype(BF16)
        vh = kv[:, w + hh * HEAD_DIM: w + (hh + 1) * HEAD_DIM].astype(BF16)
        s = _dot_nt(qh, kh)
        p = jnp.exp(s - jnp.max(s, axis=-1, keepdims=True))
        z = jnp.sum(p, axis=-1, keepdims=True)
        outs.append((_dot(p.astype(BF16), vh) / z).astype(BF16))
    o = jnp.concatenate(outs, axis=-1)
    o_ref[0] = x + _dot(o, wo_ref[...])


def mem_attn(x, gain, wq, kv, q_gain, k_gain, wo, layer, tm):
    B, T, D = x.shape
    L, W2 = kv.shape[1], kv.shape[2]
    W = W2 // 2
    tm = min(tm, T)
    assert T % tm == 0
    return pl.pallas_call(
        functools.partial(_mem_attn_kernel, heads=W // HEAD_DIM),
        name="mem_attn",
        grid=(B, T // tm),
        in_specs=[pl.BlockSpec((1, tm, D), lambda b, i: (b, i, 0)),
                  pl.BlockSpec((1, D), lambda b, i: (0, 0)),
                  pl.BlockSpec((None, D, W), lambda b, i: (layer, 0, 0)),
                  pl.BlockSpec((1, L, W2), lambda b, i: (b, 0, 0)),
                  pl.BlockSpec((1, HEAD_DIM), lambda b, i: (0, 0)),
                  pl.BlockSpec((1, HEAD_DIM), lambda b, i: (0, 0)),
                  pl.BlockSpec((None, W, D), lambda b, i: (layer, 0, 0))],
        out_specs=pl.BlockSpec((1, tm, D), lambda b, i: (b, i, 0)),
        out_shape=jax.ShapeDtypeStruct((B, T, D), F32),
        compiler_params=_params("parallel", "parallel"),
    )(x, gain.reshape(1, D), wq, kv, q_gain.reshape(1, HEAD_DIM), k_gain.reshape(1, HEAD_DIM), wo)


def _eye(n):
    return (lax.broadcasted_iota(jnp.int32, (n, n), 0)
            == lax.broadcasted_iota(jnp.int32, (n, n), 1)).astype(BF16)


def _ones_row(rows, n):
    return (lax.broadcasted_iota(jnp.int32, (rows, n), 0) == 0).astype(BF16)


def _split3_f32(x):
    return tuple(t.astype(F32) for t in _split3(x))


COL_GROUPS = 1


def _col_groups(n):
    w = n // COL_GROUPS
    return [slice(i * w, (i + 1) * w) for i in range(COL_GROUPS)]


def _attn_init(n):
    return tuple((jnp.full((1, c.stop - c.start), NEG_INF, F32), jnp.zeros((VT_ROWS, c.stop - c.start), F32))
                 for c in _col_groups(n))


def _logit_bound(q_gain, k_gain):
    return HEAD_DIM * jnp.max(jnp.abs(q_gain)) * jnp.max(jnp.abs(k_gain)) * (ATTN_SCALE * LOG2E)


def _with_softmax_shift(bound, body):
    lax.cond(bound <= SOFTMAX_BOUND_LIMIT, lambda: body(bound), lambda: body(None))


def _attn_tile(k_tile, vt_tile, qa, carry, score_fn=None, shift=None):
    out = []
    for (m, acc), c in zip(carry, _col_groups(qa.shape[1])):
        s = _dot(k_tile, qa[:, c])
        if score_fn is not None:
            s = score_fn(s, c)
        if shift is not None:
            out.append((m, acc + _dot(vt_tile, jnp.exp2(s - shift).astype(BF16))))
            continue
        m_new = jnp.maximum(m, jnp.max(s, axis=0, keepdims=True))
        p = jnp.exp2(s - m_new).astype(BF16)
        out.append((m_new, jnp.exp2(m - m_new) * acc + _dot(vt_tile, p)))
    return tuple(out)


def _attn_finish(carry):
    return jnp.concatenate([acc for _, acc in carry], axis=1)


def _fox_gate_kernel(g_ref, b_ref, c_ref, kf_ref, *, heads):
    T = g_ref.shape[1]
    n = min(GATE_ROWS, T)
    tri = (lax.broadcasted_iota(jnp.int32, (n, n), 0)
           >= lax.broadcasted_iota(jnp.int32, (n, n), 1)).astype(BF16)
    lane = lax.broadcasted_iota(jnp.int32, (n, LANES), 1)

    def body(i, carry):
        rows = pl.ds(pl.multiple_of(i * n, n), n)
        cs = _exact_dot(tri, _log_sigmoid(g_ref[0, rows, :] + b_ref[...])) + carry
        c_ref[0, rows, :] = cs
        c2 = cs * LOG2E
        for h in range(heads):
            d1, d2, d3 = _split3_f32(c2[:, h:h + 1])
            feat = jnp.where(lane < 3, 1.0,
                             jnp.where(lane == 3, -d1, jnp.where(lane == 4, -d2,
                                                                 jnp.where(lane == 5, -d3, 0.0))))
            kf_ref[0, h, rows, :] = feat.astype(BF16)
        return cs[n - 1:n, :]

    lax.fori_loop(0, T // n, body, jnp.zeros((1, LANES), F32))


def fox_gate(g, bias, heads):
    B, T, _ = g.shape
    return pl.pallas_call(
        functools.partial(_fox_gate_kernel, heads=heads),
        name="fox_gate",
        grid=(B,),
        in_specs=[pl.BlockSpec((1, T, LANES), lambda b: (b, 0, 0)),
                  pl.BlockSpec((1, LANES), lambda b: (0, 0))],
        out_specs=[pl.BlockSpec((1, T, LANES), lambda b: (b, 0, 0)),
                   pl.BlockSpec((1, heads, T, LANES), lambda b: (b, 0, 0, 0))],
        out_shape=[jax.ShapeDtypeStruct((B, T, LANES), F32),
                   jax.ShapeDtypeStruct((B, heads, T, LANES), BF16)],
        compiler_params=_params("parallel"),
    )(g, bias)


def _fox_attn_kernel(q_ref, k_ref, v_ref, kf_ref, c_ref, qg_ref, kg_ref, o_ref, ka_ref, vt_ref, *, tq):
    qi = pl.program_id(2)
    T = k_ref.shape[1]
    eye = _eye(HEAD_DIM)

    @pl.when(qi == 0)
    def _():
        ka_ref[:, :HEAD_DIM] = _rms(k_ref[0], kg_ref[...]).astype(BF16)
        ka_ref[:, HEAD_DIM:] = kf_ref[0, 0]
        vt_ref[:HEAD_DIM, :] = _dot_nt(eye, v_ref[0].astype(BF16)).astype(BF16)
        vt_ref[HEAD_DIM:, :] = _ones_row(VT_ROWS - HEAD_DIM, T)

    q0 = pl.multiple_of(qi * tq, tq)
    qn = (_rms(q_ref[0], qg_ref[...]) * (ATTN_SCALE * LOG2E)).astype(BF16)
    qt = _dot_nt(eye, qn).astype(BF16)
    c1, c2, c3 = _split3_f32(c_ref[0, :, pl.ds(q0, tq)] * LOG2E)
    r = lax.broadcasted_iota(jnp.int32, (16, tq), 0)
    feat = jnp.where(r == 0, c1, jnp.where(r == 1, c2, jnp.where(r == 2, c3,
                                                                 jnp.where(r < 6, 1.0, 0.0))))
    qa = jnp.concatenate([qt, feat.astype(BF16), jnp.zeros((AUG - HEAD_DIM - 16, tq), BF16)], axis=0)

    kpos = lax.broadcasted_iota(jnp.int32, (tq, 1), 0)
    tpos = lax.broadcasted_iota(jnp.int32, (1, tq), 1)

    def attend(shift):
        def tile(j, carry, score_fn=None):
            k0 = pl.multiple_of(j * tq, tq)
            return _attn_tile(ka_ref[pl.ds(k0, tq), :], vt_ref[:, pl.ds(k0, tq)], qa, carry, score_fn, shift)

        carry = tile(qi, _attn_init(tq), lambda s, c: jnp.where(kpos <= tpos[:, c], s, NEG_INF))
        acc = _attn_finish(lax.fori_loop(0, qi, tile, carry))
        ot = (acc[:HEAD_DIM] / acc[HEAD_DIM:HEAD_DIM + 1]).astype(BF16)
        for c in range(tq // LANES):
            cs = slice(c * LANES, (c + 1) * LANES)
            o_ref[0, cs, :] = _dot_nt(eye, ot[:, cs]).astype(o_ref.dtype)

    _with_softmax_shift(_logit_bound(qg_ref[...], kg_ref[...]), attend)


def fox_attn(proj, kfeat, c_rows, q_gain, k_gain, tq):
    B, T, _ = proj.shape
    H = FOX_HEADS
    tq = min(tq, T)
    assert T % tq == 0 and tq % LANES == 0
    return pl.pallas_call(
        functools.partial(_fox_attn_kernel, tq=tq),
        name="fox_attn",
        grid=(B, H, T // tq),
        in_specs=[pl.BlockSpec((1, tq, HEAD_DIM), lambda b, h, i: (b, i, h)),
                  pl.BlockSpec((1, T, HEAD_DIM), lambda b, h, i: (b, 0, H + h)),
                  pl.BlockSpec((1, T, HEAD_DIM), lambda b, h, i: (b, 0, 2 * H + h)),
                  pl.BlockSpec((1, 1, T, LANES), lambda b, h, i: (b, h, 0, 0)),
                  pl.BlockSpec((1, 1, T), lambda b, h, i: (b * H + h, 0, 0)),
                  pl.BlockSpec((1, HEAD_DIM), lambda b, h, i: (0, 0)),
                  pl.BlockSpec((1, HEAD_DIM), lambda b, h, i: (0, 0))],
        out_specs=pl.BlockSpec((1, tq, HEAD_DIM), lambda b, h, i: (b, i, h)),
        out_shape=jax.ShapeDtypeStruct((B, T, H * HEAD_DIM), BF16),
        scratch_shapes=[pltpu.VMEM((T, AUG), BF16), pltpu.VMEM((VT_ROWS, T), BF16)],
        compiler_params=_params("parallel", "parallel", "arbitrary"),
    )(proj, proj, proj, kfeat, c_rows, q_gain.reshape(1, HEAD_DIM), k_gain.reshape(1, HEAD_DIM))


def _hgrn_kernel(q_ref, f_ref, i_ref, g_ref, la_ref, l1_ref, oml_ref, og_ref, o_ref, *, bs, chunk, span):
    T = q_ref.shape[1]
    R = LANES
    ri = lax.broadcasted_iota(jnp.int32, (R, R), 0)
    ci = lax.broadcasted_iota(jnp.int32, (R, R), 1)

    def prefix_sel(n):
        return (((ri // n) == (ci // n)) & (ri >= ci)).astype(BF16)

    s_idx = lax.broadcasted_iota(jnp.int32, (bs, HEAD_DIM), 0)
    tril = (lax.broadcasted_iota(jnp.int32, (chunk, chunk), 0)
            >= lax.broadcasted_iota(jnp.int32, (chunk, chunk), 1))

    def gates(row0, n):
        rows = pl.ds(row0, n)
        z = f_ref[0, rows, :]
        a = la_ref[0]
        y = l1_ref[0] + _log_sigmoid(z)
        log_f = jnp.maximum(a, y) + jnp.log1p(jnp.exp(-jnp.abs(a - y)))
        hq = q_ref[0, rows, :]
        return log_f, oml_ref[0] * jax.nn.sigmoid(-z), hq * jax.nn.sigmoid(hq), i_ref[0, rows, :]

    def prefixes(log_f, n):
        sel = prefix_sel(n)
        pieces = _split3(log_f)
        b = jnp.concatenate([sum(_dot(sel, p[g * R:(g + 1) * R]) for p in pieces)
                             for g in range(log_f.shape[0] // R)], axis=0)
        bt = jnp.concatenate([jnp.broadcast_to(b[(u + 1) * n - 1:(u + 1) * n, :], (n, HEAD_DIM))
                              for u in range(log_f.shape[0] // n)], axis=0)
        return b, bt

    def finish(row0, n, o):
        rows = pl.ds(row0, n)
        hg = g_ref[0, rows, :]
        o_ref[0, rows, :] = (_rms(o, og_ref[...]) * (hg * jax.nn.sigmoid(hg))).astype(o_ref.dtype)

    def matmul_path(row0, st, kk, qq, vv, b, bt):
        qe = (qq * jnp.exp(b)).astype(BF16)
        k_inv = (kk * jnp.exp(-b)).astype(BF16)
        ke = (kk * jnp.exp(bt - b)).astype(BF16)
        dec = jnp.exp(bt)
        vb = vv.astype(BF16)
        sls = [slice(u * chunk, (u + 1) * chunk) for u in range(span // chunk)]
        scores = [jnp.where(tril, _dot_nt(qe[sl], k_inv[sl]), 0.0).astype(BF16) for sl in sls]
        intra = [_dot(a, vb[sl]) for a, sl in zip(scores, sls)]
        update = [_dot_tn(vb[sl], ke[sl]) for sl in sls]
        outs = []
        for sl, o_in, upd in zip(sls, intra, update):
            outs.append(_dot_nt(qe[sl], st.astype(BF16)) + o_in)
            st = dec[sl.start:sl.start + 1, :] * st + upd
        finish(row0, span, jnp.concatenate(outs, axis=0))
        return st

    def pairwise_step(i, st, base):
        row0 = pl.multiple_of(base + i * R, R)
        log_f, kk, qq, vv = gates(row0, R)
        b, bt = prefixes(log_f, bs)
        qe = (qq * jnp.exp(b)).astype(BF16)
        ke = (kk * jnp.exp(bt - b)).astype(BF16)
        dec = jnp.exp(bt)
        vb = vv.astype(BF16)
        outs = []
        for u in range(R // bs):
            sl = slice(u * bs, (u + 1) * bs)
            bu, vu, qk, ku = b[sl], vv[sl], qq[sl], kk[sl]
            intra = []
            for t in range(bs):
                w = jnp.exp(jnp.where(s_idx <= t, bu[t:t + 1, :] - bu, NEG_INF))
                a_t = jnp.sum(w * (qk[t:t + 1, :] * ku), axis=-1, keepdims=True)
                intra.append(jnp.sum(a_t * vu, axis=0, keepdims=True))
            outs.append(_dot_nt(qe[sl], st.astype(BF16)) + jnp.concatenate(intra, axis=0))
            st = dec[u * bs:u * bs + 1, :] * st + _dot_tn(vb[sl], ke[sl])
        finish(row0, R, jnp.concatenate(outs, axis=0))
        return st

    def step(i, st):
        row0 = pl.multiple_of(i * span, span)
        log_f, kk, qq, vv = gates(row0, span)
        b, bt = prefixes(log_f, chunk)
        return lax.cond(
            jnp.min(bt) >= HGRN_SAFE_LOG_DECAY,
            lambda st: matmul_path(row0, st, kk, qq, vv, b, bt),
            lambda st: lax.fori_loop(0, span // R, functools.partial(pairwise_step, base=row0), st),
            st)

    lax.fori_loop(0, T // span, step, jnp.zeros((HEAD_DIM, HEAD_DIM), F32))


def hgrn(proj, col0, log_lb, log1m_lb, one_m_lb, o_gain):
    B, T, _ = proj.shape
    H = HGRN_HEADS
    row = lambda off: pl.BlockSpec((1, T, HEAD_DIM), lambda b, h: (b, 0, col0 + off * H + h))
    par = pl.BlockSpec((1, 1, HEAD_DIM), lambda b, h: (h, 0, 0))
    log_lb, log1m_lb, one_m_lb = (p.reshape(H, 1, HEAD_DIM) for p in (log_lb, log1m_lb, one_m_lb))
    return pl.pallas_call(
        functools.partial(_hgrn_kernel, bs=HGRN_BLOCK, chunk=HGRN_CHUNK, span=min(HGRN_SPAN, T)),
        name="hgrn",
        grid=(B, H),
        in_specs=[row(0), row(1), row(2), row(3), par, par, par,
                  pl.BlockSpec((1, HEAD_DIM), lambda b, h: (0, 0))],
        out_specs=pl.BlockSpec((1, T, HEAD_DIM), lambda b, h: (b, 0, h)),
        out_shape=jax.ShapeDtypeStruct((B, T, H * HEAD_DIM), BF16),
        compiler_params=_params("parallel", "parallel"),
    )(proj, proj, proj, proj, log_lb, log1m_lb, one_m_lb, o_gain.reshape(1, HEAD_DIM))


def _compress_kernel(t_ref, pe_ref, w1_ref, w2_ref, g_ref, o_ref, *, normalize):
    N = t_ref.shape[1] // CMP_STRIDE
    top = bot = jnp.zeros((N, w1_ref.shape[2]), F32)
    for l in range(CMP_STRIDE):
        xl = t_ref[0, pl.ds(l, N, stride=CMP_STRIDE), :]
        top = top + _dot((xl + pe_ref[l:l + 1, :]).astype(BF16), w1_ref[l])
        bot = bot + _dot((xl + pe_ref[CMP_STRIDE + l:CMP_STRIDE + l + 1, :]).astype(BF16),
                         w1_ref[CMP_STRIDE + l])
    hid = top + pltpu.roll(bot, N - 1, 0)
    out = _dot(jax.nn.gelu(hid).astype(BF16), w2_ref[...])
    if normalize:
        out = _rms(out, g_ref[...])
    valid = lax.broadcasted_iota(jnp.int32, out.shape, 0) < N - 1
    o_ref[0, 0] = jnp.where(valid, out, 0.0)


def compress(proj, col0, pe, w1, w2, layer, gain, normalize):
    B, T, _ = proj.shape
    G = NSA_KV_HEADS
    N = T // CMP_STRIDE
    Hc = w1.shape[3]
    assert CMP_BLOCK == 2 * CMP_STRIDE
    return pl.pallas_call(
        functools.partial(_compress_kernel, normalize=normalize),
        name="nsa_compress",
        grid=(B, G),
        in_specs=[pl.BlockSpec((1, T, HEAD_DIM), lambda b, g: (b, 0, col0 + g)),
                  pl.BlockSpec((None, CMP_BLOCK, HEAD_DIM), lambda b, g: (layer, 0, 0)),
                  pl.BlockSpec((None, CMP_BLOCK, HEAD_DIM, Hc), lambda b, g: (layer, 0, 0, 0)),
                  pl.BlockSpec((None, Hc, HEAD_DIM), lambda b, g: (layer, 0, 0)),
                  pl.BlockSpec((1, HEAD_DIM), lambda b, g: (0, 0))],
        out_specs=pl.BlockSpec((1, 1, N, HEAD_DIM), lambda b, g: (b, g, 0, 0)),
        out_shape=jax.ShapeDtypeStruct((B, G, N, HEAD_DIM), F32),
        compiler_params=_params("parallel", "parallel"),
    )(proj, pe, w1, w2, gain.reshape(1, HEAD_DIM))


_SEL_LANE0 = 64
_CMP_OV_ROW0 = VT_ROWS


def _nsa_kernel(*refs, **static):
    qg_ref, kg_ref = refs[8], refs[9]
    _with_softmax_shift(_logit_bound(qg_ref[...], kg_ref[...]),
                        lambda shift: _nsa_step(*refs, shift=shift, **static))


def _nsa_step(q_ref, ks_ref, vs_ref, kw_ref, vw_ref, kc_ref, vc_ref, gt_ref, qg_ref, kg_ref,
              o_ref, ksa_ref, kwa_ref, vst_ref, vwt_ref, kca_ref, vcov_ref, tiles_ref, *, tq, tk, n_sel, shift):
    g = pl.program_id(1)
    qi = pl.program_id(2)
    T = ks_ref.shape[1]
    J = NSA_GROUP
    n_slc = T // SLC_BLOCK
    nc = kc_ref.shape[2]
    n_cmp = nc - 1
    cols = J * tq
    eye = _eye(HEAD_DIM)

    @pl.when(qi == 0)
    def _():
        row = lax.broadcasted_iota(jnp.int32, (T, LANES), 0)
        lane = lax.broadcasted_iota(jnp.int32, (T, LANES), 1)
        blk = row // SLC_BLOCK
        feats = jnp.where(lane < 3, blk,
                          jnp.where(lane < 6, row % SLC_BLOCK,
                                    jnp.where(lane < 9, 1,
                                              jnp.where(lane - _SEL_LANE0 == blk, 1, 0))))
        feats = feats.astype(F32).astype(BF16)
        ksa_ref[:, :HEAD_DIM] = _rms(ks_ref[0], kg_ref[1:2, :]).astype(BF16)
        ksa_ref[:, HEAD_DIM:] = feats
        kwa_ref[:, :HEAD_DIM] = _rms(kw_ref[0], kg_ref[2:3, :]).astype(BF16)
        kwa_ref[:, HEAD_DIM:] = feats
        vst_ref[:HEAD_DIM, :] = _dot_nt(eye, vs_ref[0].astype(BF16)).astype(BF16)
        vst_ref[HEAD_DIM:, :] = _ones_row(VT_ROWS - HEAD_DIM, T)
        vwt_ref[:HEAD_DIM, :] = _dot_nt(eye, vw_ref[0].astype(BF16)).astype(BF16)
        vwt_ref[HEAD_DIM:, :] = _ones_row(VT_ROWS - HEAD_DIM, T)
        n_r = lax.broadcasted_iota(jnp.int32, (nc, LANES), 0)
        n_l = lax.broadcasted_iota(jnp.int32, (nc, LANES), 1)
        per = SLC_BLOCK // CMP_STRIDE
        cfe = jnp.where(n_l < 3, (n_r // per).astype(F32),
                        jnp.where(n_l < 6, (CMP_STRIDE * (n_r % per)).astype(F32) + 0.5 * (CMP_BLOCK - 1),
                                  jnp.where(n_l < 9, 1.0, 0.0)))
        kca_ref[:, :HEAD_DIM] = kc_ref[0, 0].astype(BF16)
        kca_ref[:, HEAD_DIM:] = cfe.astype(BF16)
        vcov_ref[:HEAD_DIM, :] = _dot_nt(eye, vc_ref[0, 0].astype(BF16)).astype(BF16)
        vcov_ref[HEAD_DIM:_CMP_OV_ROW0, :] = _ones_row(VT_ROWS - HEAD_DIM, nc)
        m_r = lax.broadcasted_iota(jnp.int32, (SLC_BLOCK, nc), 0)
        n_c = lax.broadcasted_iota(jnp.int32, (SLC_BLOCK, nc), 1)
        vcov_ref[_CMP_OV_ROW0:, :] = ((n_c * CMP_STRIDE <= m_r * SLC_BLOCK + (SLC_BLOCK - 1))
                                      & (n_c * CMP_STRIDE + (CMP_BLOCK - 1) >= m_r * SLC_BLOCK)
                                      & (n_c < n_cmp) & (m_r < n_slc)).astype(BF16)

    q0 = pl.multiple_of(qi * tq, tq)
    qraw = q_ref[0]
    qt = jnp.concatenate(
        [_dot_nt(eye, (_rms(qraw[:, j * HEAD_DIM:(j + 1) * HEAD_DIM], qg_ref[...])
                       * (ATTN_SCALE * LOG2E)).astype(BF16)).astype(BF16)
         for j in range(J)], axis=1)

    col = lax.broadcasted_iota(jnp.int32, (1, cols), 1)
    tt = q0 + col % tq
    slope2 = jnp.exp2(-0.5 * (g * J + col // tq + 1).astype(F32)) * LOG2E
    s1, s2, s3 = _split3_f32(slope2)
    r1, r2, r3 = _split3_f32(-slope2 * tt.astype(F32))
    r16 = lax.broadcasted_iota(jnp.int32, (16, cols), 0)
    feat = jnp.zeros((16, cols), F32)
    for i, v in enumerate((s1 * SLC_BLOCK, s2 * SLC_BLOCK, s3 * SLC_BLOCK, s1, s2, s3, r1, r2, r3)):
        feat = jnp.where(r16 == i, v, feat)
    feat = feat.astype(BF16)
    qa_base = jnp.concatenate([qt, feat, jnp.zeros((AUG - HEAD_DIM - 16, cols), BF16)], axis=0)

    n_col = lax.broadcasted_iota(jnp.int32, (nc, 1), 0)
    cmp_end = n_col * CMP_STRIDE + (CMP_BLOCK - 1)
    kca = kca_ref[...]
    r_hi, r_lo = [], []
    for c in _col_groups(cols):
        s = _dot(kca, qa_base[:, c])
        s = jnp.where(cmp_end <= tt[:, c], s, NEG_INF)
        if shift is None:
            m = jnp.max(s, axis=0, keepdims=True)
            p = jnp.exp2(s - jnp.where(m == NEG_INF, 0.0, m))
        else:
            p = jnp.exp2(s - shift)
        p_hi = p.astype(BF16)
        r_hi.append(_dot(vcov_ref[...], p_hi))
        r_lo.append(_dot(vcov_ref[HEAD_DIM:, :], (p - p_hi.astype(F32)).astype(BF16)))
    r_hi = jnp.concatenate(r_hi, axis=1)
    r_lo = jnp.concatenate(r_lo, axis=1)
    z = r_hi[HEAD_DIM:HEAD_DIM + 1] + r_lo[0:1]
    inv_z = 1.0 / jnp.where(z > 0, z, 1.0)

    wq = min(tq, WINDOW_Q)
    span = WINDOW + wq
    d = (lax.broadcasted_iota(jnp.int32, (span, 1), 0)
         - lax.broadcasted_iota(jnp.int32, (1, wq), 1))
    parts = []
    for c in range(tq // wq):
        k0w = pl.multiple_of(jnp.maximum(q0 + c * wq - WINDOW, 0), wq)
        off = q0 + c * wq - k0w
        bias = jnp.where(d <= off, jnp.where(d > off - WINDOW, 0.0, NEG_INF), NEG_INF)
        bias = jnp.concatenate([bias] * J, axis=1)
        qa_c = jnp.concatenate([qa_base[:, j * tq + c * wq:j * tq + (c + 1) * wq] for j in range(J)], axis=1)
        parts.append(_attn_finish(_attn_tile(
            kwa_ref[pl.ds(k0w, span), :], vwt_ref[:, pl.ds(k0w, span)], qa_c, _attn_init(J * wq),
            lambda s, sub, bias=bias: s + bias[:, sub], shift)))
    acc_w = jnp.concatenate([p[:, j * wq:(j + 1) * wq] for j in range(J) for p in parts], axis=1)

    imp4 = (r_hi[_CMP_OV_ROW0:] + r_lo[_CMP_OV_ROW0 - HEAD_DIM:]) * inv_z
    imp = imp4[:, 0:tq]
    for j in range(1, J):
        imp = imp + imp4[:, j * tq:(j + 1) * tq]
    blk = lax.broadcasted_iota(jnp.int32, (SLC_BLOCK, tq), 0)
    t_q = q0 + lax.broadcasted_iota(jnp.int32, (1, tq), 1)
    imp = jnp.where(blk * SLC_BLOCK > t_q, NEG_INF, imp)
    imp = jnp.where(blk == t_q // SLC_BLOCK, jnp.inf, jnp.where(blk == 0, jnp.inf, imp))
    blk_f = blk.astype(F32)
    vals = jnp.where(blk >= n_slc, NEG_INF, imp)
    picked = jnp.zeros((SLC_BLOCK, tq), F32)
    for _ in range(n_sel):
        best = jnp.max(vals, axis=0, keepdims=True)
        first = jnp.min(jnp.where(vals == best, blk_f, float(SLC_BLOCK)), axis=0, keepdims=True)
        hit = blk_f == first
        picked = jnp.where(hit, 1.0, picked)
        vals = jnp.where(hit, NEG_INF, vals)
    sel = (picked > 0.5) & (blk < n_slc)
    not_sel = jnp.where(sel, 0.0, -MASK_BIG).astype(BF16)
    qa_slc = jnp.concatenate([qt, feat, jnp.zeros((_SEL_LANE0 - 16, cols), BF16)]
                             + [jnp.concatenate([not_sel] * J, axis=1)], axis=0)

    def slc_tile(jt, carry, score_fn=None):
        k0 = pl.multiple_of(jt * tk, tk)
        return _attn_tile(ksa_ref[pl.ds(k0, tk), :], vst_ref[:, pl.ds(k0, tk)], qa_slc, carry, score_fn, shift)

    jd = q0 // tk
    kpos = jd * tk + lax.broadcasted_iota(jnp.int32, (tk, 1), 0)
    carry = slc_tile(jd, _attn_init(cols), lambda s, c: jnp.where(kpos <= tt[:, c], s, NEG_INF))
    sel_f = jnp.where(sel, 1.0, 0.0)
    per_tile = tk // SLC_BLOCK
    n_live = jnp.int32(0)
    for jt in range(T // tk - 1):
        live = (jnp.max(sel_f[jt * per_tile:(jt + 1) * per_tile, :]) > 0.5) & (jt < jd)
        tiles_ref[n_live] = jt
        n_live = n_live + live.astype(jnp.int32)
    acc_s = _attn_finish(lax.fori_loop(0, n_live, lambda i, c: slc_tile(tiles_ref[i], c), carry))

    gates_t = jnp.transpose(jax.nn.sigmoid(gt_ref[0]))
    gate = lambda c: jnp.concatenate([gates_t[3 * j + c:3 * j + c + 1, :] for j in range(J)], axis=1)
    o_t = ((gate(0) * inv_z) * r_hi[:HEAD_DIM]
           + (gate(1) / acc_s[HEAD_DIM:HEAD_DIM + 1]) * acc_s[:HEAD_DIM]
           + (gate(2) / acc_w[HEAD_DIM:HEAD_DIM + 1]) * acc_w[:HEAD_DIM]).astype(BF16)
    for j in range(J):
        for c in range(tq // LANES):
            o_ref[0, c * LANES:(c + 1) * LANES, j * HEAD_DIM:(j + 1) * HEAD_DIM] = _dot_nt(
                eye, o_t[:, j * tq + c * LANES:j * tq + (c + 1) * LANES]).astype(o_ref.dtype)


def nsa_attn(proj, gates, k_cmp, v_cmp, q_gain, k_gain, tq, tk):
    B, T, _ = proj.shape
    G, J = NSA_KV_HEADS, NSA_GROUP
    tk = min(tk, T)
    nc = k_cmp.shape[2]
    assert T % tk == 0 and tk % tq == 0 and T >= WINDOW + tq
    assert T // SLC_BLOCK <= LANES - _SEL_LANE0 and SLC_BLOCK <= LANES - _SEL_LANE0
    qcols = NSA_HEADS
    kv = lambda slab: pl.BlockSpec((1, T, HEAD_DIM), lambda b, g, i: (b, 0, qcols + slab * G + g))
    cmp_spec = pl.BlockSpec((1, 1, nc, HEAD_DIM), lambda b, g, i: (b, g, 0, 0))
    return pl.pallas_call(
        functools.partial(_nsa_kernel, tq=tq, tk=tk, n_sel=min(SLC_TOPK, T // SLC_BLOCK)),
        name="nsa_attn",
        grid=(B, G, T // tq),
        in_specs=[pl.BlockSpec((1, tq, J * HEAD_DIM), lambda b, g, i: (b, i, g)),
                  kv(2), kv(3), kv(4), kv(5), cmp_spec, cmp_spec,
                  pl.BlockSpec((1, tq, LANES), lambda b, g, i: (b, i, g)),
                  pl.BlockSpec((1, HEAD_DIM), lambda b, g, i: (0, 0)),
                  pl.BlockSpec((3, HEAD_DIM), lambda b, g, i: (0, 0))],
        out_specs=pl.BlockSpec((1, tq, J * HEAD_DIM), lambda b, g, i: (b, i, g)),
        out_shape=jax.ShapeDtypeStruct((B, T, NSA_HEADS * HEAD_DIM), BF16),
        scratch_shapes=[pltpu.VMEM((T, AUG), BF16), pltpu.VMEM((T, AUG), BF16),
                        pltpu.VMEM((VT_ROWS, T), BF16), pltpu.VMEM((VT_ROWS, T), BF16),
                        pltpu.VMEM((nc, AUG), BF16),
                        pltpu.VMEM((VT_ROWS + SLC_BLOCK, nc), BF16),
                        pltpu.SMEM((T // tk,), jnp.int32)],
        compiler_params=_params("parallel", "parallel", "arbitrary"),
    )(proj, proj, proj, proj, proj, k_cmp, v_cmp, gates, q_gain.reshape(1, HEAD_DIM), k_gain)


def even_weights(w_in, w_out):
    fw = FOX_HEADS * HEAD_DIM
    w_main = jnp.concatenate([w_in[:, :, :3 * fw], w_in[:, :, 3 * fw + FOX_HEADS:]], axis=2).astype(BF16)
    w_gate = jnp.pad(w_in[:, :, 3 * fw:3 * fw + FOX_HEADS], ((0, 0), (0, 0), (0, LANES - FOX_HEADS)))
    return w_main, w_gate.astype(BF16), w_out.astype(BF16)


def even_mixer(x, gain, weights, e, f_bias, q_gain, k_gain, lb, o_gain):
    B, T, D = x.shape
    xf = x.reshape(B * T, D)
    w_main, w_gate, w_out = weights
    proj, gate = rms_matmul(xf, gain, w_main, e, 1024, 1024, side_w=w_gate)
    proj, gate = proj.reshape(B, T, -1), gate.reshape(B, T, LANES)
    f_bias = jnp.pad(f_bias.reshape(1, FOX_HEADS).astype(F32), ((0, 0), (0, LANES - FOX_HEADS)))
    c, kfeat = fox_gate(gate, f_bias, FOX_HEADS)
    c_rows = c[:, :, :FOX_HEADS].transpose(0, 2, 1).reshape(B * FOX_HEADS, 1, T)
    o_fox = fox_attn(proj, kfeat, c_rows, q_gain, k_gain, 1024)
    lb = lb.astype(F32).reshape(HGRN_HEADS, HEAD_DIM)
    o_h = hgrn(proj, 3 * FOX_HEADS, jnp.log(lb), jnp.log1p(-lb), 1.0 - lb, o_gain)
    out = out_proj([o_fox.reshape(B * T, -1), o_h.reshape(B * T, -1)], w_out, e, xf, 512)
    return out.reshape(B, T, D)


def nsa_weights(w_in, w_out, w1_k, w2_k, w1_v, w2_v):
    L, D, _ = w_in.shape
    G, J = NSA_KV_HEADS, NSA_GROUP
    n_main = (NSA_HEADS + 6 * G) * HEAD_DIM
    w_gate = w_in[:, :, n_main:].reshape(L, D, G, 3 * J)
    w_gate = jnp.pad(w_gate, ((0, 0), (0, 0), (0, 0), (0, LANES - 3 * J))).reshape(L, D, G * LANES)
    return (jnp.swapaxes(w_in[:, :, :n_main], 1, 2).astype(BF16), w_gate.astype(BF16), w_out.astype(BF16),
            w1_k.astype(BF16), w2_k.astype(BF16), w1_v.astype(BF16), w2_v.astype(BF16))


def nsa_mixer(x, gain, weights, o, q_gain, k_gain, pe_k, pe_v):
    B, T, D = x.shape
    G = NSA_KV_HEADS
    xf = x.reshape(B * T, D)
    w_main, w_gate, w_out, w1_k, w2_k, w1_v, w2_v = weights
    proj, gates = rms_matmul(xf, gain, w_main, o, 1024, 1024, side_w=w_gate, w_transposed=True)
    proj, gates = proj.reshape(B, T, -1), gates.reshape(B, T, G * LANES)
    k_cmp = compress(proj, NSA_HEADS, pe_k, w1_k, w2_k, o, k_gain[0], True)
    v_cmp = compress(proj, NSA_HEADS + G, pe_v, w1_v, w2_v, o, k_gain[0], False)
    att = nsa_attn(proj, gates, k_cmp, v_cmp, q_gain, k_gain, 512, 512)
    out = out_proj([att.reshape(B * T, -1)], w_out, o, xf, 512)
    return out.reshape(B, T, D)


def kernel(x, mem, norm_mix, norm_mem, norm_ffn, mem_in_gain, even_w_in, even_w_out, fox_f_bias,
           fox_q_gain, fox_k_gain, hgrn_lb_logits, hgrn_o_gain, odd_w_in, odd_w_out, nsa_q_gain,
           nsa_k_gain, cmp_pe_k, cmp_w1_k, cmp_w2_k, cmp_pe_v, cmp_w1_v, cmp_w2_v, mem_wq, mem_wkv,
           mem_wo, mem_q_gain, mem_k_gain, ffn_w1, ffn_w3, ffn_w2):
    B, T, D = x.shape
    depth = norm_mix.shape[0]
    L = mem.shape[1]
    lb_cum = jnp.cumsum(jax.nn.softmax(hgrn_lb_logits.astype(F32), axis=0), axis=0)
    hgrn_lb = lb_cum - lb_cum[0:1]
    memf = mem.reshape(B * L, D)
    even_w = even_weights(even_w_in, even_w_out)
    nsa_w = nsa_weights(odd_w_in, odd_w_out, cmp_w1_k, cmp_w2_k, cmp_w1_v, cmp_w2_v)
    mem_wq, mem_wkv, mem_wo = (w.astype(BF16) for w in (mem_wq, mem_wkv, mem_wo))
    for layer in range(depth):
        if layer % 2 == 0:
            e = layer // 2
            x = even_mixer(x, norm_mix[layer], even_w, e, fox_f_bias[e], fox_q_gain[e], fox_k_gain[e],
                           hgrn_lb[e], hgrn_o_gain[e])
        else:
            o = layer // 2
            x = nsa_mixer(x, norm_mix[layer], nsa_w, o, nsa_q_gain[o], nsa_k_gain[o], cmp_pe_k, cmp_pe_v)
        kv = rms_matmul(memf, mem_in_gain[layer], mem_wkv, layer, 256, 1024)
        x = mem_attn(x, norm_mem[layer], mem_wq, kv.reshape(B, L, -1), mem_q_gain[layer],
                     mem_k_gain[layer], mem_wo, layer, 1024)
        x = ffn(x.reshape(B * T, D), norm_ffn[layer], ffn_w1, ffn_w3, ffn_w2, layer,
                1024, 256).reshape(B, T, D)
    return x
```

```python
import functools

import jax
import jax.numpy as jnp
from jax import lax
from jax.experimental import pallas as pl
from jax.experimental.pallas import tpu as pltpu

F32 = jnp.float32
BF16 = jnp.bfloat16
NEG_INF = float("-inf")

EPS = 1e-6
HEAD_DIM = 128
ATTN_SCALE = HEAD_DIM ** -0.5
LANES = 128
FOX_HEADS = 8
HGRN_HEADS = 8
HGRN_BLOCK = 16
HGRN_CHUNK = 64
HGRN_SPAN = 2048
GATE_ROWS = 512
HGRN_SAFE_LOG_DECAY = -80.0
NSA_HEADS = 16
NSA_KV_HEADS = 4
NSA_GROUP = NSA_HEADS // NSA_KV_HEADS
CMP_BLOCK = 32
CMP_STRIDE = 16
SLC_BLOCK = 64
SLC_TOPK = 16
WINDOW = 512
WINDOW_Q = 256
MEM_HEADS = 4
V7X_VMEM_LIMIT = 60 * 1024 * 1024
LOG2E = 1.4426950408889634
MASK_BIG = 1e30
SOFTMAX_BOUND_LIMIT = 40.0
AUG = 2 * LANES
VT_ROWS = HEAD_DIM + 16


def _params(*sem):
    return pltpu.CompilerParams(dimension_semantics=sem, vmem_limit_bytes=V7X_VMEM_LIMIT)


def _rms(x, gain):
    return x * lax.rsqrt(jnp.mean(x * x, axis=-1, keepdims=True) + EPS) * gain


def _dot(a, b):
    return jnp.dot(a, b, preferred_element_type=F32)


def _dot_nt(a, b):
    return lax.dot_general(a, b, (((1,), (1,)), ((), ())), preferred_element_type=F32)


def _dot_tn(a, b):
    return lax.dot_general(a, b, (((0,), (0,)), ((), ())), preferred_element_type=F32)


def _split3(x):
    x1 = x.astype(BF16)
    r1 = x - x1.astype(F32)
    x2 = r1.astype(BF16)
    return x1, x2, (r1 - x2.astype(F32)).astype(BF16)


def _exact_dot(sel, x):
    x1, x2, x3 = _split3(x)
    return _dot(sel, x1) + _dot(sel, x2) + _dot(sel, x3)


def _log_sigmoid(z):
    return jnp.minimum(z, 0.0) - jnp.log1p(jnp.exp(-jnp.abs(z)))


def _rms_matmul_kernel(x_ref, g_ref, w_ref, *rest, w_transposed):
    side_w_ref, o_ref, side_o_ref, hn_ref = rest if len(rest) == 4 else (None, rest[0], None, rest[1])

    @pl.when(pl.program_id(1) == 0)
    def _():
        hn_ref[...] = _rms(x_ref[...], g_ref[...]).astype(BF16)
        if side_w_ref is not None:
            side_o_ref[...] = _dot(hn_ref[...], side_w_ref[...])

    o_ref[...] = (_dot_nt if w_transposed else _dot)(hn_ref[...], w_ref[...])


def rms_matmul(x, gain, w, layer, tm, tn, side_w=None, w_transposed=False):
    M, K = x.shape
    N = w.shape[1] if w_transposed else w.shape[2]
    tm, tn = min(tm, M), min(tn, N)
    assert M % tm == 0 and N % tn == 0
    w_spec = (pl.BlockSpec((None, tn, K), lambda i, j: (layer, j, 0)) if w_transposed
              else pl.BlockSpec((None, K, tn), lambda i, j: (layer, 0, j)))
    in_specs = [pl.BlockSpec((tm, K), lambda i, j: (i, 0)),
                pl.BlockSpec((1, K), lambda i, j: (0, 0)),
                w_spec]
    out_specs = pl.BlockSpec((tm, tn), lambda i, j: (i, j))
    out_shape = jax.ShapeDtypeStruct((M, N), F32)
    args = (x, gain.reshape(1, K), w)
    if side_w is not None:
        ns = side_w.shape[2]
        in_specs.append(pl.BlockSpec((None, K, ns), lambda i, j: (layer, 0, 0)))
        out_specs = [out_specs, pl.BlockSpec((tm, ns), lambda i, j: (i, 0))]
        out_shape = [out_shape, jax.ShapeDtypeStruct((M, ns), F32)]
        args += (side_w,)
    return pl.pallas_call(
        functools.partial(_rms_matmul_kernel, w_transposed=w_transposed),
        name="rms_matmul",
        grid=(M // tm, N // tn),
        in_specs=in_specs,
        out_specs=out_specs,
        out_shape=out_shape,
        scratch_shapes=[pltpu.VMEM((tm, K), BF16)],
        compiler_params=_params("parallel", "arbitrary"),
    )(*args)


def _out_proj_kernel(*refs, n_in):
    a_refs, w_refs = refs[:n_in], refs[n_in:2 * n_in]
    res_ref, o_ref = refs[2 * n_in], refs[2 * n_in + 1]
    acc = res_ref[...]
    for a_ref, w_ref in zip(a_refs, w_refs):
        acc = acc + _dot(a_ref[...], w_ref[...])
    o_ref[...] = acc


def out_proj(acts, w, layer, res, tm):
    M, N = res.shape
    tm = min(tm, M)
    assert M % tm == 0
    n_in = len(acts)
    ka = acts[0].shape[1]
    assert all(a.shape[1] == ka for a in acts) and n_in * ka == w.shape[1]
    row_block = lambda r: pl.BlockSpec((None, ka, N), lambda i: (layer, r, 0))
    weights = [w] * n_in
    in_specs = ([pl.BlockSpec((tm, ka), lambda i: (i, 0)) for _ in acts]
                + [row_block(r) for r in range(n_in)]
                + [pl.BlockSpec((tm, N), lambda i: (i, 0))])
    return pl.pallas_call(
        functools.partial(_out_proj_kernel, n_in=n_in),
        name="out_proj",
        grid=(M // tm,),
        in_specs=in_specs,
        out_specs=pl.BlockSpec((tm, N), lambda i: (i, 0)),
        out_shape=jax.ShapeDtypeStruct((M, N), F32),
        compiler_params=_params("parallel"),
    )(*acts, *weights, res)


def _ffn_kernel(x_ref, g_ref, w1_ref, w3_ref, w2_ref, o_ref, hn_ref):
    f = pl.program_id(1)

    @pl.when(f == 0)
    def _():
        x = x_ref[...]
        hn_ref[...] = _rms(x, g_ref[...]).astype(BF16)
        o_ref[...] = x

    h = hn_ref[...]
    a = _dot(h, w1_ref[...].astype(BF16))
    b = _dot(h, w3_ref[...].astype(BF16))
    u = (a * jax.nn.sigmoid(a) * b).astype(BF16)
    o_ref[...] += _dot(u, w2_ref[...].astype(BF16))


def ffn(x, gain, w1, w3, w2, layer, tm, tf):
    M, D = x.shape
    Fh = w1.shape[2]
    tm, tf = min(tm, M), min(tf, Fh)
    assert M % tm == 0 and Fh % tf == 0
    return pl.pallas_call(
        _ffn_kernel,
        name="ffn",
        grid=(M // tm, Fh // tf),
        in_specs=[pl.BlockSpec((tm, D), lambda i, f: (i, 0), pipeline_mode=pl.Buffered(1)),
                  pl.BlockSpec((1, D), lambda i, f: (0, 0)),
                  pl.BlockSpec((None, D, tf), lambda i, f: (layer, 0, f)),
                  pl.BlockSpec((None, D, tf), lambda i, f: (layer, 0, f)),
                  pl.BlockSpec((None, tf, D), lambda i, f: (layer, f, 0))],
        out_specs=pl.BlockSpec((tm, D), lambda i, f: (i, 0)),
        out_shape=jax.ShapeDtypeStruct((M, D), F32),
        scratch_shapes=[pltpu.VMEM((tm, D), BF16)],
        compiler_params=_params("parallel", "arbitrary"),
    )(x, gain.reshape(1, D), w1, w3, w2)


def _mem_attn_kernel(x_ref, g_ref, wq_ref, kv_ref, qg_ref, kg_ref, wo_ref, o_ref, *, heads):
    x = x_ref[0]
    h = _rms(x, g_ref[...]).astype(BF16)
    q = _dot(h, wq_ref[...])
    kv = kv_ref[0]
    w = heads * HEAD_DIM
    outs = []
    for hh in range(heads):
        sl = slice(hh * HEAD_DIM, (hh + 1) * HEAD_DIM)
        qh = (_rms(q[:, sl], qg_ref[...]) * ATTN_SCALE).astype(BF16)
        kh = _rms(kv[:, sl], kg_ref[...]).astype(BF16)
        vh = kv[:, w + hh * HEAD_DIM: w + (hh + 1) * HEAD_DIM].astype(BF16)
        s = _dot_nt(qh, kh)
        p = jnp.exp(s - jnp.max(s, axis=-1, keepdims=True))
        z = jnp.sum(p, axis=-1, keepdims=True)
        outs.append((_dot(p.astype(BF16), vh) / z).astype(BF16))
    o = jnp.concatenate(outs, axis=-1)
    o_ref[0] = x + _dot(o, wo_ref[...])


def mem_attn(x, gain, wq, kv, q_gain, k_gain, wo, layer, tm):
    B, T, D = x.shape
    L, W2 = kv.shape[1], kv.shape[2]
    W = W2 // 2
    tm = min(tm, T)
    assert T % tm == 0
    return pl.pallas_call(
        functools.partial(_mem_attn_kernel, heads=W // HEAD_DIM),
        name="mem_attn",
        grid=(B, T // tm),
        in_specs=[pl.BlockSpec((1, tm, D), lambda b, i: (b, i, 0)),
                  pl.BlockSpec((1, D), lambda b, i: (0, 0)),
                  pl.BlockSpec((None, D, W), lambda b, i: (layer, 0, 0)),
                  pl.BlockSpec((1, L, W2), lambda b, i: (b, 0, 0)),
                  pl.BlockSpec((1, HEAD_DIM), lambda b, i: (0, 0)),
                  pl.BlockSpec((1, HEAD_DIM), lambda b, i: (0, 0)),
                  pl.BlockSpec((None, W, D), lambda b, i: (layer, 0, 0))],
        out_specs=pl.BlockSpec((1, tm, D), lambda b, i: (b, i, 0)),
        out_shape=jax.ShapeDtypeStruct((B, T, D), F32),
        compiler_params=_params("parallel", "parallel"),
    )(x, gain.reshape(1, D), wq, kv, q_gain.reshape(1, HEAD_DIM), k_gain.reshape(1, HEAD_DIM), wo)


def _eye(n):
    return (lax.broadcasted_iota(jnp.int32, (n, n), 0)
            == lax.broadcasted_iota(jnp.int32, (n, n), 1)).astype(BF16)


def _ones_row(rows, n):
    return (lax.broadcasted_iota(jnp.int32, (rows, n), 0) == 0).astype(BF16)


def _split3_f32(x):
    return tuple(t.astype(F32) for t in _split3(x))


COL_GROUPS = 1


def _col_groups(n):
    w = n // COL_GROUPS
    return [slice(i * w, (i + 1) * w) for i in range(COL_GROUPS)]


def _attn_init(n):
    return tuple((jnp.full((1, c.stop - c.start), NEG_INF, F32), jnp.zeros((VT_ROWS, c.stop - c.start), F32))
                 for c in _col_groups(n))


def _logit_bound(q_gain, k_gain):
    return HEAD_DIM * jnp.max(jnp.abs(q_gain)) * jnp.max(jnp.abs(k_gain)) * (ATTN_SCALE * LOG2E)


def _with_softmax_shift(bound, body):
    lax.cond(bound <= SOFTMAX_BOUND_LIMIT, lambda: body(bound), lambda: body(None))


def _attn_tile(k_tile, vt_tile, qa, carry, score_fn=None, shift=None):
    out = []
    for (m, acc), c in zip(carry, _col_groups(qa.shape[1])):
        s = _dot(k_tile, qa[:, c])
        if score_fn is not None:
            s = score_fn(s, c)
        if shift is not None:
            out.append((m, acc + _dot(vt_tile, jnp.exp2(s - shift).astype(BF16))))
            continue
        m_new = jnp.maximum(m, jnp.max(s, axis=0, keepdims=True))
        p = jnp.exp2(s - m_new).astype(BF16)
        out.append((m_new, jnp.exp2(m - m_new) * acc + _dot(vt_tile, p)))
    return tuple(out)


def _attn_finish(carry):
    return jnp.concatenate([acc for _, acc in carry], axis=1)


def _fox_gate_kernel(g_ref, b_ref, c_ref, kf_ref, *, heads):
    T = g_ref.shape[1]
    n = min(GATE_ROWS, T)
    tri = (lax.broadcasted_iota(jnp.int32, (n, n), 0)
           >= lax.broadcasted_iota(jnp.int32, (n, n), 1)).astype(BF16)
    lane = lax.broadcasted_iota(jnp.int32, (n, LANES), 1)

    def body(i, carry):
        rows = pl.ds(pl.multiple_of(i * n, n), n)
        cs = _exact_dot(tri, _log_sigmoid(g_ref[0, rows, :] + b_ref[...])) + carry
        c_ref[0, rows, :] = cs
        c2 = cs * LOG2E
        for h in range(heads):
            d1, d2, d3 = _split3_f32(c2[:, h:h + 1])
            feat = jnp.where(lane < 3, 1.0,
                             jnp.where(lane == 3, -d1, jnp.where(lane == 4, -d2,
                                                                 jnp.where(lane == 5, -d3, 0.0))))
            kf_ref[0, h, rows, :] = feat.astype(BF16)
        return cs[n - 1:n, :]

    lax.fori_loop(0, T // n, body, jnp.zeros((1, LANES), F32))


def fox_gate(g, bias, heads):
    B, T, _ = g.shape
    return pl.pallas_call(
        functools.partial(_fox_gate_kernel, heads=heads),
        name="fox_gate",
        grid=(B,),
        in_specs=[pl.BlockSpec((1, T, LANES), lambda b: (b, 0, 0)),
                  pl.BlockSpec((1, LANES), lambda b: (0, 0))],
        out_specs=[pl.BlockSpec((1, T, LANES), lambda b: (b, 0, 0)),
                   pl.BlockSpec((1, heads, T, LANES), lambda b: (b, 0, 0, 0))],
        out_shape=[jax.ShapeDtypeStruct((B, T, LANES), F32),
                   jax.ShapeDtypeStruct((B, heads, T, LANES), BF16)],
        compiler_params=_params("parallel"),
    )(g, bias)


def _fox_attn_kernel(q_ref, k_ref, v_ref, kf_ref, c_ref, qg_ref, kg_ref, o_ref, ka_ref, vt_ref, *, tq):
    qi = pl.program_id(2)
    T = k_ref.shape[1]
    eye = _eye(HEAD_DIM)

    @pl.when(qi == 0)
    def _():
        ka_ref[:, :HEAD_DIM] = _rms(k_ref[0], kg_ref[...]).astype(BF16)
        ka_ref[:, HEAD_DIM:] = kf_ref[0, 0]
        vt_ref[:HEAD_DIM, :] = _dot_nt(eye, v_ref[0].astype(BF16)).astype(BF16)
        vt_ref[HEAD_DIM:, :] = _ones_row(VT_ROWS - HEAD_DIM, T)

    q0 = pl.multiple_of(qi * tq, tq)
    qn = (_rms(q_ref[0], qg_ref[...]) * (ATTN_SCALE * LOG2E)).astype(BF16)
    qt = _dot_nt(eye, qn).astype(BF16)
    c1, c2, c3 = _split3_f32(c_ref[0, :, pl.ds(q0, tq)] * LOG2E)
    r = lax.broadcasted_iota(jnp.int32, (16, tq), 0)
    feat = jnp.where(r == 0, c1, jnp.where(r == 1, c2, jnp.where(r == 2, c3,
                                                                 jnp.where(r < 6, 1.0, 0.0))))
    qa = jnp.concatenate([qt, feat.astype(BF16), jnp.zeros((AUG - HEAD_DIM - 16, tq), BF16)], axis=0)

    kpos = lax.broadcasted_iota(jnp.int32, (tq, 1), 0)
    tpos = lax.broadcasted_iota(jnp.int32, (1, tq), 1)

    def attend(shift):
        def tile(j, carry, score_fn=None):
            k0 = pl.multiple_of(j * tq, tq)
            return _attn_tile(ka_ref[pl.ds(k0, tq), :], vt_ref[:, pl.ds(k0, tq)], qa, carry, score_fn, shift)

        carry = tile(qi, _attn_init(tq), lambda s, c: jnp.where(kpos <= tpos[:, c], s, NEG_INF))
        acc = _attn_finish(lax.fori_loop(0, qi, tile, carry))
        ot = (acc[:HEAD_DIM] / acc[HEAD_DIM:HEAD_DIM + 1]).astype(BF16)
        for c in range(tq // LANES):
            cs = slice(c * LANES, (c + 1) * LANES)
            o_ref[0, cs, :] = _dot_nt(eye, ot[:, cs]).astype(o_ref.dtype)

    _with_softmax_shift(_logit_bound(qg_ref[...], kg_ref[...]), attend)


def fox_attn(proj, kfeat, c_rows, q_gain, k_gain, tq):
    B, T, _ = proj.shape
    H = FOX_HEADS
    tq = min(tq, T)
    assert T % tq == 0 and tq % LANES == 0
    return pl.pallas_call(
        functools.partial(_fox_attn_kernel, tq=tq),
        name="fox_attn",
        grid=(B, H, T // tq),
        in_specs=[pl.BlockSpec((1, tq, HEAD_DIM), lambda b, h, i: (b, i, h)),
                  pl.BlockSpec((1, T, HEAD_DIM), lambda b, h, i: (b, 0, H + h)),
                  pl.BlockSpec((1, T, HEAD_DIM), lambda b, h, i: (b, 0, 2 * H + h)),
                  pl.BlockSpec((1, 1, T, LANES), lambda b, h, i: (b, h, 0, 0)),
                  pl.BlockSpec((1, 1, T), lambda b, h, i: (b * H + h, 0, 0)),
                  pl.BlockSpec((1, HEAD_DIM), lambda b, h, i: (0, 0)),
                  pl.BlockSpec((1, HEAD_DIM), lambda b, h, i: (0, 0))],
        out_specs=pl.BlockSpec((1, tq, HEAD_DIM), lambda b, h, i: (b, i, h)),
        out_shape=jax.ShapeDtypeStruct((B, T, H * HEAD_DIM), BF16),
        scratch_shapes=[pltpu.VMEM((T, AUG), BF16), pltpu.VMEM((VT_ROWS, T), BF16)],
        compiler_params=_params("parallel", "parallel", "arbitrary"),
    )(proj, proj, proj, kfeat, c_rows, q_gain.reshape(1, HEAD_DIM), k_gain.reshape(1, HEAD_DIM))


def _hgrn_kernel(q_ref, f_ref, i_ref, g_ref, la_ref, l1_ref, oml_ref, og_ref, o_ref, *, bs, chunk, span):
    T = q_ref.shape[1]
    R = LANES
    ri = lax.broadcasted_iota(jnp.int32, (R, R), 0)
    ci = lax.broadcasted_iota(jnp.int32, (R, R), 1)

    def prefix_sel(n):
        return (((ri // n) == (ci // n)) & (ri >= ci)).astype(BF16)

    s_idx = lax.broadcasted_iota(jnp.int32, (bs, HEAD_DIM), 0)
    tril = (lax.broadcasted_iota(jnp.int32, (chunk, chunk), 0)
            >= lax.broadcasted_iota(jnp.int32, (chunk, chunk), 1))

    def gates(row0, n):
        rows = pl.ds(row0, n)
        z = f_ref[0, rows, :]
        a = la_ref[0]
        y = l1_ref[0] + _log_sigmoid(z)
        log_f = jnp.maximum(a, y) + jnp.log1p(jnp.exp(-jnp.abs(a - y)))
        hq = q_ref[0, rows, :]
        return log_f, oml_ref[0] * jax.nn.sigmoid(-z), hq * jax.nn.sigmoid(hq), i_ref[0, rows, :]

    def prefixes(log_f, n):
        sel = prefix_sel(n)
        pieces = _split3(log_f)
        b = jnp.concatenate([sum(_dot(sel, p[g * R:(g + 1) * R]) for p in pieces)
                             for g in range(log_f.shape[0] // R)], axis=0)
        bt = jnp.concatenate([jnp.broadcast_to(b[(u + 1) * n - 1:(u + 1) * n, :], (n, HEAD_DIM))
                              for u in range(log_f.shape[0] // n)], axis=0)
        return b, bt

    def finish(row0, n, o):
        rows = pl.ds(row0, n)
        hg = g_ref[0, rows, :]
        o_ref[0, rows, :] = (_rms(o, og_ref[...]) * (hg * jax.nn.sigmoid(hg))).astype(o_ref.dtype)

    def matmul_path(row0, st, kk, qq, vv, b, bt):
        qe = (qq * jnp.exp(b)).astype(BF16)
        k_inv = (kk * jnp.exp(-b)).astype(BF16)
        ke = (kk * jnp.exp(bt - b)).astype(BF16)
        dec = jnp.exp(bt)
        vb = vv.astype(BF16)
        sls = [slice(u * chunk, (u + 1) * chunk) for u in range(span // chunk)]
        scores = [jnp.where(tril, _dot_nt(qe[sl], k_inv[sl]), 0.0).astype(BF16) for sl in sls]
        intra = [_dot(a, vb[sl]) for a, sl in zip(scores, sls)]
        update = [_dot_tn(vb[sl], ke[sl]) for sl in sls]
        outs = []
        for sl, o_in, upd in zip(sls, intra, update):
            outs.append(_dot_nt(qe[sl], st.astype(BF16)) + o_in)
            st = dec[sl.start:sl.start + 1, :] * st + upd
        finish(row0, span, jnp.concatenate(outs, axis=0))
        return st

    def pairwise_step(i, st, base):
        row0 = pl.multiple_of(base + i * R, R)
        log_f, kk, qq, vv = gates(row0, R)
        b, bt = prefixes(log_f, bs)
        qe = (qq * jnp.exp(b)).astype(BF16)
        ke = (kk * jnp.exp(bt - b)).astype(BF16)
        dec = jnp.exp(bt)
        vb = vv.astype(BF16)
        outs = []
        for u in range(R // bs):
            sl = slice(u * bs, (u + 1) * bs)
            bu, vu, qk, ku = b[sl], vv[sl], qq[sl], kk[sl]
            intra = []
            for t in range(bs):
                w = jnp.exp(jnp.where(s_idx <= t, bu[t:t + 1, :] - bu, NEG_INF))
                a_t = jnp.sum(w * (qk[t:t + 1, :] * ku), axis=-1, keepdims=True)
                intra.append(jnp.sum(a_t * vu, axis=0, keepdims=True))
            outs.append(_dot_nt(qe[sl], st.astype(BF16)) + jnp.concatenate(intra, axis=0))
            st = dec[u * bs:u * bs + 1, :] * st + _dot_tn(vb[sl], ke[sl])
        finish(row0, R, jnp.concatenate(outs, axis=0))
        return st

    def step(i, st):
        row0 = pl.multiple_of(i * span, span)
        log_f, kk, qq, vv = gates(row0, span)
        b, bt = prefixes(log_f, chunk)
        return lax.cond(
            jnp.min(bt) >= HGRN_SAFE_LOG_DECAY,
            lambda st: matmul_path(row0, st, kk, qq, vv, b, bt),
            lambda st: lax.fori_loop(0, span // R, functools.partial(pairwise_step, base=row0), st),
            st)

    lax.fori_loop(0, T // span, step, jnp.zeros((HEAD_DIM, HEAD_DIM), F32))


def hgrn(proj, col0, log_lb, log1m_lb, one_m_lb, o_gain):
    B, T, _ = proj.shape
    H = HGRN_HEADS
    row = lambda off: pl.BlockSpec((1, T, HEAD_DIM), lambda b, h: (b, 0, col0 + off * H + h))
    par = pl.BlockSpec((1, 1, HEAD_DIM), lambda b, h: (h, 0, 0))
    log_lb, log1m_lb, one_m_lb = (p.reshape(H, 1, HEAD_DIM) for p in (log_lb, log1m_lb, one_m_lb))
    return pl.pallas_call(
        functools.partial(_hgrn_kernel, bs=HGRN_BLOCK, chunk=HGRN_CHUNK, span=min(HGRN_SPAN, T)),
        name="hgrn",
        grid=(B, H),
        in_specs=[row(0), row(1), row(2), row(3), par, par, par,
                  pl.BlockSpec((1, HEAD_DIM), lambda b, h: (0, 0))],
        out_specs=pl.BlockSpec((1, T, HEAD_DIM), lambda b, h: (b, 0, h)),
        out_shape=jax.ShapeDtypeStruct((B, T, H * HEAD_DIM), BF16),
        compiler_params=_params("parallel", "parallel"),
    )(proj, proj, proj, proj, log_lb, log1m_lb, one_m_lb, o_gain.reshape(1, HEAD_DIM))


def _compress_kernel(t_ref, pe_ref, w1_ref, w2_ref, g_ref, o_ref, *, normalize):
    N = t_ref.shape[1] // CMP_STRIDE
    top = bot = jnp.zeros((N, w1_ref.shape[2]), F32)
    for l in range(CMP_STRIDE):
        xl = t_ref[0, pl.ds(l, N, stride=CMP_STRIDE), :]
        top = top + _dot((xl + pe_ref[l:l + 1, :]).astype(BF16), w1_ref[l])
        bot = bot + _dot((xl + pe_ref[CMP_STRIDE + l:CMP_STRIDE + l + 1, :]).astype(BF16),
                         w1_ref[CMP_STRIDE + l])
    hid = top + pltpu.roll(bot, N - 1, 0)
    out = _dot(jax.nn.gelu(hid).astype(BF16), w2_ref[...])
    if normalize:
        out = _rms(out, g_ref[...])
    valid = lax.broadcasted_iota(jnp.int32, out.shape, 0) < N - 1
    o_ref[0, 0] = jnp.where(valid, out, 0.0)


def compress(proj, col0, pe, w1, w2, layer, gain, normalize):
    B, T, _ = proj.shape
    G = NSA_KV_HEADS
    N = T // CMP_STRIDE
    Hc = w1.shape[3]
    assert CMP_BLOCK == 2 * CMP_STRIDE
    return pl.pallas_call(
        functools.partial(_compress_kernel, normalize=normalize),
        name="nsa_compress",
        grid=(B, G),
        in_specs=[pl.BlockSpec((1, T, HEAD_DIM), lambda b, g: (b, 0, col0 + g)),
                  pl.BlockSpec((None, CMP_BLOCK, HEAD_DIM), lambda b, g: (layer, 0, 0)),
                  pl.BlockSpec((None, CMP_BLOCK, HEAD_DIM, Hc), lambda b, g: (layer, 0, 0, 0)),
                  pl.BlockSpec((None, Hc, HEAD_DIM), lambda b, g: (layer, 0, 0)),
                  pl.BlockSpec((1, HEAD_DIM), lambda b, g: (0, 0))],
        out_specs=pl.BlockSpec((1, 1, N, HEAD_DIM), lambda b, g: (b, g, 0, 0)),
        out_shape=jax.ShapeDtypeStruct((B, G, N, HEAD_DIM), F32),
        compiler_params=_params("parallel", "parallel"),
    )(proj, pe, w1, w2, gain.reshape(1, HEAD_DIM))


_SEL_LANE0 = 64
_CMP_OV_ROW0 = VT_ROWS


def _nsa_kernel(*refs, **static):
    qg_ref, kg_ref = refs[8], refs[9]
    _with_softmax_shift(_logit_bound(qg_ref[...], kg_ref[...]),
                        lambda shift: _nsa_step(*refs, shift=shift, **static))


def _nsa_step(q_ref, ks_ref, vs_ref, kw_ref, vw_ref, kc_ref, vc_ref, gt_ref, qg_ref, kg_ref,
              o_ref, ksa_ref, kwa_ref, vst_ref, vwt_ref, kca_ref, vcov_ref, tiles_ref, *, tq, tk, n_sel, shift):
    g = pl.program_id(1)
    qi = pl.program_id(2)
    T = ks_ref.shape[1]
    J = NSA_GROUP
    n_slc = T // SLC_BLOCK
    nc = kc_ref.shape[2]
    n_cmp = nc - 1
    cols = J * tq
    eye = _eye(HEAD_DIM)

    @pl.when(qi == 0)
    def _():
        row = lax.broadcasted_iota(jnp.int32, (T, LANES), 0)
        lane = lax.broadcasted_iota(jnp.int32, (T, LANES), 1)
        blk = row // SLC_BLOCK
        feats = jnp.where(lane < 3, blk,
                          jnp.where(lane < 6, row % SLC_BLOCK,
                                    jnp.where(lane < 9, 1,
                                              jnp.where(lane - _SEL_LANE0 == blk, 1, 0))))
        feats = feats.astype(F32).astype(BF16)
        ksa_ref[:, :HEAD_DIM] = _rms(ks_ref[0], kg_ref[1:2, :]).astype(BF16)
        ksa_ref[:, HEAD_DIM:] = feats
        kwa_ref[:, :HEAD_DIM] = _rms(kw_ref[0], kg_ref[2:3, :]).astype(BF16)
        kwa_ref[:, HEAD_DIM:] = feats
        vst_ref[:HEAD_DIM, :] = _dot_nt(eye, vs_ref[0].astype(BF16)).astype(BF16)
        vst_ref[HEAD_DIM:, :] = _ones_row(VT_ROWS - HEAD_DIM, T)
        vwt_ref[:HEAD_DIM, :] = _dot_nt(eye, vw_ref[0].astype(BF16)).astype(BF16)
        vwt_ref[HEAD_DIM:, :] = _ones_row(VT_ROWS - HEAD_DIM, T)
        n_r = lax.broadcasted_iota(jnp.int32, (nc, LANES), 0)
        n_l = lax.broadcasted_iota(jnp.int32, (nc, LANES), 1)
        per = SLC_BLOCK // CMP_STRIDE
        cfe = jnp.where(n_l < 3, (n_r // per).astype(F32),
                        jnp.where(n_l < 6, (CMP_STRIDE * (n_r % per)).astype(F32) + 0.5 * (CMP_BLOCK - 1),
                                  jnp.where(n_l < 9, 1.0, 0.0)))
        kca_ref[:, :HEAD_DIM] = kc_ref[0, 0].astype(BF16)
        kca_ref[:, HEAD_DIM:] = cfe.astype(BF16)
        vcov_ref[:HEAD_DIM, :] = _dot_nt(eye, vc_ref[0, 0].astype(BF16)).astype(BF16)
        vcov_ref[HEAD_DIM:_CMP_OV_ROW0, :] = _ones_row(VT_ROWS - HEAD_DIM, nc)
        m_r = lax.broadcasted_iota(jnp.int32, (SLC_BLOCK, nc), 0)
        n_c = lax.broadcasted_iota(jnp.int32, (SLC_BLOCK, nc), 1)
        vcov_ref[_CMP_OV_ROW0:, :] = ((n_c * CMP_STRIDE <= m_r * SLC_BLOCK + (SLC_BLOCK - 1))
                                      & (n_c * CMP_STRIDE + (CMP_BLOCK - 1) >= m_r * SLC_BLOCK)
                                      & (n_c < n_cmp) & (m_r < n_slc)).astype(BF16)

    q0 = pl.multiple_of(qi * tq, tq)
    qraw = q_ref[0]
    qt = jnp.concatenate(
        [_dot_nt(eye, (_rms(qraw[:, j * HEAD_DIM:(j + 1) * HEAD_DIM], qg_ref[...])
                       * (ATTN_SCALE * LOG2E)).astype(BF16)).astype(BF16)
         for j in range(J)], axis=1)

    col = lax.broadcasted_iota(jnp.int32, (1, cols), 1)
    tt = q0 + col % tq
    slope2 = jnp.exp2(-0.5 * (g * J + col // tq + 1).astype(F32)) * LOG2E
    s1, s2, s3 = _split3_f32(slope2)
    r1, r2, r3 = _split3_f32(-slope2 * tt.astype(F32))
    r16 = lax.broadcasted_iota(jnp.int32, (16, cols), 0)
    feat = jnp.zeros((16, cols), F32)
    for i, v in enumerate((s1 * SLC_BLOCK, s2 * SLC_BLOCK, s3 * SLC_BLOCK, s1, s2, s3, r1, r2, r3)):
        feat = jnp.where(r16 == i, v, feat)
    feat = feat.astype(BF16)
    qa_base = jnp.concatenate([qt, feat, jnp.zeros((AUG - HEAD_DIM - 16, cols), BF16)], axis=0)

    n_col = lax.broadcasted_iota(jnp.int32, (nc, 1), 0)
    cmp_end = n_col * CMP_STRIDE + (CMP_BLOCK - 1)
    kca = kca_ref[...]
    r_hi, r_lo = [], []
    for c in _col_groups(cols):
        s = _dot(kca, qa_base[:, c])
        s = jnp.where(cmp_end <= tt[:, c], s, NEG_INF)
        if shift is None:
            m = jnp.max(s, axis=0, keepdims=True)
            p = jnp.exp2(s - jnp.where(m == NEG_INF, 0.0, m))
        else:
            p = jnp.exp2(s - shift)
        p_hi = p.astype(BF16)
        r_hi.append(_dot(vcov_ref[...], p_hi))
        r_lo.append(_dot(vcov_ref[HEAD_DIM:, :], (p - p_hi.astype(F32)).astype(BF16)))
    r_hi = jnp.concatenate(r_hi, axis=1)
    r_lo = jnp.concatenate(r_lo, axis=1)
    z = r_hi[HEAD_DIM:HEAD_DIM + 1] + r_lo[0:1]
    inv_z = 1.0 / jnp.where(z > 0, z, 1.0)

    wq = min(tq, WINDOW_Q)
    span = WINDOW + wq
    d = (lax.broadcasted_iota(jnp.int32, (span, 1), 0)
         - lax.broadcasted_iota(jnp.int32, (1, wq), 1))
    parts = []
    for c in range(tq // wq):
        k0w = pl.multiple_of(jnp.maximum(q0 + c * wq - WINDOW, 0), wq)
        off = q0 + c * wq - k0w
        bias = jnp.where(d <= off, jnp.where(d > off - WINDOW, 0.0, NEG_INF), NEG_INF)
        bias = jnp.concatenate([bias] * J, axis=1)
        qa_c = jnp.concatenate([qa_base[:, j * tq + c * wq:j * tq + (c + 1) * wq] for j in range(J)], axis=1)
        parts.append(_attn_finish(_attn_tile(
            kwa_ref[pl.ds(k0w, span), :], vwt_ref[:, pl.ds(k0w, span)], qa_c, _attn_init(J * wq),
            lambda s, sub, bias=bias: s + bias[:, sub], shift)))
    acc_w = jnp.concatenate([p[:, j * wq:(j + 1) * wq] for j in range(J) for p in parts], axis=1)

    imp4 = (r_hi[_CMP_OV_ROW0:] + r_lo[_CMP_OV_ROW0 - HEAD_DIM:]) * inv_z
    imp = imp4[:, 0:tq]
    for j in range(1, J):
        imp = imp + imp4[:, j * tq:(j + 1) * tq]
    blk = lax.broadcasted_iota(jnp.int32, (SLC_BLOCK, tq), 0)
    t_q = q0 + lax.broadcasted_iota(jnp.int32, (1, tq), 1)
    imp = jnp.where(blk * SLC_BLOCK > t_q, NEG_INF, imp)
    imp = jnp.where(blk == t_q // SLC_BLOCK, jnp.inf, jnp.where(blk == 0, jnp.inf, imp))
    blk_f = blk.astype(F32)
    vals = jnp.where(blk >= n_slc, NEG_INF, imp)
    picked = jnp.zeros((SLC_BLOCK, tq), F32)
    for _ in range(n_sel):
        best = jnp.max(vals, axis=0, keepdims=True)
        first = jnp.min(jnp.where(vals == best, blk_f, float(SLC_BLOCK)), axis=0, keepdims=True)
        hit = blk_f == first
        picked = jnp.where(hit, 1.0, picked)
        vals = jnp.where(hit, NEG_INF, vals)
    sel = (picked > 0.5) & (blk < n_slc)
    not_sel = jnp.where(sel, 0.0, -MASK_BIG).astype(BF16)
    qa_slc = jnp.concatenate([qt, feat, jnp.zeros((_SEL_LANE0 - 16, cols), BF16)]
                             + [jnp.concatenate([not_sel] * J, axis=1)], axis=0)

    def slc_tile(jt, carry, score_fn=None):
        k0 = pl.multiple_of(jt * tk, tk)
        return _attn_tile(ksa_ref[pl.ds(k0, tk), :], vst_ref[:, pl.ds(k0, tk)], qa_slc, carry, score_fn, shift)

    jd = q0 // tk
    kpos = jd * tk + lax.broadcasted_iota(jnp.int32, (tk, 1), 0)
    carry = slc_tile(jd, _attn_init(cols), lambda s, c: jnp.where(kpos <= tt[:, c], s, NEG_INF))
    sel_f = jnp.where(sel, 1.0, 0.0)
    per_tile = tk // SLC_BLOCK
    n_live = jnp.int32(0)
    for jt in range(T // tk - 1):
        live = (jnp.max(sel_f[jt * per_tile:(jt + 1) * per_tile, :]) > 0.5) & (jt < jd)
        tiles_ref[n_live] = jt
        n_live = n_live + live.astype(jnp.int32)
    acc_s = _attn_finish(lax.fori_loop(0, n_live, lambda i, c: slc_tile(tiles_ref[i], c), carry))

    gates_t = jnp.transpose(jax.nn.sigmoid(gt_ref[0]))
    gate = lambda c: jnp.concatenate([gates_t[3 * j + c:3 * j + c + 1, :] for j in range(J)], axis=1)
    o_t = ((gate(0) * inv_z) * r_hi[:HEAD_DIM]
           + (gate(1) / acc_s[HEAD_DIM:HEAD_DIM + 1]) * acc_s[:HEAD_DIM]
           + (gate(2) / acc_w[HEAD_DIM:HEAD_DIM + 1]) * acc_w[:HEAD_DIM]).astype(BF16)
    for j in range(J):
        for c in range(tq // LANES):
            o_ref[0, c * LANES:(c + 1) * LANES, j * HEAD_DIM:(j + 1) * HEAD_DIM] = _dot_nt(
                eye, o_t[:, j * tq + c * LANES:j * tq + (c + 1) * LANES]).astype(o_ref.dtype)


def nsa_attn(proj, gates, k_cmp, v_cmp, q_gain, k_gain, tq, tk):
    B, T, _ = proj.shape
    G, J = NSA_KV_HEADS, NSA_GROUP
    tk = min(tk, T)
    nc = k_cmp.shape[2]
    assert T % tk == 0 and tk % tq == 0 and T >= WINDOW + tq
    assert T // SLC_BLOCK <= LANES - _SEL_LANE0 and SLC_BLOCK <= LANES - _SEL_LANE0
    qcols = NSA_HEADS
    kv = lambda slab: pl.BlockSpec((1, T, HEAD_DIM), lambda b, g, i: (b, 0, qcols + slab * G + g))
    cmp_spec = pl.BlockSpec((1, 1, nc, HEAD_DIM), lambda b, g, i: (b, g, 0, 0))
    return pl.pallas_call(
        functools.partial(_nsa_kernel, tq=tq, tk=tk, n_sel=min(SLC_TOPK, T // SLC_BLOCK)),
        name="nsa_attn",
        grid=(B, G, T // tq),
        in_specs=[pl.BlockSpec((1, tq, J * HEAD_DIM), lambda b, g, i: (b, i, g)),
                  kv(2), kv(3), kv(4), kv(5), cmp_spec, cmp_spec,
                  pl.BlockSpec((1, tq, LANES), lambda b, g, i: (b, i, g)),
                  pl.BlockSpec((1, HEAD_DIM), lambda b, g, i: (0, 0)),
                  pl.BlockSpec((3, HEAD_DIM), lambda b, g, i: (0, 0))],
        out_specs=pl.BlockSpec((1, tq, J * HEAD_DIM), lambda b, g, i: (b, i, g)),
        out_shape=jax.ShapeDtypeStruct((B, T, NSA_HEADS * HEAD_DIM), BF16),
        scratch_shapes=[pltpu.VMEM((T, AUG), BF16), pltpu.VMEM((T, AUG), BF16),
                        pltpu.VMEM((VT_ROWS, T), BF16), pltpu.VMEM((VT_ROWS, T), BF16),
                        pltpu.VMEM((nc, AUG), BF16),
                        pltpu.VMEM((VT_ROWS + SLC_BLOCK, nc), BF16),
                        pltpu.SMEM((T // tk,), jnp.int32)],
        compiler_params=_params("parallel", "parallel", "arbitrary"),
    )(proj, proj, proj, proj, proj, k_cmp, v_cmp, gates, q_gain.reshape(1, HEAD_DIM), k_gain)


def even_weights(w_in, w_out):
    fw = FOX_HEADS * HEAD_DIM
    w_main = jnp.concatenate([w_in[:, :, :3 * fw], w_in[:, :, 3 * fw + FOX_HEADS:]], axis=2).astype(BF16)
    w_gate = jnp.pad(w_in[:, :, 3 * fw:3 * fw + FOX_HEADS], ((0, 0), (0, 0), (0, LANES - FOX_HEADS)))
    return w_main, w_gate.astype(BF16), w_out.astype(BF16)


def even_mixer(x, gain, weights, e, f_bias, q_gain, k_gain, lb, o_gain):
    B, T, D = x.shape
    xf = x.reshape(B * T, D)
    w_main, w_gate, w_out = weights
    proj, gate = rms_matmul(xf, gain, w_main, e, 1024, 1024, side_w=w_gate)
    proj, gate = proj.reshape(B, T, -1), gate.reshape(B, T, LANES)
    f_bias = jnp.pad(f_bias.reshape(1, FOX_HEADS).astype(F32), ((0, 0), (0, LANES - FOX_HEADS)))
    c, kfeat = fox_gate(gate, f_bias, FOX_HEADS)
    c_rows = c[:, :, :FOX_HEADS].transpose(0, 2, 1).reshape(B * FOX_HEADS, 1, T)
    o_fox = fox_attn(proj, kfeat, c_rows, q_gain, k_gain, 1024)
    lb = lb.astype(F32).reshape(HGRN_HEADS, HEAD_DIM)
    o_h = hgrn(proj, 3 * FOX_HEADS, jnp.log(lb), jnp.log1p(-lb), 1.0 - lb, o_gain)
    out = out_proj([o_fox.reshape(B * T, -1), o_h.reshape(B * T, -1)], w_out, e, xf, 512)
    return out.reshape(B, T, D)


def nsa_weights(w_in, w_out, w1_k, w2_k, w1_v, w2_v):
    L, D, _ = w_in.shape
    G, J = NSA_KV_HEADS, NSA_GROUP
    n_main = (NSA_HEADS + 6 * G) * HEAD_DIM
    w_gate = w_in[:, :, n_main:].reshape(L, D, G, 3 * J)
    w_gate = jnp.pad(w_gate, ((0, 0), (0, 0), (0, 0), (0, LANES - 3 * J))).reshape(L, D, G * LANES)
    return (jnp.swapaxes(w_in[:, :, :n_main], 1, 2).astype(BF16), w_gate.astype(BF16), w_out.astype(BF16),
            w1_k.astype(BF16), w2_k.astype(BF16), w1_v.astype(BF16), w2_v.astype(BF16))


def nsa_mixer(x, gain, weights, o, q_gain, k_gain, pe_k, pe_v):
    B, T, D = x.shape
    G = NSA_KV_HEADS
    xf = x.reshape(B * T, D)
    w_main, w_gate, w_out, w1_k, w2_k, w1_v, w2_v = weights
    proj, gates = rms_matmul(xf, gain, w_main, o, 1024, 1024, side_w=w_gate, w_transposed=True)
    proj, gates = proj.reshape(B, T, -1), gates.reshape(B, T, G * LANES)
    k_cmp = compress(proj, NSA_HEADS, pe_k, w1_k, w2_k, o, k_gain[0], True)
    v_cmp = compress(proj, NSA_HEADS + G, pe_v, w1_v, w2_v, o, k_gain[0], False)
    att = nsa_attn(proj, gates, k_cmp, v_cmp, q_gain, k_gain, 512, 512)
    out = out_proj([att.reshape(B * T, -1)], w_out, o, xf, 512)
    return out.reshape(B, T, D)


def kernel(x, mem, norm_mix, norm_mem, norm_ffn, mem_in_gain, even_w_in, even_w_out, fox_f_bias,
           fox_q_gain, fox_k_gain, hgrn_lb_logits, hgrn_o_gain, odd_w_in, odd_w_out, nsa_q_gain,
           nsa_k_gain, cmp_pe_k, cmp_w1_k, cmp_w2_k, cmp_pe_v, cmp_w1_v, cmp_w2_v, mem_wq, mem_wkv,
           mem_wo, mem_q_gain, mem_k_gain, ffn_w1, ffn_w3, ffn_w2):
    B, T, D = x.shape
    depth = norm_mix.shape[0]
    L = mem.shape[1]
    lb_cum = jnp.cumsum(jax.nn.softmax(hgrn_lb_logits.astype(F32), axis=0), axis=0)
    hgrn_lb = lb_cum - lb_cum[0:1]
    memf = mem.reshape(B * L, D)
    even_w = even_weights(even_w_in, even_w_out)
    nsa_w = nsa_weights(odd_w_in, odd_w_out, cmp_w1_k, cmp_w2_k, cmp_w1_v, cmp_w2_v)
    mem_wq, mem_wkv, mem_wo = (w.astype(BF16) for w in (mem_wq, mem_wkv, mem_wo))
    for layer in range(depth):
        if layer % 2 == 0:
            e = layer // 2
            x = even_mixer(x, norm_mix[layer], even_w, e, fox_f_bias[e], fox_q_gain[e], fox_k_gain[e],
                           hgrn_lb[e], hgrn_o_gain[e])
        else:
            o = layer // 2
            x = nsa_mixer(x, norm_mix[layer], nsa_w, o, nsa_q_gain[o], nsa_k_gain[o], cmp_pe_k, cmp_pe_v)
        kv = rms_matmul(memf, mem_in_gain[layer], mem_wkv, layer, 256, 1024)
        x = mem_attn(x, norm_mem[layer], mem_wq, kv.reshape(B, L, -1), mem_q_gain[layer],
                     mem_k_gain[layer], mem_wo, layer, 1024)
        x = ffn(x.reshape(B * T, D), norm_ffn[layer], ffn_w1, ffn_w3, ffn_w2, layer,
                1024, 512).reshape(B, T, D)
    return x
```

```python
import functools

import jax
import jax.numpy as jnp
from jax import lax
from jax.experimental import pallas as pl
from jax.experimental.pallas import tpu as pltpu

F32 = jnp.float32
BF16 = jnp.bfloat16
NEG_INF = float("-inf")

EPS = 1e-6
HEAD_DIM = 128
ATTN_SCALE = HEAD_DIM ** -0.5
LANES = 128
FOX_HEADS = 8
HGRN_HEADS = 8
HGRN_BLOCK = 16
HGRN_CHUNK = 64
HGRN_SPAN = 2048
GATE_ROWS = 512
HGRN_SAFE_LOG_DECAY = -80.0
NSA_HEADS = 16
NSA_KV_HEADS = 4
NSA_GROUP = NSA_HEADS // NSA_KV_HEADS
CMP_BLOCK = 32
CMP_STRIDE = 16
SLC_BLOCK = 64
SLC_TOPK = 16
WINDOW = 512
WINDOW_Q = 128
MEM_HEADS = 4
V7X_VMEM_LIMIT = 56 * 1024 * 1024
LOG2E = 1.4426950408889634
MASK_BIG = 1e30
SOFTMAX_BOUND_LIMIT = 40.0
AUG = 2 * LANES
VT_ROWS = HEAD_DIM + 16


def _params(*sem):
    return pltpu.CompilerParams(dimension_semantics=sem, vmem_limit_bytes=V7X_VMEM_LIMIT)


def _rms(x, gain):
    return x * lax.rsqrt(jnp.mean(x * x, axis=-1, keepdims=True) + EPS) * gain


def _dot(a, b):
    return jnp.dot(a, b, preferred_element_type=F32)


def _dot_nt(a, b):
    return lax.dot_general(a, b, (((1,), (1,)), ((), ())), preferred_element_type=F32)


def _dot_tn(a, b):
    return lax.dot_general(a, b, (((0,), (0,)), ((), ())), preferred_element_type=F32)


def _split3(x):
    x1 = x.astype(BF16)
    r1 = x - x1.astype(F32)
    x2 = r1.astype(BF16)
    return x1, x2, (r1 - x2.astype(F32)).astype(BF16)


def _exact_dot(sel, x):
    x1, x2, x3 = _split3(x)
    return _dot(sel, x1) + _dot(sel, x2) + _dot(sel, x3)


def _log_sigmoid(z):
    return jnp.minimum(z, 0.0) - jnp.log1p(jnp.exp(-jnp.abs(z)))


def _rms_matmul_kernel(x_ref, g_ref, w_ref, *rest, w_transposed):
    side_w_ref, o_ref, side_o_ref, hn_ref = rest if len(rest) == 4 else (None, rest[0], None, rest[1])

    @pl.when(pl.program_id(1) == 0)
    def _():
        hn_ref[...] = _rms(x_ref[...], g_ref[...]).astype(BF16)
        if side_w_ref is not None:
            side_o_ref[...] = _dot(hn_ref[...], side_w_ref[...])

    o_ref[...] = (_dot_nt if w_transposed else _dot)(hn_ref[...], w_ref[...])


def rms_matmul(x, gain, w, layer, tm, tn, side_w=None, w_transposed=False):
    M, K = x.shape
    N = w.shape[1] if w_transposed else w.shape[2]
    tm, tn = min(tm, M), min(tn, N)
    assert M % tm == 0 and N % tn == 0
    w_spec = (pl.BlockSpec((None, tn, K), lambda i, j: (layer, j, 0)) if w_transposed
              else pl.BlockSpec((None, K, tn), lambda i, j: (layer, 0, j)))
    in_specs = [pl.BlockSpec((tm, K), lambda i, j: (i, 0)),
                pl.BlockSpec((1, K), lambda i, j: (0, 0)),
                w_spec]
    out_specs = pl.BlockSpec((tm, tn), lambda i, j: (i, j))
    out_shape = jax.ShapeDtypeStruct((M, N), F32)
    args = (x, gain.reshape(1, K), w)
    if side_w is not None:
        ns = side_w.shape[2]
        in_specs.append(pl.BlockSpec((None, K, ns), lambda i, j: (layer, 0, 0)))
        out_specs = [out_specs, pl.BlockSpec((tm, ns), lambda i, j: (i, 0))]
        out_shape = [out_shape, jax.ShapeDtypeStruct((M, ns), F32)]
        args += (side_w,)
    return pl.pallas_call(
        functools.partial(_rms_matmul_kernel, w_transposed=w_transposed),
        name="rms_matmul",
        grid=(M // tm, N // tn),
        in_specs=in_specs,
        out_specs=out_specs,
        out_shape=out_shape,
        scratch_shapes=[pltpu.VMEM((tm, K), BF16)],
        compiler_params=_params("parallel", "arbitrary"),
    )(*args)


def _out_proj_kernel(*refs, n_in):
    a_refs, w_refs = refs[:n_in], refs[n_in:2 * n_in]
    res_ref, o_ref = refs[2 * n_in], refs[2 * n_in + 1]
    acc = res_ref[...]
    for a_ref, w_ref in zip(a_refs, w_refs):
        acc = acc + _dot(a_ref[...], w_ref[...])
    o_ref[...] = acc


def out_proj(acts, w, layer, res, tm):
    M, N = res.shape
    tm = min(tm, M)
    assert M % tm == 0
    n_in = len(acts)
    ka = acts[0].shape[1]
    assert all(a.shape[1] == ka for a in acts) and n_in * ka == w.shape[1]
    row_block = lambda r: pl.BlockSpec((None, ka, N), lambda i: (layer, r, 0))
    weights = [w] * n_in
    in_specs = ([pl.BlockSpec((tm, ka), lambda i: (i, 0)) for _ in acts]
                + [row_block(r) for r in range(n_in)]
                + [pl.BlockSpec((tm, N), lambda i: (i, 0))])
    return pl.pallas_call(
        functools.partial(_out_proj_kernel, n_in=n_in),
        name="out_proj",
        grid=(M // tm,),
        in_specs=in_specs,
        out_specs=pl.BlockSpec((tm, N), lambda i: (i, 0)),
        out_shape=jax.ShapeDtypeStruct((M, N), F32),
        compiler_params=_params("parallel"),
    )(*acts, *weights, res)


def _ffn_kernel(x_ref, g_ref, w1_ref, w3_ref, w2_ref, o_ref, hn_ref):
    f = pl.program_id(1)

    @pl.when(f == 0)
    def _():
        x = x_ref[...]
        hn_ref[...] = _rms(x, g_ref[...]).astype(BF16)
        o_ref[...] = x

    h = hn_ref[...]
    a = _dot(h, w1_ref[...].astype(BF16))
    b = _dot(h, w3_ref[...].astype(BF16))
    u = (a * jax.nn.sigmoid(a) * b).astype(BF16)
    o_ref[...] += _dot(u, w2_ref[...].astype(BF16))


def ffn(x, gain, w1, w3, w2, layer, tm, tf):
    M, D = x.shape
    Fh = w1.shape[2]
    tm, tf = min(tm, M), min(tf, Fh)
    assert M % tm == 0 and Fh % tf == 0
    return pl.pallas_call(
        _ffn_kernel,
        name="ffn",
        grid=(M // tm, Fh // tf),
        in_specs=[pl.BlockSpec((tm, D), lambda i, f: (i, 0)),
                  pl.BlockSpec((1, D), lambda i, f: (0, 0)),
                  pl.BlockSpec((None, D, tf), lambda i, f: (layer, 0, f)),
                  pl.BlockSpec((None, D, tf), lambda i, f: (layer, 0, f)),
                  pl.BlockSpec((None, tf, D), lambda i, f: (layer, f, 0))],
        out_specs=pl.BlockSpec((tm, D), lambda i, f: (i, 0)),
        out_shape=jax.ShapeDtypeStruct((M, D), F32),
        scratch_shapes=[pltpu.VMEM((tm, D), BF16)],
        compiler_params=_params("parallel", "arbitrary"),
    )(x, gain.reshape(1, D), w1, w3, w2)


def _mem_attn_kernel(x_ref, g_ref, wq_ref, kv_ref, qg_ref, kg_ref, wo_ref, o_ref, *, heads):
    x = x_ref[0]
    h = _rms(x, g_ref[...]).astype(BF16)
    q = _dot(h, wq_ref[...])
    kv = kv_ref[0]
    w = heads * HEAD_DIM
    outs = []
    for hh in range(heads):
        sl = slice(hh * HEAD_DIM, (hh + 1) * HEAD_DIM)
        qh = (_rms(q[:, sl], qg_ref[...]) * ATTN_SCALE).astype(BF16)
        kh = _rms(kv[:, sl], kg_ref[...]).astype(BF16)
        vh = kv[:, w + hh * HEAD_DIM: w + (hh + 1) * HEAD_DIM].astype(BF16)
        s = _dot_nt(qh, kh)
        p = jnp.exp(s - jnp.max(s, axis=-1, keepdims=True))
        z = jnp.sum(p, axis=-1, keepdims=True)
        outs.append((_dot(p.astype(BF16), vh) / z).astype(BF16))
    o = jnp.concatenate(outs, axis=-1)
    o_ref[0] = x + _dot(o, wo_ref[...])


def mem_attn(x, gain, wq, kv, q_gain, k_gain, wo, layer, tm):
    B, T, D = x.shape
    L, W2 = kv.shape[1], kv.shape[2]
    W = W2 // 2
    tm = min(tm, T)
    assert T % tm == 0
    return pl.pallas_call(
        functools.partial(_mem_attn_kernel, heads=W // HEAD_DIM),
        name="mem_attn",
        grid=(B, T // tm),
        in_specs=[pl.BlockSpec((1, tm, D), lambda b, i: (b, i, 0)),
                  pl.BlockSpec((1, D), lambda b, i: (0, 0)),
                  pl.BlockSpec((None, D, W), lambda b, i: (layer, 0, 0)),
                  pl.BlockSpec((1, L, W2), lambda b, i: (b, 0, 0)),
                  pl.BlockSpec((1, HEAD_DIM), lambda b, i: (0, 0)),
                  pl.BlockSpec((1, HEAD_DIM), lambda b, i: (0, 0)),
                  pl.BlockSpec((None, W, D), lambda b, i: (layer, 0, 0))],
        out_specs=pl.BlockSpec((1, tm, D), lambda b, i: (b, i, 0)),
        out_shape=jax.ShapeDtypeStruct((B, T, D), F32),
        compiler_params=_params("parallel", "parallel"),
    )(x, gain.reshape(1, D), wq, kv, q_gain.reshape(1, HEAD_DIM), k_gain.reshape(1, HEAD_DIM), wo)


def _eye(n):
    return (lax.broadcasted_iota(jnp.int32, (n, n), 0)
            == lax.broadcasted_iota(jnp.int32, (n, n), 1)).astype(BF16)


def _ones_row(rows, n):
    return (lax.broadcasted_iota(jnp.int32, (rows, n), 0) == 0).astype(BF16)


def _split3_f32(x):
    return tuple(t.astype(F32) for t in _split3(x))


COL_GROUPS = 1


def _col_groups(n):
    w = n // COL_GROUPS
    return [slice(i * w, (i + 1) * w) for i in range(COL_GROUPS)]


def _attn_init(n):
    return tuple((jnp.full((1, c.stop - c.start), NEG_INF, F32), jnp.zeros((VT_ROWS, c.stop - c.start), F32))
                 for c in _col_groups(n))


def _logit_bound(q_gain, k_gain):
    return HEAD_DIM * jnp.max(jnp.abs(q_gain)) * jnp.max(jnp.abs(k_gain)) * (ATTN_SCALE * LOG2E)


def _with_softmax_shift(bound, body):
    lax.cond(bound <= SOFTMAX_BOUND_LIMIT, lambda: body(bound), lambda: body(None))


def _attn_tile(k_tile, vt_tile, qa, carry, score_fn=None, shift=None):
    out = []
    for (m, acc), c in zip(carry, _col_groups(qa.shape[1])):
        s = _dot(k_tile, qa[:, c])
        if score_fn is not None:
            s = score_fn(s, c)
        if shift is not None:
            out.append((m, acc + _dot(vt_tile, jnp.exp2(s - shift).astype(BF16))))
            continue
        m_new = jnp.maximum(m, jnp.max(s, axis=0, keepdims=True))
        p = jnp.exp2(s - m_new).astype(BF16)
        out.append((m_new, jnp.exp2(m - m_new) * acc + _dot(vt_tile, p)))
    return tuple(out)


def _attn_finish(carry):
    return jnp.concatenate([acc for _, acc in carry], axis=1)


def _fox_gate_kernel(g_ref, b_ref, c_ref, kf_ref, *, heads):
    T = g_ref.shape[1]
    n = min(GATE_ROWS, T)
    tri = (lax.broadcasted_iota(jnp.int32, (n, n), 0)
           >= lax.broadcasted_iota(jnp.int32, (n, n), 1)).astype(BF16)
    lane = lax.broadcasted_iota(jnp.int32, (n, LANES), 1)

    def body(i, carry):
        rows = pl.ds(pl.multiple_of(i * n, n), n)
        cs = _exact_dot(tri, _log_sigmoid(g_ref[0, rows, :] + b_ref[...])) + carry
        c_ref[0, rows, :] = cs
        c2 = cs * LOG2E
        for h in range(heads):
            d1, d2, d3 = _split3_f32(c2[:, h:h + 1])
            feat = jnp.where(lane < 3, 1.0,
                             jnp.where(lane == 3, -d1, jnp.where(lane == 4, -d2,
                                                                 jnp.where(lane == 5, -d3, 0.0))))
            kf_ref[0, h, rows, :] = feat.astype(BF16)
        return cs[n - 1:n, :]

    lax.fori_loop(0, T // n, body, jnp.zeros((1, LANES), F32))


def fox_gate(g, bias, heads):
    B, T, _ = g.shape
    return pl.pallas_call(
        functools.partial(_fox_gate_kernel, heads=heads),
        name="fox_gate",
        grid=(B,),
        in_specs=[pl.BlockSpec((1, T, LANES), lambda b: (b, 0, 0)),
                  pl.BlockSpec((1, LANES), lambda b: (0, 0))],
        out_specs=[pl.BlockSpec((1, T, LANES), lambda b: (b, 0, 0)),
                   pl.BlockSpec((1, heads, T, LANES), lambda b: (b, 0, 0, 0))],
        out_shape=[jax.ShapeDtypeStruct((B, T, LANES), F32),
                   jax.ShapeDtypeStruct((B, heads, T, LANES), BF16)],
        compiler_params=_params("parallel"),
    )(g, bias)


def _fox_attn_kernel(q_ref, k_ref, v_ref, kf_ref, c_ref, qg_ref, kg_ref, o_ref, ka_ref, vt_ref, *, tq):
    qi = pl.program_id(2)
    T = k_ref.shape[1]
    eye = _eye(HEAD_DIM)

    @pl.when(qi == 0)
    def _():
        ka_ref[:, :HEAD_DIM] = _rms(k_ref[0], kg_ref[...]).astype(BF16)
        ka_ref[:, HEAD_DIM:] = kf_ref[0, 0]
        vt_ref[:HEAD_DIM, :] = _dot_nt(eye, v_ref[0].astype(BF16)).astype(BF16)
        vt_ref[HEAD_DIM:, :] = _ones_row(VT_ROWS - HEAD_DIM, T)

    q0 = pl.multiple_of(qi * tq, tq)
    qn = (_rms(q_ref[0], qg_ref[...]) * (ATTN_SCALE * LOG2E)).astype(BF16)
    qt = _dot_nt(eye, qn).astype(BF16)
    c1, c2, c3 = _split3_f32(c_ref[0, :, pl.ds(q0, tq)] * LOG2E)
    r = lax.broadcasted_iota(jnp.int32, (16, tq), 0)
    feat = jnp.where(r == 0, c1, jnp.where(r == 1, c2, jnp.where(r == 2, c3,
                                                                 jnp.where(r < 6, 1.0, 0.0))))
    qa = jnp.concatenate([qt, feat.astype(BF16), jnp.zeros((AUG - HEAD_DIM - 16, tq), BF16)], axis=0)

    kpos = lax.broadcasted_iota(jnp.int32, (tq, 1), 0)
    tpos = lax.broadcasted_iota(jnp.int32, (1, tq), 1)

    def attend(shift):
        def tile(j, carry, score_fn=None):
            k0 = pl.multiple_of(j * tq, tq)
            return _attn_tile(ka_ref[pl.ds(k0, tq), :], vt_ref[:, pl.ds(k0, tq)], qa, carry, score_fn, shift)

        half = tq // 2
        causal = lambda s, c: jnp.where(kpos[:half] <= tpos[:, :s.shape[1]], s, NEG_INF)
        (m, acc), = _attn_tile(ka_ref[pl.ds(q0, half), :], vt_ref[:, pl.ds(q0, half)], qa, _attn_init(tq),
                               causal, shift)
        (m2, acc2), = _attn_tile(ka_ref[pl.ds(q0 + half, half), :], vt_ref[:, pl.ds(q0 + half, half)],
                                 qa[:, half:], ((m[:, half:], acc[:, half:]),), causal, shift)
        carry = ((jnp.concatenate([m[:, :half], m2], axis=1),
                  jnp.concatenate([acc[:, :half], acc2], axis=1)),)
        acc = _attn_finish(lax.fori_loop(0, qi, tile, carry))
        ot = (acc[:HEAD_DIM] / acc[HEAD_DIM:HEAD_DIM + 1]).astype(BF16)
        for c in range(tq // LANES):
            cs = slice(c * LANES, (c + 1) * LANES)
            o_ref[0, cs, :] = _dot_nt(eye, ot[:, cs]).astype(o_ref.dtype)

    _with_softmax_shift(_logit_bound(qg_ref[...], kg_ref[...]), attend)


def fox_attn(proj, kfeat, c_rows, q_gain, k_gain, tq):
    B, T, _ = proj.shape
    H = FOX_HEADS
    tq = min(tq, T)
    assert T % tq == 0 and tq % LANES == 0
    return pl.pallas_call(
        functools.partial(_fox_attn_kernel, tq=tq),
        name="fox_attn",
        grid=(B, H, T // tq),
        in_specs=[pl.BlockSpec((1, tq, HEAD_DIM), lambda b, h, i: (b, i, h)),
                  pl.BlockSpec((1, T, HEAD_DIM), lambda b, h, i: (b, 0, H + h)),
                  pl.BlockSpec((1, T, HEAD_DIM), lambda b, h, i: (b, 0, 2 * H + h)),
                  pl.BlockSpec((1, 1, T, LANES), lambda b, h, i: (b, h, 0, 0)),
                  pl.BlockSpec((1, 1, T), lambda b, h, i: (b * H + h, 0, 0)),
                  pl.BlockSpec((1, HEAD_DIM), lambda b, h, i: (0, 0)),
                  pl.BlockSpec((1, HEAD_DIM), lambda b, h, i: (0, 0))],
        out_specs=pl.BlockSpec((1, tq, HEAD_DIM), lambda b, h, i: (b, i, h)),
        out_shape=jax.ShapeDtypeStruct((B, T, H * HEAD_DIM), BF16),
        scratch_shapes=[pltpu.VMEM((T, AUG), BF16), pltpu.VMEM((VT_ROWS, T), BF16)],
        compiler_params=_params("parallel", "parallel", "arbitrary"),
    )(proj, proj, proj, kfeat, c_rows, q_gain.reshape(1, HEAD_DIM), k_gain.reshape(1, HEAD_DIM))


def _hgrn_kernel(q_ref, f_ref, i_ref, g_ref, la_ref, l1_ref, oml_ref, og_ref, o_ref, *, bs, chunk, span):
    T = q_ref.shape[1]
    R = LANES
    ri = lax.broadcasted_iota(jnp.int32, (R, R), 0)
    ci = lax.broadcasted_iota(jnp.int32, (R, R), 1)

    def prefix_sel(n):
        return (((ri // n) == (ci // n)) & (ri >= ci)).astype(BF16)

    s_idx = lax.broadcasted_iota(jnp.int32, (bs, HEAD_DIM), 0)
    tril = (lax.broadcasted_iota(jnp.int32, (chunk, chunk), 0)
            >= lax.broadcasted_iota(jnp.int32, (chunk, chunk), 1))

    def gates(row0, n):
        rows = pl.ds(row0, n)
        z = f_ref[0, rows, :]
        a = la_ref[0]
        y = l1_ref[0] + _log_sigmoid(z)
        log_f = jnp.maximum(a, y) + jnp.log1p(jnp.exp(-jnp.abs(a - y)))
        hq = q_ref[0, rows, :]
        return log_f, oml_ref[0] * jax.nn.sigmoid(-z), hq * jax.nn.sigmoid(hq), i_ref[0, rows, :]

    def prefixes(log_f, n):
        sel = prefix_sel(n)
        pieces = _split3(log_f)
        b = jnp.concatenate([sum(_dot(sel, p[g * R:(g + 1) * R]) for p in pieces)
                             for g in range(log_f.shape[0] // R)], axis=0)
        bt = jnp.concatenate([jnp.broadcast_to(b[(u + 1) * n - 1:(u + 1) * n, :], (n, HEAD_DIM))
                              for u in range(log_f.shape[0] // n)], axis=0)
        return b, bt

    def finish(row0, n, o):
        rows = pl.ds(row0, n)
        hg = g_ref[0, rows, :]
        o_ref[0, rows, :] = (_rms(o, og_ref[...]) * (hg * jax.nn.sigmoid(hg))).astype(o_ref.dtype)

    def matmul_path(row0, st, kk, qq, vv, b, bt):
        qe = (qq * jnp.exp(b)).astype(BF16)
        k_inv = (kk * jnp.exp(-b)).astype(BF16)
        ke = (kk * jnp.exp(bt - b)).astype(BF16)
        dec = jnp.exp(bt)
        vb = vv.astype(BF16)
        sls = [slice(u * chunk, (u + 1) * chunk) for u in range(span // chunk)]
        scores = [jnp.where(tril, _dot_nt(qe[sl], k_inv[sl]), 0.0).astype(BF16) for sl in sls]
        intra = [_dot(a, vb[sl]) for a, sl in zip(scores, sls)]
        update = [_dot_tn(vb[sl], ke[sl]) for sl in sls]
        outs = []
        for sl, o_in, upd in zip(sls, intra, update):
            outs.append(_dot_nt(qe[sl], st.astype(BF16)) + o_in)
            st = dec[sl.start:sl.start + 1, :] * st + upd
        finish(row0, span, jnp.concatenate(outs, axis=0))
        return st

    def pairwise_step(i, st, base):
        row0 = pl.multiple_of(base + i * R, R)
        log_f, kk, qq, vv = gates(row0, R)
        b, bt = prefixes(log_f, bs)
        qe = (qq * jnp.exp(b)).astype(BF16)
        ke = (kk * jnp.exp(bt - b)).astype(BF16)
        dec = jnp.exp(bt)
        vb = vv.astype(BF16)
        outs = []
        for u in range(R // bs):
            sl = slice(u * bs, (u + 1) * bs)
            bu, vu, qk, ku = b[sl], vv[sl], qq[sl], kk[sl]
            intra = []
            for t in range(bs):
                w = jnp.exp(jnp.where(s_idx <= t, bu[t:t + 1, :] - bu, NEG_INF))
                a_t = jnp.sum(w * (qk[t:t + 1, :] * ku), axis=-1, keepdims=True)
                intra.append(jnp.sum(a_t * vu, axis=0, keepdims=True))
            outs.append(_dot_nt(qe[sl], st.astype(BF16)) + jnp.concatenate(intra, axis=0))
            st = dec[u * bs:u * bs + 1, :] * st + _dot_tn(vb[sl], ke[sl])
        finish(row0, R, jnp.concatenate(outs, axis=0))
        return st

    def step(i, st):
        row0 = pl.multiple_of(i * span, span)
        log_f, kk, qq, vv = gates(row0, span)
        b, bt = prefixes(log_f, chunk)
        return lax.cond(
            jnp.min(bt) >= HGRN_SAFE_LOG_DECAY,
            lambda st: matmul_path(row0, st, kk, qq, vv, b, bt),
            lambda st: lax.fori_loop(0, span // R, functools.partial(pairwise_step, base=row0), st),
            st)

    lax.fori_loop(0, T // span, step, jnp.zeros((HEAD_DIM, HEAD_DIM), F32))


def hgrn(proj, col0, log_lb, log1m_lb, one_m_lb, o_gain):
    B, T, _ = proj.shape
    H = HGRN_HEADS
    row = lambda off: pl.BlockSpec((1, T, HEAD_DIM), lambda b, h: (b, 0, col0 + off * H + h))
    par = pl.BlockSpec((1, 1, HEAD_DIM), lambda b, h: (h, 0, 0))
    log_lb, log1m_lb, one_m_lb = (p.reshape(H, 1, HEAD_DIM) for p in (log_lb, log1m_lb, one_m_lb))
    return pl.pallas_call(
        functools.partial(_hgrn_kernel, bs=HGRN_BLOCK, chunk=HGRN_CHUNK, span=min(HGRN_SPAN, T)),
        name="hgrn",
        grid=(B, H),
        in_specs=[row(0), row(1), row(2), row(3), par, par, par,
                  pl.BlockSpec((1, HEAD_DIM), lambda b, h: (0, 0))],
        out_specs=pl.BlockSpec((1, T, HEAD_DIM), lambda b, h: (b, 0, h)),
        out_shape=jax.ShapeDtypeStruct((B, T, H * HEAD_DIM), BF16),
        compiler_params=_params("parallel", "parallel"),
    )(proj, proj, proj, proj, log_lb, log1m_lb, one_m_lb, o_gain.reshape(1, HEAD_DIM))


def _compress_kernel(t_ref, pe_ref, w1_ref, w2_ref, g_ref, o_ref, *, normalize):
    N = t_ref.shape[1] // CMP_STRIDE
    top = bot = jnp.zeros((N, w1_ref.shape[2]), F32)
    for l in range(CMP_STRIDE):
        xl = t_ref[0, pl.ds(l, N, stride=CMP_STRIDE), :]
        top = top + _dot((xl + pe_ref[l:l + 1, :]).astype(BF16), w1_ref[l])
        bot = bot + _dot((xl + pe_ref[CMP_STRIDE + l:CMP_STRIDE + l + 1, :]).astype(BF16),
                         w1_ref[CMP_STRIDE + l])
    hid = top + pltpu.roll(bot, N - 1, 0)
    out = _dot(jax.nn.gelu(hid).astype(BF16), w2_ref[...])
    if normalize:
        out = _rms(out, g_ref[...])
    valid = lax.broadcasted_iota(jnp.int32, out.shape, 0) < N - 1
    o_ref[0, 0] = jnp.where(valid, out, 0.0)


def compress(proj, col0, pe, w1, w2, layer, gain, normalize):
    B, T, _ = proj.shape
    G = NSA_KV_HEADS
    N = T // CMP_STRIDE
    Hc = w1.shape[3]
    assert CMP_BLOCK == 2 * CMP_STRIDE
    return pl.pallas_call(
        functools.partial(_compress_kernel, normalize=normalize),
        name="nsa_compress",
        grid=(B, G),
        in_specs=[pl.BlockSpec((1, T, HEAD_DIM), lambda b, g: (b, 0, col0 + g)),
                  pl.BlockSpec((None, CMP_BLOCK, HEAD_DIM), lambda b, g: (layer, 0, 0)),
                  pl.BlockSpec((None, CMP_BLOCK, HEAD_DIM, Hc), lambda b, g: (layer, 0, 0, 0)),
                  pl.BlockSpec((None, Hc, HEAD_DIM), lambda b, g: (layer, 0, 0)),
                  pl.BlockSpec((1, HEAD_DIM), lambda b, g: (0, 0))],
        out_specs=pl.BlockSpec((1, 1, N, HEAD_DIM), lambda b, g: (b, g, 0, 0)),
        out_shape=jax.ShapeDtypeStruct((B, G, N, HEAD_DIM), F32),
        compiler_params=_params("parallel", "parallel"),
    )(proj, pe, w1, w2, gain.reshape(1, HEAD_DIM))


_SEL_LANE0 = 64
_CMP_OV_ROW0 = VT_ROWS


def _nsa_kernel(*refs, **static):
    qg_ref, kg_ref = refs[8], refs[9]
    _with_softmax_shift(_logit_bound(qg_ref[...], kg_ref[...]),
                        lambda shift: _nsa_step(*refs, shift=shift, **static))


def _nsa_step(q_ref, ks_ref, vs_ref, kw_ref, vw_ref, kc_ref, vc_ref, gt_ref, qg_ref, kg_ref,
              o_ref, ksa_ref, kwa_ref, vst_ref, vwt_ref, kca_ref, vcov_ref, tiles_ref, *, tq, tk, n_sel, shift):
    g = pl.program_id(1)
    qi = pl.program_id(2)
    T = ks_ref.shape[1]
    J = NSA_GROUP
    n_slc = T // SLC_BLOCK
    nc = kc_ref.shape[2]
    n_cmp = nc - 1
    cols = J * tq
    eye = _eye(HEAD_DIM)

    @pl.when(qi == 0)
    def _():
        row = lax.broadcasted_iota(jnp.int32, (T, LANES), 0)
        lane = lax.broadcasted_iota(jnp.int32, (T, LANES), 1)
        blk = row // SLC_BLOCK
        feats = jnp.where(lane < 3, blk,
                          jnp.where(lane < 6, row % SLC_BLOCK,
                                    jnp.where(lane < 9, 1,
                                              jnp.where(lane - _SEL_LANE0 == blk, 1, 0))))
        feats = feats.astype(F32).astype(BF16)
        ksa_ref[:, :HEAD_DIM] = _rms(ks_ref[0], kg_ref[1:2, :]).astype(BF16)
        ksa_ref[:, HEAD_DIM:] = feats
        kwa_ref[:, :HEAD_DIM] = _rms(kw_ref[0], kg_ref[2:3, :]).astype(BF16)
        kwa_ref[:, HEAD_DIM:] = feats
        vst_ref[:HEAD_DIM, :] = _dot_nt(eye, vs_ref[0].astype(BF16)).astype(BF16)
        vst_ref[HEAD_DIM:, :] = _ones_row(VT_ROWS - HEAD_DIM, T)
        vwt_ref[:HEAD_DIM, :] = _dot_nt(eye, vw_ref[0].astype(BF16)).astype(BF16)
        vwt_ref[HEAD_DIM:, :] = _ones_row(VT_ROWS - HEAD_DIM, T)
        n_r = lax.broadcasted_iota(jnp.int32, (nc, LANES), 0)
        n_l = lax.broadcasted_iota(jnp.int32, (nc, LANES), 1)
        per = SLC_BLOCK // CMP_STRIDE
        cfe = jnp.where(n_l < 3, (n_r // per).astype(F32),
                        jnp.where(n_l < 6, (CMP_STRIDE * (n_r % per)).astype(F32) + 0.5 * (CMP_BLOCK - 1),
                                  jnp.where(n_l < 9, 1.0, 0.0)))
        kca_ref[:, :HEAD_DIM] = kc_ref[0, 0].astype(BF16)
        kca_ref[:, HEAD_DIM:] = cfe.astype(BF16)
        vcov_ref[:HEAD_DIM, :] = _dot_nt(eye, vc_ref[0, 0].astype(BF16)).astype(BF16)
        vcov_ref[HEAD_DIM:_CMP_OV_ROW0, :] = _ones_row(VT_ROWS - HEAD_DIM, nc)
        m_r = lax.broadcasted_iota(jnp.int32, (SLC_BLOCK, nc), 0)
        n_c = lax.broadcasted_iota(jnp.int32, (SLC_BLOCK, nc), 1)
        vcov_ref[_CMP_OV_ROW0:, :] = ((n_c * CMP_STRIDE <= m_r * SLC_BLOCK + (SLC_BLOCK - 1))
                                      & (n_c * CMP_STRIDE + (CMP_BLOCK - 1) >= m_r * SLC_BLOCK)
                                      & (n_c < n_cmp) & (m_r < n_slc)).astype(BF16)

    q0 = pl.multiple_of(qi * tq, tq)
    qraw = q_ref[0]
    qt = jnp.concatenate(
        [_dot_nt(eye, (_rms(qraw[:, j * HEAD_DIM:(j + 1) * HEAD_DIM], qg_ref[...])
                       * (ATTN_SCALE * LOG2E)).astype(BF16)).astype(BF16)
         for j in range(J)], axis=1)

    col = lax.broadcasted_iota(jnp.int32, (1, cols), 1)
    tt = q0 + col % tq
    slope2 = jnp.exp2(-0.5 * (g * J + col // tq + 1).astype(F32)) * LOG2E
    s1, s2, s3 = _split3_f32(slope2)
    r1, r2, r3 = _split3_f32(-slope2 * tt.astype(F32))
    r16 = lax.broadcasted_iota(jnp.int32, (16, cols), 0)
    feat = jnp.zeros((16, cols), F32)
    for i, v in enumerate((s1 * SLC_BLOCK, s2 * SLC_BLOCK, s3 * SLC_BLOCK, s1, s2, s3, r1, r2, r3)):
        feat = jnp.where(r16 == i, v, feat)
    feat = feat.astype(BF16)
    qa_base = jnp.concatenate([qt, feat, jnp.zeros((AUG - HEAD_DIM - 16, cols), BF16)], axis=0)

    n_col = lax.broadcasted_iota(jnp.int32, (nc, 1), 0)
    cmp_end = n_col * CMP_STRIDE + (CMP_BLOCK - 1)
    kca = kca_ref[...]
    r_hi, r_lo = [], []
    for c in _col_groups(cols):
        s = _dot(kca, qa_base[:, c])
        s = jnp.where(cmp_end <= tt[:, c], s, NEG_INF)
        if shift is None:
            m = jnp.max(s, axis=0, keepdims=True)
            p = jnp.exp2(s - jnp.where(m == NEG_INF, 0.0, m))
        else:
            p = jnp.exp2(s - shift)
        p_hi = p.astype(BF16)
        r_hi.append(_dot(vcov_ref[...], p_hi))
        r_lo.append(_dot(vcov_ref[HEAD_DIM:, :], (p - p_hi.astype(F32)).astype(BF16)))
    r_hi = jnp.concatenate(r_hi, axis=1)
    r_lo = jnp.concatenate(r_lo, axis=1)
    z = r_hi[HEAD_DIM:HEAD_DIM + 1] + r_lo[0:1]
    inv_z = 1.0 / jnp.where(z > 0, z, 1.0)

    wq = min(tq, WINDOW_Q)
    span = WINDOW + wq
    d = (lax.broadcasted_iota(jnp.int32, (span, 1), 0)
         - lax.broadcasted_iota(jnp.int32, (1, wq), 1))
    parts = []
    for c in range(tq // wq):
        k0w = pl.multiple_of(jnp.maximum(q0 + c * wq - WINDOW, 0), wq)
        off = q0 + c * wq - k0w
        bias = jnp.where(d <= off, jnp.where(d > off - WINDOW, 0.0, NEG_INF), NEG_INF)
        bias = jnp.concatenate([bias] * J, axis=1)
        qa_c = jnp.concatenate([qa_base[:, j * tq + c * wq:j * tq + (c + 1) * wq] for j in range(J)], axis=1)
        parts.append(_attn_finish(_attn_tile(
            kwa_ref[pl.ds(k0w, span), :], vwt_ref[:, pl.ds(k0w, span)], qa_c, _attn_init(J * wq),
            lambda s, sub, bias=bias: s + bias[:, sub], shift)))
    acc_w = jnp.concatenate([p[:, j * wq:(j + 1) * wq] for j in range(J) for p in parts], axis=1)

    imp4 = (r_hi[_CMP_OV_ROW0:] + r_lo[_CMP_OV_ROW0 - HEAD_DIM:]) * inv_z
    imp = imp4[:, 0:tq]
    for j in range(1, J):
        imp = imp + imp4[:, j * tq:(j + 1) * tq]
    blk = lax.broadcasted_iota(jnp.int32, (SLC_BLOCK, tq), 0)
    t_q = q0 + lax.broadcasted_iota(jnp.int32, (1, tq), 1)
    imp = jnp.where(blk * SLC_BLOCK > t_q, NEG_INF, imp)
    imp = jnp.where(blk == t_q // SLC_BLOCK, jnp.inf, jnp.where(blk == 0, jnp.inf, imp))
    blk_f = blk.astype(F32)
    vals = jnp.where(blk >= n_slc, NEG_INF, imp)
    picked = jnp.zeros((SLC_BLOCK, tq), F32)
    for _ in range(n_sel):
        best = jnp.max(vals, axis=0, keepdims=True)
        first = jnp.min(jnp.where(vals == best, blk_f, float(SLC_BLOCK)), axis=0, keepdims=True)
        hit = blk_f == first
        picked = jnp.where(hit, 1.0, picked)
        vals = jnp.where(hit, NEG_INF, vals)
    sel = (picked > 0.5) & (blk < n_slc)
    not_sel = jnp.where(sel, 0.0, -MASK_BIG).astype(BF16)
    qa_slc = jnp.concatenate([qt, feat, jnp.zeros((_SEL_LANE0 - 16, cols), BF16)]
                             + [jnp.concatenate([not_sel] * J, axis=1)], axis=0)

    def slc_tile(jt, carry, score_fn=None):
        k0 = pl.multiple_of(jt * tk, tk)
        return _attn_tile(ksa_ref[pl.ds(k0, tk), :], vst_ref[:, pl.ds(k0, tk)], qa_slc, carry, score_fn, shift)

    jd = q0 // tk
    kpos = jd * tk + lax.broadcasted_iota(jnp.int32, (tk, 1), 0)
    carry = slc_tile(jd, _attn_init(cols), lambda s, c: jnp.where(kpos <= tt[:, c], s, NEG_INF))
    sel_f = jnp.where(sel, 1.0, 0.0)
    per_tile = tk // SLC_BLOCK
    n_live = jnp.int32(0)
    for jt in range(T // tk - 1):
        live = (jnp.max(sel_f[jt * per_tile:(jt + 1) * per_tile, :]) > 0.5) & (jt < jd)
        tiles_ref[n_live] = jt
        n_live = n_live + live.astype(jnp.int32)
    acc_s = _attn_finish(lax.fori_loop(0, n_live, lambda i, c: slc_tile(tiles_ref[i], c), carry))

    gates_t = jnp.transpose(jax.nn.sigmoid(gt_ref[0]))
    gate = lambda c: jnp.concatenate([gates_t[3 * j + c:3 * j + c + 1, :] for j in range(J)], axis=1)
    o_t = ((gate(0) * inv_z) * r_hi[:HEAD_DIM]
           + (gate(1) / acc_s[HEAD_DIM:HEAD_DIM + 1]) * acc_s[:HEAD_DIM]
           + (gate(2) / acc_w[HEAD_DIM:HEAD_DIM + 1]) * acc_w[:HEAD_DIM]).astype(BF16)
    for j in range(J):
        for c in range(tq // LANES):
            o_ref[0, c * LANES:(c + 1) * LANES, j * HEAD_DIM:(j + 1) * HEAD_DIM] = _dot_nt(
                eye, o_t[:, j * tq + c * LANES:j * tq + (c + 1) * LANES]).astype(o_ref.dtype)


def nsa_attn(proj, gates, k_cmp, v_cmp, q_gain, k_gain, tq, tk):
    B, T, _ = proj.shape
    G, J = NSA_KV_HEADS, NSA_GROUP
    tk = min(tk, T)
    nc = k_cmp.shape[2]
    assert T % tk == 0 and tk % tq == 0 and T >= WINDOW + tq
    assert T // SLC_BLOCK <= LANES - _SEL_LANE0 and SLC_BLOCK <= LANES - _SEL_LANE0
    qcols = NSA_HEADS
    kv = lambda slab: pl.BlockSpec((1, T, HEAD_DIM), lambda b, g, i: (b, 0, qcols + slab * G + g))
    cmp_spec = pl.BlockSpec((1, 1, nc, HEAD_DIM), lambda b, g, i: (b, g, 0, 0))
    return pl.pallas_call(
        functools.partial(_nsa_kernel, tq=tq, tk=tk, n_sel=min(SLC_TOPK, T // SLC_BLOCK)),
        name="nsa_attn",
        grid=(B, G, T // tq),
        in_specs=[pl.BlockSpec((1, tq, J * HEAD_DIM), lambda b, g, i: (b, i, g)),
                  kv(2), kv(3), kv(4), kv(5), cmp_spec, cmp_spec,
                  pl.BlockSpec((1, tq, LANES), lambda b, g, i: (b, i, g)),
                  pl.BlockSpec((1, HEAD_DIM), lambda b, g, i: (0, 0)),
                  pl.BlockSpec((3, HEAD_DIM), lambda b, g, i: (0, 0))],
        out_specs=pl.BlockSpec((1, tq, J * HEAD_DIM), lambda b, g, i: (b, i, g)),
        out_shape=jax.ShapeDtypeStruct((B, T, NSA_HEADS * HEAD_DIM), BF16),
        scratch_shapes=[pltpu.VMEM((T, AUG), BF16), pltpu.VMEM((T, AUG), BF16),
                        pltpu.VMEM((VT_ROWS, T), BF16), pltpu.VMEM((VT_ROWS, T), BF16),
                        pltpu.VMEM((nc, AUG), BF16),
                        pltpu.VMEM((VT_ROWS + SLC_BLOCK, nc), BF16),
                        pltpu.SMEM((T // tk,), jnp.int32)],
        compiler_params=_params("parallel", "parallel", "arbitrary"),
    )(proj, proj, proj, proj, proj, k_cmp, v_cmp, gates, q_gain.reshape(1, HEAD_DIM), k_gain)


def even_weights(w_in, w_out):
    fw = FOX_HEADS * HEAD_DIM
    w_main = jnp.concatenate([w_in[:, :, :3 * fw], w_in[:, :, 3 * fw + FOX_HEADS:]], axis=2).astype(BF16)
    w_gate = jnp.pad(w_in[:, :, 3 * fw:3 * fw + FOX_HEADS], ((0, 0), (0, 0), (0, LANES - FOX_HEADS)))
    return w_main, w_gate.astype(BF16), w_out.astype(BF16)


def even_mixer(x, gain, weights, e, f_bias, q_gain, k_gain, lb, o_gain):
    B, T, D = x.shape
    xf = x.reshape(B * T, D)
    w_main, w_gate, w_out = weights
    proj, gate = rms_matmul(xf, gain, w_main, e, 1024, 1024, side_w=w_gate)
    proj, gate = proj.reshape(B, T, -1), gate.reshape(B, T, LANES)
    f_bias = jnp.pad(f_bias.reshape(1, FOX_HEADS).astype(F32), ((0, 0), (0, LANES - FOX_HEADS)))
    c, kfeat = fox_gate(gate, f_bias, FOX_HEADS)
    c_rows = c[:, :, :FOX_HEADS].transpose(0, 2, 1).reshape(B * FOX_HEADS, 1, T)
    o_fox = fox_attn(proj, kfeat, c_rows, q_gain, k_gain, 1024)
    lb = lb.astype(F32).reshape(HGRN_HEADS, HEAD_DIM)
    o_h = hgrn(proj, 3 * FOX_HEADS, jnp.log(lb), jnp.log1p(-lb), 1.0 - lb, o_gain)
    out = out_proj([o_fox.reshape(B * T, -1), o_h.reshape(B * T, -1)], w_out, e, xf, 512)
    return out.reshape(B, T, D)


def nsa_weights(w_in, w_out, w1_k, w2_k, w1_v, w2_v):
    L, D, _ = w_in.shape
    G, J = NSA_KV_HEADS, NSA_GROUP
    n_main = (NSA_HEADS + 6 * G) * HEAD_DIM
    w_gate = w_in[:, :, n_main:].reshape(L, D, G, 3 * J)
    w_gate = jnp.pad(w_gate, ((0, 0), (0, 0), (0, 0), (0, LANES - 3 * J))).reshape(L, D, G * LANES)
    return (jnp.swapaxes(w_in[:, :, :n_main], 1, 2).astype(BF16), w_gate.astype(BF16), w_out.astype(BF16),
            w1_k.astype(BF16), w2_k.astype(BF16), w1_v.astype(BF16), w2_v.astype(BF16))


def nsa_mixer(x, gain, weights, o, q_gain, k_gain, pe_k, pe_v):
    B, T, D = x.shape
    G = NSA_KV_HEADS
    xf = x.reshape(B * T, D)
    w_main, w_gate, w_out, w1_k, w2_k, w1_v, w2_v = weights
    proj, gates = rms_matmul(xf, gain, w_main, o, 1024, 1024, side_w=w_gate, w_transposed=True)
    proj, gates = proj.reshape(B, T, -1), gates.reshape(B, T, G * LANES)
    k_cmp = compress(proj, NSA_HEADS, pe_k, w1_k, w2_k, o, k_gain[0], True)
    v_cmp = compress(proj, NSA_HEADS + G, pe_v, w1_v, w2_v, o, k_gain[0], False)
    att = nsa_attn(proj, gates, k_cmp, v_cmp, q_gain, k_gain, 512, 512)
    out = out_proj([att.reshape(B * T, -1)], w_out, o, xf, 512)
    return out.reshape(B, T, D)


def kernel(x, mem, norm_mix, norm_mem, norm_ffn, mem_in_gain, even_w_in, even_w_out, fox_f_bias,
           fox_q_gain, fox_k_gain, hgrn_lb_logits, hgrn_o_gain, odd_w_in, odd_w_out, nsa_q_gain,
           nsa_k_gain, cmp_pe_k, cmp_w1_k, cmp_w2_k, cmp_pe_v, cmp_w1_v, cmp_w2_v, mem_wq, mem_wkv,
           mem_wo, mem_q_gain, mem_k_gain, ffn_w1, ffn_w3, ffn_w2):
    B, T, D = x.shape
    depth = norm_mix.shape[0]
    L = mem.shape[1]
    lb_cum = jnp.cumsum(jax.nn.softmax(hgrn_lb_logits.astype(F32), axis=0), axis=0)
    hgrn_lb = lb_cum - lb_cum[0:1]
    memf = mem.reshape(B * L, D)
    even_w = even_weights(even_w_in, even_w_out)
    nsa_w = nsa_weights(odd_w_in, odd_w_out, cmp_w1_k, cmp_w2_k, cmp_w1_v, cmp_w2_v)
    mem_wq, mem_wkv, mem_wo = (w.astype(BF16) for w in (mem_wq, mem_wkv, mem_wo))
    for layer in range(depth):
        if layer % 2 == 0:
            e = layer // 2
            x = even_mixer(x, norm_mix[layer], even_w, e, fox_f_bias[e], fox_q_gain[e], fox_k_gain[e],
                           hgrn_lb[e], hgrn_o_gain[e])
        else:
            o = layer // 2
            x = nsa_mixer(x, norm_mix[layer], nsa_w, o, nsa_q_gain[o], nsa_k_gain[o], cmp_pe_k, cmp_pe_v)
        kv = rms_matmul(memf, mem_in_gain[layer], mem_wkv, layer, 256, 1024)
        x = mem_attn(x, norm_mem[layer], mem_wq, kv.reshape(B, L, -1), mem_q_gain[layer],
                     mem_k_gain[layer], mem_wo, layer, 1024)
        x = ffn(x.reshape(B * T, D), norm_ffn[layer], ffn_w1, ffn_w3, ffn_w2, layer,
                1024, 256).reshape(B, T, D)
    return x
```

```python
import functools

import jax
import jax.numpy as jnp
from jax import lax
from jax.experimental import pallas as pl
from jax.experimental.pallas import tpu as pltpu

F32 = jnp.float32
BF16 = jnp.bfloat16
NEG_INF = float("-inf")

EPS = 1e-6
HEAD_DIM = 128
ATTN_SCALE = HEAD_DIM ** -0.5
LANES = 128
FOX_HEADS = 8
HGRN_HEADS = 8
HGRN_BLOCK = 16
HGRN_CHUNK = 64
HGRN_SPAN = 2048
GATE_ROWS = 512
HGRN_SAFE_LOG_DECAY = -80.0
NSA_HEADS = 16
NSA_KV_HEADS = 4
NSA_GROUP = NSA_HEADS // NSA_KV_HEADS
CMP_BLOCK = 32
CMP_STRIDE = 16
SLC_BLOCK = 64
SLC_TOPK = 16
WINDOW = 512
WINDOW_Q = 128
MEM_HEADS = 4
V7X_VMEM_LIMIT = 56 * 1024 * 1024
LOG2E = 1.4426950408889634
MASK_BIG = 1e30
SOFTMAX_BOUND_LIMIT = 40.0
AUG = 2 * LANES
VT_ROWS = HEAD_DIM + 16


def _params(*sem):
    return pltpu.CompilerParams(dimension_semantics=sem, vmem_limit_bytes=V7X_VMEM_LIMIT)


def _rms(x, gain):
    return x * lax.rsqrt(jnp.mean(x * x, axis=-1, keepdims=True) + EPS) * gain


def _dot(a, b):
    return jnp.dot(a, b, preferred_element_type=F32)


def _dot_nt(a, b):
    return lax.dot_general(a, b, (((1,), (1,)), ((), ())), preferred_element_type=F32)


def _dot_tn(a, b):
    return lax.dot_general(a, b, (((0,), (0,)), ((), ())), preferred_element_type=F32)


def _split3(x):
    x1 = x.astype(BF16)
    r1 = x - x1.astype(F32)
    x2 = r1.astype(BF16)
    return x1, x2, (r1 - x2.astype(F32)).astype(BF16)


def _exact_dot(sel, x):
    x1, x2, x3 = _split3(x)
    return _dot(sel, x1) + _dot(sel, x2) + _dot(sel, x3)


def _log_sigmoid(z):
    return jnp.minimum(z, 0.0) - jnp.log1p(jnp.exp(-jnp.abs(z)))


def _rms_matmul_kernel(x_ref, g_ref, w_ref, *rest, w_transposed):
    side_w_ref, o_ref, side_o_ref, hn_ref = rest if len(rest) == 4 else (None, rest[0], None, rest[1])

    @pl.when(pl.program_id(1) == 0)
    def _():
        hn_ref[...] = _rms(x_ref[...], g_ref[...]).astype(BF16)
        if side_w_ref is not None:
            side_o_ref[...] = _dot(hn_ref[...], side_w_ref[...])

    o_ref[...] = (_dot_nt if w_transposed else _dot)(hn_ref[...], w_ref[...])


def rms_matmul(x, gain, w, layer, tm, tn, side_w=None, w_transposed=False):
    M, K = x.shape
    N = w.shape[1] if w_transposed else w.shape[2]
    tm, tn = min(tm, M), min(tn, N)
    assert M % tm == 0 and N % tn == 0
    w_spec = (pl.BlockSpec((None, tn, K), lambda i, j: (layer, j, 0)) if w_transposed
              else pl.BlockSpec((None, K, tn), lambda i, j: (layer, 0, j)))
    in_specs = [pl.BlockSpec((tm, K), lambda i, j: (i, 0)),
                pl.BlockSpec((1, K), lambda i, j: (0, 0)),
                w_spec]
    out_specs = pl.BlockSpec((tm, tn), lambda i, j: (i, j))
    out_shape = jax.ShapeDtypeStruct((M, N), F32)
    args = (x, gain.reshape(1, K), w)
    if side_w is not None:
        ns = side_w.shape[2]
        in_specs.append(pl.BlockSpec((None, K, ns), lambda i, j: (layer, 0, 0)))
        out_specs = [out_specs, pl.BlockSpec((tm, ns), lambda i, j: (i, 0))]
        out_shape = [out_shape, jax.ShapeDtypeStruct((M, ns), F32)]
        args += (side_w,)
    return pl.pallas_call(
        functools.partial(_rms_matmul_kernel, w_transposed=w_transposed),
        name="rms_matmul",
        grid=(M // tm, N // tn),
        in_specs=in_specs,
        out_specs=out_specs,
        out_shape=out_shape,
        scratch_shapes=[pltpu.VMEM((tm, K), BF16)],
        compiler_params=_params("parallel", "arbitrary"),
    )(*args)


def _out_proj_kernel(*refs, n_in):
    a_refs, w_refs = refs[:n_in], refs[n_in:2 * n_in]
    res_ref, o_ref = refs[2 * n_in], refs[2 * n_in + 1]
    acc = res_ref[...]
    for a_ref, w_ref in zip(a_refs, w_refs):
        acc = acc + _dot(a_ref[...], w_ref[...])
    o_ref[...] = acc


def out_proj(acts, w, layer, res, tm):
    M, N = res.shape
    tm = min(tm, M)
    assert M % tm == 0
    n_in = len(acts)
    ka = acts[0].shape[1]
    assert all(a.shape[1] == ka for a in acts) and n_in * ka == w.shape[1]
    row_block = lambda r: pl.BlockSpec((None, ka, N), lambda i: (layer, r, 0))
    weights = [w] * n_in
    in_specs = ([pl.BlockSpec((tm, ka), lambda i: (i, 0)) for _ in acts]
                + [row_block(r) for r in range(n_in)]
                + [pl.BlockSpec((tm, N), lambda i: (i, 0))])
    return pl.pallas_call(
        functools.partial(_out_proj_kernel, n_in=n_in),
        name="out_proj",
        grid=(M // tm,),
        in_specs=in_specs,
        out_specs=pl.BlockSpec((tm, N), lambda i: (i, 0)),
        out_shape=jax.ShapeDtypeStruct((M, N), F32),
        compiler_params=_params("parallel"),
    )(*acts, *weights, res)


def _ffn_kernel(x_ref, g_ref, w1_ref, w3_ref, w2_ref, o_ref, hn_ref):
    f = pl.program_id(1)

    @pl.when(f == 0)
    def _():
        x = x_ref[...]
        hn_ref[...] = _rms(x, g_ref[...]).astype(BF16)
        o_ref[...] = x

    h = hn_ref[...]
    a = _dot(h, w1_ref[...].astype(BF16))
    b = _dot(h, w3_ref[...].astype(BF16))
    u = (a * jax.nn.sigmoid(a) * b).astype(BF16)
    o_ref[...] += _dot(u, w2_ref[...].astype(BF16))


def ffn(x, gain, w1, w3, w2, layer, tm, tf):
    M, D = x.shape
    Fh = w1.shape[2]
    tm, tf = min(tm, M), min(tf, Fh)
    assert M % tm == 0 and Fh % tf == 0
    return pl.pallas_call(
        _ffn_kernel,
        name="ffn",
        grid=(M // tm, Fh // tf),
        in_specs=[pl.BlockSpec((tm, D), lambda i, f: (i, 0)),
                  pl.BlockSpec((1, D), lambda i, f: (0, 0)),
                  pl.BlockSpec((None, D, tf), lambda i, f: (layer, 0, f)),
                  pl.BlockSpec((None, D, tf), lambda i, f: (layer, 0, f)),
                  pl.BlockSpec((None, tf, D), lambda i, f: (layer, f, 0))],
        out_specs=pl.BlockSpec((tm, D), lambda i, f: (i, 0)),
        out_shape=jax.ShapeDtypeStruct((M, D), F32),
        scratch_shapes=[pltpu.VMEM((tm, D), BF16)],
        compiler_params=_params("parallel", "arbitrary"),
    )(x, gain.reshape(1, D), w1, w3, w2)


def _mem_attn_kernel(x_ref, g_ref, wq_ref, kv_ref, qg_ref, kg_ref, wo_ref, o_ref, *, heads):
    x = x_ref[0]
    h = _rms(x, g_ref[...]).astype(BF16)
    q = _dot(h, wq_ref[...])
    kv = kv_ref[0]
    w = heads * HEAD_DIM
    def attend(shift):
        outs = []
        for hh in range(heads):
            sl = slice(hh * HEAD_DIM, (hh + 1) * HEAD_DIM)
            qh = (_rms(q[:, sl], qg_ref[...]) * (ATTN_SCALE * LOG2E)).astype(BF16)
            kh = _rms(kv[:, sl], kg_ref[...]).astype(BF16)
            vh = kv[:, w + hh * HEAD_DIM: w + (hh + 1) * HEAD_DIM].astype(BF16)
            s = _dot_nt(qh, kh)
            p = jnp.exp2(s - (jnp.max(s, axis=-1, keepdims=True) if shift is None else shift))
            z = jnp.sum(p, axis=-1, keepdims=True)
            outs.append((_dot(p.astype(BF16), vh) / z).astype(BF16))
        o = jnp.concatenate(outs, axis=-1)
        o_ref[0] = x + _dot(o, wo_ref[...])

    _with_softmax_shift(_logit_bound(qg_ref[...], kg_ref[...]), attend)


def mem_attn(x, gain, wq, kv, q_gain, k_gain, wo, layer, tm):
    B, T, D = x.shape
    L, W2 = kv.shape[1], kv.shape[2]
    W = W2 // 2
    tm = min(tm, T)
    assert T % tm == 0
    return pl.pallas_call(
        functools.partial(_mem_attn_kernel, heads=W // HEAD_DIM),
        name="mem_attn",
        grid=(B, T // tm),
        in_specs=[pl.BlockSpec((1, tm, D), lambda b, i: (b, i, 0)),
                  pl.BlockSpec((1, D), lambda b, i: (0, 0)),
                  pl.BlockSpec((None, D, W), lambda b, i: (layer, 0, 0)),
                  pl.BlockSpec((1, L, W2), lambda b, i: (b, 0, 0)),
                  pl.BlockSpec((1, HEAD_DIM), lambda b, i: (0, 0)),
                  pl.BlockSpec((1, HEAD_DIM), lambda b, i: (0, 0)),
                  pl.BlockSpec((None, W, D), lambda b, i: (layer, 0, 0))],
        out_specs=pl.BlockSpec((1, tm, D), lambda b, i: (b, i, 0)),
        out_shape=jax.ShapeDtypeStruct((B, T, D), F32),
        compiler_params=_params("parallel", "parallel"),
    )(x, gain.reshape(1, D), wq, kv, q_gain.reshape(1, HEAD_DIM), k_gain.reshape(1, HEAD_DIM), wo)


def _eye(n):
    return (lax.broadcasted_iota(jnp.int32, (n, n), 0)
            == lax.broadcasted_iota(jnp.int32, (n, n), 1)).astype(BF16)


def _ones_row(rows, n):
    return (lax.broadcasted_iota(jnp.int32, (rows, n), 0) == 0).astype(BF16)


def _split3_f32(x):
    return tuple(t.astype(F32) for t in _split3(x))


COL_GROUPS = 1


def _col_groups(n):
    w = n // COL_GROUPS
    return [slice(i * w, (i + 1) * w) for i in range(COL_GROUPS)]


def _attn_init(n):
    return tuple((jnp.full((1, c.stop - c.start), NEG_INF, F32), jnp.zeros((VT_ROWS, c.stop - c.start), F32))
                 for c in _col_groups(n))


def _logit_bound(q_gain, k_gain):
    return HEAD_DIM * jnp.max(jnp.abs(q_gain)) * jnp.max(jnp.abs(k_gain)) * (ATTN_SCALE * LOG2E)


def _with_softmax_shift(bound, body):
    lax.cond(bound <= SOFTMAX_BOUND_LIMIT, lambda: body(bound), lambda: body(None))


def _attn_tile(k_tile, vt_tile, qa, carry, score_fn=None, shift=None):
    out = []
    for (m, acc), c in zip(carry, _col_groups(qa.shape[1])):
        s = _dot(k_tile, qa[:, c])
        if score_fn is not None:
            s = score_fn(s, c)
        if shift is not None:
            out.append((m, acc + _dot(vt_tile, jnp.exp2(s - shift).astype(BF16))))
            continue
        m_new = jnp.maximum(m, jnp.max(s, axis=0, keepdims=True))
        p = jnp.exp2(s - m_new).astype(BF16)
        out.append((m_new, jnp.exp2(m - m_new) * acc + _dot(vt_tile, p)))
    return tuple(out)


def _attn_finish(carry):
    return jnp.concatenate([acc for _, acc in carry], axis=1)


def _fox_gate_kernel(g_ref, b_ref, c_ref, kf_ref, *, heads):
    T = g_ref.shape[1]
    n = min(GATE_ROWS, T)
    tri = (lax.broadcasted_iota(jnp.int32, (n, n), 0)
           >= lax.broadcasted_iota(jnp.int32, (n, n), 1)).astype(BF16)
    lane = lax.broadcasted_iota(jnp.int32, (n, LANES), 1)

    def body(i, carry):
        rows = pl.ds(pl.multiple_of(i * n, n), n)
        cs = _exact_dot(tri, _log_sigmoid(g_ref[0, rows, :] + b_ref[...])) + carry
        c_ref[0, rows, :] = cs
        c2 = cs * LOG2E
        for h in range(heads):
            d1, d2, d3 = _split3_f32(c2[:, h:h + 1])
            feat = jnp.where(lane < 3, 1.0,
                             jnp.where(lane == 3, -d1, jnp.where(lane == 4, -d2,
                                                                 jnp.where(lane == 5, -d3, 0.0))))
            kf_ref[0, h, rows, :] = feat.astype(BF16)
        return cs[n - 1:n, :]

    lax.fori_loop(0, T // n, body, jnp.zeros((1, LANES), F32))


def fox_gate(g, bias, heads):
    B, T, _ = g.shape
    return pl.pallas_call(
        functools.partial(_fox_gate_kernel, heads=heads),
        name="fox_gate",
        grid=(B,),
        in_specs=[pl.BlockSpec((1, T, LANES), lambda b: (b, 0, 0)),
                  pl.BlockSpec((1, LANES), lambda b: (0, 0))],
        out_specs=[pl.BlockSpec((1, T, LANES), lambda b: (b, 0, 0)),
                   pl.BlockSpec((1, heads, T, LANES), lambda b: (b, 0, 0, 0))],
        out_shape=[jax.ShapeDtypeStruct((B, T, LANES), F32),
                   jax.ShapeDtypeStruct((B, heads, T, LANES), BF16)],
        compiler_params=_params("parallel"),
    )(g, bias)


def _fox_attn_kernel(q_ref, k_ref, v_ref, kf_ref, c_ref, qg_ref, kg_ref, o_ref, ka_ref, vt_ref, *, tq):
    qi = pl.program_id(2)
    T = k_ref.shape[1]
    eye = _eye(HEAD_DIM)

    @pl.when(qi == 0)
    def _():
        ka_ref[:, :HEAD_DIM] = _rms(k_ref[0], kg_ref[...]).astype(BF16)
        ka_ref[:, HEAD_DIM:] = kf_ref[0, 0]
        vt_ref[:HEAD_DIM, :] = _dot_nt(eye, v_ref[0].astype(BF16)).astype(BF16)
        vt_ref[HEAD_DIM:, :] = _ones_row(VT_ROWS - HEAD_DIM, T)

    q0 = pl.multiple_of(qi * tq, tq)
    qn = (_rms(q_ref[0], qg_ref[...]) * (ATTN_SCALE * LOG2E)).astype(BF16)
    qt = _dot_nt(eye, qn).astype(BF16)
    c1, c2, c3 = _split3_f32(c_ref[0, :, pl.ds(q0, tq)] * LOG2E)
    r = lax.broadcasted_iota(jnp.int32, (16, tq), 0)
    feat = jnp.where(r == 0, c1, jnp.where(r == 1, c2, jnp.where(r == 2, c3,
                                                                 jnp.where(r < 6, 1.0, 0.0))))
    qa = jnp.concatenate([qt, feat.astype(BF16), jnp.zeros((AUG - HEAD_DIM - 16, tq), BF16)], axis=0)

    kpos = lax.broadcasted_iota(jnp.int32, (tq, 1), 0)
    tpos = lax.broadcasted_iota(jnp.int32, (1, tq), 1)

    def attend(shift):
        def tile(j, carry, score_fn=None):
            k0 = pl.multiple_of(j * tq, tq)
            return _attn_tile(ka_ref[pl.ds(k0, tq), :], vt_ref[:, pl.ds(k0, tq)], qa, carry, score_fn, shift)

        half = tq // 2
        causal = lambda s, c: jnp.where(kpos[:half] <= tpos[:, :s.shape[1]], s, NEG_INF)
        (m, acc), = _attn_tile(ka_ref[pl.ds(q0, half), :], vt_ref[:, pl.ds(q0, half)], qa, _attn_init(tq),
                               causal, shift)
        (m2, acc2), = _attn_tile(ka_ref[pl.ds(q0 + half, half), :], vt_ref[:, pl.ds(q0 + half, half)],
                                 qa[:, half:], ((m[:, half:], acc[:, half:]),), causal, shift)
        carry = ((jnp.concatenate([m[:, :half], m2], axis=1),
                  jnp.concatenate([acc[:, :half], acc2], axis=1)),)
        acc = _attn_finish(lax.fori_loop(0, qi, tile, carry))
        ot = (acc[:HEAD_DIM] / acc[HEAD_DIM:HEAD_DIM + 1]).astype(BF16)
        for c in range(tq // LANES):
            cs = slice(c * LANES, (c + 1) * LANES)
            o_ref[0, cs, :] = _dot_nt(eye, ot[:, cs]).astype(o_ref.dtype)

    _with_softmax_shift(_logit_bound(qg_ref[...], kg_ref[...]), attend)


def fox_attn(proj, kfeat, c_rows, q_gain, k_gain, tq):
    B, T, _ = proj.shape
    H = FOX_HEADS
    tq = min(tq, T)
    assert T % tq == 0 and tq % LANES == 0
    return pl.pallas_call(
        functools.partial(_fox_attn_kernel, tq=tq),
        name="fox_attn",
        grid=(B, H, T // tq),
        in_specs=[pl.BlockSpec((1, tq, HEAD_DIM), lambda b, h, i: (b, i, h)),
                  pl.BlockSpec((1, T, HEAD_DIM), lambda b, h, i: (b, 0, H + h)),
                  pl.BlockSpec((1, T, HEAD_DIM), lambda b, h, i: (b, 0, 2 * H + h)),
                  pl.BlockSpec((1, 1, T, LANES), lambda b, h, i: (b, h, 0, 0)),
                  pl.BlockSpec((1, 1, T), lambda b, h, i: (b * H + h, 0, 0)),
                  pl.BlockSpec((1, HEAD_DIM), lambda b, h, i: (0, 0)),
                  pl.BlockSpec((1, HEAD_DIM), lambda b, h, i: (0, 0))],
        out_specs=pl.BlockSpec((1, tq, HEAD_DIM), lambda b, h, i: (b, i, h)),
        out_shape=jax.ShapeDtypeStruct((B, T, H * HEAD_DIM), BF16),
        scratch_shapes=[pltpu.VMEM((T, AUG), BF16), pltpu.VMEM((VT_ROWS, T), BF16)],
        compiler_params=_params("parallel", "parallel", "arbitrary"),
    )(proj, proj, proj, kfeat, c_rows, q_gain.reshape(1, HEAD_DIM), k_gain.reshape(1, HEAD_DIM))


def _hgrn_kernel(q_ref, f_ref, i_ref, g_ref, la_ref, l1_ref, oml_ref, og_ref, o_ref, *, bs, chunk, span):
    T = q_ref.shape[1]
    R = LANES
    ri = lax.broadcasted_iota(jnp.int32, (R, R), 0)
    ci = lax.broadcasted_iota(jnp.int32, (R, R), 1)

    def prefix_sel(n):
        return (((ri // n) == (ci // n)) & (ri >= ci)).astype(BF16)

    s_idx = lax.broadcasted_iota(jnp.int32, (bs, HEAD_DIM), 0)
    tril = (lax.broadcasted_iota(jnp.int32, (chunk, chunk), 0)
            >= lax.broadcasted_iota(jnp.int32, (chunk, chunk), 1))

    def gates(row0, n):
        rows = pl.ds(row0, n)
        z = f_ref[0, rows, :]
        a = la_ref[0]
        y = l1_ref[0] + _log_sigmoid(z)
        log_f = jnp.maximum(a, y) + jnp.log1p(jnp.exp(-jnp.abs(a - y)))
        hq = q_ref[0, rows, :]
        return log_f, oml_ref[0] * jax.nn.sigmoid(-z), hq * jax.nn.sigmoid(hq), i_ref[0, rows, :]

    def prefixes(log_f, n):
        sel = prefix_sel(n)
        pieces = _split3(log_f)
        b = jnp.concatenate([sum(_dot(sel, p[g * R:(g + 1) * R]) for p in pieces)
                             for g in range(log_f.shape[0] // R)], axis=0)
        bt = jnp.concatenate([jnp.broadcast_to(b[(u + 1) * n - 1:(u + 1) * n, :], (n, HEAD_DIM))
                              for u in range(log_f.shape[0] // n)], axis=0)
        return b, bt

    def finish(row0, n, o):
        rows = pl.ds(row0, n)
        hg = g_ref[0, rows, :]
        o_ref[0, rows, :] = (_rms(o, og_ref[...]) * (hg * jax.nn.sigmoid(hg))).astype(o_ref.dtype)

    def matmul_path(row0, st, kk, qq, vv, b, bt):
        qe = (qq * jnp.exp(b)).astype(BF16)
        k_inv = (kk * jnp.exp(-b)).astype(BF16)
        ke = (kk * jnp.exp(bt - b)).astype(BF16)
        dec = jnp.exp(bt)
        vb = vv.astype(BF16)
        sls = [slice(u * chunk, (u + 1) * chunk) for u in range(span // chunk)]
        scores = [jnp.where(tril, _dot_nt(qe[sl], k_inv[sl]), 0.0).astype(BF16) for sl in sls]
        intra = [_dot(a, vb[sl]) for a, sl in zip(scores, sls)]
        update = [_dot_tn(vb[sl], ke[sl]) for sl in sls]
        outs = []
        for sl, o_in, upd in zip(sls, intra, update):
            outs.append(_dot_nt(qe[sl], st.astype(BF16)) + o_in)
            st = dec[sl.start:sl.start + 1, :] * st + upd
        finish(row0, span, jnp.concatenate(outs, axis=0))
        return st

    def pairwise_step(i, st, base):
        row0 = pl.multiple_of(base + i * R, R)
        log_f, kk, qq, vv = gates(row0, R)
        b, bt = prefixes(log_f, bs)
        qe = (qq * jnp.exp(b)).astype(BF16)
        ke = (kk * jnp.exp(bt - b)).astype(BF16)
        dec = jnp.exp(bt)
        vb = vv.astype(BF16)
        outs = []
        for u in range(R // bs):
            sl = slice(u * bs, (u + 1) * bs)
            bu, vu, qk, ku = b[sl], vv[sl], qq[sl], kk[sl]
            intra = []
            for t in range(bs):
                w = jnp.exp(jnp.where(s_idx <= t, bu[t:t + 1, :] - bu, NEG_INF))
                a_t = jnp.sum(w * (qk[t:t + 1, :] * ku), axis=-1, keepdims=True)
                intra.append(jnp.sum(a_t * vu, axis=0, keepdims=True))
            outs.append(_dot_nt(qe[sl], st.astype(BF16)) + jnp.concatenate(intra, axis=0))
            st = dec[u * bs:u * bs + 1, :] * st + _dot_tn(vb[sl], ke[sl])
        finish(row0, R, jnp.concatenate(outs, axis=0))
        return st

    def step(i, st):
        row0 = pl.multiple_of(i * span, span)
        log_f, kk, qq, vv = gates(row0, span)
        b, bt = prefixes(log_f, chunk)
        return lax.cond(
            jnp.min(bt) >= HGRN_SAFE_LOG_DECAY,
            lambda st: matmul_path(row0, st, kk, qq, vv, b, bt),
            lambda st: lax.fori_loop(0, span // R, functools.partial(pairwise_step, base=row0), st),
            st)

    lax.fori_loop(0, T // span, step, jnp.zeros((HEAD_DIM, HEAD_DIM), F32))


def hgrn(proj, col0, log_lb, log1m_lb, one_m_lb, o_gain):
    B, T, _ = proj.shape
    H = HGRN_HEADS
    row = lambda off: pl.BlockSpec((1, T, HEAD_DIM), lambda b, h: (b, 0, col0 + off * H + h))
    par = pl.BlockSpec((1, 1, HEAD_DIM), lambda b, h: (h, 0, 0))
    log_lb, log1m_lb, one_m_lb = (p.reshape(H, 1, HEAD_DIM) for p in (log_lb, log1m_lb, one_m_lb))
    return pl.pallas_call(
        functools.partial(_hgrn_kernel, bs=HGRN_BLOCK, chunk=HGRN_CHUNK, span=min(HGRN_SPAN, T)),
        name="hgrn",
        grid=(B, H),
        in_specs=[row(0), row(1), row(2), row(3), par, par, par,
                  pl.BlockSpec((1, HEAD_DIM), lambda b, h: (0, 0))],
        out_specs=pl.BlockSpec((1, T, HEAD_DIM), lambda b, h: (b, 0, h)),
        out_shape=jax.ShapeDtypeStruct((B, T, H * HEAD_DIM), BF16),
        compiler_params=_params("parallel", "parallel"),
    )(proj, proj, proj, proj, log_lb, log1m_lb, one_m_lb, o_gain.reshape(1, HEAD_DIM))


def _compress_kernel(t_ref, pe_ref, w1_ref, w2_ref, g_ref, o_ref, *, normalize):
    N = t_ref.shape[1] // CMP_STRIDE
    top = bot = jnp.zeros((N, w1_ref.shape[2]), F32)
    for l in range(CMP_STRIDE):
        xl = t_ref[0, pl.ds(l, N, stride=CMP_STRIDE), :]
        top = top + _dot((xl + pe_ref[l:l + 1, :]).astype(BF16), w1_ref[l])
        bot = bot + _dot((xl + pe_ref[CMP_STRIDE + l:CMP_STRIDE + l + 1, :]).astype(BF16),
                         w1_ref[CMP_STRIDE + l])
    hid = top + pltpu.roll(bot, N - 1, 0)
    out = _dot(jax.nn.gelu(hid).astype(BF16), w2_ref[...])
    if normalize:
        out = _rms(out, g_ref[...])
    valid = lax.broadcasted_iota(jnp.int32, out.shape, 0) < N - 1
    o_ref[0, 0] = jnp.where(valid, out, 0.0)


def compress(proj, col0, pe, w1, w2, layer, gain, normalize):
    B, T, _ = proj.shape
    G = NSA_KV_HEADS
    N = T // CMP_STRIDE
    Hc = w1.shape[3]
    assert CMP_BLOCK == 2 * CMP_STRIDE
    return pl.pallas_call(
        functools.partial(_compress_kernel, normalize=normalize),
        name="nsa_compress",
        grid=(B, G),
        in_specs=[pl.BlockSpec((1, T, HEAD_DIM), lambda b, g: (b, 0, col0 + g)),
                  pl.BlockSpec((None, CMP_BLOCK, HEAD_DIM), lambda b, g: (layer, 0, 0)),
                  pl.BlockSpec((None, CMP_BLOCK, HEAD_DIM, Hc), lambda b, g: (layer, 0, 0, 0)),
                  pl.BlockSpec((None, Hc, HEAD_DIM), lambda b, g: (layer, 0, 0)),
                  pl.BlockSpec((1, HEAD_DIM), lambda b, g: (0, 0))],
        out_specs=pl.BlockSpec((1, 1, N, HEAD_DIM), lambda b, g: (b, g, 0, 0)),
        out_shape=jax.ShapeDtypeStruct((B, G, N, HEAD_DIM), F32),
        compiler_params=_params("parallel", "parallel"),
    )(proj, pe, w1, w2, gain.reshape(1, HEAD_DIM))


_SEL_LANE0 = 64
_CMP_OV_ROW0 = VT_ROWS


def _nsa_kernel(*refs, **static):
    qg_ref, kg_ref = refs[8], refs[9]
    _with_softmax_shift(_logit_bound(qg_ref[...], kg_ref[...]),
                        lambda shift: _nsa_step(*refs, shift=shift, **static))


def _nsa_step(q_ref, ks_ref, vs_ref, kw_ref, vw_ref, kc_ref, vc_ref, gt_ref, qg_ref, kg_ref,
              o_ref, ksa_ref, kwa_ref, vst_ref, vwt_ref, kca_ref, vcov_ref, tiles_ref, *, tq, tk, n_sel, shift):
    g = pl.program_id(1)
    qi = pl.program_id(2)
    T = ks_ref.shape[1]
    J = NSA_GROUP
    n_slc = T // SLC_BLOCK
    nc = kc_ref.shape[2]
    n_cmp = nc - 1
    cols = J * tq
    eye = _eye(HEAD_DIM)

    @pl.when(qi == 0)
    def _():
        row = lax.broadcasted_iota(jnp.int32, (T, LANES), 0)
        lane = lax.broadcasted_iota(jnp.int32, (T, LANES), 1)
        blk = row // SLC_BLOCK
        feats = jnp.where(lane < 3, blk,
                          jnp.where(lane < 6, row % SLC_BLOCK,
                                    jnp.where(lane < 9, 1,
                                              jnp.where(lane - _SEL_LANE0 == blk, 1, 0))))
        feats = feats.astype(F32).astype(BF16)
        ksa_ref[:, :HEAD_DIM] = _rms(ks_ref[0], kg_ref[1:2, :]).astype(BF16)
        ksa_ref[:, HEAD_DIM:] = feats
        kwa_ref[:, :HEAD_DIM] = _rms(kw_ref[0], kg_ref[2:3, :]).astype(BF16)
        kwa_ref[:, HEAD_DIM:] = feats
        vst_ref[:HEAD_DIM, :] = _dot_nt(eye, vs_ref[0].astype(BF16)).astype(BF16)
        vst_ref[HEAD_DIM:, :] = _ones_row(VT_ROWS - HEAD_DIM, T)
        vwt_ref[:HEAD_DIM, :] = _dot_nt(eye, vw_ref[0].astype(BF16)).astype(BF16)
        vwt_ref[HEAD_DIM:, :] = _ones_row(VT_ROWS - HEAD_DIM, T)
        n_r = lax.broadcasted_iota(jnp.int32, (nc, LANES), 0)
        n_l = lax.broadcasted_iota(jnp.int32, (nc, LANES), 1)
        per = SLC_BLOCK // CMP_STRIDE
        cfe = jnp.where(n_l < 3, (n_r // per).astype(F32),
                        jnp.where(n_l < 6, (CMP_STRIDE * (n_r % per)).astype(F32) + 0.5 * (CMP_BLOCK - 1),
                                  jnp.where(n_l < 9, 1.0, 0.0)))
        kca_ref[:, :HEAD_DIM] = kc_ref[0, 0].astype(BF16)
        kca_ref[:, HEAD_DIM:] = cfe.astype(BF16)
        vcov_ref[:HEAD_DIM, :] = _dot_nt(eye, vc_ref[0, 0].astype(BF16)).astype(BF16)
        vcov_ref[HEAD_DIM:_CMP_OV_ROW0, :] = _ones_row(VT_ROWS - HEAD_DIM, nc)
        m_r = lax.broadcasted_iota(jnp.int32, (SLC_BLOCK, nc), 0)
        n_c = lax.broadcasted_iota(jnp.int32, (SLC_BLOCK, nc), 1)
        vcov_ref[_CMP_OV_ROW0:, :] = ((n_c * CMP_STRIDE <= m_r * SLC_BLOCK + (SLC_BLOCK - 1))
                                      & (n_c * CMP_STRIDE + (CMP_BLOCK - 1) >= m_r * SLC_BLOCK)
                                      & (n_c < n_cmp) & (m_r < n_slc)).astype(BF16)

    q0 = pl.multiple_of(qi * tq, tq)
    qraw = q_ref[0]
    qt = jnp.concatenate(
        [_dot_nt(eye, (_rms(qraw[:, j * HEAD_DIM:(j + 1) * HEAD_DIM], qg_ref[...])
                       * (ATTN_SCALE * LOG2E)).astype(BF16)).astype(BF16)
         for j in range(J)], axis=1)

    col = lax.broadcasted_iota(jnp.int32, (1, cols), 1)
    tt = q0 + col % tq
    slope2 = jnp.exp2(-0.5 * (g * J + col // tq + 1).astype(F32)) * LOG2E
    s1, s2, s3 = _split3_f32(slope2)
    r1, r2, r3 = _split3_f32(-slope2 * tt.astype(F32))
    r16 = lax.broadcasted_iota(jnp.int32, (16, cols), 0)
    feat = jnp.zeros((16, cols), F32)
    for i, v in enumerate((s1 * SLC_BLOCK, s2 * SLC_BLOCK, s3 * SLC_BLOCK, s1, s2, s3, r1, r2, r3)):
        feat = jnp.where(r16 == i, v, feat)
    feat = feat.astype(BF16)
    qa_base = jnp.concatenate([qt, feat, jnp.zeros((AUG - HEAD_DIM - 16, cols), BF16)], axis=0)

    n_col = lax.broadcasted_iota(jnp.int32, (nc, 1), 0)
    cmp_end = n_col * CMP_STRIDE + (CMP_BLOCK - 1)
    kca = kca_ref[...]
    r_hi, r_lo = [], []
    for c in _col_groups(cols):
        s = _dot(kca, qa_base[:, c])
        s = jnp.where(cmp_end <= tt[:, c], s, NEG_INF)
        if shift is None:
            m = jnp.max(s, axis=0, keepdims=True)
            p = jnp.exp2(s - jnp.where(m == NEG_INF, 0.0, m))
        else:
            p = jnp.exp2(s - shift)
        p_hi = p.astype(BF16)
        r_hi.append(_dot(vcov_ref[...], p_hi))
        r_lo.append(_dot(vcov_ref[HEAD_DIM:, :], (p - p_hi.astype(F32)).astype(BF16)))
    r_hi = jnp.concatenate(r_hi, axis=1)
    r_lo = jnp.concatenate(r_lo, axis=1)
    z = r_hi[HEAD_DIM:HEAD_DIM + 1] + r_lo[0:1]
    inv_z = 1.0 / jnp.where(z > 0, z, 1.0)

    wq = min(tq, WINDOW_Q)
    span = WINDOW + wq
    d = (lax.broadcasted_iota(jnp.int32, (span, 1), 0)
         - lax.broadcasted_iota(jnp.int32, (1, wq), 1))
    parts = []
    for c in range(tq // wq):
        k0w = pl.multiple_of(jnp.maximum(q0 + c * wq - WINDOW, 0), wq)
        off = q0 + c * wq - k0w
        bias = jnp.where(d <= off, jnp.where(d > off - WINDOW, 0.0, NEG_INF), NEG_INF)
        bias = jnp.concatenate([bias] * J, axis=1)
        qa_c = jnp.concatenate([qa_base[:, j * tq + c * wq:j * tq + (c + 1) * wq] for j in range(J)], axis=1)
        parts.append(_attn_finish(_attn_tile(
            kwa_ref[pl.ds(k0w, span), :], vwt_ref[:, pl.ds(k0w, span)], qa_c, _attn_init(J * wq),
            lambda s, sub, bias=bias: s + bias[:, sub], shift)))
    acc_w = jnp.concatenate([p[:, j * wq:(j + 1) * wq] for j in range(J) for p in parts], axis=1)

    imp4 = (r_hi[_CMP_OV_ROW0:] + r_lo[_CMP_OV_ROW0 - HEAD_DIM:]) * inv_z
    imp = imp4[:, 0:tq]
    for j in range(1, J):
        imp = imp + imp4[:, j * tq:(j + 1) * tq]
    blk = lax.broadcasted_iota(jnp.int32, (SLC_BLOCK, tq), 0)
    t_q = q0 + lax.broadcasted_iota(jnp.int32, (1, tq), 1)
    imp = jnp.where(blk * SLC_BLOCK > t_q, NEG_INF, imp)
    imp = jnp.where(blk == t_q // SLC_BLOCK, jnp.inf, jnp.where(blk == 0, jnp.inf, imp))
    blk_f = blk.astype(F32)
    vals = jnp.where(blk >= n_slc, NEG_INF, imp)
    picked = jnp.zeros((SLC_BLOCK, tq), F32)
    for _ in range(n_sel):
        best = jnp.max(vals, axis=0, keepdims=True)
        first = jnp.min(jnp.where(vals == best, blk_f, float(SLC_BLOCK)), axis=0, keepdims=True)
        hit = blk_f == first
        picked = jnp.where(hit, 1.0, picked)
        vals = jnp.where(hit, NEG_INF, vals)
    sel = (picked > 0.5) & (blk < n_slc)
    not_sel = jnp.where(sel, 0.0, -MASK_BIG).astype(BF16)
    qa_slc = jnp.concatenate([qt, feat, jnp.zeros((_SEL_LANE0 - 16, cols), BF16)]
                             + [jnp.concatenate([not_sel] * J, axis=1)], axis=0)

    def slc_tile(jt, carry, score_fn=None):
        k0 = pl.multiple_of(jt * tk, tk)
        return _attn_tile(ksa_ref[pl.ds(k0, tk), :], vst_ref[:, pl.ds(k0, tk)], qa_slc, carry, score_fn, shift)

    jd = q0 // tk
    kpos = jd * tk + lax.broadcasted_iota(jnp.int32, (tk, 1), 0)
    carry = slc_tile(jd, _attn_init(cols), lambda s, c: jnp.where(kpos <= tt[:, c], s, NEG_INF))
    sel_f = jnp.where(sel, 1.0, 0.0)
    per_tile = tk // SLC_BLOCK
    n_live = jnp.int32(0)
    for jt in range(T // tk - 1):
        live = (jnp.max(sel_f[jt * per_tile:(jt + 1) * per_tile, :]) > 0.5) & (jt < jd)
        tiles_ref[n_live] = jt
        n_live = n_live + live.astype(jnp.int32)
    acc_s = _attn_finish(lax.fori_loop(0, n_live, lambda i, c: slc_tile(tiles_ref[i], c), carry))

    gates_t = jnp.transpose(jax.nn.sigmoid(gt_ref[0]))
    gate = lambda c: jnp.concatenate([gates_t[3 * j + c:3 * j + c + 1, :] for j in range(J)], axis=1)
    o_t = ((gate(0) * inv_z) * r_hi[:HEAD_DIM]
           + (gate(1) / acc_s[HEAD_DIM:HEAD_DIM + 1]) * acc_s[:HEAD_DIM]
           + (gate(2) / acc_w[HEAD_DIM:HEAD_DIM + 1]) * acc_w[:HEAD_DIM]).astype(BF16)
    for j in range(J):
        for c in range(tq // LANES):
            o_ref[0, c * LANES:(c + 1) * LANES, j * HEAD_DIM:(j + 1) * HEAD_DIM] = _dot_nt(
                eye, o_t[:, j * tq + c * LANES:j * tq + (c + 1) * LANES]).astype(o_ref.dtype)


def nsa_attn(proj, gates, k_cmp, v_cmp, q_gain, k_gain, tq, tk):
    B, T, _ = proj.shape
    G, J = NSA_KV_HEADS, NSA_GROUP
    tk = min(tk, T)
    nc = k_cmp.shape[2]
    assert T % tk == 0 and tk % tq == 0 and T >= WINDOW + tq
    assert T // SLC_BLOCK <= LANES - _SEL_LANE0 and SLC_BLOCK <= LANES - _SEL_LANE0
    qcols = NSA_HEADS
    kv = lambda slab: pl.BlockSpec((1, T, HEAD_DIM), lambda b, g, i: (b, 0, qcols + slab * G + g))
    cmp_spec = pl.BlockSpec((1, 1, nc, HEAD_DIM), lambda b, g, i: (b, g, 0, 0))
    return pl.pallas_call(
        functools.partial(_nsa_kernel, tq=tq, tk=tk, n_sel=min(SLC_TOPK, T // SLC_BLOCK)),
        name="nsa_attn",
        grid=(B, G, T // tq),
        in_specs=[pl.BlockSpec((1, tq, J * HEAD_DIM), lambda b, g, i: (b, i, g)),
                  kv(2), kv(3), kv(4), kv(5), cmp_spec, cmp_spec,
                  pl.BlockSpec((1, tq, LANES), lambda b, g, i: (b, i, g)),
                  pl.BlockSpec((1, HEAD_DIM), lambda b, g, i: (0, 0)),
                  pl.BlockSpec((3, HEAD_DIM), lambda b, g, i: (0, 0))],
        out_specs=pl.BlockSpec((1, tq, J * HEAD_DIM), lambda b, g, i: (b, i, g)),
        out_shape=jax.ShapeDtypeStruct((B, T, NSA_HEADS * HEAD_DIM), BF16),
        scratch_shapes=[pltpu.VMEM((T, AUG), BF16), pltpu.VMEM((T, AUG), BF16),
                        pltpu.VMEM((VT_ROWS, T), BF16), pltpu.VMEM((VT_ROWS, T), BF16),
                        pltpu.VMEM((nc, AUG), BF16),
                        pltpu.VMEM((VT_ROWS + SLC_BLOCK, nc), BF16),
                        pltpu.SMEM((T // tk,), jnp.int32)],
        compiler_params=_params("parallel", "parallel", "arbitrary"),
    )(proj, proj, proj, proj, proj, k_cmp, v_cmp, gates, q_gain.reshape(1, HEAD_DIM), k_gain)


def even_weights(w_in, w_out):
    fw = FOX_HEADS * HEAD_DIM
    w_main = jnp.concatenate([w_in[:, :, :3 * fw], w_in[:, :, 3 * fw + FOX_HEADS:]], axis=2).astype(BF16)
    w_gate = jnp.pad(w_in[:, :, 3 * fw:3 * fw + FOX_HEADS], ((0, 0), (0, 0), (0, LANES - FOX_HEADS)))
    return w_main, w_gate.astype(BF16), w_out.astype(BF16)


def even_mixer(x, gain, weights, e, f_bias, q_gain, k_gain, lb, o_gain):
    B, T, D = x.shape
    xf = x.reshape(B * T, D)
    w_main, w_gate, w_out = weights
    proj, gate = rms_matmul(xf, gain, w_main, e, 1024, 1024, side_w=w_gate)
    proj, gate = proj.reshape(B, T, -1), gate.reshape(B, T, LANES)
    f_bias = jnp.pad(f_bias.reshape(1, FOX_HEADS).astype(F32), ((0, 0), (0, LANES - FOX_HEADS)))
    c, kfeat = fox_gate(gate, f_bias, FOX_HEADS)
    c_rows = c[:, :, :FOX_HEADS].transpose(0, 2, 1).reshape(B * FOX_HEADS, 1, T)
    o_fox = fox_attn(proj, kfeat, c_rows, q_gain, k_gain, 1024)
    lb = lb.astype(F32).reshape(HGRN_HEADS, HEAD_DIM)
    o_h = hgrn(proj, 3 * FOX_HEADS, jnp.log(lb), jnp.log1p(-lb), 1.0 - lb, o_gain)
    out = out_proj([o_fox.reshape(B * T, -1), o_h.reshape(B * T, -1)], w_out, e, xf, 512)
    return out.reshape(B, T, D)


def nsa_weights(w_in, w_out, w1_k, w2_k, w1_v, w2_v):
    L, D, _ = w_in.shape
    G, J = NSA_KV_HEADS, NSA_GROUP
    n_main = (NSA_HEADS + 6 * G) * HEAD_DIM
    w_gate = w_in[:, :, n_main:].reshape(L, D, G, 3 * J)
    w_gate = jnp.pad(w_gate, ((0, 0), (0, 0), (0, 0), (0, LANES - 3 * J))).reshape(L, D, G * LANES)
    return (jnp.swapaxes(w_in[:, :, :n_main], 1, 2).astype(BF16), w_gate.astype(BF16), w_out.astype(BF16),
            w1_k.astype(BF16), w2_k.astype(BF16), w1_v.astype(BF16), w2_v.astype(BF16))


def nsa_mixer(x, gain, weights, o, q_gain, k_gain, pe_k, pe_v):
    B, T, D = x.shape
    G = NSA_KV_HEADS
    xf = x.reshape(B * T, D)
    w_main, w_gate, w_out, w1_k, w2_k, w1_v, w2_v = weights
    proj, gates = rms_matmul(xf, gain, w_main, o, 1024, 1024, side_w=w_gate, w_transposed=True)
    proj, gates = proj.reshape(B, T, -1), gates.reshape(B, T, G * LANES)
    k_cmp = compress(proj, NSA_HEADS, pe_k, w1_k, w2_k, o, k_gain[0], True)
    v_cmp = compress(proj, NSA_HEADS + G, pe_v, w1_v, w2_v, o, k_gain[0], False)
    att = nsa_attn(proj, gates, k_cmp, v_cmp, q_gain, k_gain, 512, 512)
    out = out_proj([att.reshape(B * T, -1)], w_out, o, xf, 512)
    return out.reshape(B, T, D)


def kernel(x, mem, norm_mix, norm_mem, norm_ffn, mem_in_gain, even_w_in, even_w_out, fox_f_bias,
           fox_q_gain, fox_k_gain, hgrn_lb_logits, hgrn_o_gain, odd_w_in, odd_w_out, nsa_q_gain,
           nsa_k_gain, cmp_pe_k, cmp_w1_k, cmp_w2_k, cmp_pe_v, cmp_w1_v, cmp_w2_v, mem_wq, mem_wkv,
           mem_wo, mem_q_gain, mem_k_gain, ffn_w1, ffn_w3, ffn_w2):
    B, T, D = x.shape
    depth = norm_mix.shape[0]
    L = mem.shape[1]
    lb_cum = jnp.cumsum(jax.nn.softmax(hgrn_lb_logits.astype(F32), axis=0), axis=0)
    hgrn_lb = lb_cum - lb_cum[0:1]
    memf = mem.reshape(B * L, D)
    even_w = even_weights(even_w_in, even_w_out)
    nsa_w = nsa_weights(odd_w_in, odd_w_out, cmp_w1_k, cmp_w2_k, cmp_w1_v, cmp_w2_v)
    mem_wq, mem_wkv, mem_wo = (w.astype(BF16) for w in (mem_wq, mem_wkv, mem_wo))
    for layer in range(depth):
        if layer % 2 == 0:
            e = layer // 2
            x = even_mixer(x, norm_mix[layer], even_w, e, fox_f_bias[e], fox_q_gain[e], fox_k_gain[e],
                           hgrn_lb[e], hgrn_o_gain[e])
        else:
            o = layer // 2
            x = nsa_mixer(x, norm_mix[layer], nsa_w, o, nsa_q_gain[o], nsa_k_gain[o], cmp_pe_k, cmp_pe_v)
        kv = rms_matmul(memf, mem_in_gain[layer], mem_wkv, layer, 256, 1024)
        x = mem_attn(x, norm_mem[layer], mem_wq, kv.reshape(B, L, -1), mem_q_gain[layer],
                     mem_k_gain[layer], mem_wo, layer, 1024)
        x = ffn(x.reshape(B * T, D), norm_ffn[layer], ffn_w1, ffn_w3, ffn_w2, layer,
                1024, 256).reshape(B, T, D)
    return x
```
